```python
import jax, jax.numpy as jnp
from jax import lax
import numpy as np

D_MODEL = 1024
BATCH = 4
SEQ = 4096
DEPTH = 4

N_MIXERS = 3
SB_HEADS = 16
SB_HEAD_DIM = D_MODEL // SB_HEADS
SB_BLOCK = 128
SSD_EXPAND = 2
SSD_D_INNER = SSD_EXPAND * D_MODEL
SSD_HEAD_DIM = 64
SSD_HEADS = SSD_D_INNER // SSD_HEAD_DIM
SSD_GROUPS = 8
SSD_HPG = SSD_HEADS // SSD_GROUPS
SSD_STATE = 128
SSD_CONV = 4
SSD_CHUNK = 128
SSD_CONV_DIM = SSD_D_INNER + 2 * SSD_GROUPS * SSD_STATE
SSD_IN_DIM = SSD_D_INNER + SSD_CONV_DIM + SSD_HEADS
SSD_NORM_GROUP = SSD_D_INNER // SSD_GROUPS
SC_WIDTH = 3
D_FF = 2816
RMS_EPS = 1e-6
N_SB_LAYERS = (DEPTH + 2) // 3
N_SSD_LAYERS = (DEPTH + 1) // 3
N_SC_LAYERS = DEPTH // 3

kernel_name = "hybrid_sb_ssd_shortconv_macaron"


def rmsnorm(x, g):
    xf = x.astype(jnp.float32)
    y = xf * lax.rsqrt(jnp.mean(xf * xf, axis=-1, keepdims=True) + RMS_EPS)
    return (y * g.astype(jnp.float32)).astype(x.dtype)


def swiglu(x, w_gu, w_down):
    gate, up = jnp.split(x @ w_gu, 2, axis=-1)
    return (jax.nn.silu(gate) * up) @ w_down


def causal_depthwise_conv(x, w):
    k, c = w.shape
    return lax.conv_general_dilated(
        x, w[:, None, :].astype(x.dtype), window_strides=(1,), padding=[(k - 1, 0)],
        dimension_numbers=('NWC', 'WIO', 'NWC'), feature_group_count=c)


def stick_breaking_attention(x, w_qkv, w_o):
    b, l, _ = x.shape
    q, k, v = jnp.split(x @ w_qkv, 3, axis=-1)
    q = q.reshape(b, l, SB_HEADS, SB_HEAD_DIM)
    k = k.reshape(b, l, SB_HEADS, SB_HEAD_DIM)
    v = v.reshape(b, l, SB_HEADS, SB_HEAD_DIM)
    n_blk = l // SB_BLOCK
    q_blocks = jnp.moveaxis(q.reshape(b, n_blk, SB_BLOCK, SB_HEADS, SB_HEAD_DIM), 1, 0)
    key_pos = jnp.arange(l)
    scale = SB_HEAD_DIM ** -0.5

    def one_block(args):
        q_blk, blk = args
        z = jnp.einsum('bqhd,bkhd->bhqk', q_blk, k).astype(jnp.float32) * scale
        q_pos = blk * SB_BLOCK + jnp.arange(SB_BLOCK)
        mask = key_pos[None, :] < q_pos[:, None]
        log_beta = jax.nn.log_sigmoid(z)
        log_keep = jnp.where(mask, jax.nn.log_sigmoid(-z), 0.0)
        tail = lax.cumsum(log_keep, axis=3, reverse=True) - log_keep
        att = jnp.where(mask, jnp.exp(log_beta + tail), 0.0)
        return jnp.einsum('bhqk,bkhd->bqhd', att.astype(v.dtype), v)

    o = lax.map(one_block, (q_blocks, jnp.arange(n_blk)))
    o = jnp.moveaxis(o, 0, 1).reshape(b, l, D_MODEL)
    return o @ w_o


def ssd_chunked(xs, dt, a, bm, cm):
    b, l, g, r, p = xs.shape
    n = bm.shape[-1]
    nc, cl = l // SSD_CHUNK, SSD_CHUNK
    x_c = xs.reshape(b, nc, cl, g, r, p)
    dt_c = dt.reshape(b, nc, cl, g, r)
    b_c = bm.reshape(b, nc, cl, g, n)
    c_c = cm.reshape(b, nc, cl, g, n)
    a_cum = jnp.cumsum(dt_c * a, axis=2)
    seg = a_cum[:, :, :, None] - a_cum[:, :, None, :]
    causal = jnp.tril(jnp.ones((cl, cl), dtype=bool))[:, :, None, None]
    decay = jnp.exp(jnp.where(causal, seg, -jnp.inf))
    cb = jnp.einsum('bclgn,bcsgn->bclsg', c_c, b_c)
    w = cb[..., None] * decay * dt_c[:, :, None]
    y_diag = jnp.einsum('bclsgr,bcsgrp->bclgrp', w, x_c)
    a_last = a_cum[:, :, -1]
    to_end = jnp.exp(a_last[:, :, None] - a_cum) * dt_c
    states = jnp.einsum('bcsgn,bcsgr,bcsgrp->bcgrpn', b_c, to_end, x_c)

    def step(h, inp):
        st, al = inp
        return h * jnp.exp(al)[..., None, None] + st, h

    h0 = jnp.zeros((b, g, r, p, n), dtype=xs.dtype)
    _, h_prev = lax.scan(step, h0, (jnp.moveaxis(states, 1, 0), jnp.moveaxis(a_last, 1, 0)))
    h_prev = jnp.moveaxis(h_prev, 0, 1)
    y_off = jnp.einsum('bclgn,bcgrpn,bclgr->bclgrp', c_c, h_prev, jnp.exp(a_cum))
    return (y_diag + y_off).reshape(b, l, g, r, p)


def ssd_mixer(x, w_in, conv_w, conv_b, dt_bias, a_log, d_skip, norm_g, w_out):
    b, l, _ = x.shape
    z, xbc, dt = jnp.split(x @ w_in, [SSD_D_INNER, SSD_D_INNER + SSD_CONV_DIM], axis=-1)
    xbc = jax.nn.silu(causal_depthwise_conv(xbc, conv_w) + conv_b)
    xs, bm, cm = jnp.split(xbc, [SSD_D_INNER, SSD_D_INNER + SSD_GROUPS * SSD_STATE], axis=-1)
    xs = xs.astype(jnp.float32).reshape(b, l, SSD_GROUPS, SSD_HPG, SSD_HEAD_DIM)
    bm = bm.astype(jnp.float32).reshape(b, l, SSD_GROUPS, SSD_STATE)
    cm = cm.astype(jnp.float32).reshape(b, l, SSD_GROUPS, SSD_STATE)
    dt = jax.nn.softplus(dt.astype(jnp.float32) + dt_bias.astype(jnp.float32))
    dt = dt.reshape(b, l, SSD_GROUPS, SSD_HPG)
    a = -jnp.exp(a_log.astype(jnp.float32)).reshape(SSD_GROUPS, SSD_HPG)
    y = ssd_chunked(xs, dt, a, bm, cm)
    y = y + d_skip.astype(jnp.float32).reshape(SSD_GROUPS, SSD_HPG, 1) * xs
    y = y.reshape(b, l, SSD_D_INNER).astype(x.dtype) * jax.nn.silu(z)
    y = rmsnorm(y.reshape(b, l, SSD_GROUPS, SSD_NORM_GROUP),
                norm_g.reshape(SSD_GROUPS, SSD_NORM_GROUP)).reshape(b, l, SSD_D_INNER)
    return y @ w_out


def short_conv_mixer(x, w_in, conv_w, w_out):
    b_gate, c_gate, h = jnp.split(x @ w_in, 3, axis=-1)
    u = causal_depthwise_conv(c_gate * h, conv_w)
    return (b_gate * u) @ w_out


def setup_inputs(seed: int = 0) -> dict:
    key = jax.random.key(seed)
    ks = jax.random.split(key, 24)
    f32 = jnp.float32

    def wn(k, shape, fan_in):
        return jax.random.normal(k, shape, f32) * (fan_in ** -0.5)

    def gain(k, shape):
        return 1.0 + 0.01 * jax.random.normal(k, shape, f32)

    dt0 = jnp.exp(jax.random.uniform(ks[14], (N_SSD_LAYERS, SSD_HEADS), f32)
                  * (np.log(0.1) - np.log(0.001)) + np.log(0.001))
    dt_bias = dt0 + jnp.log(-jnp.expm1(-dt0))
    return {
        "x": jax.random.normal(ks[0], (BATCH, SEQ, D_MODEL), f32),
        "ffn1_norm": gain(ks[1], (DEPTH, D_MODEL)),
        "ffn1_w_gu": wn(ks[2], (DEPTH, D_MODEL, 2 * D_FF), D_MODEL),
        "ffn1_w_down": wn(ks[3], (DEPTH, D_FF, D_MODEL), D_FF),
        "mix_norm": gain(ks[4], (DEPTH, D_MODEL)),
        "ffn2_norm": gain(ks[5], (DEPTH, D_MODEL)),
        "ffn2_w_gu": wn(ks[6], (DEPTH, D_MODEL, 2 * D_FF), D_MODEL),
        "ffn2_w_down": wn(ks[7], (DEPTH, D_FF, D_MODEL), D_FF),
        "sb_w_qkv": wn(ks[8], (N_SB_LAYERS, D_MODEL, 3 * D_MODEL), D_MODEL),
        "sb_w_o": wn(ks[9], (N_SB_LAYERS, D_MODEL, D_MODEL), D_MODEL),
        "ssd_w_in": wn(ks[10], (N_SSD_LAYERS, D_MODEL, SSD_IN_DIM), D_MODEL),
        "ssd_conv_w": wn(ks[11], (N_SSD_LAYERS, SSD_CONV, SSD_CONV_DIM), SSD_CONV),
        "ssd_conv_b": 0.01 * jax.random.normal(ks[12], (N_SSD_LAYERS, SSD_CONV_DIM), f32),
        "ssd_dt_bias": dt_bias,
        "ssd_a_log": jnp.log(jax.random.uniform(ks[15], (N_SSD_LAYERS, SSD_HEADS), f32, 1.0, 16.0)),
        "ssd_d": gain(ks[16], (N_SSD_LAYERS, SSD_HEADS)),
        "ssd_norm": gain(ks[17], (N_SSD_LAYERS, SSD_D_INNER)),
        "ssd_w_out": wn(ks[18], (N_SSD_LAYERS, SSD_D_INNER, D_MODEL), SSD_D_INNER),
        "sc_w_in": wn(ks[19], (N_SC_LAYERS, D_MODEL, 3 * D_MODEL), D_MODEL),
        "sc_conv_w": wn(ks[20], (N_SC_LAYERS, SC_WIDTH, D_MODEL), SC_WIDTH),
        "sc_w_out": wn(ks[21], (N_SC_LAYERS, D_MODEL, D_MODEL), D_MODEL),
        "final_norm": gain(ks[22], (D_MODEL,)),
    }


def reference(x, ffn1_norm, ffn1_w_gu, ffn1_w_down, mix_norm, ffn2_norm, ffn2_w_gu, ffn2_w_down,
              sb_w_qkv, sb_w_o, ssd_w_in, ssd_conv_w, ssd_conv_b, ssd_dt_bias, ssd_a_log, ssd_d,
              ssd_norm, ssd_w_out, sc_w_in, sc_conv_w, sc_w_out, final_norm):
    for i in range(DEPTH):
        x = x + 0.5 * swiglu(rmsnorm(x, ffn1_norm[i]), ffn1_w_gu[i], ffn1_w_down[i])
        h = rmsnorm(x, mix_norm[i])
        kind, j = i % N_MIXERS, i // N_MIXERS
        if kind == 0:
            m = stick_breaking_attention(h, sb_w_qkv[j], sb_w_o[j])
        elif kind == 1:
            m = ssd_mixer(h, ssd_w_in[j], ssd_conv_w[j], ssd_conv_b[j], ssd_dt_bias[j],
                          ssd_a_log[j], ssd_d[j], ssd_norm[j], ssd_w_out[j])
        else:
            m = short_conv_mixer(h, sc_w_in[j], sc_conv_w[j], sc_w_out[j])
        x = x + m
        x = x + 0.5 * swiglu(rmsnorm(x, ffn2_norm[i]), ffn2_w_gu[i], ffn2_w_down[i])
    return rmsnorm(x, final_norm)
```

```python
import functools

import jax
import jax.numpy as jnp
from jax import lax
from jax.experimental import pallas as pl
from jax.experimental.pallas import tpu as pltpu

F32 = jnp.float32
BF16 = jnp.bfloat16

RMS_EPS = 1e-6
N_MIXERS = 3
SB_HEAD_DIM = 64
SSD_HEAD_DIM = 64
SSD_GROUPS = 8
SSD_HPG = 4
SSD_STATE = 128
SSD_CONV = 4
SSD_CHUNK = 128
SC_WIDTH = 3

LANES = 128
SUBLANES = 8
VMEM_LIMIT_BYTES = 56 * 1024 * 1024

FFN_TM = 512
FFN_TF = 256
PROJ_TM = 512
PROJ_TN = 512
ATT_T = 256
SC_TM = 256
HALO = SUBLANES


def _params(n_axes):
    return pltpu.CompilerParams(
        dimension_semantics=("arbitrary",) * n_axes,
        vmem_limit_bytes=VMEM_LIMIT_BYTES)


def _resident(shape):
    zeros = (0,) * len(shape)
    return pl.BlockSpec(shape, lambda *_: zeros, pipeline_mode=pl.Buffered(1))


def _rms(x, g):
    ms = jnp.mean(x * x, axis=-1, keepdims=True)
    return x * lax.rsqrt(ms + RMS_EPS) * g


def _silu(x):
    return x * jax.nn.sigmoid(x)


def _dot(a, b):
    return jnp.dot(a, b, preferred_element_type=F32)


def _dot_nt(a, b):
    return lax.dot_general(a, b, (((1,), (1,)), ((), ())), preferred_element_type=F32)


def _dot_tn(a, b):
    return lax.dot_general(a, b, (((0,), (0,)), ((), ())), preferred_element_type=F32)


def _ffn_kernel(x_ref, g_ref, wgu_ref, wd_ref, *refs, n_chunks, tf, final_norm):
    fg_ref = refs[0] if final_norm else None
    o_ref, xn_ref, h_ref = refs[-3:]
    x = x_ref[...]
    xn_ref[...] = _rms(x, g_ref[...]).astype(BF16)
    for c in range(n_chunks):
        gu = _dot(xn_ref[...], wgu_ref[c])
        h_ref[:, c * tf:(c + 1) * tf] = (_silu(gu[:, :tf]) * gu[:, tf:]).astype(BF16)
    y = x + 0.5 * _dot(h_ref[...], wd_ref[...])
    if final_norm:
        y = _rms(y, fg_ref[...])
    o_ref[...] = y


def _ffn(x, norm_g, w_gu, w_down, final_g=None):
    t, d = x.shape
    d_ff = w_down.shape[0]
    tm, tf = FFN_TM, FFN_TF
    n_chunks = d_ff // tf
    assert t % tm == 0 and d_ff % tf == 0
    wgu = w_gu.astype(BF16).reshape(d, 2, n_chunks, tf).transpose(2, 0, 1, 3).reshape(
        n_chunks, d, 2 * tf)
    wd = w_down.astype(BF16)
    final_norm = final_g is not None
    extra = [final_g.reshape(1, d)] if final_norm else []
    return pl.pallas_call(
        functools.partial(_ffn_kernel, n_chunks=n_chunks, tf=tf, final_norm=final_norm),
        grid=(t // tm,),
        in_specs=[
            pl.BlockSpec((tm, d), lambda i: (i, 0)),
            _resident((1, d)),
            _resident((n_chunks, d, 2 * tf)),
            _resident((d_ff, d)),
        ] + [_resident((1, d))] * len(extra),
        out_specs=pl.BlockSpec((tm, d), lambda i: (i, 0)),
        out_shape=jax.ShapeDtypeStruct((t, d), F32),
        scratch_shapes=[pltpu.VMEM((tm, d), BF16), pltpu.VMEM((tm, d_ff), BF16)],
        compiler_params=_params(1),
        name="ffn",
    )(x, norm_g.reshape(1, d), wgu, wd, *extra)


def _norm_proj_kernel(x_ref, g_ref, *refs, n_out, tn):
    w_refs, o_refs = refs[:n_out], refs[n_out:]
    xn = _rms(x_ref[...], g_ref[...]).astype(BF16)
    for w_ref, o_ref in zip(w_refs, o_refs):
        n = w_ref.shape[1]
        for n0 in range(0, n, tn):
            n1 = min(n0 + tn, n)
            o_ref[:, n0:n1] = _dot(xn, w_ref[:, n0:n1]).astype(o_ref.dtype)


def _norm_proj(x, norm_g, weights, out_dtypes):
    t, d = x.shape
    tm = PROJ_TM
    assert t % tm == 0
    n_out = len(weights)
    return pl.pallas_call(
        functools.partial(_norm_proj_kernel, n_out=n_out, tn=PROJ_TN),
        grid=(t // tm,),
        in_specs=[pl.BlockSpec((tm, d), lambda i: (i, 0)), _resident((1, d))]
        + [_resident(w.shape) for w in weights],
        out_specs=[pl.BlockSpec((tm, w.shape[1]), lambda i: (i, 0)) for w in weights],
        out_shape=[jax.ShapeDtypeStruct((t, w.shape[1]), dt) for w, dt in zip(weights, out_dtypes)],
        compiler_params=_params(1),
        name="norm_proj",
    )(x, norm_g.reshape(1, d), *weights)


def _proj_res_kernel(x_ref, y_ref, w_ref, o_ref):
    o_ref[...] = x_ref[...] + _dot(y_ref[...], w_ref[...])


def _proj_res(x, y, w):
    t, d = x.shape
    k = y.shape[1]
    tm = PROJ_TM
    assert t % tm == 0
    return pl.pallas_call(
        _proj_res_kernel,
        grid=(t // tm,),
        in_specs=[
            pl.BlockSpec((tm, d), lambda i: (i, 0)),
            pl.BlockSpec((tm, k), lambda i: (i, 0)),
            _resident((k, d)),
        ],
        out_specs=pl.BlockSpec((tm, d), lambda i: (i, 0)),
        out_shape=jax.ShapeDtypeStruct((t, d), F32),
        compiler_params=_params(1),
        name="proj_res",
    )(x, y, w)


def _sb_attn_kernel(q_ref, k_ref, v_ref, o_ref, *, t):
    qi = pl.program_id(2)
    hd = SB_HEAD_DIM
    q2 = q_ref[0]
    in_a = lax.broadcasted_iota(jnp.int32, (1, 2 * hd), 1) < hd
    q_heads = (jnp.where(in_a, q2, 0), jnp.where(in_a, 0, q2))
    row = lax.broadcasted_iota(jnp.int32, (t, t), 0)
    col = lax.broadcasted_iota(jnp.int32, (t, t), 1)
    strict_lower = col < row
    tri = jnp.where(row > col, 1.0, 0.0).astype(BF16)
    tri2 = jnp.concatenate([tri, tri], axis=0)

    def key_tile(j, carry, masked):
        acc, carries = carry[0], carry[1:]
        ks = pl.multiple_of(j * t, t)
        k2 = k_ref[0, pl.ds(ks, t), :]
        v2 = v_ref[0, pl.ds(ks, t), :]
        atts, new_carries = [], []
        for qh, c in zip(q_heads, carries):
            z = _dot_nt(qh, k2)
            l = jnp.log(1.0 + jnp.exp(-jnp.abs(z)))
            sp = jnp.maximum(z, 0.0) + l
            lb = jnp.minimum(z, 0.0) - l
            if masked:
                sp = jnp.where(strict_lower, sp, 0.0)
            hi = sp.astype(BF16)
            lo = (sp - hi.astype(F32)).astype(BF16)
            tail = _dot(jnp.concatenate([hi, lo], axis=1), tri2)
            att = jnp.exp(lb - tail - c)
            if masked:
                att = jnp.where(strict_lower, att, 0.0)
            atts.append(att.astype(BF16))
            new_carries.append(c + jnp.sum(sp, axis=-1, keepdims=True))
        vv = jnp.concatenate([jnp.where(in_a, v2, 0), jnp.where(in_a, 0, v2)], axis=0)
        acc = acc + _dot(jnp.concatenate(atts, axis=1), vv)
        return (acc, *new_carries)

    zero_c = jnp.zeros((t, 1), F32)
    carry = key_tile(qi, (jnp.zeros((t, 2 * hd), F32), zero_c, zero_c), True)
    carry = lax.fori_loop(0, qi, lambda i, cr: key_tile(qi - 1 - i, cr, False), carry)
    o_ref[0] = carry[0].astype(o_ref.dtype)


def _sb_attention(q, k, v):
    b, l, d = q.shape
    t = ATT_T
    hp = 2 * SB_HEAD_DIM
    assert l % t == 0 and d % hp == 0 and hp == LANES
    return pl.pallas_call(
        functools.partial(_sb_attn_kernel, t=t),
        grid=(b, d // hp, l // t),
        in_specs=[
            pl.BlockSpec((1, t, hp), lambda bi, hi, qi: (bi, qi, hi)),
            pl.BlockSpec((1, l, hp), lambda bi, hi, qi: (bi, 0, hi)),
            pl.BlockSpec((1, l, hp), lambda bi, hi, qi: (bi, 0, hi)),
        ],
        out_specs=pl.BlockSpec((1, t, hp), lambda bi, hi, qi: (bi, qi, hi)),
        out_shape=jax.ShapeDtypeStruct((b, l, d), BF16),
        compiler_params=_params(3),
        name="sb_attn",
    )(q, k, v)


def _sb_layer(x, b, l, norm_g, w_qkv, w_o):
    t, d = x.shape
    scale = SB_HEAD_DIM ** -0.5
    wq = (w_qkv[:, :d] * scale).astype(BF16)
    wk = w_qkv[:, d:2 * d].astype(BF16)
    wv = w_qkv[:, 2 * d:].astype(BF16)
    q, k, v = _norm_proj(x, norm_g, [wq, wk, wv], [BF16] * 3)
    o = _sb_attention(q.reshape(b, l, d), k.reshape(b, l, d), v.reshape(b, l, d))
    return _proj_res(x, o.reshape(t, d), w_o.astype(BF16))


def _expand_heads(m, width):
    rows = m.shape[0]
    lane = lax.broadcasted_iota(jnp.int32, (rows, SSD_HPG * width), 1)
    out = jnp.broadcast_to(m[:, SSD_HPG - 1:SSD_HPG], (rows, SSD_HPG * width))
    for r in range(SSD_HPG - 2, -1, -1):
        out = jnp.where(lane < (r + 1) * width, m[:, r:r + 1], out)
    return out


def _to_columns(rows8):
    n = rows8.shape[1]
    padded = jnp.concatenate([rows8, jnp.zeros((n - rows8.shape[0], n), F32)], axis=0)
    return padded.T[:, :rows8.shape[0]]


def _ssd_kernel(xr_ref, br_ref, cr_ref, z_ref, dt_ref, wx_ref, wb_ref, wc_ref, bx_ref, bb_ref,
                bc_ref, dtb_ref, alog_ref, dsk_ref, ng_ref, o_ref, h_ref, ext_ref):
    c = pl.program_id(2)
    cl, hp = SSD_CHUNK, SSD_HPG * SSD_HEAD_DIM

    @pl.when(c == 0)
    def _():
        h_ref[...] = jnp.zeros_like(h_ref)
        ext_ref[0:HALO, :] = jnp.zeros((HALO, ext_ref.shape[1]), F32)

    ext_ref[HALO:HALO + cl, :] = jnp.concatenate([xr_ref[...], br_ref[...], cr_ref[...]], axis=1)
    cw = jnp.concatenate([wx_ref[...], wb_ref[...], wc_ref[...]], axis=1)
    conv = jnp.concatenate([bx_ref[...], bb_ref[...], bc_ref[...]], axis=1)
    for kk in range(SSD_CONV):
        conv = conv + cw[kk:kk + 1, :] * ext_ref[pl.ds(HALO - (SSD_CONV - 1) + kk, cl), :]
    ext_ref[0:HALO, :] = ext_ref[cl:cl + HALO, :]
    act = _silu(conv)
    xs = act[:, :hp]
    bm = act[:, hp:hp + SSD_STATE].astype(BF16)
    cm = act[:, hp + SSD_STATE:].astype(BF16)

    dt_in = dt_ref[0, 0] + dtb_ref[0]
    dt = jnp.maximum(dt_in, 0.0) + jnp.log(1.0 + jnp.exp(-jnp.abs(dt_in)))
    a_cum = dt * (-jnp.exp(alog_ref[0]))
    lane = lax.broadcasted_iota(jnp.int32, a_cum.shape, 1)
    sh = 1
    while sh < cl:
        a_cum = a_cum + jnp.where(lane >= sh, pltpu.roll(a_cum, sh, axis=1), 0.0)
        sh *= 2
    a_last = a_cum[:, cl - 1:cl]
    to_end = jnp.exp(a_last - a_cum) * dt
    a_col = _to_columns(a_cum)
    te_col = _to_columns(to_end)

    cb = _dot_nt(cm, bm)
    row = lax.broadcasted_iota(jnp.int32, (cl, cl), 0)
    col = lax.broadcasted_iota(jnp.int32, (cl, cl), 1)
    causal = col <= row
    lane_hp = lax.broadcasted_iota(jnp.int32, (cl, hp), 1)
    ws, xblocks = [], []
    for r in range(SSD_HPG):
        seg = a_col[:, r:r + 1] - a_cum[r:r + 1, :]
        decay = jnp.exp(jnp.where(causal, seg, -jnp.inf))
        ws.append((cb * decay * dt[r:r + 1, :]).astype(BF16))
        xblocks.append(jnp.where((lane_hp >= r * SSD_HEAD_DIM) & (lane_hp < (r + 1) * SSD_HEAD_DIM), xs, 0.0).astype(BF16))
    y = _dot(jnp.concatenate(ws, axis=1), jnp.concatenate(xblocks, axis=0))

    h_prev = h_ref[...]
    y = y + _dot(cm, h_prev.astype(BF16)) * _expand_heads(jnp.exp(a_col), SSD_HEAD_DIM)
    xw = (xs * _expand_heads(te_col, SSD_HEAD_DIM)).astype(BF16)
    h_decay = _expand_heads(jnp.exp(a_col[cl - 1:cl, :]), SSD_HEAD_DIM)
    h_ref[...] = h_prev * h_decay + _dot_tn(bm, xw)

    y = (y + dsk_ref[0] * xs) * _silu(z_ref[...])
    o_ref[...] = _rms(y, ng_ref[0]).astype(o_ref.dtype)


def _ssd_core(xbc, z, dt_t, conv_w, conv_b, dt_bias, a_log, d_skip, norm_g, b, l):
    t = xbc.shape[0]
    g, hpg, cl, n = SSD_GROUPS, SSD_HPG, SSD_CHUNK, SSD_STATE
    hp = hpg * SSD_HEAD_DIM
    d_inner = g * hp
    nc = l // cl
    assert l % cl == 0 and hp % LANES == 0 and n == LANES
    bb, cb = d_inner // n, d_inner // n + g
    pad = lambda p: jnp.pad(p.reshape(g, hpg, 1), ((0, 0), (0, SUBLANES - hpg), (0, 0)))
    per_lane = lambda p: jnp.repeat(p.reshape(g, hpg), SSD_HEAD_DIM, axis=1).reshape(g, 1, hp)
    conv_b = conv_b.reshape(1, -1)
    tok = lambda bi, gi, ci: bi * nc + ci
    return pl.pallas_call(
        _ssd_kernel,
        grid=(b, g, nc),
        in_specs=[
            pl.BlockSpec((cl, hp), lambda bi, gi, ci: (tok(bi, gi, ci), gi)),
            pl.BlockSpec((cl, n), lambda bi, gi, ci: (tok(bi, gi, ci), bb + gi)),
            pl.BlockSpec((cl, n), lambda bi, gi, ci: (tok(bi, gi, ci), cb + gi)),
            pl.BlockSpec((cl, hp), lambda bi, gi, ci: (tok(bi, gi, ci), gi)),
            pl.BlockSpec((1, 1, SUBLANES, cl), lambda bi, gi, ci: (bi, gi, 0, ci)),
            pl.BlockSpec((SSD_CONV, hp), lambda bi, gi, ci: (0, gi)),
            pl.BlockSpec((SSD_CONV, n), lambda bi, gi, ci: (0, bb + gi)),
            pl.BlockSpec((SSD_CONV, n), lambda bi, gi, ci: (0, cb + gi)),
            pl.BlockSpec((1, hp), lambda bi, gi, ci: (0, gi)),
            pl.BlockSpec((1, n), lambda bi, gi, ci: (0, bb + gi)),
            pl.BlockSpec((1, n), lambda bi, gi, ci: (0, cb + gi)),
            pl.BlockSpec((1, SUBLANES, 1), lambda bi, gi, ci: (gi, 0, 0)),
            pl.BlockSpec((1, SUBLANES, 1), lambda bi, gi, ci: (gi, 0, 0)),
            pl.BlockSpec((1, 1, hp), lambda bi, gi, ci: (gi, 0, 0)),
            pl.BlockSpec((1, 1, hp), lambda bi, gi, ci: (gi, 0, 0)),
        ],
        out_specs=pl.BlockSpec((cl, hp), lambda bi, gi, ci: (tok(bi, gi, ci), gi)),
        out_shape=jax.ShapeDtypeStruct((t, d_inner), BF16),
        scratch_shapes=[pltpu.VMEM((n, hp), F32), pltpu.VMEM((cl + HALO, hp + 2 * n), F32)],
        compiler_params=_params(3),
        name="ssd_core",
    )(xbc, xbc, xbc, z, dt_t, conv_w, conv_w, conv_w, conv_b, conv_b, conv_b,
      pad(dt_bias), pad(a_log), per_lane(d_skip), norm_g.reshape(g, 1, hp))


def _ssd_layer(x, b, l, norm_g, w_in, conv_w, conv_b, dt_bias, a_log, d_skip, ssd_norm, w_out):
    g, hpg = SSD_GROUPS, SSD_HPG
    d_inner = w_out.shape[0]
    conv_dim = conv_w.shape[1]
    n_heads = g * hpg
    w_in = w_in.astype(BF16)
    w_z = w_in[:, :d_inner]
    w_xbc = w_in[:, d_inner:d_inner + conv_dim]
    w_dt = jnp.pad(w_in[:, d_inner + conv_dim:], ((0, 0), (0, LANES - n_heads)))
    z, xbc, dt = _norm_proj(x, norm_g, [w_z, w_xbc, w_dt], [F32] * 3)
    dt_t = dt[:, :n_heads].reshape(b, l, g, hpg).transpose(0, 2, 3, 1)
    dt_t = jnp.pad(dt_t, ((0, 0), (0, 0), (0, SUBLANES - hpg), (0, 0)))
    y = _ssd_core(xbc, z, dt_t, conv_w, conv_b, dt_bias, a_log, d_skip, ssd_norm, b, l)
    return _proj_res(x, y, w_out.astype(BF16))


def _short_conv_kernel(x_ref, g_ref, wb_ref, wc_ref, wh_ref, cw_ref, wo_ref, o_ref, ext_ref):
    tm = x_ref.shape[0]

    @pl.when(pl.program_id(1) == 0)
    def _():
        ext_ref[0:HALO, :] = jnp.zeros((HALO, ext_ref.shape[1]), F32)

    x = x_ref[...]
    xn = _rms(x, g_ref[...]).astype(BF16)
    ext_ref[HALO:HALO + tm, :] = _dot(xn, wc_ref[...]) * _dot(xn, wh_ref[...])
    cw = cw_ref[...]
    u = cw[0:1, :] * ext_ref[pl.ds(HALO - (SC_WIDTH - 1), tm), :]
    for kk in range(1, SC_WIDTH):
        u = u + cw[kk:kk + 1, :] * ext_ref[pl.ds(HALO - (SC_WIDTH - 1) + kk, tm), :]
    ext_ref[0:HALO, :] = ext_ref[tm:tm + HALO, :]
    gated = (_dot(xn, wb_ref[...]) * u).astype(BF16)
    o_ref[...] = x + _dot(gated, wo_ref[...])


def _short_conv_layer(x, b, l, norm_g, w_in, conv_w, w_out):
    t, d = x.shape
    tm = SC_TM
    nl = l // tm
    assert l % tm == 0
    w_in = w_in.astype(BF16)
    return pl.pallas_call(
        _short_conv_kernel,
        grid=(b, nl),
        in_specs=[
            pl.BlockSpec((tm, d), lambda bi, li: (bi * nl + li, 0)),
            _resident((1, d)),
            _resident((d, d)), _resident((d, d)), _resident((d, d)),
            _resident((SC_WIDTH, d)),
            _resident((d, d)),
        ],
        out_specs=pl.BlockSpec((tm, d), lambda bi, li: (bi * nl + li, 0)),
        out_shape=jax.ShapeDtypeStruct((t, d), F32),
        scratch_shapes=[pltpu.VMEM((tm + HALO, d), F32)],
        compiler_params=_params(2),
        name="short_conv",
    )(x, norm_g.reshape(1, d), w_in[:, :d], w_in[:, d:2 * d], w_in[:, 2 * d:], conv_w,
      w_out.astype(BF16))


def kernel(x, ffn1_norm, ffn1_w_gu, ffn1_w_down, mix_norm, ffn2_norm, ffn2_w_gu, ffn2_w_down,
           sb_w_qkv, sb_w_o, ssd_w_in, ssd_conv_w, ssd_conv_b, ssd_dt_bias, ssd_a_log, ssd_d,
           ssd_norm, ssd_w_out, sc_w_in, sc_conv_w, sc_w_out, final_norm):
    b, l, d = x.shape
    depth = ffn1_norm.shape[0]
    h = x.reshape(b * l, d)
    for i in range(depth):
        h = _ffn(h, ffn1_norm[i], ffn1_w_gu[i], ffn1_w_down[i])
        kind, j = i % N_MIXERS, i // N_MIXERS
        if kind == 0:
            h = _sb_layer(h, b, l, mix_norm[i], sb_w_qkv[j], sb_w_o[j])
        elif kind == 1:
            h = _ssd_layer(h, b, l, mix_norm[i], ssd_w_in[j], ssd_conv_w[j], ssd_conv_b[j],
                           ssd_dt_bias[j], ssd_a_log[j], ssd_d[j], ssd_norm[j], ssd_w_out[j])
        else:
            h = _short_conv_layer(h, b, l, mix_norm[i], sc_w_in[j], sc_conv_w[j], sc_w_out[j])
        h = _ffn(h, ffn2_norm[i], ffn2_w_gu[i], ffn2_w_down[i],
                 final_g=final_norm if i == depth - 1 else None)
    return h.reshape(b, l, d)
```

```python
import functools

import jax
import jax.numpy as jnp
from jax import lax
from jax.experimental import pallas as pl
from jax.experimental.pallas import tpu as pltpu

F32 = jnp.float32
BF16 = jnp.bfloat16

RMS_EPS = 1e-6
LOG2_E = 1.4426950408889634
N_MIXERS = 3
SB_HEAD_DIM = 64
SSD_HEAD_DIM = 64
SSD_GROUPS = 8
SSD_HPG = 4
SSD_STATE = 128
SSD_CONV = 4
SSD_CHUNK = 128
SC_WIDTH = 3

LANES = 128
SUBLANES = 8
VMEM_LIMIT_BYTES = 56 * 1024 * 1024

FFN_TM = 512
FFN_TF = 256
PROJ_TM = 512
PROJ_TN = 512
ATT_T = 256
ATT_SUB = 2
SC_TM = 256
HALO = SUBLANES


def _params(n_axes):
    return pltpu.CompilerParams(
        dimension_semantics=("arbitrary",) * n_axes,
        vmem_limit_bytes=VMEM_LIMIT_BYTES)


def _resident(shape):
    zeros = (0,) * len(shape)
    return pl.BlockSpec(shape, lambda *_: zeros, pipeline_mode=pl.Buffered(1))


def _rms(x, g):
    ms = jnp.mean(x * x, axis=-1, keepdims=True)
    return x * lax.rsqrt(ms + RMS_EPS) * g


def _silu(x):
    return x * jax.nn.sigmoid(x)


def _dot(a, b):
    return jnp.dot(a, b, preferred_element_type=F32)


def _dot_nt(a, b):
    return lax.dot_general(a, b, (((1,), (1,)), ((), ())), preferred_element_type=F32)


def _dot_tn(a, b):
    return lax.dot_general(a, b, (((0,), (0,)), ((), ())), preferred_element_type=F32)


def _ffn_kernel(x_ref, g_ref, wgu_ref, wd_ref, *refs, n_chunks, tf, final_norm):
    fg_ref = refs[0] if final_norm else None
    o_ref, xn_ref, h_ref = refs[-3:]
    x = x_ref[...]
    d_ff = n_chunks * tf
    xn_ref[...] = _rms(x, g_ref[...]).astype(BF16)
    for c in range(n_chunks):
        gate = _dot(xn_ref[...], wgu_ref[:, c * tf:(c + 1) * tf])
        up = _dot(xn_ref[...], wgu_ref[:, d_ff + c * tf:d_ff + (c + 1) * tf])
        h_ref[:, c * tf:(c + 1) * tf] = (_silu(gate) * up).astype(BF16)
    y = x + 0.5 * _dot(h_ref[...], wd_ref[...])
    if final_norm:
        y = _rms(y, fg_ref[...])
    o_ref[...] = y


def _ffn(x, norm_g, w_gu, w_down, final_g=None):
    t, d = x.shape
    d_ff = w_down.shape[0]
    tm, tf = FFN_TM, FFN_TF
    n_chunks = d_ff // tf
    assert t % tm == 0 and d_ff % tf == 0
    wgu = w_gu.astype(BF16)
    wd = w_down.astype(BF16)
    final_norm = final_g is not None
    extra = [final_g.reshape(1, d)] if final_norm else []
    return pl.pallas_call(
        functools.partial(_ffn_kernel, n_chunks=n_chunks, tf=tf, final_norm=final_norm),
        grid=(t // tm,),
        in_specs=[
            pl.BlockSpec((tm, d), lambda i: (i, 0)),
            _resident((1, d)),
            _resident((d, 2 * d_ff)),
            _resident((d_ff, d)),
        ] + [_resident((1, d))] * len(extra),
        out_specs=pl.BlockSpec((tm, d), lambda i: (i, 0)),
        out_shape=jax.ShapeDtypeStruct((t, d), F32),
        scratch_shapes=[pltpu.VMEM((tm, d), BF16), pltpu.VMEM((tm, d_ff), BF16)],
        compiler_params=_params(1),
        name="ffn",
    )(x, norm_g.reshape(1, d), wgu, wd, *extra)


def _norm_proj_kernel(x_ref, g_ref, *refs, n_out, tn):
    w_refs, o_refs = refs[:n_out], refs[n_out:]
    xn = _rms(x_ref[...], g_ref[...]).astype(BF16)
    for w_ref, o_ref in zip(w_refs, o_refs):
        n = w_ref.shape[1]
        for n0 in range(0, n, tn):
            n1 = min(n0 + tn, n)
            o_ref[:, n0:n1] = _dot(xn, w_ref[:, n0:n1]).astype(o_ref.dtype)


def _norm_proj(x, norm_g, weights, out_dtypes):
    t, d = x.shape
    tm = PROJ_TM
    assert t % tm == 0
    n_out = len(weights)
    return pl.pallas_call(
        functools.partial(_norm_proj_kernel, n_out=n_out, tn=PROJ_TN),
        grid=(t // tm,),
        in_specs=[pl.BlockSpec((tm, d), lambda i: (i, 0)), _resident((1, d))]
        + [_resident(w.shape) for w in weights],
        out_specs=[pl.BlockSpec((tm, w.shape[1]), lambda i: (i, 0)) for w in weights],
        out_shape=[jax.ShapeDtypeStruct((t, w.shape[1]), dt) for w, dt in zip(weights, out_dtypes)],
        compiler_params=_params(1),
        name="norm_proj",
    )(x, norm_g.reshape(1, d), *weights)


def _proj_res_kernel(x_ref, y_ref, w_ref, o_ref):
    o_ref[...] = x_ref[...] + _dot(y_ref[...], w_ref[...])


def _proj_res(x, y, w):
    t, d = x.shape
    k = y.shape[1]
    tm = PROJ_TM
    assert t % tm == 0
    return pl.pallas_call(
        _proj_res_kernel,
        grid=(t // tm,),
        in_specs=[
            pl.BlockSpec((tm, d), lambda i: (i, 0)),
            pl.BlockSpec((tm, k), lambda i: (i, 0)),
            _resident((k, d)),
        ],
        out_specs=pl.BlockSpec((tm, d), lambda i: (i, 0)),
        out_shape=jax.ShapeDtypeStruct((t, d), F32),
        compiler_params=_params(1),
        name="proj_res",
    )(x, y, w)


def _sb_attn_kernel(q_ref, k_ref, v_ref, o_ref, *, t, n_sub):
    qi = pl.program_id(2)
    hd = SB_HEAD_DIM
    in_a = lax.broadcasted_iota(jnp.int32, (1, 2 * hd), 1) < hd
    q_parts = []
    for s in range(n_sub):
        q2 = q_ref[0, s * t:(s + 1) * t, :]
        q_parts += [jnp.where(in_a, q2, 0), jnp.where(in_a, 0, q2)]
    qs = jnp.concatenate(q_parts, axis=0)
    row = lax.broadcasted_iota(jnp.int32, (t, t), 0)
    col = lax.broadcasted_iota(jnp.int32, (t, t), 1)
    strict_lower = jnp.concatenate([col < row] * 2, axis=0)
    tri = jnp.where(row > col, 1.0, 0.0).astype(BF16)
    sign_bit = jnp.uint32(0x80000000)

    def mask_diagonal(x):
        head = jnp.where(strict_lower, x[:2 * t], 0.0)
        return head if x.shape[0] == 2 * t else jnp.concatenate([head, x[2 * t:]], axis=0)

    def key_tile(j, carry, s0, diag):
        acc, c = carry
        r0 = 2 * s0 * t
        ks = pl.multiple_of(j * t, t)
        k2 = k_ref[0, pl.ds(ks, t), :]
        v2 = v_ref[0, pl.ds(ks, t), :]
        z = _dot_nt(qs[r0:], k2)
        neg_abs = lax.bitcast_convert_type(lax.bitcast_convert_type(z, jnp.uint32) | sign_bit, F32)
        lb = jnp.minimum(z, 0.0) - jnp.log2(1.0 + jnp.exp2(neg_abs))
        sp = z - lb
        if diag:
            sp = mask_diagonal(sp)
        tail = _dot(sp.astype(BF16), tri)
        att = jnp.exp2(lb - tail - c[r0:])
        if diag:
            att = mask_diagonal(att)
        att = att.astype(BF16)
        att2 = jnp.concatenate(
            [jnp.concatenate([att[(2 * i) * t:(2 * i + 1) * t], att[(2 * i + 1) * t:(2 * i + 2) * t]],
                             axis=1) for i in range(n_sub - s0)], axis=0)
        vv = jnp.concatenate([jnp.where(in_a, v2, 0), jnp.where(in_a, 0, v2)], axis=0)
        acc_new = acc[s0 * t:] + _dot(att2, vv)
        c_new = c[r0:] + jnp.sum(sp, axis=-1, keepdims=True)
        if s0:
            acc_new = jnp.concatenate([acc[:s0 * t], acc_new], axis=0)
            c_new = jnp.concatenate([c[:r0], c_new], axis=0)
        return acc_new, c_new

    carry = (jnp.zeros((n_sub * t, 2 * hd), F32), jnp.zeros((2 * n_sub * t, 1), F32))
    for s0 in range(n_sub - 1, -1, -1):
        carry = key_tile(n_sub * qi + s0, carry, s0, True)
    n_full = n_sub * qi
    carry = lax.fori_loop(0, n_full, lambda i, cr: key_tile(n_full - 1 - i, cr, 0, False), carry)
    o_ref[0] = carry[0].astype(o_ref.dtype)


def _sb_attention(q, k, v):
    b, l, d = q.shape
    t, n_sub = ATT_T, ATT_SUB
    tq = t * n_sub
    hp = 2 * SB_HEAD_DIM
    assert l % tq == 0 and d % hp == 0 and hp == LANES
    return pl.pallas_call(
        functools.partial(_sb_attn_kernel, t=t, n_sub=n_sub),
        grid=(b, d // hp, l // tq),
        in_specs=[
            pl.BlockSpec((1, tq, hp), lambda bi, hi, qi: (bi, qi, hi)),
            pl.BlockSpec((1, l, hp), lambda bi, hi, qi: (bi, 0, hi)),
            pl.BlockSpec((1, l, hp), lambda bi, hi, qi: (bi, 0, hi)),
        ],
        out_specs=pl.BlockSpec((1, tq, hp), lambda bi, hi, qi: (bi, qi, hi)),
        out_shape=jax.ShapeDtypeStruct((b, l, d), BF16),
        compiler_params=_params(3),
        name="sb_attn",
    )(q, k, v)


def _sb_layer(x, b, l, norm_g, w_qkv, w_o):
    t, d = x.shape
    scale = LOG2_E * SB_HEAD_DIM ** -0.5
    wq = (w_qkv[:, :d] * scale).astype(BF16)
    wk = w_qkv[:, d:2 * d].astype(BF16)
    wv = w_qkv[:, 2 * d:].astype(BF16)
    q, k, v = _norm_proj(x, norm_g, [wq, wk, wv], [BF16] * 3)
    o = _sb_attention(q.reshape(b, l, d), k.reshape(b, l, d), v.reshape(b, l, d))
    return _proj_res(x, o.reshape(t, d), w_o.astype(BF16))


def _expand_heads(m, width):
    rows = m.shape[0]
    lane = lax.broadcasted_iota(jnp.int32, (rows, SSD_HPG * width), 1)
    out = jnp.broadcast_to(m[:, SSD_HPG - 1:SSD_HPG], (rows, SSD_HPG * width))
    for r in range(SSD_HPG - 2, -1, -1):
        out = jnp.where(lane < (r + 1) * width, m[:, r:r + 1], out)
    return out


def _to_columns(rows8):
    n = rows8.shape[1]
    padded = jnp.concatenate([rows8, jnp.zeros((n - rows8.shape[0], n), F32)], axis=0)
    return padded.T[:, :rows8.shape[0]]


def _ssd_kernel(xr_ref, br_ref, cr_ref, z_ref, dt_ref, wx_ref, wb_ref, wc_ref, bx_ref, bb_ref,
                bc_ref, dtb_ref, alog_ref, dsk_ref, ng_ref, o_ref, h_ref, ext_ref):
    c = pl.program_id(2)
    cl, hp = SSD_CHUNK, SSD_HPG * SSD_HEAD_DIM

    @pl.when(c == 0)
    def _():
        h_ref[...] = jnp.zeros_like(h_ref)
        ext_ref[0:HALO, :] = jnp.zeros((HALO, ext_ref.shape[1]), F32)

    ext_ref[HALO:HALO + cl, :] = jnp.concatenate([xr_ref[...], br_ref[...], cr_ref[...]], axis=1)
    cw = jnp.concatenate([wx_ref[...], wb_ref[...], wc_ref[...]], axis=1)
    conv = jnp.concatenate([bx_ref[...], bb_ref[...], bc_ref[...]], axis=1)
    for kk in range(SSD_CONV):
        conv = conv + cw[kk:kk + 1, :] * ext_ref[pl.ds(HALO - (SSD_CONV - 1) + kk, cl), :]
    ext_ref[0:HALO, :] = ext_ref[cl:cl + HALO, :]
    act = _silu(conv)
    xs = act[:, :hp]
    bm = act[:, hp:hp + SSD_STATE].astype(BF16)
    cm = act[:, hp + SSD_STATE:].astype(BF16)

    dt_in = dt_ref[0, 0] + dtb_ref[0]
    dt = jnp.maximum(dt_in, 0.0) + jnp.log(1.0 + jnp.exp(-jnp.abs(dt_in)))
    a_cum = dt * (-jnp.exp(alog_ref[0]))
    lane = lax.broadcasted_iota(jnp.int32, a_cum.shape, 1)
    sh = 1
    while sh < cl:
        a_cum = a_cum + jnp.where(lane >= sh, pltpu.roll(a_cum, sh, axis=1), 0.0)
        sh *= 2
    a_last = a_cum[:, cl - 1:cl]
    to_end = jnp.exp(a_last - a_cum) * dt
    a_col = _to_columns(a_cum)
    te_col = _to_columns(to_end)

    cb = _dot_nt(cm, bm)
    row = lax.broadcasted_iota(jnp.int32, (cl, cl), 0)
    col = lax.broadcasted_iota(jnp.int32, (cl, cl), 1)
    causal = col <= row
    lane_hp = lax.broadcasted_iota(jnp.int32, (cl, hp), 1)
    ws, xblocks = [], []
    for r in range(SSD_HPG):
        seg = a_col[:, r:r + 1] - a_cum[r:r + 1, :]
        decay = jnp.exp(jnp.where(causal, seg, -jnp.inf))
        ws.append((cb * decay * dt[r:r + 1, :]).astype(BF16))
        xblocks.append(jnp.where((lane_hp >= r * SSD_HEAD_DIM) & (lane_hp < (r + 1) * SSD_HEAD_DIM), xs, 0.0).astype(BF16))
    y = _dot(jnp.concatenate(ws, axis=1), jnp.concatenate(xblocks, axis=0))

    h_prev = h_ref[...]
    y = y + _dot(cm, h_prev.astype(BF16)) * _expand_heads(jnp.exp(a_col), SSD_HEAD_DIM)
    xw = (xs * _expand_heads(te_col, SSD_HEAD_DIM)).astype(BF16)
    h_decay = _expand_heads(jnp.exp(a_col[cl - 1:cl, :]), SSD_HEAD_DIM)
    h_ref[...] = h_prev * h_decay + _dot_tn(bm, xw)

    y = (y + dsk_ref[0] * xs) * _silu(z_ref[...])
    o_ref[...] = _rms(y, ng_ref[0]).astype(o_ref.dtype)


def _ssd_core(xbc, z, dt_t, conv_w, conv_b, dt_bias, a_log, d_skip, norm_g, b, l):
    t = xbc.shape[0]
    g, hpg, cl, n = SSD_GROUPS, SSD_HPG, SSD_CHUNK, SSD_STATE
    hp = hpg * SSD_HEAD_DIM
    d_inner = g * hp
    nc = l // cl
    assert l % cl == 0 and hp % LANES == 0 and n == LANES
    bb, cb = d_inner // n, d_inner // n + g
    pad = lambda p: jnp.pad(p.reshape(g, hpg, 1), ((0, 0), (0, SUBLANES - hpg), (0, 0)))
    per_lane = lambda p: jnp.repeat(p.reshape(g, hpg), SSD_HEAD_DIM, axis=1).reshape(g, 1, hp)
    conv_b = conv_b.reshape(1, -1)
    tok = lambda bi, gi, ci: bi * nc + ci
    return pl.pallas_call(
        _ssd_kernel,
        grid=(b, g, nc),
        in_specs=[
            pl.BlockSpec((cl, hp), lambda bi, gi, ci: (tok(bi, gi, ci), gi)),
            pl.BlockSpec((cl, n), lambda bi, gi, ci: (tok(bi, gi, ci), bb + gi)),
            pl.BlockSpec((cl, n), lambda bi, gi, ci: (tok(bi, gi, ci), cb + gi)),
            pl.BlockSpec((cl, hp), lambda bi, gi, ci: (tok(bi, gi, ci), gi)),
            pl.BlockSpec((1, 1, SUBLANES, cl), lambda bi, gi, ci: (bi, gi, 0, ci)),
            pl.BlockSpec((SSD_CONV, hp), lambda bi, gi, ci: (0, gi)),
            pl.BlockSpec((SSD_CONV, n), lambda bi, gi, ci: (0, bb + gi)),
            pl.BlockSpec((SSD_CONV, n), lambda bi, gi, ci: (0, cb + gi)),
            pl.BlockSpec((1, hp), lambda bi, gi, ci: (0, gi)),
            pl.BlockSpec((1, n), lambda bi, gi, ci: (0, bb + gi)),
            pl.BlockSpec((1, n), lambda bi, gi, ci: (0, cb + gi)),
            pl.BlockSpec((1, SUBLANES, 1), lambda bi, gi, ci: (gi, 0, 0)),
            pl.BlockSpec((1, SUBLANES, 1), lambda bi, gi, ci: (gi, 0, 0)),
            pl.BlockSpec((1, 1, hp), lambda bi, gi, ci: (gi, 0, 0)),
            pl.BlockSpec((1, 1, hp), lambda bi, gi, ci: (gi, 0, 0)),
        ],
        out_specs=pl.BlockSpec((cl, hp), lambda bi, gi, ci: (tok(bi, gi, ci), gi)),
        out_shape=jax.ShapeDtypeStruct((t, d_inner), BF16),
        scratch_shapes=[pltpu.VMEM((n, hp), F32), pltpu.VMEM((cl + HALO, hp + 2 * n), F32)],
        compiler_params=_params(3),
        name="ssd_core",
    )(xbc, xbc, xbc, z, dt_t, conv_w, conv_w, conv_w, conv_b, conv_b, conv_b,
      pad(dt_bias), pad(a_log), per_lane(d_skip), norm_g.reshape(g, 1, hp))


def _ssd_layer(x, b, l, norm_g, w_in, conv_w, conv_b, dt_bias, a_log, d_skip, ssd_norm, w_out):
    g, hpg = SSD_GROUPS, SSD_HPG
    d_inner = w_out.shape[0]
    conv_dim = conv_w.shape[1]
    n_heads = g * hpg
    w_in = w_in.astype(BF16)
    w_z = w_in[:, :d_inner]
    w_xbc = w_in[:, d_inner:d_inner + conv_dim]
    w_dt = jnp.pad(w_in[:, d_inner + conv_dim:], ((0, 0), (0, LANES - n_heads)))
    z, xbc, dt = _norm_proj(x, norm_g, [w_z, w_xbc, w_dt], [F32] * 3)
    dt_t = dt[:, :n_heads].reshape(b, l, g, hpg).transpose(0, 2, 3, 1)
    dt_t = jnp.pad(dt_t, ((0, 0), (0, 0), (0, SUBLANES - hpg), (0, 0)))
    y = _ssd_core(xbc, z, dt_t, conv_w, conv_b, dt_bias, a_log, d_skip, ssd_norm, b, l)
    return _proj_res(x, y, w_out.astype(BF16))


def _short_conv_kernel(x_ref, g_ref, wb_ref, wc_ref, wh_ref, cw_ref, wo_ref, o_ref, ext_ref):
    tm = x_ref.shape[0]

    @pl.when(pl.program_id(1) == 0)
    def _():
        ext_ref[0:HALO, :] = jnp.zeros((HALO, ext_ref.shape[1]), F32)

    x = x_ref[...]
    xn = _rms(x, g_ref[...]).astype(BF16)
    ext_ref[HALO:HALO + tm, :] = _dot(xn, wc_ref[...]) * _dot(xn, wh_ref[...])
    cw = cw_ref[...]
    u = cw[0:1, :] * ext_ref[pl.ds(HALO - (SC_WIDTH - 1), tm), :]
    for kk in range(1, SC_WIDTH):
        u = u + cw[kk:kk + 1, :] * ext_ref[pl.ds(HALO - (SC_WIDTH - 1) + kk, tm), :]
    ext_ref[0:HALO, :] = ext_ref[tm:tm + HALO, :]
    gated = (_dot(xn, wb_ref[...]) * u).astype(BF16)
    o_ref[...] = x + _dot(gated, wo_ref[...])


def _short_conv_layer(x, b, l, norm_g, w_in, conv_w, w_out):
    t, d = x.shape
    tm = SC_TM
    nl = l // tm
    assert l % tm == 0
    w_in = w_in.astype(BF16)
    return pl.pallas_call(
        _short_conv_kernel,
        grid=(b, nl),
        in_specs=[
            pl.BlockSpec((tm, d), lambda bi, li: (bi * nl + li, 0)),
            _resident((1, d)),
            _resident((d, d)), _resident((d, d)), _resident((d, d)),
            _resident((SC_WIDTH, d)),
            _resident((d, d)),
        ],
        out_specs=pl.BlockSpec((tm, d), lambda bi, li: (bi * nl + li, 0)),
        out_shape=jax.ShapeDtypeStruct((t, d), F32),
        scratch_shapes=[pltpu.VMEM((tm + HALO, d), F32)],
        compiler_params=_params(2),
        name="short_conv",
    )(x, norm_g.reshape(1, d), w_in[:, :d], w_in[:, d:2 * d], w_in[:, 2 * d:], conv_w,
      w_out.astype(BF16))


def kernel(x, ffn1_norm, ffn1_w_gu, ffn1_w_down, mix_norm, ffn2_norm, ffn2_w_gu, ffn2_w_down,
           sb_w_qkv, sb_w_o, ssd_w_in, ssd_conv_w, ssd_conv_b, ssd_dt_bias, ssd_a_log, ssd_d,
           ssd_norm, ssd_w_out, sc_w_in, sc_conv_w, sc_w_out, final_norm):
    b, l, d = x.shape
    depth = ffn1_norm.shape[0]
    h = x.reshape(b * l, d)
    for i in range(depth):
        h = _ffn(h, ffn1_norm[i], ffn1_w_gu[i], ffn1_w_down[i])
        kind, j = i % N_MIXERS, i // N_MIXERS
        if kind == 0:
            h = _sb_layer(h, b, l, mix_norm[i], sb_w_qkv[j], sb_w_o[j])
        elif kind == 1:
            h = _ssd_layer(h, b, l, mix_norm[i], ssd_w_in[j], ssd_conv_w[j], ssd_conv_b[j],
                           ssd_dt_bias[j], ssd_a_log[j], ssd_d[j], ssd_norm[j], ssd_w_out[j])
        else:
            h = _short_conv_layer(h, b, l, mix_norm[i], sc_w_in[j], sc_conv_w[j], sc_w_out[j])
        h = _ffn(h, ffn2_norm[i], ffn2_w_gu[i], ffn2_w_down[i],
                 final_g=final_norm if i == depth - 1 else None)
    return h.reshape(b, l, d)
```

```python
import functools

import jax
import jax.numpy as jnp
from jax import lax
from jax.experimental import pallas as pl
from jax.experimental.pallas import tpu as pltpu

F32 = jnp.float32
BF16 = jnp.bfloat16

RMS_EPS = 1e-6
LOG2_E = 1.4426950408889634
N_MIXERS = 3
SB_HEAD_DIM = 64
SSD_HEAD_DIM = 64
SSD_GROUPS = 8
SSD_HPG = 4
SSD_STATE = 128
SSD_CONV = 4
SSD_CHUNK = 128
SC_WIDTH = 3

LANES = 128
SUBLANES = 8
VMEM_LIMIT_BYTES = 56 * 1024 * 1024

FFN_TM = 512
FFN_TF = 256
PROJ_TM = 512
PROJ_TN = 512
ATT_T = 256
ATT_DEAD_CARRY = 160.0
SSD_GROUPS_PER_STEP = 4
SC_TM = 256
HALO = SUBLANES


def _params(n_axes):
    return pltpu.CompilerParams(
        dimension_semantics=("arbitrary",) * n_axes,
        vmem_limit_bytes=VMEM_LIMIT_BYTES)


def _resident(shape):
    zeros = (0,) * len(shape)
    return pl.BlockSpec(shape, lambda *_: zeros, pipeline_mode=pl.Buffered(1))


def _rms(x, g):
    ms = jnp.mean(x * x, axis=-1, keepdims=True)
    return x * lax.rsqrt(ms + RMS_EPS) * g


def _silu(x):
    return x * jax.nn.sigmoid(x)


def _dot(a, b):
    return jnp.dot(a, b, preferred_element_type=F32)


def _dot_nt(a, b):
    return lax.dot_general(a, b, (((1,), (1,)), ((), ())), preferred_element_type=F32)


def _dot_tn(a, b):
    return lax.dot_general(a, b, (((0,), (0,)), ((), ())), preferred_element_type=F32)


def _ffn_kernel(x_ref, g_ref, wgu_ref, wd_ref, *refs, n_chunks, tf, final_norm):
    fg_ref = refs[0] if final_norm else None
    o_ref, xn_ref, h_ref = refs[-3:]
    x = x_ref[...]
    d_ff = n_chunks * tf
    xn_ref[...] = _rms(x, g_ref[...]).astype(BF16)
    for c in range(n_chunks):
        gate = _dot(xn_ref[...], wgu_ref[:, c * tf:(c + 1) * tf])
        up = _dot(xn_ref[...], wgu_ref[:, d_ff + c * tf:d_ff + (c + 1) * tf])
        h_ref[:, c * tf:(c + 1) * tf] = (_silu(gate) * up).astype(BF16)
    y = x + 0.5 * _dot(h_ref[...], wd_ref[...])
    if final_norm:
        y = _rms(y, fg_ref[...])
    o_ref[...] = y


def _ffn(x, norm_g, w_gu, w_down, final_g=None):
    t, d = x.shape
    d_ff = w_down.shape[0]
    tm, tf = FFN_TM, FFN_TF
    n_chunks = d_ff // tf
    assert t % tm == 0 and d_ff % tf == 0
    wgu = w_gu.astype(BF16)
    wd = w_down.astype(BF16)
    final_norm = final_g is not None
    extra = [final_g.reshape(1, d)] if final_norm else []
    return pl.pallas_call(
        functools.partial(_ffn_kernel, n_chunks=n_chunks, tf=tf, final_norm=final_norm),
        grid=(t // tm,),
        in_specs=[
            pl.BlockSpec((tm, d), lambda i: (i, 0)),
            _resident((1, d)),
            _resident((d, 2 * d_ff)),
            _resident((d_ff, d)),
        ] + [_resident((1, d))] * len(extra),
        out_specs=pl.BlockSpec((tm, d), lambda i: (i, 0)),
        out_shape=jax.ShapeDtypeStruct((t, d), F32),
        scratch_shapes=[pltpu.VMEM((tm, d), BF16), pltpu.VMEM((tm, d_ff), BF16)],
        compiler_params=_params(1),
        name="ffn",
    )(x, norm_g.reshape(1, d), wgu, wd, *extra)


def _norm_proj_kernel(x_ref, g_ref, *refs, n_out, tn):
    w_refs, o_refs = refs[:n_out], refs[n_out:]
    xn = _rms(x_ref[...], g_ref[...]).astype(BF16)
    for w_ref, o_ref in zip(w_refs, o_refs):
        n = w_ref.shape[1]
        for n0 in range(0, n, tn):
            n1 = min(n0 + tn, n)
            o_ref[:, n0:n1] = _dot(xn, w_ref[:, n0:n1]).astype(o_ref.dtype)


def _norm_proj(x, norm_g, weights, out_dtypes):
    t, d = x.shape
    tm = PROJ_TM
    assert t % tm == 0
    n_out = len(weights)
    return pl.pallas_call(
        functools.partial(_norm_proj_kernel, n_out=n_out, tn=PROJ_TN),
        grid=(t // tm,),
        in_specs=[pl.BlockSpec((tm, d), lambda i: (i, 0)), _resident((1, d))]
        + [_resident(w.shape) for w in weights],
        out_specs=[pl.BlockSpec((tm, w.shape[1]), lambda i: (i, 0)) for w in weights],
        out_shape=[jax.ShapeDtypeStruct((t, w.shape[1]), dt) for w, dt in zip(weights, out_dtypes)],
        compiler_params=_params(1),
        name="norm_proj",
    )(x, norm_g.reshape(1, d), *weights)


def _proj_res_kernel(x_ref, y_ref, w_ref, o_ref):
    o_ref[...] = x_ref[...] + _dot(y_ref[...], w_ref[...])


def _proj_res(x, y, w):
    t, d = x.shape
    k = y.shape[1]
    tm = PROJ_TM
    assert t % tm == 0
    return pl.pallas_call(
        _proj_res_kernel,
        grid=(t // tm,),
        in_specs=[
            pl.BlockSpec((tm, d), lambda i: (i, 0)),
            pl.BlockSpec((tm, k), lambda i: (i, 0)),
            _resident((k, d)),
        ],
        out_specs=pl.BlockSpec((tm, d), lambda i: (i, 0)),
        out_shape=jax.ShapeDtypeStruct((t, d), F32),
        compiler_params=_params(1),
        name="proj_res",
    )(x, y, w)


def _sb_attn_kernel(q_ref, k_ref, v_ref, o_ref, lb_a, sp_a, rs_a, lb_b, sp_b, rs_b, acc_ref, c_ref,
                    *, t):
    qi = pl.program_id(2)
    hd = SB_HEAD_DIM
    rows = 4 * t
    buf_a, buf_b = (lb_a, sp_a, rs_a), (lb_b, sp_b, rs_b)
    in_a = lax.broadcasted_iota(jnp.int32, (1, 2 * hd), 1) < hd
    q_parts = []
    for s in range(2):
        q2 = q_ref[0, s * t:(s + 1) * t, :]
        q_parts += [jnp.where(in_a, q2, 0), jnp.where(in_a, 0, q2)]
    qs = jnp.concatenate(q_parts, axis=0)
    row = lax.broadcasted_iota(jnp.int32, (t, t), 0)
    col = lax.broadcasted_iota(jnp.int32, (t, t), 1)
    strict_lower = jnp.concatenate([col < row] * 2, axis=0)
    tri = jnp.where(row > col, 1.0, 0.0).astype(BF16)
    sign_bit = jnp.uint32(0x80000000)

    def mask_diagonal(x, fill):
        head = jnp.where(strict_lower, x[:2 * t], fill)
        return head if x.shape[0] == 2 * t else jnp.concatenate([head, x[2 * t:]], axis=0)

    def scores(j, buf, s0, diag):
        lb_ref, sp_ref, rs_ref = buf
        r0 = 2 * s0 * t
        k2 = k_ref[0, pl.ds(pl.multiple_of(j * t, t), t), :]
        z = _dot_nt(qs[r0:], k2)
        neg_abs = lax.bitcast_convert_type(lax.bitcast_convert_type(z, jnp.uint32) | sign_bit, F32)
        lb = jnp.minimum(z, 0.0) - jnp.log2(1.0 + jnp.exp2(neg_abs))
        sp = z - lb
        if diag:
            sp, lb = mask_diagonal(sp, 0.0), mask_diagonal(lb, -jnp.inf)
        lb_ref[r0:, :] = lb
        sp_ref[r0:, :] = sp.astype(BF16)
        rs_ref[r0:, :] = jnp.broadcast_to(jnp.sum(sp, axis=-1, keepdims=True), (rows - r0, LANES))
        if r0:
            lb_ref[:r0, :] = jnp.full((r0, t), -jnp.inf, F32)
            sp_ref[:r0, :] = jnp.zeros((r0, t), BF16)
            rs_ref[:r0, :] = jnp.zeros((r0, LANES), F32)

    def values(j, buf):
        lb_ref, sp_ref, rs_ref = buf
        v2 = v_ref[0, pl.ds(pl.multiple_of(j * t, t), t), :]
        c = c_ref[...]
        tail = _dot(sp_ref[...], tri)
        att = jnp.exp2(lb_ref[...] - tail - jnp.concatenate([c] * (t // LANES), axis=1)).astype(BF16)
        c_ref[...] = c + rs_ref[...]
        att2 = jnp.concatenate(
            [jnp.concatenate([att[(2 * s) * t:(2 * s + 1) * t], att[(2 * s + 1) * t:(2 * s + 2) * t]],
                             axis=1) for s in range(2)], axis=0)
        vv = jnp.concatenate([jnp.where(in_a, v2, 0), jnp.where(in_a, 0, v2)], axis=0)
        acc_ref[...] += _dot(att2, vv)

    acc_ref[...] = jnp.zeros_like(acc_ref)
    c_ref[...] = jnp.zeros_like(c_ref)
    top = 2 * qi + 1
    scores(top, buf_a, 1, True)
    values(top, buf_a)
    scores(top - 1, buf_b, 0, True)

    def tile_pair(state):
        i = state[0]
        j = top - 1 - 2 * i
        values(j, buf_b)
        scores(j - 1, buf_a, 0, False)
        values(j - 1, buf_a)
        scores(j - 2, buf_b, 0, False)
        return i + 1, jnp.min(c_ref[...])

    def live(state):
        return (state[0] < qi) & (state[1] < ATT_DEAD_CARRY)

    n_pairs, c_min = lax.while_loop(live, tile_pair, (jnp.int32(0), jnp.float32(0.0)))

    @pl.when(c_min < ATT_DEAD_CARRY)
    def _():
        values(top - 1 - 2 * n_pairs, buf_b)

    o_ref[0] = acc_ref[...].astype(o_ref.dtype)


def _sb_attention(q, k, v):
    b, l, d = q.shape
    t = ATT_T
    tq = 2 * t
    hp = 2 * SB_HEAD_DIM
    assert l % tq == 0 and d % hp == 0 and hp == LANES
    rows = 4 * t
    stage = [pltpu.VMEM((rows, t), F32), pltpu.VMEM((rows, t), BF16), pltpu.VMEM((rows, LANES), F32)]
    return pl.pallas_call(
        functools.partial(_sb_attn_kernel, t=t),
        grid=(b, d // hp, l // tq),
        in_specs=[
            pl.BlockSpec((1, tq, hp), lambda bi, hi, qi: (bi, qi, hi)),
            pl.BlockSpec((1, l, hp), lambda bi, hi, qi: (bi, 0, hi)),
            pl.BlockSpec((1, l, hp), lambda bi, hi, qi: (bi, 0, hi)),
        ],
        out_specs=pl.BlockSpec((1, tq, hp), lambda bi, hi, qi: (bi, qi, hi)),
        out_shape=jax.ShapeDtypeStruct((b, l, d), BF16),
        scratch_shapes=stage + stage + [pltpu.VMEM((tq, hp), F32), pltpu.VMEM((rows, LANES), F32)],
        compiler_params=_params(3),
        name="sb_attn",
    )(q, k, v)


def _sb_layer(x, b, l, norm_g, w_qkv, w_o):
    t, d = x.shape
    scale = LOG2_E * SB_HEAD_DIM ** -0.5
    wq = (w_qkv[:, :d] * scale).astype(BF16)
    wk = w_qkv[:, d:2 * d].astype(BF16)
    wv = w_qkv[:, 2 * d:].astype(BF16)
    q, k, v = _norm_proj(x, norm_g, [wq, wk, wv], [BF16] * 3)
    o = _sb_attention(q.reshape(b, l, d), k.reshape(b, l, d), v.reshape(b, l, d))
    return _proj_res(x, o.reshape(t, d), w_o.astype(BF16))


def _expand_heads(m, width):
    rows = m.shape[0]
    lane = lax.broadcasted_iota(jnp.int32, (rows, SSD_HPG * width), 1)
    out = jnp.broadcast_to(m[:, SSD_HPG - 1:SSD_HPG], (rows, SSD_HPG * width))
    for r in range(SSD_HPG - 2, -1, -1):
        out = jnp.where(lane < (r + 1) * width, m[:, r:r + 1], out)
    return out


def _to_columns(rows8):
    n = rows8.shape[1]
    padded = jnp.concatenate([rows8, jnp.zeros((n - rows8.shape[0], n), F32)], axis=0)
    return padded.T[:, :rows8.shape[0]]


def _ssd_kernel(xr_ref, br_ref, cr_ref, z_ref, dt_ref, wx_ref, wb_ref, wc_ref, bx_ref, bb_ref,
                bc_ref, dtb_ref, alog_ref, dsk_ref, ng_ref, o_ref, h_ref, ext_ref, *, n_groups):
    c = pl.program_id(2)
    cl, hp, n = SSD_CHUNK, SSD_HPG * SSD_HEAD_DIM, SSD_STATE

    @pl.when(c == 0)
    def _():
        h_ref[...] = jnp.zeros_like(h_ref)
        ext_ref[0:HALO, :] = jnp.zeros((HALO, ext_ref.shape[1]), F32)

    ext_ref[HALO:HALO + cl, :] = jnp.concatenate([xr_ref[...], br_ref[...], cr_ref[...]], axis=1)
    cw = jnp.concatenate([wx_ref[...], wb_ref[...], wc_ref[...]], axis=1)
    conv = jnp.concatenate([bx_ref[...], bb_ref[...], bc_ref[...]], axis=1)
    for kk in range(SSD_CONV):
        conv = conv + cw[kk:kk + 1, :] * ext_ref[pl.ds(HALO - (SSD_CONV - 1) + kk, cl), :]
    ext_ref[0:HALO, :] = ext_ref[cl:cl + HALO, :]
    act = _silu(conv)
    xs_all = act[:, :n_groups * hp]
    bm_all = act[:, n_groups * hp:n_groups * (hp + n)].astype(BF16)
    cm_all = act[:, n_groups * (hp + n):].astype(BF16)

    row = lax.broadcasted_iota(jnp.int32, (cl, cl), 0)
    col = lax.broadcasted_iota(jnp.int32, (cl, cl), 1)
    causal = col <= row
    lane_hp = lax.broadcasted_iota(jnp.int32, (cl, hp), 1)
    lane_cl = lax.broadcasted_iota(jnp.int32, (SUBLANES, cl), 1)

    for g in range(n_groups):
        xs = xs_all[:, g * hp:(g + 1) * hp]
        bm = bm_all[:, g * n:(g + 1) * n]
        cm = cm_all[:, g * n:(g + 1) * n]

        dt_in = dt_ref[0, g] + dtb_ref[g]
        dt = jnp.maximum(dt_in, 0.0) + jnp.log(1.0 + jnp.exp(-jnp.abs(dt_in)))
        a_cum = dt * (-jnp.exp(alog_ref[g]))
        sh = 1
        while sh < cl:
            a_cum = a_cum + jnp.where(lane_cl >= sh, pltpu.roll(a_cum, sh, axis=1), 0.0)
            sh *= 2
        a_last = a_cum[:, cl - 1:cl]
        to_end = jnp.exp(a_last - a_cum) * dt
        a_col = _to_columns(a_cum)
        te_col = _to_columns(to_end)

        cb = _dot_nt(cm, bm)
        ws, xblocks = [], []
        for r in range(SSD_HPG):
            seg = a_col[:, r:r + 1] - a_cum[r:r + 1, :]
            decay = jnp.exp(jnp.where(causal, seg, -jnp.inf))
            ws.append((cb * decay * dt[r:r + 1, :]).astype(BF16))
            in_head = (lane_hp >= r * SSD_HEAD_DIM) & (lane_hp < (r + 1) * SSD_HEAD_DIM)
            xblocks.append(jnp.where(in_head, xs, 0.0).astype(BF16))
        y = _dot(jnp.concatenate(ws, axis=1), jnp.concatenate(xblocks, axis=0))

        h_prev = h_ref[g]
        y = y + _dot(cm, h_prev.astype(BF16)) * _expand_heads(jnp.exp(a_col), SSD_HEAD_DIM)
        xw = (xs * _expand_heads(te_col, SSD_HEAD_DIM)).astype(BF16)
        h_decay = _expand_heads(jnp.exp(a_col[cl - 1:cl, :]), SSD_HEAD_DIM)
        h_ref[g] = h_prev * h_decay + _dot_tn(bm, xw)

        y = (y + dsk_ref[g] * xs) * _silu(z_ref[:, g * hp:(g + 1) * hp])
        o_ref[:, g * hp:(g + 1) * hp] = _rms(y, ng_ref[g]).astype(o_ref.dtype)


def _ssd_core(xbc, z, dt_t, conv_w, conv_b, dt_bias, a_log, d_skip, norm_g, b, l):
    t = xbc.shape[0]
    g, hpg, cl, n, ng = SSD_GROUPS, SSD_HPG, SSD_CHUNK, SSD_STATE, SSD_GROUPS_PER_STEP
    hp = hpg * SSD_HEAD_DIM
    d_inner = g * hp
    nc = l // cl
    assert l % cl == 0 and hp % LANES == 0 and n == LANES and g % ng == 0
    bb, cb = d_inner // (ng * n), (d_inner + g * n) // (ng * n)
    pad = lambda p: jnp.pad(p.reshape(g, hpg, 1), ((0, 0), (0, SUBLANES - hpg), (0, 0)))
    per_lane = lambda p: jnp.repeat(p.reshape(g, hpg), SSD_HEAD_DIM, axis=1).reshape(g, 1, hp)
    conv_b = conv_b.reshape(1, -1)
    tok = lambda bi, gi, ci: bi * nc + ci
    return pl.pallas_call(
        functools.partial(_ssd_kernel, n_groups=ng),
        grid=(b, g // ng, nc),
        in_specs=[
            pl.BlockSpec((cl, ng * hp), lambda bi, gi, ci: (tok(bi, gi, ci), gi)),
            pl.BlockSpec((cl, ng * n), lambda bi, gi, ci: (tok(bi, gi, ci), bb + gi)),
            pl.BlockSpec((cl, ng * n), lambda bi, gi, ci: (tok(bi, gi, ci), cb + gi)),
            pl.BlockSpec((cl, ng * hp), lambda bi, gi, ci: (tok(bi, gi, ci), gi)),
            pl.BlockSpec((1, ng, SUBLANES, cl), lambda bi, gi, ci: (bi, gi, 0, ci)),
            pl.BlockSpec((SSD_CONV, ng * hp), lambda bi, gi, ci: (0, gi)),
            pl.BlockSpec((SSD_CONV, ng * n), lambda bi, gi, ci: (0, bb + gi)),
            pl.BlockSpec((SSD_CONV, ng * n), lambda bi, gi, ci: (0, cb + gi)),
            pl.BlockSpec((1, ng * hp), lambda bi, gi, ci: (0, gi)),
            pl.BlockSpec((1, ng * n), lambda bi, gi, ci: (0, bb + gi)),
            pl.BlockSpec((1, ng * n), lambda bi, gi, ci: (0, cb + gi)),
            pl.BlockSpec((ng, SUBLANES, 1), lambda bi, gi, ci: (gi, 0, 0)),
            pl.BlockSpec((ng, SUBLANES, 1), lambda bi, gi, ci: (gi, 0, 0)),
            pl.BlockSpec((ng, 1, hp), lambda bi, gi, ci: (gi, 0, 0)),
            pl.BlockSpec((ng, 1, hp), lambda bi, gi, ci: (gi, 0, 0)),
        ],
        out_specs=pl.BlockSpec((cl, ng * hp), lambda bi, gi, ci: (tok(bi, gi, ci), gi)),
        out_shape=jax.ShapeDtypeStruct((t, d_inner), BF16),
        scratch_shapes=[pltpu.VMEM((ng, n, hp), F32),
                        pltpu.VMEM((cl + HALO, ng * (hp + 2 * n)), F32)],
        compiler_params=_params(3),
        name="ssd_core",
    )(xbc, xbc, xbc, z, dt_t, conv_w, conv_w, conv_w, conv_b, conv_b, conv_b,
      pad(dt_bias), pad(a_log), per_lane(d_skip), norm_g.reshape(g, 1, hp))


def _ssd_layer(x, b, l, norm_g, w_in, conv_w, conv_b, dt_bias, a_log, d_skip, ssd_norm, w_out):
    g, hpg = SSD_GROUPS, SSD_HPG
    d_inner = w_out.shape[0]
    conv_dim = conv_w.shape[1]
    n_heads = g * hpg
    w_in = w_in.astype(BF16)
    w_z = w_in[:, :d_inner]
    w_xbc = w_in[:, d_inner:d_inner + conv_dim]
    w_dt = jnp.pad(w_in[:, d_inner + conv_dim:], ((0, 0), (0, LANES - n_heads)))
    z, xbc, dt = _norm_proj(x, norm_g, [w_z, w_xbc, w_dt], [F32] * 3)
    dt_t = dt[:, :n_heads].reshape(b, l, g, hpg).transpose(0, 2, 3, 1)
    dt_t = jnp.pad(dt_t, ((0, 0), (0, 0), (0, SUBLANES - hpg), (0, 0)))
    y = _ssd_core(xbc, z, dt_t, conv_w, conv_b, dt_bias, a_log, d_skip, ssd_norm, b, l)
    return _proj_res(x, y, w_out.astype(BF16))


def _short_conv_kernel(x_ref, g_ref, wb_ref, wc_ref, wh_ref, cw_ref, wo_ref, o_ref, ext_ref):
    tm = x_ref.shape[0]

    @pl.when(pl.program_id(1) == 0)
    def _():
        ext_ref[0:HALO, :] = jnp.zeros((HALO, ext_ref.shape[1]), F32)

    x = x_ref[...]
    xn = _rms(x, g_ref[...]).astype(BF16)
    ext_ref[HALO:HALO + tm, :] = _dot(xn, wc_ref[...]) * _dot(xn, wh_ref[...])
    cw = cw_ref[...]
    u = cw[0:1, :] * ext_ref[pl.ds(HALO - (SC_WIDTH - 1), tm), :]
    for kk in range(1, SC_WIDTH):
        u = u + cw[kk:kk + 1, :] * ext_ref[pl.ds(HALO - (SC_WIDTH - 1) + kk, tm), :]
    ext_ref[0:HALO, :] = ext_ref[tm:tm + HALO, :]
    gated = (_dot(xn, wb_ref[...]) * u).astype(BF16)
    o_ref[...] = x + _dot(gated, wo_ref[...])


def _short_conv_layer(x, b, l, norm_g, w_in, conv_w, w_out):
    t, d = x.shape
    tm = SC_TM
    nl = l // tm
    assert l % tm == 0
    w_in = w_in.astype(BF16)
    return pl.pallas_call(
        _short_conv_kernel,
        grid=(b, nl),
        in_specs=[
            pl.BlockSpec((tm, d), lambda bi, li: (bi * nl + li, 0)),
            _resident((1, d)),
            _resident((d, d)), _resident((d, d)), _resident((d, d)),
            _resident((SC_WIDTH, d)),
            _resident((d, d)),
        ],
        out_specs=pl.BlockSpec((tm, d), lambda bi, li: (bi * nl + li, 0)),
        out_shape=jax.ShapeDtypeStruct((t, d), F32),
        scratch_shapes=[pltpu.VMEM((tm + HALO, d), F32)],
        compiler_params=_params(2),
        name="short_conv",
    )(x, norm_g.reshape(1, d), w_in[:, :d], w_in[:, d:2 * d], w_in[:, 2 * d:], conv_w,
      w_out.astype(BF16))


def kernel(x, ffn1_norm, ffn1_w_gu, ffn1_w_down, mix_norm, ffn2_norm, ffn2_w_gu, ffn2_w_down,
           sb_w_qkv, sb_w_o, ssd_w_in, ssd_conv_w, ssd_conv_b, ssd_dt_bias, ssd_a_log, ssd_d,
           ssd_norm, ssd_w_out, sc_w_in, sc_conv_w, sc_w_out, final_norm):
    b, l, d = x.shape
    depth = ffn1_norm.shape[0]
    h = x.reshape(b * l, d)
    for i in range(depth):
        h = _ffn(h, ffn1_norm[i], ffn1_w_gu[i], ffn1_w_down[i])
        kind, j = i % N_MIXERS, i // N_MIXERS
        if kind == 0:
            h = _sb_layer(h, b, l, mix_norm[i], sb_w_qkv[j], sb_w_o[j])
        elif kind == 1:
            h = _ssd_layer(h, b, l, mix_norm[i], ssd_w_in[j], ssd_conv_w[j], ssd_conv_b[j],
                           ssd_dt_bias[j], ssd_a_log[j], ssd_d[j], ssd_norm[j], ssd_w_out[j])
        else:
            h = _short_conv_layer(h, b, l, mix_norm[i], sc_w_in[j], sc_conv_w[j], sc_w_out[j])
        h = _ffn(h, ffn2_norm[i], ffn2_w_gu[i], ffn2_w_down[i],
                 final_g=final_norm if i == depth - 1 else None)
    return h.reshape(b, l, d)
```

```python
import functools

import jax
import jax.numpy as jnp
from jax import lax
from jax.experimental import pallas as pl
from jax.experimental.pallas import tpu as pltpu

F32 = jnp.float32
BF16 = jnp.bfloat16

RMS_EPS = 1e-6
LOG2_E = 1.4426950408889634
N_MIXERS = 3
SB_HEAD_DIM = 64
SSD_HEAD_DIM = 64
SSD_GROUPS = 8
SSD_HPG = 4
SSD_STATE = 128
SSD_CONV = 4
SSD_CHUNK = 128
SC_WIDTH = 3

LANES = 128
SUBLANES = 8
VMEM_LIMIT_BYTES = 56 * 1024 * 1024

FFN_TM = 512
FFN_TF = 256
PROJ_TM = 512
PROJ_TN = 512
ATT_T = 256
ATT_DEAD_CARRY = 160.0
SSD_GROUPS_PER_STEP = 4
SC_TM = 256
HALO = SUBLANES


def _params(n_axes):
    return pltpu.CompilerParams(
        dimension_semantics=("arbitrary",) * n_axes,
        vmem_limit_bytes=VMEM_LIMIT_BYTES)


def _resident(shape):
    zeros = (0,) * len(shape)
    return pl.BlockSpec(shape, lambda *_: zeros, pipeline_mode=pl.Buffered(1))


def _rms(x, g):
    ms = jnp.mean(x * x, axis=-1, keepdims=True)
    return x * lax.rsqrt(ms + RMS_EPS) * g


def _silu(x):
    return x * jax.nn.sigmoid(x)


def _dot(a, b):
    return jnp.dot(a, b, preferred_element_type=F32)


def _dot_nt(a, b):
    return lax.dot_general(a, b, (((1,), (1,)), ((), ())), preferred_element_type=F32)


def _dot_tn(a, b):
    return lax.dot_general(a, b, (((0,), (0,)), ((), ())), preferred_element_type=F32)


def _ffn_kernel(x_ref, g_ref, wgu_ref, wd_ref, *refs, n_chunks, tf, final_norm):
    fg_ref = refs[0] if final_norm else None
    o_ref, xn_ref, h_ref = refs[-3:]
    x = x_ref[...]
    d_ff = n_chunks * tf
    xn_ref[...] = _rms(x, g_ref[...]).astype(BF16)
    for c in range(n_chunks):
        gate = _dot(xn_ref[...], wgu_ref[:, c * tf:(c + 1) * tf])
        up = _dot(xn_ref[...], wgu_ref[:, d_ff + c * tf:d_ff + (c + 1) * tf])
        h_ref[:, c * tf:(c + 1) * tf] = (_silu(gate) * up).astype(BF16)
    y = x + 0.5 * _dot(h_ref[...], wd_ref[...])
    if final_norm:
        y = _rms(y, fg_ref[...])
    o_ref[...] = y


def _ffn(x, norm_g, w_gu, w_down, final_g=None):
    t, d = x.shape
    d_ff = w_down.shape[0]
    tm, tf = FFN_TM, FFN_TF
    n_chunks = d_ff // tf
    assert t % tm == 0 and d_ff % tf == 0
    wgu = w_gu.astype(BF16)
    wd = w_down.astype(BF16)
    final_norm = final_g is not None
    extra = [final_g.reshape(1, d)] if final_norm else []
    return pl.pallas_call(
        functools.partial(_ffn_kernel, n_chunks=n_chunks, tf=tf, final_norm=final_norm),
        grid=(t // tm,),
        in_specs=[
            pl.BlockSpec((tm, d), lambda i: (i, 0)),
            _resident((1, d)),
            _resident((d, 2 * d_ff)),
            _resident((d_ff, d)),
        ] + [_resident((1, d))] * len(extra),
        out_specs=pl.BlockSpec((tm, d), lambda i: (i, 0)),
        out_shape=jax.ShapeDtypeStruct((t, d), F32),
        scratch_shapes=[pltpu.VMEM((tm, d), BF16), pltpu.VMEM((tm, d_ff), BF16)],
        compiler_params=_params(1),
        name="ffn",
    )(x, norm_g.reshape(1, d), wgu, wd, *extra)


def _norm_proj_kernel(x_ref, g_ref, *refs, n_out, tn):
    w_refs, o_refs = refs[:n_out], refs[n_out:]
    xn = _rms(x_ref[...], g_ref[...]).astype(BF16)
    for w_ref, o_ref in zip(w_refs, o_refs):
        n = w_ref.shape[1]
        for n0 in range(0, n, tn):
            n1 = min(n0 + tn, n)
            o_ref[:, n0:n1] = _dot(xn, w_ref[:, n0:n1]).astype(o_ref.dtype)


def _norm_proj(x, norm_g, weights, out_dtypes):
    t, d = x.shape
    tm = PROJ_TM
    assert t % tm == 0
    n_out = len(weights)
    return pl.pallas_call(
        functools.partial(_norm_proj_kernel, n_out=n_out, tn=PROJ_TN),
        grid=(t // tm,),
        in_specs=[pl.BlockSpec((tm, d), lambda i: (i, 0)), _resident((1, d))]
        + [_resident(w.shape) for w in weights],
        out_specs=[pl.BlockSpec((tm, w.shape[1]), lambda i: (i, 0)) for w in weights],
        out_shape=[jax.ShapeDtypeStruct((t, w.shape[1]), dt) for w, dt in zip(weights, out_dtypes)],
        compiler_params=_params(1),
        name="norm_proj",
    )(x, norm_g.reshape(1, d), *weights)


def _proj_res_kernel(x_ref, y_ref, w_ref, o_ref):
    o_ref[...] = x_ref[...] + _dot(y_ref[...], w_ref[...])


def _proj_res(x, y, w):
    t, d = x.shape
    k = y.shape[1]
    tm = PROJ_TM
    assert t % tm == 0
    return pl.pallas_call(
        _proj_res_kernel,
        grid=(t // tm,),
        in_specs=[
            pl.BlockSpec((tm, d), lambda i: (i, 0)),
            pl.BlockSpec((tm, k), lambda i: (i, 0)),
            _resident((k, d)),
        ],
        out_specs=pl.BlockSpec((tm, d), lambda i: (i, 0)),
        out_shape=jax.ShapeDtypeStruct((t, d), F32),
        compiler_params=_params(1),
        name="proj_res",
    )(x, y, w)


def _sb_attn_kernel(q_ref, k_ref, v_ref, o_ref, lb_a, sp_a, rs_a, lb_b, sp_b, rs_b, acc_ref, c_ref,
                    *, t):
    qi = pl.program_id(2)
    hd = SB_HEAD_DIM
    buf_a, buf_b = (lb_a, sp_a, rs_a), (lb_b, sp_b, rs_b)
    both, first, second = (0, 2), (0, 1), (1, 2)
    in_a = lax.broadcasted_iota(jnp.int32, (1, 2 * hd), 1) < hd
    q_parts = []
    for s in range(2):
        q2 = q_ref[0, s * t:(s + 1) * t, :]
        q_parts += [jnp.where(in_a, q2, 0), jnp.where(in_a, 0, q2)]
    qs = jnp.concatenate(q_parts, axis=0)
    row = lax.broadcasted_iota(jnp.int32, (t, t), 0)
    col = lax.broadcasted_iota(jnp.int32, (t, t), 1)
    strict_lower = jnp.concatenate([col < row] * 2, axis=0)
    tri = jnp.where(row > col, 1.0, 0.0).astype(BF16)
    sign_bit = jnp.uint32(0x80000000)

    def mask_diagonal(x, fill):
        head = jnp.where(strict_lower, x[:2 * t], fill)
        return head if x.shape[0] == 2 * t else jnp.concatenate([head, x[2 * t:]], axis=0)

    def scores(j, buf, subs, diag):
        lb_ref, sp_ref, rs_ref = buf
        r0, r1 = 2 * subs[0] * t, 2 * subs[1] * t
        k2 = k_ref[0, pl.ds(pl.multiple_of(j * t, t), t), :]
        z = _dot_nt(qs[r0:r1], k2)
        neg_abs = lax.bitcast_convert_type(lax.bitcast_convert_type(z, jnp.uint32) | sign_bit, F32)
        lb = jnp.minimum(z, 0.0) - jnp.log2(1.0 + jnp.exp2(neg_abs))
        sp = z - lb
        if diag:
            sp, lb = mask_diagonal(sp, 0.0), mask_diagonal(lb, -jnp.inf)
        lb_ref[r0:r1, :] = lb
        sp_ref[r0:r1, :] = sp.astype(BF16)
        rs_ref[r0:r1, :] = jnp.broadcast_to(jnp.sum(sp, axis=-1, keepdims=True), (r1 - r0, LANES))

    def values(j, buf, subs):
        lb_ref, sp_ref, rs_ref = buf
        r0, r1 = 2 * subs[0] * t, 2 * subs[1] * t
        v2 = v_ref[0, pl.ds(pl.multiple_of(j * t, t), t), :]
        c = c_ref[r0:r1, :]
        tail = _dot(sp_ref[r0:r1, :], tri)
        c_wide = jnp.concatenate([c] * (t // LANES), axis=1)
        att = jnp.exp2(lb_ref[r0:r1, :] - tail - c_wide).astype(BF16)
        c_ref[r0:r1, :] = c + rs_ref[r0:r1, :]
        att2 = jnp.concatenate(
            [jnp.concatenate([att[(2 * s) * t:(2 * s + 1) * t], att[(2 * s + 1) * t:(2 * s + 2) * t]],
                             axis=1) for s in range(subs[1] - subs[0])], axis=0)
        vv = jnp.concatenate([jnp.where(in_a, v2, 0), jnp.where(in_a, 0, v2)], axis=0)
        acc_ref[subs[0] * t:subs[1] * t, :] += _dot(att2, vv)

    def tile_pair(state):
        i = state[0]
        j = top - 3 - 2 * i
        values(j, buf_b, both)
        scores(j - 1, buf_a, both, False)
        values(j - 1, buf_a, both)
        scores(j - 2, buf_b, both, False)
        return i + 1, jnp.min(c_ref[...])

    def live(state):
        return (state[0] < qi - 1) & (state[1] < ATT_DEAD_CARRY)

    acc_ref[...] = jnp.zeros_like(acc_ref)
    c_ref[...] = jnp.zeros_like(c_ref)
    top = 2 * qi + 1

    def diagonal_tiles():
        scores(top, buf_a, second, True)
        values(top, buf_a, second)
        scores(top - 1, buf_b, both, True)
        values(top - 1, buf_b, both)

    @pl.when(qi == 0)
    def _():
        diagonal_tiles()

    @pl.when(qi > 0)
    def _():
        diagonal_tiles()
        scores(top - 2, buf_a, first, False)
        values(top - 2, buf_a, first)
        c_min0 = jnp.min(c_ref[...])

        @pl.when(c_min0 < ATT_DEAD_CARRY)
        def _():
            scores(top - 2, buf_a, second, False)
            values(top - 2, buf_a, second)
            scores(top - 3, buf_b, both, False)
            n_pairs, c_min = lax.while_loop(live, tile_pair, (jnp.int32(0), c_min0))

            @pl.when(c_min < ATT_DEAD_CARRY)
            def _():
                values(top - 3 - 2 * n_pairs, buf_b, both)

    o_ref[0] = acc_ref[...].astype(o_ref.dtype)


def _sb_attention(q, k, v):
    b, l, d = q.shape
    t = ATT_T
    tq = 2 * t
    hp = 2 * SB_HEAD_DIM
    assert l % tq == 0 and d % hp == 0 and hp == LANES
    rows = 4 * t
    stage = [pltpu.VMEM((rows, t), F32), pltpu.VMEM((rows, t), BF16), pltpu.VMEM((rows, LANES), F32)]
    return pl.pallas_call(
        functools.partial(_sb_attn_kernel, t=t),
        grid=(b, d // hp, l // tq),
        in_specs=[
            pl.BlockSpec((1, tq, hp), lambda bi, hi, qi: (bi, qi, hi)),
            pl.BlockSpec((1, l, hp), lambda bi, hi, qi: (bi, 0, hi)),
            pl.BlockSpec((1, l, hp), lambda bi, hi, qi: (bi, 0, hi)),
        ],
        out_specs=pl.BlockSpec((1, tq, hp), lambda bi, hi, qi: (bi, qi, hi)),
        out_shape=jax.ShapeDtypeStruct((b, l, d), BF16),
        scratch_shapes=stage + stage + [pltpu.VMEM((tq, hp), F32), pltpu.VMEM((rows, LANES), F32)],
        compiler_params=_params(3),
        name="sb_attn",
    )(q, k, v)


def _sb_layer(x, b, l, norm_g, w_qkv, w_o):
    t, d = x.shape
    scale = LOG2_E * SB_HEAD_DIM ** -0.5
    wq = (w_qkv[:, :d] * scale).astype(BF16)
    wk = w_qkv[:, d:2 * d].astype(BF16)
    wv = w_qkv[:, 2 * d:].astype(BF16)
    q, k, v = _norm_proj(x, norm_g, [wq, wk, wv], [BF16] * 3)
    o = _sb_attention(q.reshape(b, l, d), k.reshape(b, l, d), v.reshape(b, l, d))
    return _proj_res(x, o.reshape(t, d), w_o.astype(BF16))


def _expand_heads(m, width):
    rows = m.shape[0]
    lane = lax.broadcasted_iota(jnp.int32, (rows, SSD_HPG * width), 1)
    out = jnp.broadcast_to(m[:, SSD_HPG - 1:SSD_HPG], (rows, SSD_HPG * width))
    for r in range(SSD_HPG - 2, -1, -1):
        out = jnp.where(lane < (r + 1) * width, m[:, r:r + 1], out)
    return out


def _to_columns(rows8):
    n = rows8.shape[1]
    padded = jnp.concatenate([rows8, jnp.zeros((n - rows8.shape[0], n), F32)], axis=0)
    return padded.T[:, :rows8.shape[0]]


def _ssd_kernel(xr_ref, br_ref, cr_ref, z_ref, dt_ref, wx_ref, wb_ref, wc_ref, bx_ref, bb_ref,
                bc_ref, dtb_ref, alog_ref, dsk_ref, ng_ref, o_ref, h_ref, hist_ref, *, n_groups):
    c = pl.program_id(2)
    cl, hp, n = SSD_CHUNK, SSD_HPG * SSD_HEAD_DIM, SSD_STATE

    @pl.when(c == 0)
    def _():
        h_ref[...] = jnp.zeros_like(h_ref)
        hist_ref[...] = jnp.zeros_like(hist_ref)

    raw = jnp.concatenate([xr_ref[...], br_ref[...], cr_ref[...]], axis=1)
    ext = jnp.concatenate([hist_ref[...], raw], axis=0)
    hist_ref[...] = raw[cl - HALO:, :]
    cw = jnp.concatenate([wx_ref[...], wb_ref[...], wc_ref[...]], axis=1)
    acc = cw[0:1, :] * ext
    for kk in range(1, SSD_CONV):
        acc = cw[kk:kk + 1, :] * ext + pltpu.roll(acc, 1, axis=0)
    bias = jnp.concatenate([bx_ref[...], bb_ref[...], bc_ref[...]], axis=1)
    act = _silu(acc[HALO:, :] + bias)
    xs_all = act[:, :n_groups * hp]
    bm_all = act[:, n_groups * hp:n_groups * (hp + n)].astype(BF16)
    cm_all = act[:, n_groups * (hp + n):].astype(BF16)

    rows = n_groups * SUBLANES
    dt_in = (dt_ref[0] + dtb_ref[...]).reshape(rows, cl)
    dt_all = jnp.maximum(dt_in, 0.0) + jnp.log(1.0 + jnp.exp(-jnp.abs(dt_in)))
    a_all = dt_all * (-jnp.exp(alog_ref[...].reshape(rows, 1)))
    lane_cl = lax.broadcasted_iota(jnp.int32, (rows, cl), 1)
    sh = 1
    while sh < cl:
        a_all = a_all + jnp.where(lane_cl >= sh, pltpu.roll(a_all, sh, axis=1), 0.0)
        sh *= 2
    te_all = jnp.exp(a_all[:, cl - 1:cl] - a_all) * dt_all
    a_col_all = _to_columns(a_all)
    te_col_all = _to_columns(te_all)

    row = lax.broadcasted_iota(jnp.int32, (cl, cl), 0)
    col = lax.broadcasted_iota(jnp.int32, (cl, cl), 1)
    causal = col <= row
    lane_hp = lax.broadcasted_iota(jnp.int32, (cl, hp), 1)

    for g in range(n_groups):
        xs = xs_all[:, g * hp:(g + 1) * hp]
        bm = bm_all[:, g * n:(g + 1) * n]
        cm = cm_all[:, g * n:(g + 1) * n]
        dt = dt_all[g * SUBLANES:(g + 1) * SUBLANES]
        a_cum = a_all[g * SUBLANES:(g + 1) * SUBLANES]
        a_col = a_col_all[:, g * SUBLANES:(g + 1) * SUBLANES]
        te_col = te_col_all[:, g * SUBLANES:(g + 1) * SUBLANES]

        cb = _dot_nt(cm, bm)
        ws, xblocks = [], []
        for r in range(SSD_HPG):
            seg = a_col[:, r:r + 1] - a_cum[r:r + 1, :]
            decay = jnp.exp(jnp.where(causal, seg, -jnp.inf))
            ws.append((cb * decay * dt[r:r + 1, :]).astype(BF16))
            in_head = (lane_hp >= r * SSD_HEAD_DIM) & (lane_hp < (r + 1) * SSD_HEAD_DIM)
            xblocks.append(jnp.where(in_head, xs, 0.0).astype(BF16))
        y = _dot(jnp.concatenate(ws, axis=1), jnp.concatenate(xblocks, axis=0))

        h_prev = h_ref[g]
        y = y + _dot(cm, h_prev.astype(BF16)) * _expand_heads(jnp.exp(a_col), SSD_HEAD_DIM)
        xw = (xs * _expand_heads(te_col, SSD_HEAD_DIM)).astype(BF16)
        h_decay = _expand_heads(jnp.exp(a_col[cl - 1:cl, :]), SSD_HEAD_DIM)
        h_ref[g] = h_prev * h_decay + _dot_tn(bm, xw)

        y = (y + dsk_ref[g] * xs) * _silu(z_ref[:, g * hp:(g + 1) * hp])
        o_ref[:, g * hp:(g + 1) * hp] = _rms(y, ng_ref[g]).astype(o_ref.dtype)


def _ssd_core(xbc, z, dt_t, conv_w, conv_b, dt_bias, a_log, d_skip, norm_g, b, l):
    t = xbc.shape[0]
    g, hpg, cl, n, ng = SSD_GROUPS, SSD_HPG, SSD_CHUNK, SSD_STATE, SSD_GROUPS_PER_STEP
    hp = hpg * SSD_HEAD_DIM
    d_inner = g * hp
    nc = l // cl
    assert l % cl == 0 and hp % LANES == 0 and n == LANES and g % ng == 0
    bb, cb = d_inner // (ng * n), (d_inner + g * n) // (ng * n)
    pad = lambda p: jnp.pad(p.reshape(g, hpg, 1), ((0, 0), (0, SUBLANES - hpg), (0, 0)))
    per_lane = lambda p: jnp.repeat(p.reshape(g, hpg), SSD_HEAD_DIM, axis=1).reshape(g, 1, hp)
    conv_b = conv_b.reshape(1, -1)
    tok = lambda bi, gi, ci: bi * nc + ci
    return pl.pallas_call(
        functools.partial(_ssd_kernel, n_groups=ng),
        grid=(b, g // ng, nc),
        in_specs=[
            pl.BlockSpec((cl, ng * hp), lambda bi, gi, ci: (tok(bi, gi, ci), gi)),
            pl.BlockSpec((cl, ng * n), lambda bi, gi, ci: (tok(bi, gi, ci), bb + gi)),
            pl.BlockSpec((cl, ng * n), lambda bi, gi, ci: (tok(bi, gi, ci), cb + gi)),
            pl.BlockSpec((cl, ng * hp), lambda bi, gi, ci: (tok(bi, gi, ci), gi)),
            pl.BlockSpec((1, ng, SUBLANES, cl), lambda bi, gi, ci: (bi, gi, 0, ci)),
            pl.BlockSpec((SSD_CONV, ng * hp), lambda bi, gi, ci: (0, gi)),
            pl.BlockSpec((SSD_CONV, ng * n), lambda bi, gi, ci: (0, bb + gi)),
            pl.BlockSpec((SSD_CONV, ng * n), lambda bi, gi, ci: (0, cb + gi)),
            pl.BlockSpec((1, ng * hp), lambda bi, gi, ci: (0, gi)),
            pl.BlockSpec((1, ng * n), lambda bi, gi, ci: (0, bb + gi)),
            pl.BlockSpec((1, ng * n), lambda bi, gi, ci: (0, cb + gi)),
            pl.BlockSpec((ng, SUBLANES, 1), lambda bi, gi, ci: (gi, 0, 0)),
            pl.BlockSpec((ng, SUBLANES, 1), lambda bi, gi, ci: (gi, 0, 0)),
            pl.BlockSpec((ng, 1, hp), lambda bi, gi, ci: (gi, 0, 0)),
            pl.BlockSpec((ng, 1, hp), lambda bi, gi, ci: (gi, 0, 0)),
        ],
        out_specs=pl.BlockSpec((cl, ng * hp), lambda bi, gi, ci: (tok(bi, gi, ci), gi)),
        out_shape=jax.ShapeDtypeStruct((t, d_inner), BF16),
        scratch_shapes=[pltpu.VMEM((ng, n, hp), F32), pltpu.VMEM((HALO, ng * (hp + 2 * n)), F32)],
        compiler_params=_params(3),
        name="ssd_core",
    )(xbc, xbc, xbc, z, dt_t, conv_w, conv_w, conv_w, conv_b, conv_b, conv_b,
      pad(dt_bias), pad(a_log), per_lane(d_skip), norm_g.reshape(g, 1, hp))


def _ssd_layer(x, b, l, norm_g, w_in, conv_w, conv_b, dt_bias, a_log, d_skip, ssd_norm, w_out):
    g, hpg = SSD_GROUPS, SSD_HPG
    d_inner = w_out.shape[0]
    conv_dim = conv_w.shape[1]
    n_heads = g * hpg
    w_in = w_in.astype(BF16)
    w_z = w_in[:, :d_inner]
    w_xbc = w_in[:, d_inner:d_inner + conv_dim]
    w_dt = jnp.pad(w_in[:, d_inner + conv_dim:], ((0, 0), (0, LANES - n_heads)))
    z, xbc, dt = _norm_proj(x, norm_g, [w_z, w_xbc, w_dt], [F32] * 3)
    dt_t = dt[:, :n_heads].reshape(b, l, g, hpg).transpose(0, 2, 3, 1)
    dt_t = jnp.pad(dt_t, ((0, 0), (0, 0), (0, SUBLANES - hpg), (0, 0)))
    y = _ssd_core(xbc, z, dt_t, conv_w, conv_b, dt_bias, a_log, d_skip, ssd_norm, b, l)
    return _proj_res(x, y, w_out.astype(BF16))


def _short_conv_kernel(x_ref, g_ref, wb_ref, wc_ref, wh_ref, cw_ref, wo_ref, o_ref, ext_ref):
    tm = x_ref.shape[0]

    @pl.when(pl.program_id(1) == 0)
    def _():
        ext_ref[0:HALO, :] = jnp.zeros((HALO, ext_ref.shape[1]), F32)

    x = x_ref[...]
    xn = _rms(x, g_ref[...]).astype(BF16)
    ext_ref[HALO:HALO + tm, :] = _dot(xn, wc_ref[...]) * _dot(xn, wh_ref[...])
    cw = cw_ref[...]
    u = cw[0:1, :] * ext_ref[pl.ds(HALO - (SC_WIDTH - 1), tm), :]
    for kk in range(1, SC_WIDTH):
        u = u + cw[kk:kk + 1, :] * ext_ref[pl.ds(HALO - (SC_WIDTH - 1) + kk, tm), :]
    ext_ref[0:HALO, :] = ext_ref[tm:tm + HALO, :]
    gated = (_dot(xn, wb_ref[...]) * u).astype(BF16)
    o_ref[...] = x + _dot(gated, wo_ref[...])


def _short_conv_layer(x, b, l, norm_g, w_in, conv_w, w_out):
    t, d = x.shape
    tm = SC_TM
    nl = l // tm
    assert l % tm == 0
    w_in = w_in.astype(BF16)
    return pl.pallas_call(
        _short_conv_kernel,
        grid=(b, nl),
        in_specs=[
            pl.BlockSpec((tm, d), lambda bi, li: (bi * nl + li, 0)),
            _resident((1, d)),
            _resident((d, d)), _resident((d, d)), _resident((d, d)),
            _resident((SC_WIDTH, d)),
            _resident((d, d)),
        ],
        out_specs=pl.BlockSpec((tm, d), lambda bi, li: (bi * nl + li, 0)),
        out_shape=jax.ShapeDtypeStruct((t, d), F32),
        scratch_shapes=[pltpu.VMEM((tm + HALO, d), F32)],
        compiler_params=_params(2),
        name="short_conv",
    )(x, norm_g.reshape(1, d), w_in[:, :d], w_in[:, d:2 * d], w_in[:, 2 * d:], conv_w,
      w_out.astype(BF16))


def kernel(x, ffn1_norm, ffn1_w_gu, ffn1_w_down, mix_norm, ffn2_norm, ffn2_w_gu, ffn2_w_down,
           sb_w_qkv, sb_w_o, ssd_w_in, ssd_conv_w, ssd_conv_b, ssd_dt_bias, ssd_a_log, ssd_d,
           ssd_norm, ssd_w_out, sc_w_in, sc_conv_w, sc_w_out, final_norm):
    b, l, d = x.shape
    depth = ffn1_norm.shape[0]
    h = x.reshape(b * l, d)
    for i in range(depth):
        h = _ffn(h, ffn1_norm[i], ffn1_w_gu[i], ffn1_w_down[i])
        kind, j = i % N_MIXERS, i // N_MIXERS
        if kind == 0:
            h = _sb_layer(h, b, l, mix_norm[i], sb_w_qkv[j], sb_w_o[j])
        elif kind == 1:
            h = _ssd_layer(h, b, l, mix_norm[i], ssd_w_in[j], ssd_conv_w[j], ssd_conv_b[j],
                           ssd_dt_bias[j], ssd_a_log[j], ssd_d[j], ssd_norm[j], ssd_w_out[j])
        else:
            h = _short_conv_layer(h, b, l, mix_norm[i], sc_w_in[j], sc_conv_w[j], sc_w_out[j])
        h = _ffn(h, ffn2_norm[i], ffn2_w_gu[i], ffn2_w_down[i],
                 final_g=final_norm if i == depth - 1 else None)
    return h.reshape(b, l, d)
```

```python
import functools

import jax
import jax.numpy as jnp
from jax import lax
from jax.experimental import pallas as pl
from jax.experimental.pallas import tpu as pltpu

F32 = jnp.float32
BF16 = jnp.bfloat16

RMS_EPS = 1e-6
LOG2_E = 1.4426950408889634
N_MIXERS = 3
SB_HEAD_DIM = 64
SSD_HEAD_DIM = 64
SSD_GROUPS = 8
SSD_HPG = 4
SSD_STATE = 128
SSD_CONV = 4
SSD_CHUNK = 128
SC_WIDTH = 3

LANES = 128
SUBLANES = 8
VMEM_LIMIT_BYTES = 60 * 1024 * 1024

FFN_TM = 512
FFN_TF = 256
PROJ_TM = 512
PROJ_TN = 512
ATT_T = 256
ATT_DEAD_CARRY = 160.0
SSD_GROUPS_PER_STEP = 4
SC_TM = 256
HALO = SUBLANES


def _params(n_axes):
    return pltpu.CompilerParams(
        dimension_semantics=("arbitrary",) * n_axes,
        vmem_limit_bytes=VMEM_LIMIT_BYTES)


def _resident(shape, index=None):
    index = (0,) * len(shape) if index is None else index
    return pl.BlockSpec(shape, lambda *_: index, pipeline_mode=pl.Buffered(1))


def _rms(x, g):
    ms = jnp.mean(x * x, axis=-1, keepdims=True)
    return x * lax.rsqrt(ms + RMS_EPS) * g


def _silu(x):
    return x * jax.nn.sigmoid(x)


def _dot(a, b):
    return jnp.dot(a, b, preferred_element_type=F32)


def _dot_nt(a, b):
    return lax.dot_general(a, b, (((1,), (1,)), ((), ())), preferred_element_type=F32)


def _dot_tn(a, b):
    return lax.dot_general(a, b, (((0,), (0,)), ((), ())), preferred_element_type=F32)


def _ffn_kernel(x_ref, g_ref, wgu_ref, wd_ref, *refs, n_chunks, tf, final_norm):
    fg_ref = refs[0] if final_norm else None
    o_ref, xn_ref, h_ref, wdb_ref = refs[-4:]
    x = x_ref[...]
    d_ff = n_chunks * tf
    xn_ref[...] = _rms(x, g_ref[...]).astype(BF16)
    for c in range(n_chunks):
        gate = _dot(xn_ref[...], wgu_ref[:, c * tf:(c + 1) * tf].astype(BF16))
        up = _dot(xn_ref[...], wgu_ref[:, d_ff + c * tf:d_ff + (c + 1) * tf].astype(BF16))
        h_ref[:, c * tf:(c + 1) * tf] = (_silu(gate) * up).astype(BF16)
        wdb_ref[c * tf:(c + 1) * tf, :] = wd_ref[c * tf:(c + 1) * tf, :].astype(BF16)
    y = x + 0.5 * _dot(h_ref[...], wdb_ref[...])
    if final_norm:
        y = _rms(y, fg_ref[...])
    o_ref[...] = y


def _ffn(x, norm_g, w_gu, w_down, layer, final_g=None):
    t, d = x.shape
    d_ff = w_down.shape[1]
    tm, tf = FFN_TM, FFN_TF
    n_chunks = d_ff // tf
    assert t % tm == 0 and d_ff % tf == 0
    final_norm = final_g is not None
    extra = [final_g.reshape(1, d)] if final_norm else []
    return pl.pallas_call(
        functools.partial(_ffn_kernel, n_chunks=n_chunks, tf=tf, final_norm=final_norm),
        grid=(t // tm,),
        in_specs=[
            pl.BlockSpec((tm, d), lambda i: (i, 0)),
            _resident((1, d)),
            _resident((None, d, 2 * d_ff), (layer, 0, 0)),
            _resident((None, d_ff, d), (layer, 0, 0)),
        ] + [_resident((1, d))] * len(extra),
        out_specs=pl.BlockSpec((tm, d), lambda i: (i, 0)),
        out_shape=jax.ShapeDtypeStruct((t, d), F32),
        scratch_shapes=[pltpu.VMEM((tm, d), BF16), pltpu.VMEM((tm, d_ff), BF16),
                        pltpu.VMEM((d_ff, d), BF16)],
        compiler_params=_params(1),
        name="ffn",
    )(x, norm_g.reshape(1, d), w_gu, w_down, *extra)


def _norm_proj_kernel(x_ref, g_ref, *refs, n_out, tn):
    w_refs, o_refs = refs[:n_out], refs[n_out:]
    xn = _rms(x_ref[...], g_ref[...]).astype(BF16)
    for w_ref, o_ref in zip(w_refs, o_refs):
        n = w_ref.shape[1]
        for n0 in range(0, n, tn):
            n1 = min(n0 + tn, n)
            o_ref[:, n0:n1] = _dot(xn, w_ref[:, n0:n1]).astype(o_ref.dtype)


def _norm_proj(x, norm_g, weights, out_dtypes):
    t, d = x.shape
    tm = PROJ_TM
    assert t % tm == 0
    n_out = len(weights)
    return pl.pallas_call(
        functools.partial(_norm_proj_kernel, n_out=n_out, tn=PROJ_TN),
        grid=(t // tm,),
        in_specs=[pl.BlockSpec((tm, d), lambda i: (i, 0)), _resident((1, d))]
        + [_resident(w.shape) for w in weights],
        out_specs=[pl.BlockSpec((tm, w.shape[1]), lambda i: (i, 0)) for w in weights],
        out_shape=[jax.ShapeDtypeStruct((t, w.shape[1]), dt) for w, dt in zip(weights, out_dtypes)],
        compiler_params=_params(1),
        name="norm_proj",
    )(x, norm_g.reshape(1, d), *weights)


def _proj_res_kernel(x_ref, y_ref, w_ref, o_ref):
    o_ref[...] = x_ref[...] + _dot(y_ref[...], w_ref[...])


def _proj_res(x, y, w):
    t, d = x.shape
    k = y.shape[1]
    tm = PROJ_TM
    assert t % tm == 0
    return pl.pallas_call(
        _proj_res_kernel,
        grid=(t // tm,),
        in_specs=[
            pl.BlockSpec((tm, d), lambda i: (i, 0)),
            pl.BlockSpec((tm, k), lambda i: (i, 0)),
            _resident((k, d)),
        ],
        out_specs=pl.BlockSpec((tm, d), lambda i: (i, 0)),
        out_shape=jax.ShapeDtypeStruct((t, d), F32),
        compiler_params=_params(1),
        name="proj_res",
    )(x, y, w)


def _sb_attn_kernel(q_ref, k_ref, v_ref, o_ref, lb_a, sp_a, rs_a, lb_b, sp_b, rs_b, acc_ref, c_ref,
                    *, t):
    qi = pl.program_id(2)
    hd = SB_HEAD_DIM
    buf_a, buf_b = (lb_a, sp_a, rs_a), (lb_b, sp_b, rs_b)
    both, first, second = (0, 2), (0, 1), (1, 2)
    in_a = lax.broadcasted_iota(jnp.int32, (1, 2 * hd), 1) < hd
    q_parts = []
    for s in range(2):
        q2 = q_ref[0, s * t:(s + 1) * t, :]
        q_parts += [jnp.where(in_a, q2, 0), jnp.where(in_a, 0, q2)]
    qs = jnp.concatenate(q_parts, axis=0)
    row = lax.broadcasted_iota(jnp.int32, (t, t), 0)
    col = lax.broadcasted_iota(jnp.int32, (t, t), 1)
    strict_lower = jnp.concatenate([col < row] * 2, axis=0)
    tri = jnp.where(row > col, 1.0, 0.0).astype(BF16)
    sign_bit = jnp.uint32(0x80000000)

    def mask_diagonal(x, fill):
        head = jnp.where(strict_lower, x[:2 * t], fill)
        return head if x.shape[0] == 2 * t else jnp.concatenate([head, x[2 * t:]], axis=0)

    def scores(j, buf, subs, diag):
        lb_ref, sp_ref, rs_ref = buf
        r0, r1 = 2 * subs[0] * t, 2 * subs[1] * t
        k2 = k_ref[0, pl.ds(pl.multiple_of(j * t, t), t), :]
        z = _dot_nt(qs[r0:r1], k2)
        neg_abs = lax.bitcast_convert_type(lax.bitcast_convert_type(z, jnp.uint32) | sign_bit, F32)
        lb = jnp.minimum(z, 0.0) - jnp.log2(1.0 + jnp.exp2(neg_abs))
        sp = z - lb
        if diag:
            sp, lb = mask_diagonal(sp, 0.0), mask_diagonal(lb, -jnp.inf)
        lb_ref[r0:r1, :] = lb
        sp_ref[r0:r1, :] = sp.astype(BF16)
        rs_ref[r0:r1, :] = jnp.broadcast_to(jnp.sum(sp, axis=-1, keepdims=True), (r1 - r0, LANES))

    def values(j, buf, subs):
        lb_ref, sp_ref, rs_ref = buf
        r0, r1 = 2 * subs[0] * t, 2 * subs[1] * t
        v2 = v_ref[0, pl.ds(pl.multiple_of(j * t, t), t), :]
        c = c_ref[r0:r1, :]
        tail = _dot(sp_ref[r0:r1, :], tri)
        c_wide = jnp.concatenate([c] * (t // LANES), axis=1)
        att = jnp.exp2(lb_ref[r0:r1, :] - tail - c_wide).astype(BF16)
        c_ref[r0:r1, :] = c + rs_ref[r0:r1, :]
        att2 = jnp.concatenate(
            [jnp.concatenate([att[(2 * s) * t:(2 * s + 1) * t], att[(2 * s + 1) * t:(2 * s + 2) * t]],
                             axis=1) for s in range(subs[1] - subs[0])], axis=0)
        vv = jnp.concatenate([jnp.where(in_a, v2, 0), jnp.where(in_a, 0, v2)], axis=0)
        acc_ref[subs[0] * t:subs[1] * t, :] += _dot(att2, vv)

    def tile_pair(state):
        i = state[0]
        j = top - 3 - 2 * i
        values(j, buf_b, both)
        scores(j - 1, buf_a, both, False)
        values(j - 1, buf_a, both)
        scores(j - 2, buf_b, both, False)
        return i + 1, jnp.min(c_ref[...])

    def live(state):
        return (state[0] < qi - 1) & (state[1] < ATT_DEAD_CARRY)

    acc_ref[...] = jnp.zeros_like(acc_ref)
    c_ref[...] = jnp.zeros_like(c_ref)
    top = 2 * qi + 1

    def diagonal_tiles():
        scores(top, buf_a, second, True)
        values(top, buf_a, second)
        scores(top - 1, buf_b, both, True)
        values(top - 1, buf_b, both)

    @pl.when(qi == 0)
    def _():
        diagonal_tiles()

    @pl.when(qi > 0)
    def _():
        diagonal_tiles()
        scores(top - 2, buf_a, first, False)
        values(top - 2, buf_a, first)
        c_min0 = jnp.min(c_ref[...])

        @pl.when(c_min0 < ATT_DEAD_CARRY)
        def _():
            scores(top - 2, buf_a, second, False)
            values(top - 2, buf_a, second)
            scores(top - 3, buf_b, both, False)
            n_pairs, c_min = lax.while_loop(live, tile_pair, (jnp.int32(0), c_min0))

            @pl.when(c_min < ATT_DEAD_CARRY)
            def _():
                values(top - 3 - 2 * n_pairs, buf_b, both)

    o_ref[0] = acc_ref[...].astype(o_ref.dtype)


def _sb_attention(q, k, v):
    b, l, d = q.shape
    t = ATT_T
    tq = 2 * t
    hp = 2 * SB_HEAD_DIM
    assert l % tq == 0 and d % hp == 0 and hp == LANES
    rows = 4 * t
    stage = [pltpu.VMEM((rows, t), F32), pltpu.VMEM((rows, t), BF16), pltpu.VMEM((rows, LANES), F32)]
    return pl.pallas_call(
        functools.partial(_sb_attn_kernel, t=t),
        grid=(b, d // hp, l // tq),
        in_specs=[
            pl.BlockSpec((1, tq, hp), lambda bi, hi, qi: (bi, qi, hi)),
            pl.BlockSpec((1, l, hp), lambda bi, hi, qi: (bi, 0, hi)),
            pl.BlockSpec((1, l, hp), lambda bi, hi, qi: (bi, 0, hi)),
        ],
        out_specs=pl.BlockSpec((1, tq, hp), lambda bi, hi, qi: (bi, qi, hi)),
        out_shape=jax.ShapeDtypeStruct((b, l, d), BF16),
        scratch_shapes=stage + stage + [pltpu.VMEM((tq, hp), F32), pltpu.VMEM((rows, LANES), F32)],
        compiler_params=_params(3),
        name="sb_attn",
    )(q, k, v)


def _sb_layer(x, b, l, norm_g, w_qkv, w_o):
    t, d = x.shape
    scale = LOG2_E * SB_HEAD_DIM ** -0.5
    wq = (w_qkv[:, :d] * scale).astype(BF16)
    wk = w_qkv[:, d:2 * d].astype(BF16)
    wv = w_qkv[:, 2 * d:].astype(BF16)
    q, k, v = _norm_proj(x, norm_g, [wq, wk, wv], [BF16] * 3)
    o = _sb_attention(q.reshape(b, l, d), k.reshape(b, l, d), v.reshape(b, l, d))
    return _proj_res(x, o.reshape(t, d), w_o.astype(BF16))


def _expand_heads(m, width):
    rows = m.shape[0]
    lane = lax.broadcasted_iota(jnp.int32, (rows, SSD_HPG * width), 1)
    out = jnp.broadcast_to(m[:, SSD_HPG - 1:SSD_HPG], (rows, SSD_HPG * width))
    for r in range(SSD_HPG - 2, -1, -1):
        out = jnp.where(lane < (r + 1) * width, m[:, r:r + 1], out)
    return out


def _to_columns(rows8):
    n = rows8.shape[1]
    padded = jnp.concatenate([rows8, jnp.zeros((n - rows8.shape[0], n), F32)], axis=0)
    return padded.T[:, :rows8.shape[0]]


def _ssd_kernel(xr_ref, br_ref, cr_ref, z_ref, dt_ref, dtn_ref, wx_ref, wb_ref, wc_ref, bx_ref,
                bb_ref, bc_ref, dtb_ref, alog_ref, dsk_ref, ng_ref, o_ref, h_ref, hist_ref,
                steps_row_ref, steps_col_ref, *, n_groups):
    c = pl.program_id(2)
    cl, hp, n = SSD_CHUNK, SSD_HPG * SSD_HEAD_DIM, SSD_STATE
    rows = n_groups * SUBLANES

    def stage_step_sizes(dt_blk):
        dt_in = (dt_blk + dtb_ref[...]).reshape(rows, cl)
        dt_all = jnp.maximum(dt_in, 0.0) + jnp.log(1.0 + jnp.exp(-jnp.abs(dt_in)))
        a_all = dt_all * (-jnp.exp(alog_ref[...].reshape(rows, 1)))
        lane_cl = lax.broadcasted_iota(jnp.int32, (rows, cl), 1)
        sh = 1
        while sh < cl:
            a_all = a_all + jnp.where(lane_cl >= sh, pltpu.roll(a_all, sh, axis=1), 0.0)
            sh *= 2
        te_all = jnp.exp(a_all[:, cl - 1:cl] - a_all) * dt_all
        steps_row_ref[0], steps_row_ref[1] = dt_all, a_all
        steps_col_ref[0], steps_col_ref[1] = _to_columns(a_all), _to_columns(te_all)

    @pl.when(c == 0)
    def _():
        h_ref[...] = jnp.zeros_like(h_ref)
        hist_ref[...] = jnp.zeros_like(hist_ref)
        stage_step_sizes(dt_ref[0])

    dt_all, a_all = steps_row_ref[0], steps_row_ref[1]
    a_col_all, te_col_all = steps_col_ref[0], steps_col_ref[1]
    stage_step_sizes(dtn_ref[0])

    raw = jnp.concatenate([xr_ref[...], br_ref[...], cr_ref[...]], axis=1)
    ext = jnp.concatenate([hist_ref[...], raw], axis=0)
    hist_ref[...] = raw[cl - HALO:, :]
    cw = jnp.concatenate([wx_ref[...], wb_ref[...], wc_ref[...]], axis=1)
    acc = cw[0:1, :] * ext
    for kk in range(1, SSD_CONV):
        acc = cw[kk:kk + 1, :] * ext + pltpu.roll(acc, 1, axis=0)
    bias = jnp.concatenate([bx_ref[...], bb_ref[...], bc_ref[...]], axis=1)
    act = _silu(acc[HALO:, :] + bias)
    xs_all = act[:, :n_groups * hp]
    bm_all = act[:, n_groups * hp:n_groups * (hp + n)].astype(BF16)
    cm_all = act[:, n_groups * (hp + n):].astype(BF16)

    row = lax.broadcasted_iota(jnp.int32, (cl, cl), 0)
    col = lax.broadcasted_iota(jnp.int32, (cl, cl), 1)
    causal = col <= row
    lane_hp = lax.broadcasted_iota(jnp.int32, (cl, hp), 1)

    for g in range(n_groups):
        xs = xs_all[:, g * hp:(g + 1) * hp]
        bm = bm_all[:, g * n:(g + 1) * n]
        cm = cm_all[:, g * n:(g + 1) * n]
        dt = dt_all[g * SUBLANES:(g + 1) * SUBLANES]
        a_cum = a_all[g * SUBLANES:(g + 1) * SUBLANES]
        a_col = a_col_all[:, g * SUBLANES:(g + 1) * SUBLANES]
        te_col = te_col_all[:, g * SUBLANES:(g + 1) * SUBLANES]

        cb = _dot_nt(cm, bm)
        ws, xblocks = [], []
        for r in range(SSD_HPG):
            seg = a_col[:, r:r + 1] - a_cum[r:r + 1, :]
            decay = jnp.exp(jnp.where(causal, seg, -jnp.inf))
            ws.append((cb * decay * dt[r:r + 1, :]).astype(BF16))
            in_head = (lane_hp >= r * SSD_HEAD_DIM) & (lane_hp < (r + 1) * SSD_HEAD_DIM)
            xblocks.append(jnp.where(in_head, xs, 0.0).astype(BF16))
        y = _dot(jnp.concatenate(ws, axis=1), jnp.concatenate(xblocks, axis=0))

        h_prev = h_ref[g]
        y = y + _dot(cm, h_prev.astype(BF16)) * _expand_heads(jnp.exp(a_col), SSD_HEAD_DIM)
        xw = (xs * _expand_heads(te_col, SSD_HEAD_DIM)).astype(BF16)
        h_decay = _expand_heads(jnp.exp(a_col[cl - 1:cl, :]), SSD_HEAD_DIM)
        h_ref[g] = h_prev * h_decay + _dot_tn(bm, xw)

        y = (y + dsk_ref[g] * xs) * _silu(z_ref[:, g * hp:(g + 1) * hp])
        o_ref[:, g * hp:(g + 1) * hp] = _rms(y, ng_ref[g]).astype(o_ref.dtype)


def _ssd_core(xbc, z, dt_t, conv_w, conv_b, dt_bias, a_log, d_skip, norm_g, b, l):
    t = xbc.shape[0]
    g, hpg, cl, n, ng = SSD_GROUPS, SSD_HPG, SSD_CHUNK, SSD_STATE, SSD_GROUPS_PER_STEP
    hp = hpg * SSD_HEAD_DIM
    d_inner = g * hp
    nc = l // cl
    assert l % cl == 0 and hp % LANES == 0 and n == LANES and g % ng == 0
    bb, cb = d_inner // (ng * n), (d_inner + g * n) // (ng * n)
    pad = lambda p: jnp.pad(p.reshape(g, hpg, 1), ((0, 0), (0, SUBLANES - hpg), (0, 0)))
    per_lane = lambda p: jnp.repeat(p.reshape(g, hpg), SSD_HEAD_DIM, axis=1).reshape(g, 1, hp)
    conv_b = conv_b.reshape(1, -1)
    tok = lambda bi, gi, ci: bi * nc + ci
    return pl.pallas_call(
        functools.partial(_ssd_kernel, n_groups=ng),
        grid=(b, g // ng, nc),
        in_specs=[
            pl.BlockSpec((cl, ng * hp), lambda bi, gi, ci: (tok(bi, gi, ci), gi)),
            pl.BlockSpec((cl, ng * n), lambda bi, gi, ci: (tok(bi, gi, ci), bb + gi)),
            pl.BlockSpec((cl, ng * n), lambda bi, gi, ci: (tok(bi, gi, ci), cb + gi)),
            pl.BlockSpec((cl, ng * hp), lambda bi, gi, ci: (tok(bi, gi, ci), gi)),
            pl.BlockSpec((1, ng, SUBLANES, cl), lambda bi, gi, ci: (bi, gi, 0, ci)),
            pl.BlockSpec((1, ng, SUBLANES, cl), lambda bi, gi, ci: (bi, gi, 0, jnp.minimum(ci + 1, nc - 1))),
            pl.BlockSpec((SSD_CONV, ng * hp), lambda bi, gi, ci: (0, gi)),
            pl.BlockSpec((SSD_CONV, ng * n), lambda bi, gi, ci: (0, bb + gi)),
            pl.BlockSpec((SSD_CONV, ng * n), lambda bi, gi, ci: (0, cb + gi)),
            pl.BlockSpec((1, ng * hp), lambda bi, gi, ci: (0, gi)),
            pl.BlockSpec((1, ng * n), lambda bi, gi, ci: (0, bb + gi)),
            pl.BlockSpec((1, ng * n), lambda bi, gi, ci: (0, cb + gi)),
            pl.BlockSpec((ng, SUBLANES, 1), lambda bi, gi, ci: (gi, 0, 0)),
            pl.BlockSpec((ng, SUBLANES, 1), lambda bi, gi, ci: (gi, 0, 0)),
            pl.BlockSpec((ng, 1, hp), lambda bi, gi, ci: (gi, 0, 0)),
            pl.BlockSpec((ng, 1, hp), lambda bi, gi, ci: (gi, 0, 0)),
        ],
        out_specs=pl.BlockSpec((cl, ng * hp), lambda bi, gi, ci: (tok(bi, gi, ci), gi)),
        out_shape=jax.ShapeDtypeStruct((t, d_inner), BF16),
        scratch_shapes=[pltpu.VMEM((ng, n, hp), F32), pltpu.VMEM((HALO, ng * (hp + 2 * n)), F32),
                        pltpu.VMEM((2, ng * SUBLANES, cl), F32), pltpu.VMEM((2, cl, ng * SUBLANES), F32)],
        compiler_params=_params(3),
        name="ssd_core",
    )(xbc, xbc, xbc, z, dt_t, dt_t, conv_w, conv_w, conv_w, conv_b, conv_b, conv_b,
      pad(dt_bias), pad(a_log), per_lane(d_skip), norm_g.reshape(g, 1, hp))


def _ssd_layer(x, b, l, norm_g, w_in, conv_w, conv_b, dt_bias, a_log, d_skip, ssd_norm, w_out):
    g, hpg = SSD_GROUPS, SSD_HPG
    d_inner = w_out.shape[0]
    conv_dim = conv_w.shape[1]
    n_heads = g * hpg
    w_in = w_in.astype(BF16)
    w_z = w_in[:, :d_inner]
    w_xbc = w_in[:, d_inner:d_inner + conv_dim]
    w_dt = jnp.pad(w_in[:, d_inner + conv_dim:], ((0, 0), (0, LANES - n_heads)))
    z, xbc, dt = _norm_proj(x, norm_g, [w_z, w_xbc, w_dt], [F32] * 3)
    dt_t = dt[:, :n_heads].reshape(b, l, g, hpg).transpose(0, 2, 3, 1)
    dt_t = jnp.pad(dt_t, ((0, 0), (0, 0), (0, SUBLANES - hpg), (0, 0)))
    y = _ssd_core(xbc, z, dt_t, conv_w, conv_b, dt_bias, a_log, d_skip, ssd_norm, b, l)
    return _proj_res(x, y, w_out.astype(BF16))


def _short_conv_kernel(x_ref, g_ref, wb_ref, wc_ref, wh_ref, cw_ref, wo_ref, o_ref, ext_ref):
    tm = x_ref.shape[0]

    @pl.when(pl.program_id(1) == 0)
    def _():
        ext_ref[0:HALO, :] = jnp.zeros((HALO, ext_ref.shape[1]), F32)

    x = x_ref[...]
    xn = _rms(x, g_ref[...]).astype(BF16)
    ext_ref[HALO:HALO + tm, :] = _dot(xn, wc_ref[...]) * _dot(xn, wh_ref[...])
    cw = cw_ref[...]
    u = cw[0:1, :] * ext_ref[pl.ds(HALO - (SC_WIDTH - 1), tm), :]
    for kk in range(1, SC_WIDTH):
        u = u + cw[kk:kk + 1, :] * ext_ref[pl.ds(HALO - (SC_WIDTH - 1) + kk, tm), :]
    ext_ref[0:HALO, :] = ext_ref[tm:tm + HALO, :]
    gated = (_dot(xn, wb_ref[...]) * u).astype(BF16)
    o_ref[...] = x + _dot(gated, wo_ref[...])


def _short_conv_layer(x, b, l, norm_g, w_in, conv_w, w_out):
    t, d = x.shape
    tm = SC_TM
    nl = l // tm
    assert l % tm == 0
    w_in = w_in.astype(BF16)
    return pl.pallas_call(
        _short_conv_kernel,
        grid=(b, nl),
        in_specs=[
            pl.BlockSpec((tm, d), lambda bi, li: (bi * nl + li, 0)),
            _resident((1, d)),
            _resident((d, d)), _resident((d, d)), _resident((d, d)),
            _resident((SC_WIDTH, d)),
            _resident((d, d)),
        ],
        out_specs=pl.BlockSpec((tm, d), lambda bi, li: (bi * nl + li, 0)),
        out_shape=jax.ShapeDtypeStruct((t, d), F32),
        scratch_shapes=[pltpu.VMEM((tm + HALO, d), F32)],
        compiler_params=_params(2),
        name="short_conv",
    )(x, norm_g.reshape(1, d), w_in[:, :d], w_in[:, d:2 * d], w_in[:, 2 * d:], conv_w,
      w_out.astype(BF16))


def kernel(x, ffn1_norm, ffn1_w_gu, ffn1_w_down, mix_norm, ffn2_norm, ffn2_w_gu, ffn2_w_down,
           sb_w_qkv, sb_w_o, ssd_w_in, ssd_conv_w, ssd_conv_b, ssd_dt_bias, ssd_a_log, ssd_d,
           ssd_norm, ssd_w_out, sc_w_in, sc_conv_w, sc_w_out, final_norm):
    b, l, d = x.shape
    depth = ffn1_norm.shape[0]
    h = x.reshape(b * l, d)
    for i in range(depth):
        h = _ffn(h, ffn1_norm[i], ffn1_w_gu, ffn1_w_down, i)
        kind, j = i % N_MIXERS, i // N_MIXERS
        if kind == 0:
            h = _sb_layer(h, b, l, mix_norm[i], sb_w_qkv[j], sb_w_o[j])
        elif kind == 1:
            h = _ssd_layer(h, b, l, mix_norm[i], ssd_w_in[j], ssd_conv_w[j], ssd_conv_b[j],
                           ssd_dt_bias[j], ssd_a_log[j], ssd_d[j], ssd_norm[j], ssd_w_out[j])
        else:
            h = _short_conv_layer(h, b, l, mix_norm[i], sc_w_in[j], sc_conv_w[j], sc_w_out[j])
        h = _ffn(h, ffn2_norm[i], ffn2_w_gu, ffn2_w_down, i,
                 final_g=final_norm if i == depth - 1 else None)
    return h.reshape(b, l, d)
```

```python
import functools

import jax
import jax.numpy as jnp
from jax import lax
from jax.experimental import pallas as pl
from jax.experimental.pallas import tpu as pltpu

F32 = jnp.float32
BF16 = jnp.bfloat16

RMS_EPS = 1e-6
LOG2_E = 1.4426950408889634
N_MIXERS = 3
SB_HEAD_DIM = 64
SSD_HEAD_DIM = 64
SSD_GROUPS = 8
SSD_HPG = 4
SSD_STATE = 128
SSD_CONV = 4
SSD_CHUNK = 128
SC_WIDTH = 3

LANES = 128
SUBLANES = 8
VMEM_LIMIT_BYTES = 60 * 1024 * 1024

FFN_TM = 512
FFN_TF = 256
PROJ_TM = 512
PROJ_TN = 512
ATT_T = 256
ATT_PAIRS_PER_STEP = 4
ATT_DEAD_CARRY = 160.0
SSD_GROUPS_PER_STEP = 4
SC_TM = 256
HALO = SUBLANES


def _params(n_axes):
    return pltpu.CompilerParams(
        dimension_semantics=("arbitrary",) * n_axes,
        vmem_limit_bytes=VMEM_LIMIT_BYTES)


def _resident(shape, index=None):
    index = (0,) * len(shape) if index is None else index
    return pl.BlockSpec(shape, lambda *_: index, pipeline_mode=pl.Buffered(1))


def _rms(x, g):
    ms = jnp.mean(x * x, axis=-1, keepdims=True)
    return x * lax.rsqrt(ms + RMS_EPS) * g


def _silu(x):
    return x * jax.nn.sigmoid(x)


def _dot(a, b):
    return jnp.dot(a, b, preferred_element_type=F32)


def _dot_nt(a, b):
    return lax.dot_general(a, b, (((1,), (1,)), ((), ())), preferred_element_type=F32)


def _dot_tn(a, b):
    return lax.dot_general(a, b, (((0,), (0,)), ((), ())), preferred_element_type=F32)


def _ffn_kernel(x_ref, g_ref, wgu_ref, wd_ref, *refs, n_chunks, tf, final_norm):
    fg_ref = refs[0] if final_norm else None
    o_ref, xn_ref, h_ref, wdb_ref = refs[-4:]
    x = x_ref[...]
    d_ff = n_chunks * tf
    xn_ref[...] = _rms(x, g_ref[...]).astype(BF16)
    for c in range(n_chunks):
        gate = _dot(xn_ref[...], wgu_ref[:, c * tf:(c + 1) * tf].astype(BF16))
        up = _dot(xn_ref[...], wgu_ref[:, d_ff + c * tf:d_ff + (c + 1) * tf].astype(BF16))
        h_ref[:, c * tf:(c + 1) * tf] = (_silu(gate) * up).astype(BF16)
        wdb_ref[c * tf:(c + 1) * tf, :] = wd_ref[c * tf:(c + 1) * tf, :].astype(BF16)
    y = x + 0.5 * _dot(h_ref[...], wdb_ref[...])
    if final_norm:
        y = _rms(y, fg_ref[...])
    o_ref[...] = y


def _ffn(x, norm_g, w_gu, w_down, layer, final_g=None):
    t, d = x.shape
    d_ff = w_down.shape[1]
    tm, tf = FFN_TM, FFN_TF
    n_chunks = d_ff // tf
    assert t % tm == 0 and d_ff % tf == 0
    final_norm = final_g is not None
    extra = [final_g.reshape(1, d)] if final_norm else []
    return pl.pallas_call(
        functools.partial(_ffn_kernel, n_chunks=n_chunks, tf=tf, final_norm=final_norm),
        grid=(t // tm,),
        in_specs=[
            pl.BlockSpec((tm, d), lambda i: (i, 0)),
            _resident((1, d)),
            _resident((None, d, 2 * d_ff), (layer, 0, 0)),
            _resident((None, d_ff, d), (layer, 0, 0)),
        ] + [_resident((1, d))] * len(extra),
        out_specs=pl.BlockSpec((tm, d), lambda i: (i, 0)),
        out_shape=jax.ShapeDtypeStruct((t, d), F32),
        scratch_shapes=[pltpu.VMEM((tm, d), BF16), pltpu.VMEM((tm, d_ff), BF16),
                        pltpu.VMEM((d_ff, d), BF16)],
        compiler_params=_params(1),
        name="ffn",
    )(x, norm_g.reshape(1, d), w_gu, w_down, *extra)


def _norm_proj_kernel(x_ref, g_ref, *refs, n_out, tn):
    w_refs, o_refs = refs[:n_out], refs[n_out:]
    xn = _rms(x_ref[...], g_ref[...]).astype(BF16)
    for w_ref, o_ref in zip(w_refs, o_refs):
        n = w_ref.shape[1]
        for n0 in range(0, n, tn):
            n1 = min(n0 + tn, n)
            o_ref[:, n0:n1] = _dot(xn, w_ref[:, n0:n1]).astype(o_ref.dtype)


def _norm_proj(x, norm_g, weights, out_dtypes):
    t, d = x.shape
    tm = PROJ_TM
    assert t % tm == 0
    n_out = len(weights)
    return pl.pallas_call(
        functools.partial(_norm_proj_kernel, n_out=n_out, tn=PROJ_TN),
        grid=(t // tm,),
        in_specs=[pl.BlockSpec((tm, d), lambda i: (i, 0)), _resident((1, d))]
        + [_resident(w.shape) for w in weights],
        out_specs=[pl.BlockSpec((tm, w.shape[1]), lambda i: (i, 0)) for w in weights],
        out_shape=[jax.ShapeDtypeStruct((t, w.shape[1]), dt) for w, dt in zip(weights, out_dtypes)],
        compiler_params=_params(1),
        name="norm_proj",
    )(x, norm_g.reshape(1, d), *weights)


def _proj_res_kernel(x_ref, y_ref, w_ref, o_ref):
    o_ref[...] = x_ref[...] + _dot(y_ref[...], w_ref[...])


def _proj_res(x, y, w):
    t, d = x.shape
    k = y.shape[1]
    tm = PROJ_TM
    assert t % tm == 0
    return pl.pallas_call(
        _proj_res_kernel,
        grid=(t // tm,),
        in_specs=[
            pl.BlockSpec((tm, d), lambda i: (i, 0)),
            pl.BlockSpec((tm, k), lambda i: (i, 0)),
            _resident((k, d)),
        ],
        out_specs=pl.BlockSpec((tm, d), lambda i: (i, 0)),
        out_shape=jax.ShapeDtypeStruct((t, d), F32),
        compiler_params=_params(1),
        name="proj_res",
    )(x, y, w)


def _sb_attn_kernel(q_ref, k_ref, v_ref, o_ref, lb_a, sp_a, rs_a, lb_b, sp_b, rs_b, acc_ref, c_ref,
                    *, t):
    qi = pl.program_id(2)
    hd = SB_HEAD_DIM
    buf_a, buf_b = (lb_a, sp_a, rs_a), (lb_b, sp_b, rs_b)
    both, first, second = (0, 2), (0, 1), (1, 2)
    in_a = lax.broadcasted_iota(jnp.int32, (1, 2 * hd), 1) < hd
    row = lax.broadcasted_iota(jnp.int32, (t, t), 0)
    col = lax.broadcasted_iota(jnp.int32, (t, t), 1)
    strict_lower = jnp.concatenate([col < row] * 2, axis=0)
    tri = jnp.where(row > col, 1.0, 0.0).astype(BF16)
    sign_bit = jnp.uint32(0x80000000)
    top = 2 * qi + 1

    def mask_diagonal(x, fill):
        head = jnp.where(strict_lower, x[:2 * t], fill)
        return head if x.shape[0] == 2 * t else jnp.concatenate([head, x[2 * t:]], axis=0)

    def head_pair(hp_idx, _):
        lanes = pl.ds(pl.multiple_of(hp_idx * 2 * hd, 2 * hd), 2 * hd)
        q_parts = []
        for s in range(2):
            q2 = q_ref[0, s * t:(s + 1) * t, lanes]
            q_parts += [jnp.where(in_a, q2, 0), jnp.where(in_a, 0, q2)]
        qs = jnp.concatenate(q_parts, axis=0)
        _sb_head_pair(qi, top, qs, k_ref, v_ref, lanes, buf_a, buf_b, acc_ref, c_ref, in_a, tri,
                      sign_bit, mask_diagonal, (both, first, second), t)
        o_ref[0, :, lanes] = acc_ref[...].astype(o_ref.dtype)
        return 0

    lax.fori_loop(0, q_ref.shape[2] // (2 * hd), head_pair, 0)


def _sb_head_pair(qi, top, qs, k_ref, v_ref, lanes, buf_a, buf_b, acc_ref, c_ref, in_a, tri, sign_bit,
                  mask_diagonal, sub_ranges, t):
    both, first, second = sub_ranges

    def scores(j, buf, subs, diag):
        lb_ref, sp_ref, rs_ref = buf
        r0, r1 = 2 * subs[0] * t, 2 * subs[1] * t
        k2 = k_ref[0, pl.ds(pl.multiple_of(j * t, t), t), lanes]
        z = _dot_nt(qs[r0:r1], k2)
        neg_abs = lax.bitcast_convert_type(lax.bitcast_convert_type(z, jnp.uint32) | sign_bit, F32)
        lb = jnp.minimum(z, 0.0) - jnp.log2(1.0 + jnp.exp2(neg_abs))
        sp = z - lb
        if diag:
            sp, lb = mask_diagonal(sp, 0.0), mask_diagonal(lb, -jnp.inf)
        lb_ref[r0:r1, :] = lb
        sp_ref[r0:r1, :] = sp.astype(BF16)
        rs_ref[r0:r1, :] = jnp.broadcast_to(jnp.sum(sp, axis=-1, keepdims=True), (r1 - r0, LANES))

    def values(j, buf, subs):
        lb_ref, sp_ref, rs_ref = buf
        r0, r1 = 2 * subs[0] * t, 2 * subs[1] * t
        v2 = v_ref[0, pl.ds(pl.multiple_of(j * t, t), t), lanes]
        c = c_ref[r0:r1, :]
        tail = _dot(sp_ref[r0:r1, :], tri)
        c_wide = jnp.concatenate([c] * (t // LANES), axis=1)
        att = jnp.exp2(lb_ref[r0:r1, :] - tail - c_wide).astype(BF16)
        c_ref[r0:r1, :] = c + rs_ref[r0:r1, :]
        att2 = jnp.concatenate(
            [jnp.concatenate([att[(2 * s) * t:(2 * s + 1) * t], att[(2 * s + 1) * t:(2 * s + 2) * t]],
                             axis=1) for s in range(subs[1] - subs[0])], axis=0)
        vv = jnp.concatenate([jnp.where(in_a, v2, 0), jnp.where(in_a, 0, v2)], axis=0)
        acc_ref[subs[0] * t:subs[1] * t, :] += _dot(att2, vv)

    def tile_pair(state):
        i = state[0]
        j = top - 3 - 2 * i
        values(j, buf_b, both)
        scores(j - 1, buf_a, both, False)
        values(j - 1, buf_a, both)
        scores(j - 2, buf_b, both, False)
        return i + 1, jnp.min(c_ref[...])

    def live(state):
        return (state[0] < qi - 1) & (state[1] < ATT_DEAD_CARRY)

    acc_ref[...] = jnp.zeros_like(acc_ref)
    c_ref[...] = jnp.zeros_like(c_ref)

    @pl.when(qi == 0)
    def _():
        scores(top, buf_a, second, True)
        values(top, buf_a, second)
        scores(top - 1, buf_b, both, True)
        values(top - 1, buf_b, both)

    @pl.when(qi > 0)
    def _():
        scores(top, buf_a, second, True)
        scores(top - 1, buf_b, both, True)
        values(top, buf_a, second)
        scores(top - 2, buf_a, first, False)
        values(top - 1, buf_b, both)
        values(top - 2, buf_a, first)
        c_min0 = jnp.min(c_ref[...])

        @pl.when(c_min0 < ATT_DEAD_CARRY)
        def _():
            scores(top - 2, buf_a, second, False)
            values(top - 2, buf_a, second)
            scores(top - 3, buf_b, both, False)
            n_pairs, c_min = lax.while_loop(live, tile_pair, (jnp.int32(0), c_min0))

            @pl.when(c_min < ATT_DEAD_CARRY)
            def _():
                values(top - 3 - 2 * n_pairs, buf_b, both)


def _sb_attention(q, k, v):
    b, l, d = q.shape
    t = ATT_T
    tq = 2 * t
    hp = 2 * SB_HEAD_DIM * ATT_PAIRS_PER_STEP
    assert l % tq == 0 and d % hp == 0 and 2 * SB_HEAD_DIM == LANES
    rows = 4 * t
    stage = [pltpu.VMEM((rows, t), F32), pltpu.VMEM((rows, t), BF16), pltpu.VMEM((rows, LANES), F32)]
    return pl.pallas_call(
        functools.partial(_sb_attn_kernel, t=t),
        grid=(b, d // hp, l // tq),
        in_specs=[
            pl.BlockSpec((1, tq, hp), lambda bi, hi, qi: (bi, qi, hi)),
            pl.BlockSpec((1, l, hp), lambda bi, hi, qi: (bi, 0, hi)),
            pl.BlockSpec((1, l, hp), lambda bi, hi, qi: (bi, 0, hi)),
        ],
        out_specs=pl.BlockSpec((1, tq, hp), lambda bi, hi, qi: (bi, qi, hi)),
        out_shape=jax.ShapeDtypeStruct((b, l, d), BF16),
        scratch_shapes=stage + stage + [pltpu.VMEM((tq, LANES), F32), pltpu.VMEM((rows, LANES), F32)],
        compiler_params=_params(3),
        name="sb_attn",
    )(q, k, v)


def _sb_layer(x, b, l, norm_g, w_qkv, w_o):
    t, d = x.shape
    scale = LOG2_E * SB_HEAD_DIM ** -0.5
    wq = (w_qkv[:, :d] * scale).astype(BF16)
    wk = w_qkv[:, d:2 * d].astype(BF16)
    wv = w_qkv[:, 2 * d:].astype(BF16)
    q, k, v = _norm_proj(x, norm_g, [wq, wk, wv], [BF16] * 3)
    o = _sb_attention(q.reshape(b, l, d), k.reshape(b, l, d), v.reshape(b, l, d))
    return _proj_res(x, o.reshape(t, d), w_o.astype(BF16))


def _expand_heads(m, width):
    rows = m.shape[0]
    lane = lax.broadcasted_iota(jnp.int32, (rows, SSD_HPG * width), 1)
    out = jnp.broadcast_to(m[:, SSD_HPG - 1:SSD_HPG], (rows, SSD_HPG * width))
    for r in range(SSD_HPG - 2, -1, -1):
        out = jnp.where(lane < (r + 1) * width, m[:, r:r + 1], out)
    return out


def _to_columns(rows8):
    n = rows8.shape[1]
    padded = jnp.concatenate([rows8, jnp.zeros((n - rows8.shape[0], n), F32)], axis=0)
    return padded.T[:, :rows8.shape[0]]


def _ssd_kernel(xr_ref, br_ref, cr_ref, z_ref, dt_ref, dtn_ref, wx_ref, wb_ref, wc_ref, bx_ref,
                bb_ref, bc_ref, dtb_ref, alog_ref, dsk_ref, ng_ref, o_ref, h_ref, hist_ref,
                steps_row_ref, steps_col_ref, *, n_groups):
    c = pl.program_id(2)
    cl, hp, n = SSD_CHUNK, SSD_HPG * SSD_HEAD_DIM, SSD_STATE
    rows = n_groups * SUBLANES

    def stage_step_sizes(dt_blk):
        dt_in = (dt_blk + dtb_ref[...]).reshape(rows, cl)
        dt_all = jnp.maximum(dt_in, 0.0) + jnp.log(1.0 + jnp.exp(-jnp.abs(dt_in)))
        a_all = dt_all * (-jnp.exp(alog_ref[...].reshape(rows, 1)))
        lane_cl = lax.broadcasted_iota(jnp.int32, (rows, cl), 1)
        sh = 1
        while sh < cl:
            a_all = a_all + jnp.where(lane_cl >= sh, pltpu.roll(a_all, sh, axis=1), 0.0)
            sh *= 2
        te_all = jnp.exp(a_all[:, cl - 1:cl] - a_all) * dt_all
        steps_row_ref[0], steps_row_ref[1] = dt_all, a_all
        steps_col_ref[0], steps_col_ref[1] = _to_columns(a_all), _to_columns(te_all)

    @pl.when(c == 0)
    def _():
        h_ref[...] = jnp.zeros_like(h_ref)
        hist_ref[...] = jnp.zeros_like(hist_ref)
        stage_step_sizes(dt_ref[0])

    dt_all, a_all = steps_row_ref[0], steps_row_ref[1]
    a_col_all, te_col_all = steps_col_ref[0], steps_col_ref[1]
    stage_step_sizes(dtn_ref[0])

    raw = jnp.concatenate([xr_ref[...], br_ref[...], cr_ref[...]], axis=1)
    ext = jnp.concatenate([hist_ref[...], raw], axis=0)
    hist_ref[...] = raw[cl - HALO:, :]
    cw = jnp.concatenate([wx_ref[...], wb_ref[...], wc_ref[...]], axis=1)
    acc = cw[0:1, :] * ext
    for kk in range(1, SSD_CONV):
        acc = cw[kk:kk + 1, :] * ext + pltpu.roll(acc, 1, axis=0)
    bias = jnp.concatenate([bx_ref[...], bb_ref[...], bc_ref[...]], axis=1)
    act = _silu(acc[HALO:, :] + bias)
    xs_all = act[:, :n_groups * hp]
    bm_all = act[:, n_groups * hp:n_groups * (hp + n)].astype(BF16)
    cm_all = act[:, n_groups * (hp + n):].astype(BF16)

    row = lax.broadcasted_iota(jnp.int32, (cl, cl), 0)
    col = lax.broadcasted_iota(jnp.int32, (cl, cl), 1)
    causal = col <= row
    lane_hp = lax.broadcasted_iota(jnp.int32, (cl, hp), 1)

    for g in range(n_groups):
        xs = xs_all[:, g * hp:(g + 1) * hp]
        bm = bm_all[:, g * n:(g + 1) * n]
        cm = cm_all[:, g * n:(g + 1) * n]
        dt = dt_all[g * SUBLANES:(g + 1) * SUBLANES]
        a_cum = a_all[g * SUBLANES:(g + 1) * SUBLANES]
        a_col = a_col_all[:, g * SUBLANES:(g + 1) * SUBLANES]
        te_col = te_col_all[:, g * SUBLANES:(g + 1) * SUBLANES]

        cb = _dot_nt(cm, bm)
        ws, xblocks = [], []
        for r in range(SSD_HPG):
            seg = a_col[:, r:r + 1] - a_cum[r:r + 1, :]
            decay = jnp.exp(jnp.where(causal, seg, -jnp.inf))
            ws.append((cb * decay * dt[r:r + 1, :]).astype(BF16))
            in_head = (lane_hp >= r * SSD_HEAD_DIM) & (lane_hp < (r + 1) * SSD_HEAD_DIM)
            xblocks.append(jnp.where(in_head, xs, 0.0).astype(BF16))
        y = _dot(jnp.concatenate(ws, axis=1), jnp.concatenate(xblocks, axis=0))

        h_prev = h_ref[g]
        y = y + _dot(cm, h_prev.astype(BF16)) * _expand_heads(jnp.exp(a_col), SSD_HEAD_DIM)
        xw = (xs * _expand_heads(te_col, SSD_HEAD_DIM)).astype(BF16)
        h_decay = _expand_heads(jnp.exp(a_col[cl - 1:cl, :]), SSD_HEAD_DIM)
        h_ref[g] = h_prev * h_decay + _dot_tn(bm, xw)

        y = (y + dsk_ref[g] * xs) * _silu(z_ref[:, g * hp:(g + 1) * hp])
        o_ref[:, g * hp:(g + 1) * hp] = _rms(y, ng_ref[g]).astype(o_ref.dtype)


def _ssd_core(xbc, z, dt_t, conv_w, conv_b, dt_bias, a_log, d_skip, norm_g, b, l):
    t = xbc.shape[0]
    g, hpg, cl, n, ng = SSD_GROUPS, SSD_HPG, SSD_CHUNK, SSD_STATE, SSD_GROUPS_PER_STEP
    hp = hpg * SSD_HEAD_DIM
    d_inner = g * hp
    nc = l // cl
    assert l % cl == 0 and hp % LANES == 0 and n == LANES and g % ng == 0
    bb, cb = d_inner // (ng * n), (d_inner + g * n) // (ng * n)
    pad = lambda p: jnp.pad(p.reshape(g, hpg, 1), ((0, 0), (0, SUBLANES - hpg), (0, 0)))
    per_lane = lambda p: jnp.repeat(p.reshape(g, hpg), SSD_HEAD_DIM, axis=1).reshape(g, 1, hp)
    conv_b = conv_b.reshape(1, -1)
    tok = lambda bi, gi, ci: bi * nc + ci
    return pl.pallas_call(
        functools.partial(_ssd_kernel, n_groups=ng),
        grid=(b, g // ng, nc),
        in_specs=[
            pl.BlockSpec((cl, ng * hp), lambda bi, gi, ci: (tok(bi, gi, ci), gi)),
            pl.BlockSpec((cl, ng * n), lambda bi, gi, ci: (tok(bi, gi, ci), bb + gi)),
            pl.BlockSpec((cl, ng * n), lambda bi, gi, ci: (tok(bi, gi, ci), cb + gi)),
            pl.BlockSpec((cl, ng * hp), lambda bi, gi, ci: (tok(bi, gi, ci), gi)),
            pl.BlockSpec((1, ng, SUBLANES, cl), lambda bi, gi, ci: (bi, gi, 0, ci)),
            pl.BlockSpec((1, ng, SUBLANES, cl), lambda bi, gi, ci: (bi, gi, 0, jnp.minimum(ci + 1, nc - 1))),
            pl.BlockSpec((SSD_CONV, ng * hp), lambda bi, gi, ci: (0, gi)),
            pl.BlockSpec((SSD_CONV, ng * n), lambda bi, gi, ci: (0, bb + gi)),
            pl.BlockSpec((SSD_CONV, ng * n), lambda bi, gi, ci: (0, cb + gi)),
            pl.BlockSpec((1, ng * hp), lambda bi, gi, ci: (0, gi)),
            pl.BlockSpec((1, ng * n), lambda bi, gi, ci: (0, bb + gi)),
            pl.BlockSpec((1, ng * n), lambda bi, gi, ci: (0, cb + gi)),
            pl.BlockSpec((ng, SUBLANES, 1), lambda bi, gi, ci: (gi, 0, 0)),
            pl.BlockSpec((ng, SUBLANES, 1), lambda bi, gi, ci: (gi, 0, 0)),
            pl.BlockSpec((ng, 1, hp), lambda bi, gi, ci: (gi, 0, 0)),
            pl.BlockSpec((ng, 1, hp), lambda bi, gi, ci: (gi, 0, 0)),
        ],
        out_specs=pl.BlockSpec((cl, ng * hp), lambda bi, gi, ci: (tok(bi, gi, ci), gi)),
        out_shape=jax.ShapeDtypeStruct((t, d_inner), BF16),
        scratch_shapes=[pltpu.VMEM((ng, n, hp), F32), pltpu.VMEM((HALO, ng * (hp + 2 * n)), F32),
                        pltpu.VMEM((2, ng * SUBLANES, cl), F32), pltpu.VMEM((2, cl, ng * SUBLANES), F32)],
        compiler_params=_params(3),
        name="ssd_core",
    )(xbc, xbc, xbc, z, dt_t, dt_t, conv_w, conv_w, conv_w, conv_b, conv_b, conv_b,
      pad(dt_bias), pad(a_log), per_lane(d_skip), norm_g.reshape(g, 1, hp))


def _ssd_layer(x, b, l, norm_g, w_in, conv_w, conv_b, dt_bias, a_log, d_skip, ssd_norm, w_out):
    g, hpg = SSD_GROUPS, SSD_HPG
    d_inner = w_out.shape[0]
    conv_dim = conv_w.shape[1]
    n_heads = g * hpg
    w_in = w_in.astype(BF16)
    w_z = w_in[:, :d_inner]
    w_xbc = w_in[:, d_inner:d_inner + conv_dim]
    w_dt = jnp.pad(w_in[:, d_inner + conv_dim:], ((0, 0), (0, LANES - n_heads)))
    z, xbc, dt = _norm_proj(x, norm_g, [w_z, w_xbc, w_dt], [F32] * 3)
    dt_t = dt[:, :n_heads].reshape(b, l, g, hpg).transpose(0, 2, 3, 1)
    dt_t = jnp.pad(dt_t, ((0, 0), (0, 0), (0, SUBLANES - hpg), (0, 0)))
    y = _ssd_core(xbc, z, dt_t, conv_w, conv_b, dt_bias, a_log, d_skip, ssd_norm, b, l)
    return _proj_res(x, y, w_out.astype(BF16))


def _short_conv_kernel(x_ref, g_ref, wb_ref, wc_ref, wh_ref, cw_ref, wo_ref, o_ref, ext_ref):
    tm = x_ref.shape[0]

    @pl.when(pl.program_id(1) == 0)
    def _():
        ext_ref[0:HALO, :] = jnp.zeros((HALO, ext_ref.shape[1]), F32)

    x = x_ref[...]
    xn = _rms(x, g_ref[...]).astype(BF16)
    ext_ref[HALO:HALO + tm, :] = _dot(xn, wc_ref[...]) * _dot(xn, wh_ref[...])
    cw = cw_ref[...]
    u = cw[0:1, :] * ext_ref[pl.ds(HALO - (SC_WIDTH - 1), tm), :]
    for kk in range(1, SC_WIDTH):
        u = u + cw[kk:kk + 1, :] * ext_ref[pl.ds(HALO - (SC_WIDTH - 1) + kk, tm), :]
    ext_ref[0:HALO, :] = ext_ref[tm:tm + HALO, :]
    gated = (_dot(xn, wb_ref[...]) * u).astype(BF16)
    o_ref[...] = x + _dot(gated, wo_ref[...])


def _short_conv_layer(x, b, l, norm_g, w_in, conv_w, w_out):
    t, d = x.shape
    tm = SC_TM
    nl = l // tm
    assert l % tm == 0
    w_in = w_in.astype(BF16)
    return pl.pallas_call(
        _short_conv_kernel,
        grid=(b, nl),
        in_specs=[
            pl.BlockSpec((tm, d), lambda bi, li: (bi * nl + li, 0)),
            _resident((1, d)),
            _resident((d, d)), _resident((d, d)), _resident((d, d)),
            _resident((SC_WIDTH, d)),
            _resident((d, d)),
        ],
        out_specs=pl.BlockSpec((tm, d), lambda bi, li: (bi * nl + li, 0)),
        out_shape=jax.ShapeDtypeStruct((t, d), F32),
        scratch_shapes=[pltpu.VMEM((tm + HALO, d), F32)],
        compiler_params=_params(2),
        name="short_conv",
    )(x, norm_g.reshape(1, d), w_in[:, :d], w_in[:, d:2 * d], w_in[:, 2 * d:], conv_w,
      w_out.astype(BF16))


def kernel(x, ffn1_norm, ffn1_w_gu, ffn1_w_down, mix_norm, ffn2_norm, ffn2_w_gu, ffn2_w_down,
           sb_w_qkv, sb_w_o, ssd_w_in, ssd_conv_w, ssd_conv_b, ssd_dt_bias, ssd_a_log, ssd_d,
           ssd_norm, ssd_w_out, sc_w_in, sc_conv_w, sc_w_out, final_norm):
    b, l, d = x.shape
    depth = ffn1_norm.shape[0]
    h = x.reshape(b * l, d)
    for i in range(depth):
        h = _ffn(h, ffn1_norm[i], ffn1_w_gu, ffn1_w_down, i)
        kind, j = i % N_MIXERS, i // N_MIXERS
        if kind == 0:
            h = _sb_layer(h, b, l, mix_norm[i], sb_w_qkv[j], sb_w_o[j])
        elif kind == 1:
            h = _ssd_layer(h, b, l, mix_norm[i], ssd_w_in[j], ssd_conv_w[j], ssd_conv_b[j],
                           ssd_dt_bias[j], ssd_a_log[j], ssd_d[j], ssd_norm[j], ssd_w_out[j])
        else:
            h = _short_conv_layer(h, b, l, mix_norm[i], sc_w_in[j], sc_conv_w[j], sc_w_out[j])
        h = _ffn(h, ffn2_norm[i], ffn2_w_gu, ffn2_w_down, i,
                 final_g=final_norm if i == depth - 1 else None)
    return h.reshape(b, l, d)
```

```python
import functools

import jax
import jax.numpy as jnp
from jax import lax
from jax.experimental import pallas as pl
from jax.experimental.pallas import tpu as pltpu

F32 = jnp.float32
BF16 = jnp.bfloat16

RMS_EPS = 1e-6
LOG2_E = 1.4426950408889634
N_MIXERS = 3
SB_HEAD_DIM = 64
SSD_HEAD_DIM = 64
SSD_GROUPS = 8
SSD_HPG = 4
SSD_STATE = 128
SSD_CONV = 4
SSD_CHUNK = 128
SC_WIDTH = 3

LANES = 128
SUBLANES = 8
VMEM_LIMIT_BYTES = 60 * 1024 * 1024

FFN_TM = 512
FFN_TF = 256
PROJ_TM = 512
PROJ_RES_TM = 1024
PROJ_TN = 512
ATT_T = 256
ATT_PAIRS_PER_STEP = 4
ATT_DEAD_CARRY = 160.0
SSD_GROUPS_PER_STEP = 4
SC_TM = 512
HALO = SUBLANES


def _params(n_axes):
    return pltpu.CompilerParams(
        dimension_semantics=("arbitrary",) * n_axes,
        vmem_limit_bytes=VMEM_LIMIT_BYTES)


def _resident(shape, index=None):
    index = (0,) * len(shape) if index is None else index
    return pl.BlockSpec(shape, lambda *_: index, pipeline_mode=pl.Buffered(1))


def _rms(x, g):
    ms = jnp.mean(x * x, axis=-1, keepdims=True)
    return x * lax.rsqrt(ms + RMS_EPS) * g


def _silu(x):
    h = 0.5 * x
    return h + h * jnp.tanh(h)


def _dot(a, b):
    return jnp.dot(a, b, preferred_element_type=F32)


def _dot_nt(a, b):
    return lax.dot_general(a, b, (((1,), (1,)), ((), ())), preferred_element_type=F32)


def _dot_tn(a, b):
    return lax.dot_general(a, b, (((0,), (0,)), ((), ())), preferred_element_type=F32)


def _ffn_kernel(x_ref, g_ref, wgu_ref, wd_ref, *refs, n_chunks, tf, final_norm):
    fg_ref = refs[0] if final_norm else None
    o_ref, xn_ref, h_ref, wdb_ref = refs[-4:]
    x = x_ref[...]
    d_ff = n_chunks * tf
    xn_ref[...] = _rms(x, g_ref[...]).astype(BF16)
    for c in range(n_chunks):
        gate = _dot(xn_ref[...], wgu_ref[:, c * tf:(c + 1) * tf].astype(BF16))
        up = _dot(xn_ref[...], wgu_ref[:, d_ff + c * tf:d_ff + (c + 1) * tf].astype(BF16))
        h_ref[:, c * tf:(c + 1) * tf] = (_silu(gate) * up).astype(BF16)
        wdb_ref[c * tf:(c + 1) * tf, :] = wd_ref[c * tf:(c + 1) * tf, :].astype(BF16)
    y = x + 0.5 * _dot(h_ref[...], wdb_ref[...])
    if final_norm:
        y = _rms(y, fg_ref[...])
    o_ref[...] = y


def _ffn(x, norm_g, w_gu, w_down, layer, final_g=None):
    t, d = x.shape
    d_ff = w_down.shape[1]
    tm, tf = FFN_TM, FFN_TF
    n_chunks = d_ff // tf
    assert t % tm == 0 and d_ff % tf == 0
    final_norm = final_g is not None
    extra = [final_g.reshape(1, d)] if final_norm else []
    return pl.pallas_call(
        functools.partial(_ffn_kernel, n_chunks=n_chunks, tf=tf, final_norm=final_norm),
        grid=(t // tm,),
        in_specs=[
            pl.BlockSpec((tm, d), lambda i: (i, 0)),
            _resident((1, d)),
            _resident((None, d, 2 * d_ff), (layer, 0, 0)),
            _resident((None, d_ff, d), (layer, 0, 0)),
        ] + [_resident((1, d))] * len(extra),
        out_specs=pl.BlockSpec((tm, d), lambda i: (i, 0)),
        out_shape=jax.ShapeDtypeStruct((t, d), F32),
        scratch_shapes=[pltpu.VMEM((tm, d), BF16), pltpu.VMEM((tm, d_ff), BF16),
                        pltpu.VMEM((d_ff, d), BF16)],
        compiler_params=_params(1),
        name="ffn",
    )(x, norm_g.reshape(1, d), w_gu, w_down, *extra)


def _norm_proj_kernel(x_ref, g_ref, *refs, n_out, tn):
    w_refs, o_refs = refs[:n_out], refs[n_out:]
    xn = _rms(x_ref[...], g_ref[...]).astype(BF16)
    for w_ref, o_ref in zip(w_refs, o_refs):
        n = w_ref.shape[1]
        for n0 in range(0, n, tn):
            n1 = min(n0 + tn, n)
            o_ref[:, n0:n1] = _dot(xn, w_ref[:, n0:n1]).astype(o_ref.dtype)


def _norm_proj(x, norm_g, weights, out_dtypes):
    t, d = x.shape
    tm = PROJ_TM
    assert t % tm == 0
    n_out = len(weights)
    return pl.pallas_call(
        functools.partial(_norm_proj_kernel, n_out=n_out, tn=PROJ_TN),
        grid=(t // tm,),
        in_specs=[pl.BlockSpec((tm, d), lambda i: (i, 0)), _resident((1, d))]
        + [_resident(w.shape) for w in weights],
        out_specs=[pl.BlockSpec((tm, w.shape[1]), lambda i: (i, 0)) for w in weights],
        out_shape=[jax.ShapeDtypeStruct((t, w.shape[1]), dt) for w, dt in zip(weights, out_dtypes)],
        compiler_params=_params(1),
        name="norm_proj",
    )(x, norm_g.reshape(1, d), *weights)


def _proj_res_kernel(x_ref, y_ref, w_ref, o_ref):
    o_ref[...] = x_ref[...] + _dot(y_ref[...], w_ref[...])


def _proj_res(x, y, w):
    t, d = x.shape
    k = y.shape[1]
    tm = PROJ_RES_TM
    assert t % tm == 0
    return pl.pallas_call(
        _proj_res_kernel,
        grid=(t // tm,),
        in_specs=[
            pl.BlockSpec((tm, d), lambda i: (i, 0)),
            pl.BlockSpec((tm, k), lambda i: (i, 0)),
            _resident((k, d)),
        ],
        out_specs=pl.BlockSpec((tm, d), lambda i: (i, 0)),
        out_shape=jax.ShapeDtypeStruct((t, d), F32),
        compiler_params=_params(1),
        name="proj_res",
    )(x, y, w)


def _sb_attn_kernel(q_ref, k_ref, v_ref, o_ref, lb_a, sp_a, rs_a, lb_b, sp_b, rs_b, acc_ref, c_ref,
                    *, t):
    qi = pl.program_id(2)
    hd = SB_HEAD_DIM
    buf_a, buf_b = (lb_a, sp_a, rs_a), (lb_b, sp_b, rs_b)
    both, first, second = (0, 2), (0, 1), (1, 2)
    in_a = lax.broadcasted_iota(jnp.int32, (1, 2 * hd), 1) < hd
    row = lax.broadcasted_iota(jnp.int32, (t, t), 0)
    col = lax.broadcasted_iota(jnp.int32, (t, t), 1)
    strict_lower = jnp.concatenate([col < row] * 2, axis=0)
    tri = jnp.where(row > col, 1.0, 0.0).astype(BF16)
    sign_bit = jnp.uint32(0x80000000)
    top = 2 * qi + 1

    def mask_diagonal(x, fill):
        head = jnp.where(strict_lower, x[:2 * t], fill)
        return head if x.shape[0] == 2 * t else jnp.concatenate([head, x[2 * t:]], axis=0)

    def head_pair(hp_idx, _):
        lanes = pl.ds(pl.multiple_of(hp_idx * 2 * hd, 2 * hd), 2 * hd)
        q_parts = []
        for s in range(2):
            q2 = q_ref[0, s * t:(s + 1) * t, lanes]
            q_parts += [jnp.where(in_a, q2, 0), jnp.where(in_a, 0, q2)]
        qs = jnp.concatenate(q_parts, axis=0)
        _sb_head_pair(qi, top, qs, k_ref, v_ref, lanes, buf_a, buf_b, acc_ref, c_ref, in_a, tri,
                      sign_bit, mask_diagonal, (both, first, second), t)
        o_ref[0, :, lanes] = acc_ref[...].astype(o_ref.dtype)
        return 0

    lax.fori_loop(0, q_ref.shape[2] // (2 * hd), head_pair, 0)


def _sb_head_pair(qi, top, qs, k_ref, v_ref, lanes, buf_a, buf_b, acc_ref, c_ref, in_a, tri, sign_bit,
                  mask_diagonal, sub_ranges, t):
    both, first, second = sub_ranges

    def scores(j, buf, subs, diag):
        lb_ref, sp_ref, rs_ref = buf
        r0, r1 = 2 * subs[0] * t, 2 * subs[1] * t
        k2 = k_ref[0, pl.ds(pl.multiple_of(j * t, t), t), lanes]
        z = _dot_nt(qs[r0:r1], k2)
        neg_abs = lax.bitcast_convert_type(lax.bitcast_convert_type(z, jnp.uint32) | sign_bit, F32)
        lb = jnp.minimum(z, 0.0) - jnp.log2(1.0 + jnp.exp2(neg_abs))
        sp = z - lb
        if diag:
            sp, lb = mask_diagonal(sp, 0.0), mask_diagonal(lb, -jnp.inf)
        lb_ref[r0:r1, :] = lb
        sp_ref[r0:r1, :] = sp.astype(BF16)
        rs_ref[r0:r1, :] = jnp.broadcast_to(jnp.sum(sp, axis=-1, keepdims=True), (r1 - r0, LANES))

    def values(j, buf, subs):
        lb_ref, sp_ref, rs_ref = buf
        r0, r1 = 2 * subs[0] * t, 2 * subs[1] * t
        v2 = v_ref[0, pl.ds(pl.multiple_of(j * t, t), t), lanes]
        c = c_ref[r0:r1, :]
        tail = _dot(sp_ref[r0:r1, :], tri)
        c_wide = jnp.concatenate([c] * (t // LANES), axis=1)
        att = jnp.exp2(lb_ref[r0:r1, :] - tail - c_wide).astype(BF16)
        c_ref[r0:r1, :] = c + rs_ref[r0:r1, :]
        att2 = jnp.concatenate(
            [jnp.concatenate([att[(2 * s) * t:(2 * s + 1) * t], att[(2 * s + 1) * t:(2 * s + 2) * t]],
                             axis=1) for s in range(subs[1] - subs[0])], axis=0)
        vv = jnp.concatenate([jnp.where(in_a, v2, 0), jnp.where(in_a, 0, v2)], axis=0)
        acc_ref[subs[0] * t:subs[1] * t, :] += _dot(att2, vv)

    def tile_pair(state):
        i = state[0]
        j = top - 3 - 2 * i
        values(j, buf_b, both)
        scores(j - 1, buf_a, both, False)
        values(j - 1, buf_a, both)
        scores(j - 2, buf_b, both, False)
        return i + 1, jnp.min(c_ref[...])

    def live(state):
        return (state[0] < qi - 1) & (state[1] < ATT_DEAD_CARRY)

    acc_ref[...] = jnp.zeros_like(acc_ref)
    c_ref[...] = jnp.zeros_like(c_ref)

    @pl.when(qi == 0)
    def _():
        scores(top, buf_a, second, True)
        values(top, buf_a, second)
        scores(top - 1, buf_b, both, True)
        values(top - 1, buf_b, both)

    @pl.when(qi > 0)
    def _():
        scores(top, buf_a, second, True)
        scores(top - 1, buf_b, both, True)
        values(top, buf_a, second)
        scores(top - 2, buf_a, first, False)
        values(top - 1, buf_b, both)
        values(top - 2, buf_a, first)
        c_min0 = jnp.min(c_ref[...])

        @pl.when(c_min0 < ATT_DEAD_CARRY)
        def _():
            scores(top - 2, buf_a, second, False)
            values(top - 2, buf_a, second)
            scores(top - 3, buf_b, both, False)
            n_pairs, c_min = lax.while_loop(live, tile_pair, (jnp.int32(0), c_min0))

            @pl.when(c_min < ATT_DEAD_CARRY)
            def _():
                values(top - 3 - 2 * n_pairs, buf_b, both)


def _sb_attention(q, k, v):
    b, l, d = q.shape
    t = ATT_T
    tq = 2 * t
    hp = 2 * SB_HEAD_DIM * ATT_PAIRS_PER_STEP
    assert l % tq == 0 and d % hp == 0 and 2 * SB_HEAD_DIM == LANES
    rows = 4 * t
    stage = [pltpu.VMEM((rows, t), F32), pltpu.VMEM((rows, t), BF16), pltpu.VMEM((rows, LANES), F32)]
    return pl.pallas_call(
        functools.partial(_sb_attn_kernel, t=t),
        grid=(b, d // hp, l // tq),
        in_specs=[
            pl.BlockSpec((1, tq, hp), lambda bi, hi, qi: (bi, qi, hi)),
            pl.BlockSpec((1, l, hp), lambda bi, hi, qi: (bi, 0, hi)),
            pl.BlockSpec((1, l, hp), lambda bi, hi, qi: (bi, 0, hi)),
        ],
        out_specs=pl.BlockSpec((1, tq, hp), lambda bi, hi, qi: (bi, qi, hi)),
        out_shape=jax.ShapeDtypeStruct((b, l, d), BF16),
        scratch_shapes=stage + stage + [pltpu.VMEM((tq, LANES), F32), pltpu.VMEM((rows, LANES), F32)],
        compiler_params=_params(3),
        name="sb_attn",
    )(q, k, v)


def _sb_layer(x, b, l, norm_g, w_qkv, w_o):
    t, d = x.shape
    scale = LOG2_E * SB_HEAD_DIM ** -0.5
    wq = (w_qkv[:, :d] * scale).astype(BF16)
    wk = w_qkv[:, d:2 * d].astype(BF16)
    wv = w_qkv[:, 2 * d:].astype(BF16)
    q, k, v = _norm_proj(x, norm_g, [wq, wk, wv], [BF16] * 3)
    o = _sb_attention(q.reshape(b, l, d), k.reshape(b, l, d), v.reshape(b, l, d))
    return _proj_res(x, o.reshape(t, d), w_o.astype(BF16))


def _expand_heads(m, width):
    rows = m.shape[0]
    lane = lax.broadcasted_iota(jnp.int32, (rows, SSD_HPG * width), 1)
    out = jnp.broadcast_to(m[:, SSD_HPG - 1:SSD_HPG], (rows, SSD_HPG * width))
    for r in range(SSD_HPG - 2, -1, -1):
        out = jnp.where(lane < (r + 1) * width, m[:, r:r + 1], out)
    return out


def _to_columns(rows8):
    n = rows8.shape[1]
    padded = jnp.concatenate([rows8, jnp.zeros((n - rows8.shape[0], n), F32)], axis=0)
    return padded.T[:, :rows8.shape[0]]


def _ssd_kernel(xr_ref, br_ref, cr_ref, z_ref, dt_ref, dtn_ref, wx_ref, wb_ref, wc_ref, bx_ref,
                bb_ref, bc_ref, dtb_ref, alog_ref, dsk_ref, ng_ref, o_ref, h_ref, hist_ref,
                steps_row_ref, steps_col_ref, *, n_groups):
    c = pl.program_id(2)
    cl, hp, n = SSD_CHUNK, SSD_HPG * SSD_HEAD_DIM, SSD_STATE
    rows = n_groups * SUBLANES

    def stage_step_sizes(dt_blk):
        dt_in = (dt_blk + dtb_ref[...]).reshape(rows, cl)
        dt_all = jnp.maximum(dt_in, 0.0) + jnp.log(1.0 + jnp.exp(-jnp.abs(dt_in)))
        a_all = dt_all * (-jnp.exp(alog_ref[...].reshape(rows, 1)))
        lane_cl = lax.broadcasted_iota(jnp.int32, (rows, cl), 1)
        sh = 1
        while sh < cl:
            a_all = a_all + jnp.where(lane_cl >= sh, pltpu.roll(a_all, sh, axis=1), 0.0)
            sh *= 2
        te_all = jnp.exp(a_all[:, cl - 1:cl] - a_all) * dt_all
        steps_row_ref[0], steps_row_ref[1] = dt_all, a_all
        steps_col_ref[0], steps_col_ref[1] = _to_columns(a_all), _to_columns(te_all)

    @pl.when(c == 0)
    def _():
        h_ref[...] = jnp.zeros_like(h_ref)
        hist_ref[...] = jnp.zeros_like(hist_ref)
        stage_step_sizes(dt_ref[0])

    dt_all, a_all = steps_row_ref[0], steps_row_ref[1]
    a_col_all, te_col_all = steps_col_ref[0], steps_col_ref[1]
    stage_step_sizes(dtn_ref[0])

    raw = jnp.concatenate([xr_ref[...], br_ref[...], cr_ref[...]], axis=1)
    ext = jnp.concatenate([hist_ref[...], raw], axis=0)
    hist_ref[...] = raw[cl - HALO:, :]
    cw = jnp.concatenate([wx_ref[...], wb_ref[...], wc_ref[...]], axis=1)
    acc = cw[0:1, :] * ext
    for kk in range(1, SSD_CONV):
        acc = cw[kk:kk + 1, :] * ext + pltpu.roll(acc, 1, axis=0)
    bias = jnp.concatenate([bx_ref[...], bb_ref[...], bc_ref[...]], axis=1)
    act = _silu(acc[HALO:, :] + bias)
    xs_all = act[:, :n_groups * hp]
    bm_all = act[:, n_groups * hp:n_groups * (hp + n)].astype(BF16)
    cm_all = act[:, n_groups * (hp + n):].astype(BF16)

    row = lax.broadcasted_iota(jnp.int32, (cl, cl), 0)
    col = lax.broadcasted_iota(jnp.int32, (cl, cl), 1)
    causal = col <= row
    lane_hp = lax.broadcasted_iota(jnp.int32, (cl, hp), 1)

    for g in range(n_groups):
        xs = xs_all[:, g * hp:(g + 1) * hp]
        bm = bm_all[:, g * n:(g + 1) * n]
        cm = cm_all[:, g * n:(g + 1) * n]
        dt = dt_all[g * SUBLANES:(g + 1) * SUBLANES]
        a_cum = a_all[g * SUBLANES:(g + 1) * SUBLANES]
        a_col = a_col_all[:, g * SUBLANES:(g + 1) * SUBLANES]
        te_col = te_col_all[:, g * SUBLANES:(g + 1) * SUBLANES]

        cb = _dot_nt(cm, bm)
        ws, xblocks = [], []
        for r in range(SSD_HPG):
            seg = a_col[:, r:r + 1] - a_cum[r:r + 1, :]
            decay = jnp.exp(jnp.where(causal, seg, -jnp.inf))
            ws.append((cb * decay * dt[r:r + 1, :]).astype(BF16))
            in_head = (lane_hp >= r * SSD_HEAD_DIM) & (lane_hp < (r + 1) * SSD_HEAD_DIM)
            xblocks.append(jnp.where(in_head, xs, 0.0).astype(BF16))
        y = _dot(jnp.concatenate(ws, axis=1), jnp.concatenate(xblocks, axis=0))

        h_prev = h_ref[g]
        y = y + _dot(cm, h_prev.astype(BF16)) * _expand_heads(jnp.exp(a_col), SSD_HEAD_DIM)
        xw = (xs * _expand_heads(te_col, SSD_HEAD_DIM)).astype(BF16)
        h_decay = _expand_heads(jnp.exp(a_col[cl - 1:cl, :]), SSD_HEAD_DIM)
        h_ref[g] = h_prev * h_decay + _dot_tn(bm, xw)

        y = (y + dsk_ref[g] * xs) * _silu(z_ref[:, g * hp:(g + 1) * hp])
        o_ref[:, g * hp:(g + 1) * hp] = _rms(y, ng_ref[g]).astype(o_ref.dtype)


def _ssd_core(xbc, z, dt_t, conv_w, conv_b, dt_bias, a_log, d_skip, norm_g, b, l):
    t = xbc.shape[0]
    g, hpg, cl, n, ng = SSD_GROUPS, SSD_HPG, SSD_CHUNK, SSD_STATE, SSD_GROUPS_PER_STEP
    hp = hpg * SSD_HEAD_DIM
    d_inner = g * hp
    nc = l // cl
    assert l % cl == 0 and hp % LANES == 0 and n == LANES and g % ng == 0
    bb, cb = d_inner // (ng * n), (d_inner + g * n) // (ng * n)
    pad = lambda p: jnp.pad(p.reshape(g, hpg, 1), ((0, 0), (0, SUBLANES - hpg), (0, 0)))
    per_lane = lambda p: jnp.repeat(p.reshape(g, hpg), SSD_HEAD_DIM, axis=1).reshape(g, 1, hp)
    conv_b = conv_b.reshape(1, -1)
    tok = lambda bi, gi, ci: bi * nc + ci
    return pl.pallas_call(
        functools.partial(_ssd_kernel, n_groups=ng),
        grid=(b, g // ng, nc),
        in_specs=[
            pl.BlockSpec((cl, ng * hp), lambda bi, gi, ci: (tok(bi, gi, ci), gi)),
            pl.BlockSpec((cl, ng * n), lambda bi, gi, ci: (tok(bi, gi, ci), bb + gi)),
            pl.BlockSpec((cl, ng * n), lambda bi, gi, ci: (tok(bi, gi, ci), cb + gi)),
            pl.BlockSpec((cl, ng * hp), lambda bi, gi, ci: (tok(bi, gi, ci), gi)),
            pl.BlockSpec((1, ng, SUBLANES, cl), lambda bi, gi, ci: (bi, gi, 0, ci)),
            pl.BlockSpec((1, ng, SUBLANES, cl), lambda bi, gi, ci: (bi, gi, 0, jnp.minimum(ci + 1, nc - 1))),
            pl.BlockSpec((SSD_CONV, ng * hp), lambda bi, gi, ci: (0, gi)),
            pl.BlockSpec((SSD_CONV, ng * n), lambda bi, gi, ci: (0, bb + gi)),
            pl.BlockSpec((SSD_CONV, ng * n), lambda bi, gi, ci: (0, cb + gi)),
            pl.BlockSpec((1, ng * hp), lambda bi, gi, ci: (0, gi)),
            pl.BlockSpec((1, ng * n), lambda bi, gi, ci: (0, bb + gi)),
            pl.BlockSpec((1, ng * n), lambda bi, gi, ci: (0, cb + gi)),
            pl.BlockSpec((ng, SUBLANES, 1), lambda bi, gi, ci: (gi, 0, 0)),
            pl.BlockSpec((ng, SUBLANES, 1), lambda bi, gi, ci: (gi, 0, 0)),
            pl.BlockSpec((ng, 1, hp), lambda bi, gi, ci: (gi, 0, 0)),
            pl.BlockSpec((ng, 1, hp), lambda bi, gi, ci: (gi, 0, 0)),
        ],
        out_specs=pl.BlockSpec((cl, ng * hp), lambda bi, gi, ci: (tok(bi, gi, ci), gi)),
        out_shape=jax.ShapeDtypeStruct((t, d_inner), BF16),
        scratch_shapes=[pltpu.VMEM((ng, n, hp), F32), pltpu.VMEM((HALO, ng * (hp + 2 * n)), F32),
                        pltpu.VMEM((2, ng * SUBLANES, cl), F32), pltpu.VMEM((2, cl, ng * SUBLANES), F32)],
        compiler_params=_params(3),
        name="ssd_core",
    )(xbc, xbc, xbc, z, dt_t, dt_t, conv_w, conv_w, conv_w, conv_b, conv_b, conv_b,
      pad(dt_bias), pad(a_log), per_lane(d_skip), norm_g.reshape(g, 1, hp))


def _ssd_layer(x, b, l, norm_g, w_in, conv_w, conv_b, dt_bias, a_log, d_skip, ssd_norm, w_out):
    g, hpg = SSD_GROUPS, SSD_HPG
    d_inner = w_out.shape[0]
    conv_dim = conv_w.shape[1]
    n_heads = g * hpg
    w_in = w_in.astype(BF16)
    w_z = w_in[:, :d_inner]
    w_xbc = w_in[:, d_inner:d_inner + conv_dim]
    w_dt = jnp.pad(w_in[:, d_inner + conv_dim:], ((0, 0), (0, LANES - n_heads)))
    z, xbc, dt = _norm_proj(x, norm_g, [w_z, w_xbc, w_dt], [F32] * 3)
    dt_t = dt[:, :n_heads].reshape(b, l, g, hpg).transpose(0, 2, 3, 1)
    dt_t = jnp.pad(dt_t, ((0, 0), (0, 0), (0, SUBLANES - hpg), (0, 0)))
    y = _ssd_core(xbc, z, dt_t, conv_w, conv_b, dt_bias, a_log, d_skip, ssd_norm, b, l)
    return _proj_res(x, y, w_out.astype(BF16))


def _short_conv_kernel(x_ref, g_ref, wb_ref, wc_ref, wh_ref, cw_ref, wo_ref, o_ref, ext_ref):
    tm = x_ref.shape[0]

    @pl.when(pl.program_id(1) == 0)
    def _():
        ext_ref[0:HALO, :] = jnp.zeros((HALO, ext_ref.shape[1]), F32)

    x = x_ref[...]
    xn = _rms(x, g_ref[...]).astype(BF16)
    ext_ref[HALO:HALO + tm, :] = _dot(xn, wc_ref[...]) * _dot(xn, wh_ref[...])
    cw = cw_ref[...]
    u = cw[0:1, :] * ext_ref[pl.ds(HALO - (SC_WIDTH - 1), tm), :]
    for kk in range(1, SC_WIDTH):
        u = u + cw[kk:kk + 1, :] * ext_ref[pl.ds(HALO - (SC_WIDTH - 1) + kk, tm), :]
    ext_ref[0:HALO, :] = ext_ref[tm:tm + HALO, :]
    gated = (_dot(xn, wb_ref[...]) * u).astype(BF16)
    o_ref[...] = x + _dot(gated, wo_ref[...])


def _short_conv_layer(x, b, l, norm_g, w_in, conv_w, w_out):
    t, d = x.shape
    tm = SC_TM
    nl = l // tm
    assert l % tm == 0
    w_in = w_in.astype(BF16)
    return pl.pallas_call(
        _short_conv_kernel,
        grid=(b, nl),
        in_specs=[
            pl.BlockSpec((tm, d), lambda bi, li: (bi * nl + li, 0)),
            _resident((1, d)),
            _resident((d, d)), _resident((d, d)), _resident((d, d)),
            _resident((SC_WIDTH, d)),
            _resident((d, d)),
        ],
        out_specs=pl.BlockSpec((tm, d), lambda bi, li: (bi * nl + li, 0)),
        out_shape=jax.ShapeDtypeStruct((t, d), F32),
        scratch_shapes=[pltpu.VMEM((tm + HALO, d), F32)],
        compiler_params=_params(2),
        name="short_conv",
    )(x, norm_g.reshape(1, d), w_in[:, :d], w_in[:, d:2 * d], w_in[:, 2 * d:], conv_w,
      w_out.astype(BF16))


def kernel(x, ffn1_norm, ffn1_w_gu, ffn1_w_down, mix_norm, ffn2_norm, ffn2_w_gu, ffn2_w_down,
           sb_w_qkv, sb_w_o, ssd_w_in, ssd_conv_w, ssd_conv_b, ssd_dt_bias, ssd_a_log, ssd_d,
           ssd_norm, ssd_w_out, sc_w_in, sc_conv_w, sc_w_out, final_norm):
    b, l, d = x.shape
    depth = ffn1_norm.shape[0]
    h = x.reshape(b * l, d)
    for i in range(depth):
        h = _ffn(h, ffn1_norm[i], ffn1_w_gu, ffn1_w_down, i)
        kind, j = i % N_MIXERS, i // N_MIXERS
        if kind == 0:
            h = _sb_layer(h, b, l, mix_norm[i], sb_w_qkv[j], sb_w_o[j])
        elif kind == 1:
            h = _ssd_layer(h, b, l, mix_norm[i], ssd_w_in[j], ssd_conv_w[j], ssd_conv_b[j],
                           ssd_dt_bias[j], ssd_a_log[j], ssd_d[j], ssd_norm[j], ssd_w_out[j])
        else:
            h = _short_conv_layer(h, b, l, mix_norm[i], sc_w_in[j], sc_conv_w[j], sc_w_out[j])
        h = _ffn(h, ffn2_norm[i], ffn2_w_gu, ffn2_w_down, i,
                 final_g=final_norm if i == depth - 1 else None)
    return h.reshape(b, l, d)
```

```python
import functools

import jax
import jax.numpy as jnp
from jax import lax
from jax.experimental import pallas as pl
from jax.experimental.pallas import tpu as pltpu

F32 = jnp.float32
BF16 = jnp.bfloat16

RMS_EPS = 1e-6
LOG2_E = 1.4426950408889634
N_MIXERS = 3
SB_HEAD_DIM = 64
SSD_HEAD_DIM = 64
SSD_GROUPS = 8
SSD_HPG = 4
SSD_STATE = 128
SSD_CONV = 4
SSD_CHUNK = 128
SC_WIDTH = 3

LANES = 128
SUBLANES = 8
VMEM_LIMIT_BYTES = 60 * 1024 * 1024

FFN_TM = 512
FFN_TF = 256
PROJ_TM = 512
PROJ_RES_TM = 1024
PROJ_TN = 512
ATT_T = 256
ATT_PAIRS_PER_STEP = 4
ATT_DEAD_CARRY = 160.0
SSD_GROUPS_PER_STEP = 4
SC_TM = 512
HALO = SUBLANES


def _params(n_axes):
    return pltpu.CompilerParams(
        dimension_semantics=("arbitrary",) * n_axes,
        vmem_limit_bytes=VMEM_LIMIT_BYTES)


def _resident(shape, index=None):
    index = (0,) * len(shape) if index is None else index
    return pl.BlockSpec(shape, lambda *_: index, pipeline_mode=pl.Buffered(1))


def _rms(x, g):
    ms = jnp.mean(x * x, axis=-1, keepdims=True)
    return x * lax.rsqrt(ms + RMS_EPS) * g


def _silu(x):
    h = 0.5 * x
    return h + h * jnp.tanh(h)


def _dot(a, b):
    return jnp.dot(a, b, preferred_element_type=F32)


def _dot_nt(a, b):
    return lax.dot_general(a, b, (((1,), (1,)), ((), ())), preferred_element_type=F32)


def _dot_tn(a, b):
    return lax.dot_general(a, b, (((0,), (0,)), ((), ())), preferred_element_type=F32)


def _ffn_kernel(x_ref, g_ref, wgu_ref, wd_ref, *refs, n_chunks, tf, final_norm):
    fg_ref = refs[0] if final_norm else None
    o_ref, xn_ref, h_ref, wdb_ref = refs[-4:]
    x = x_ref[...]
    d_ff = n_chunks * tf
    xn_ref[...] = _rms(x, g_ref[...]).astype(BF16)
    for c in range(n_chunks):
        gate = _dot(xn_ref[...], wgu_ref[:, c * tf:(c + 1) * tf].astype(BF16))
        up = _dot(xn_ref[...], wgu_ref[:, d_ff + c * tf:d_ff + (c + 1) * tf].astype(BF16))
        h_ref[:, c * tf:(c + 1) * tf] = (_silu(gate) * up).astype(BF16)
        wdb_ref[c * tf:(c + 1) * tf, :] = wd_ref[c * tf:(c + 1) * tf, :].astype(BF16)
    y = x + 0.5 * _dot(h_ref[...], wdb_ref[...])
    if final_norm:
        y = _rms(y, fg_ref[...])
    o_ref[...] = y


def _ffn(x, norm_g, w_gu, w_down, layer, final_g=None):
    t, d = x.shape
    d_ff = w_down.shape[1]
    tm, tf = FFN_TM, FFN_TF
    n_chunks = d_ff // tf
    assert t % tm == 0 and d_ff % tf == 0
    final_norm = final_g is not None
    extra = [final_g.reshape(1, d)] if final_norm else []
    return pl.pallas_call(
        functools.partial(_ffn_kernel, n_chunks=n_chunks, tf=tf, final_norm=final_norm),
        grid=(t // tm,),
        in_specs=[
            pl.BlockSpec((tm, d), lambda i: (i, 0)),
            _resident((1, d)),
            _resident((None, d, 2 * d_ff), (layer, 0, 0)),
            _resident((None, d_ff, d), (layer, 0, 0)),
        ] + [_resident((1, d))] * len(extra),
        out_specs=pl.BlockSpec((tm, d), lambda i: (i, 0)),
        out_shape=jax.ShapeDtypeStruct((t, d), F32),
        scratch_shapes=[pltpu.VMEM((tm, d), BF16), pltpu.VMEM((tm, d_ff), BF16),
                        pltpu.VMEM((d_ff, d), BF16)],
        compiler_params=_params(1),
        name="ffn",
    )(x, norm_g.reshape(1, d), w_gu, w_down, *extra)


def _norm_proj_kernel(x_ref, g_ref, w_ref, *o_refs, segments, tn):
    xn = _rms(x_ref[...], g_ref[...]).astype(BF16)
    for o_ref, (start, width, scale) in zip(o_refs, segments):
        for n0 in range(0, width, tn):
            n1 = min(n0 + tn, width)
            w = w_ref[:, start + n0:start + n1]
            if scale != 1.0:
                w = w * scale
            o_ref[:, n0:n1] = _dot(xn, w.astype(BF16)).astype(o_ref.dtype)


def _norm_proj(x, norm_g, w, layer, segments, out_dtypes):
    t, d = x.shape
    tm = PROJ_TM
    assert t % tm == 0 and all(s[0] % LANES == 0 for s in segments)
    return pl.pallas_call(
        functools.partial(_norm_proj_kernel, segments=tuple(segments), tn=PROJ_TN),
        grid=(t // tm,),
        in_specs=[pl.BlockSpec((tm, d), lambda i: (i, 0)), _resident((1, d)),
                  _resident((None,) + w.shape[1:], (layer, 0, 0))],
        out_specs=[pl.BlockSpec((tm, s[1]), lambda i: (i, 0)) for s in segments],
        out_shape=[jax.ShapeDtypeStruct((t, s[1]), dt) for s, dt in zip(segments, out_dtypes)],
        compiler_params=_params(1),
        name="norm_proj",
    )(x, norm_g.reshape(1, d), w)


def _proj_res_kernel(x_ref, y_ref, w_ref, o_ref):
    o_ref[...] = x_ref[...] + _dot(y_ref[...], w_ref[...].astype(BF16))


def _proj_res(x, y, w, layer):
    t, d = x.shape
    k = y.shape[1]
    tm = PROJ_RES_TM
    assert t % tm == 0
    return pl.pallas_call(
        _proj_res_kernel,
        grid=(t // tm,),
        in_specs=[
            pl.BlockSpec((tm, d), lambda i: (i, 0)),
            pl.BlockSpec((tm, k), lambda i: (i, 0)),
            _resident((None, k, d), (layer, 0, 0)),
        ],
        out_specs=pl.BlockSpec((tm, d), lambda i: (i, 0)),
        out_shape=jax.ShapeDtypeStruct((t, d), F32),
        compiler_params=_params(1),
        name="proj_res",
    )(x, y, w)


def _sb_attn_kernel(q_ref, k_ref, v_ref, o_ref, lb_a, sp_a, rs_a, lb_b, sp_b, rs_b, acc_ref, c_ref,
                    *, t):
    qi = pl.program_id(2)
    hd = SB_HEAD_DIM
    buf_a, buf_b = (lb_a, sp_a, rs_a), (lb_b, sp_b, rs_b)
    both, first, second = (0, 2), (0, 1), (1, 2)
    in_a = lax.broadcasted_iota(jnp.int32, (1, 2 * hd), 1) < hd
    row = lax.broadcasted_iota(jnp.int32, (t, t), 0)
    col = lax.broadcasted_iota(jnp.int32, (t, t), 1)
    strict_lower = jnp.concatenate([col < row] * 2, axis=0)
    tri = jnp.where(row > col, 1.0, 0.0).astype(BF16)
    sign_bit = jnp.uint32(0x80000000)
    top = 2 * qi + 1

    def mask_diagonal(x, fill):
        head = jnp.where(strict_lower, x[:2 * t], fill)
        return head if x.shape[0] == 2 * t else jnp.concatenate([head, x[2 * t:]], axis=0)

    def head_pair(hp_idx, _):
        lanes = pl.ds(pl.multiple_of(hp_idx * 2 * hd, 2 * hd), 2 * hd)
        q_parts = []
        for s in range(2):
            q2 = q_ref[0, s * t:(s + 1) * t, lanes]
            q_parts += [jnp.where(in_a, q2, 0), jnp.where(in_a, 0, q2)]
        qs = jnp.concatenate(q_parts, axis=0)
        _sb_head_pair(qi, top, qs, k_ref, v_ref, lanes, buf_a, buf_b, acc_ref, c_ref, in_a, tri,
                      sign_bit, mask_diagonal, (both, first, second), t)
        o_ref[0, :, lanes] = acc_ref[...].astype(o_ref.dtype)
        return 0

    lax.fori_loop(0, q_ref.shape[2] // (2 * hd), head_pair, 0)


def _sb_head_pair(qi, top, qs, k_ref, v_ref, lanes, buf_a, buf_b, acc_ref, c_ref, in_a, tri, sign_bit,
                  mask_diagonal, sub_ranges, t):
    both, first, second = sub_ranges

    def scores(j, buf, subs, diag):
        lb_ref, sp_ref, rs_ref = buf
        r0, r1 = 2 * subs[0] * t, 2 * subs[1] * t
        k2 = k_ref[0, pl.ds(pl.multiple_of(j * t, t), t), lanes]
        z = _dot_nt(qs[r0:r1], k2)
        neg_abs = lax.bitcast_convert_type(lax.bitcast_convert_type(z, jnp.uint32) | sign_bit, F32)
        lb = jnp.minimum(z, 0.0) - jnp.log2(1.0 + jnp.exp2(neg_abs))
        sp = z - lb
        if diag:
            sp, lb = mask_diagonal(sp, 0.0), mask_diagonal(lb, -jnp.inf)
        lb_ref[r0:r1, :] = lb
        sp_ref[r0:r1, :] = sp.astype(BF16)
        rs_ref[r0:r1, :] = jnp.broadcast_to(jnp.sum(sp, axis=-1, keepdims=True), (r1 - r0, LANES))

    def values(j, buf, subs):
        lb_ref, sp_ref, rs_ref = buf
        r0, r1 = 2 * subs[0] * t, 2 * subs[1] * t
        v2 = v_ref[0, pl.ds(pl.multiple_of(j * t, t), t), lanes]
        c = c_ref[r0:r1, :]
        tail = _dot(sp_ref[r0:r1, :], tri)
        c_wide = jnp.concatenate([c] * (t // LANES), axis=1)
        att = jnp.exp2(lb_ref[r0:r1, :] - tail - c_wide).astype(BF16)
        c_ref[r0:r1, :] = c + rs_ref[r0:r1, :]
        att2 = jnp.concatenate(
            [jnp.concatenate([att[(2 * s) * t:(2 * s + 1) * t], att[(2 * s + 1) * t:(2 * s + 2) * t]],
                             axis=1) for s in range(subs[1] - subs[0])], axis=0)
        vv = jnp.concatenate([jnp.where(in_a, v2, 0), jnp.where(in_a, 0, v2)], axis=0)
        acc_ref[subs[0] * t:subs[1] * t, :] += _dot(att2, vv)

    def tile_pair(state):
        i = state[0]
        j = top - 3 - 2 * i
        values(j, buf_b, both)
        scores(j - 1, buf_a, both, False)
        values(j - 1, buf_a, both)
        scores(j - 2, buf_b, both, False)
        return i + 1, jnp.min(c_ref[...])

    def live(state):
        return (state[0] < qi - 1) & (state[1] < ATT_DEAD_CARRY)

    acc_ref[...] = jnp.zeros_like(acc_ref)
    c_ref[...] = jnp.zeros_like(c_ref)

    @pl.when(qi == 0)
    def _():
        scores(top, buf_a, second, True)
        values(top, buf_a, second)
        scores(top - 1, buf_b, both, True)
        values(top - 1, buf_b, both)

    @pl.when(qi > 0)
    def _():
        scores(top, buf_a, second, True)
        scores(top - 1, buf_b, both, True)
        values(top, buf_a, second)
        scores(top - 2, buf_a, first, False)
        values(top - 1, buf_b, both)
        c_min0 = jnp.minimum(jnp.min(c_ref[2 * t:, :]), jnp.min(c_ref[:2 * t, :] + buf_a[2][:2 * t, :]))
        values(top - 2, buf_a, first)

        @pl.when(c_min0 < ATT_DEAD_CARRY)
        def _():
            scores(top - 2, buf_a, second, False)
            values(top - 2, buf_a, second)
            scores(top - 3, buf_b, both, False)
            n_pairs, c_min = lax.while_loop(live, tile_pair, (jnp.int32(0), c_min0))

            @pl.when(c_min < ATT_DEAD_CARRY)
            def _():
                values(top - 3 - 2 * n_pairs, buf_b, both)


def _sb_attention(q, k, v):
    b, l, d = q.shape
    t = ATT_T
    tq = 2 * t
    hp = 2 * SB_HEAD_DIM * ATT_PAIRS_PER_STEP
    assert l % tq == 0 and d % hp == 0 and 2 * SB_HEAD_DIM == LANES
    rows = 4 * t
    stage = [pltpu.VMEM((rows, t), F32), pltpu.VMEM((rows, t), BF16), pltpu.VMEM((rows, LANES), F32)]
    return pl.pallas_call(
        functools.partial(_sb_attn_kernel, t=t),
        grid=(b, d // hp, l // tq),
        in_specs=[
            pl.BlockSpec((1, tq, hp), lambda bi, hi, qi: (bi, qi, hi)),
            pl.BlockSpec((1, l, hp), lambda bi, hi, qi: (bi, 0, hi)),
            pl.BlockSpec((1, l, hp), lambda bi, hi, qi: (bi, 0, hi)),
        ],
        out_specs=pl.BlockSpec((1, tq, hp), lambda bi, hi, qi: (bi, qi, hi)),
        out_shape=jax.ShapeDtypeStruct((b, l, d), BF16),
        scratch_shapes=stage + stage + [pltpu.VMEM((tq, LANES), F32), pltpu.VMEM((rows, LANES), F32)],
        compiler_params=_params(3),
        name="sb_attn",
    )(q, k, v)


def _sb_layer(x, b, l, norm_g, w_qkv, w_o, layer):
    t, d = x.shape
    scale = LOG2_E * SB_HEAD_DIM ** -0.5
    q, k, v = _norm_proj(x, norm_g, w_qkv, layer,
                         [(0, d, scale), (d, d, 1.0), (2 * d, d, 1.0)], [BF16] * 3)
    o = _sb_attention(q.reshape(b, l, d), k.reshape(b, l, d), v.reshape(b, l, d))
    return _proj_res(x, o.reshape(t, d), w_o, layer)


def _expand_heads(m, width):
    rows = m.shape[0]
    lane = lax.broadcasted_iota(jnp.int32, (rows, SSD_HPG * width), 1)
    out = jnp.broadcast_to(m[:, SSD_HPG - 1:SSD_HPG], (rows, SSD_HPG * width))
    for r in range(SSD_HPG - 2, -1, -1):
        out = jnp.where(lane < (r + 1) * width, m[:, r:r + 1], out)
    return out


def _to_columns(rows8):
    n = rows8.shape[1]
    padded = jnp.concatenate([rows8, jnp.zeros((n - rows8.shape[0], n), F32)], axis=0)
    return padded.T[:, :rows8.shape[0]]


def _ssd_kernel(xr_ref, br_ref, cr_ref, z_ref, dt_ref, dtn_ref, wx_ref, wb_ref, wc_ref, bx_ref,
                bb_ref, bc_ref, dtb_ref, alog_ref, dsk_ref, ng_ref, o_ref, h_ref, hist_ref,
                steps_row_ref, steps_col_ref, *, n_groups):
    c = pl.program_id(2)
    cl, hp, n = SSD_CHUNK, SSD_HPG * SSD_HEAD_DIM, SSD_STATE
    rows = n_groups * SUBLANES

    def stage_step_sizes(dt_blk):
        dt_in = (dt_blk + dtb_ref[...]).reshape(rows, cl)
        dt_all = jnp.maximum(dt_in, 0.0) + jnp.log(1.0 + jnp.exp(-jnp.abs(dt_in)))
        a_all = dt_all * (-jnp.exp(alog_ref[...].reshape(rows, 1)))
        lane_cl = lax.broadcasted_iota(jnp.int32, (rows, cl), 1)
        sh = 1
        while sh < cl:
            a_all = a_all + jnp.where(lane_cl >= sh, pltpu.roll(a_all, sh, axis=1), 0.0)
            sh *= 2
        te_all = jnp.exp(a_all[:, cl - 1:cl] - a_all) * dt_all
        steps_row_ref[0], steps_row_ref[1] = dt_all, a_all
        steps_col_ref[0], steps_col_ref[1] = _to_columns(a_all), _to_columns(te_all)

    @pl.when(c == 0)
    def _():
        h_ref[...] = jnp.zeros_like(h_ref)
        hist_ref[...] = jnp.zeros_like(hist_ref)
        stage_step_sizes(dt_ref[0])

    dt_all, a_all = steps_row_ref[0], steps_row_ref[1]
    a_col_all, te_col_all = steps_col_ref[0], steps_col_ref[1]
    stage_step_sizes(dtn_ref[0])

    raw = jnp.concatenate([xr_ref[...], br_ref[...], cr_ref[...]], axis=1)
    ext = jnp.concatenate([hist_ref[...], raw], axis=0)
    hist_ref[...] = raw[cl - HALO:, :]
    cw = jnp.concatenate([wx_ref[...], wb_ref[...], wc_ref[...]], axis=1)
    acc = cw[0:1, :] * ext
    for kk in range(1, SSD_CONV):
        acc = cw[kk:kk + 1, :] * ext + pltpu.roll(acc, 1, axis=0)
    bias = jnp.concatenate([bx_ref[...], bb_ref[...], bc_ref[...]], axis=1)
    act = _silu(acc[HALO:, :] + bias)
    xs_all = act[:, :n_groups * hp]
    bm_all = act[:, n_groups * hp:n_groups * (hp + n)].astype(BF16)
    cm_all = act[:, n_groups * (hp + n):].astype(BF16)

    row = lax.broadcasted_iota(jnp.int32, (cl, cl), 0)
    col = lax.broadcasted_iota(jnp.int32, (cl, cl), 1)
    causal = col <= row
    lane_hp = lax.broadcasted_iota(jnp.int32, (cl, hp), 1)

    for g in range(n_groups):
        xs = xs_all[:, g * hp:(g + 1) * hp]
        bm = bm_all[:, g * n:(g + 1) * n]
        cm = cm_all[:, g * n:(g + 1) * n]
        dt = dt_all[g * SUBLANES:(g + 1) * SUBLANES]
        a_cum = a_all[g * SUBLANES:(g + 1) * SUBLANES]
        a_col = a_col_all[:, g * SUBLANES:(g + 1) * SUBLANES]
        te_col = te_col_all[:, g * SUBLANES:(g + 1) * SUBLANES]

        cb = _dot_nt(cm, bm)
        ws, xblocks = [], []
        for r in range(SSD_HPG):
            seg = a_col[:, r:r + 1] - a_cum[r:r + 1, :]
            decay = jnp.exp(jnp.where(causal, seg, -jnp.inf))
            ws.append((cb * decay * dt[r:r + 1, :]).astype(BF16))
            in_head = (lane_hp >= r * SSD_HEAD_DIM) & (lane_hp < (r + 1) * SSD_HEAD_DIM)
            xblocks.append(jnp.where(in_head, xs, 0.0).astype(BF16))
        y = _dot(jnp.concatenate(ws, axis=1), jnp.concatenate(xblocks, axis=0))

        h_prev = h_ref[g]
        y = y + _dot(cm, h_prev.astype(BF16)) * _expand_heads(jnp.exp(a_col), SSD_HEAD_DIM)
        xw = (xs * _expand_heads(te_col, SSD_HEAD_DIM)).astype(BF16)
        h_decay = _expand_heads(jnp.exp(a_col[cl - 1:cl, :]), SSD_HEAD_DIM)
        h_ref[g] = h_prev * h_decay + _dot_tn(bm, xw)

        y = (y + dsk_ref[g] * xs) * _silu(z_ref[:, g * hp:(g + 1) * hp])
        o_ref[:, g * hp:(g + 1) * hp] = _rms(y, ng_ref[g]).astype(o_ref.dtype)


def _ssd_core(xbc, z, dt_t, conv_w, conv_b, dt_bias, a_log, d_skip, norm_g, b, l):
    t = xbc.shape[0]
    g, hpg, cl, n, ng = SSD_GROUPS, SSD_HPG, SSD_CHUNK, SSD_STATE, SSD_GROUPS_PER_STEP
    hp = hpg * SSD_HEAD_DIM
    d_inner = g * hp
    nc = l // cl
    assert l % cl == 0 and hp % LANES == 0 and n == LANES and g % ng == 0
    bb, cb = d_inner // (ng * n), (d_inner + g * n) // (ng * n)
    pad = lambda p: jnp.pad(p.reshape(g, hpg, 1), ((0, 0), (0, SUBLANES - hpg), (0, 0)))
    per_lane = lambda p: jnp.repeat(p.reshape(g, hpg), SSD_HEAD_DIM, axis=1).reshape(g, 1, hp)
    conv_b = conv_b.reshape(1, -1)
    tok = lambda bi, gi, ci: bi * nc + ci
    return pl.pallas_call(
        functools.partial(_ssd_kernel, n_groups=ng),
        grid=(b, g // ng, nc),
        in_specs=[
            pl.BlockSpec((cl, ng * hp), lambda bi, gi, ci: (tok(bi, gi, ci), gi)),
            pl.BlockSpec((cl, ng * n), lambda bi, gi, ci: (tok(bi, gi, ci), bb + gi)),
            pl.BlockSpec((cl, ng * n), lambda bi, gi, ci: (tok(bi, gi, ci), cb + gi)),
            pl.BlockSpec((cl, ng * hp), lambda bi, gi, ci: (tok(bi, gi, ci), gi)),
            pl.BlockSpec((1, ng, SUBLANES, cl), lambda bi, gi, ci: (bi, gi, 0, ci)),
            pl.BlockSpec((1, ng, SUBLANES, cl), lambda bi, gi, ci: (bi, gi, 0, jnp.minimum(ci + 1, nc - 1))),
            pl.BlockSpec((SSD_CONV, ng * hp), lambda bi, gi, ci: (0, gi)),
            pl.BlockSpec((SSD_CONV, ng * n), lambda bi, gi, ci: (0, bb + gi)),
            pl.BlockSpec((SSD_CONV, ng * n), lambda bi, gi, ci: (0, cb + gi)),
            pl.BlockSpec((1, ng * hp), lambda bi, gi, ci: (0, gi)),
            pl.BlockSpec((1, ng * n), lambda bi, gi, ci: (0, bb + gi)),
            pl.BlockSpec((1, ng * n), lambda bi, gi, ci: (0, cb + gi)),
            pl.BlockSpec((ng, SUBLANES, 1), lambda bi, gi, ci: (gi, 0, 0)),
            pl.BlockSpec((ng, SUBLANES, 1), lambda bi, gi, ci: (gi, 0, 0)),
            pl.BlockSpec((ng, 1, hp), lambda bi, gi, ci: (gi, 0, 0)),
            pl.BlockSpec((ng, 1, hp), lambda bi, gi, ci: (gi, 0, 0)),
        ],
        out_specs=pl.BlockSpec((cl, ng * hp), lambda bi, gi, ci: (tok(bi, gi, ci), gi)),
        out_shape=jax.ShapeDtypeStruct((t, d_inner), BF16),
        scratch_shapes=[pltpu.VMEM((ng, n, hp), F32), pltpu.VMEM((HALO, ng * (hp + 2 * n)), F32),
                        pltpu.VMEM((2, ng * SUBLANES, cl), F32), pltpu.VMEM((2, cl, ng * SUBLANES), F32)],
        compiler_params=_params(3),
        name="ssd_core",
    )(xbc, xbc, xbc, z, dt_t, dt_t, conv_w, conv_w, conv_w, conv_b, conv_b, conv_b,
      pad(dt_bias), pad(a_log), per_lane(d_skip), norm_g.reshape(g, 1, hp))


def _ssd_layer(x, b, l, norm_g, w_in, conv_w, conv_b, dt_bias, a_log, d_skip, ssd_norm, w_out, layer):
    g, hpg = SSD_GROUPS, SSD_HPG
    d_inner = w_out.shape[1]
    conv_dim = conv_w.shape[1]
    n_heads = g * hpg
    z, xbc, dt = _norm_proj(
        x, norm_g, w_in, layer,
        [(0, d_inner, 1.0), (d_inner, conv_dim, 1.0), (d_inner + conv_dim, n_heads, 1.0)], [F32] * 3)
    dt_t = dt.reshape(b, l, g, hpg).transpose(0, 2, 3, 1)
    dt_t = jnp.pad(dt_t, ((0, 0), (0, 0), (0, SUBLANES - hpg), (0, 0)))
    y = _ssd_core(xbc, z, dt_t, conv_w, conv_b, dt_bias, a_log, d_skip, ssd_norm, b, l)
    return _proj_res(x, y, w_out, layer)


def _short_conv_kernel(x_ref, g_ref, wi_ref, cw_ref, wo_ref, o_ref, ext_ref):
    tm, d = x_ref.shape

    @pl.when(pl.program_id(1) == 0)
    def _():
        ext_ref[0:HALO, :] = jnp.zeros((HALO, ext_ref.shape[1]), F32)

    x = x_ref[...]
    xn = _rms(x, g_ref[...]).astype(BF16)
    w_part = lambda i: wi_ref[:, i * d:(i + 1) * d].astype(BF16)
    ext_ref[HALO:HALO + tm, :] = _dot(xn, w_part(1)) * _dot(xn, w_part(2))
    cw = cw_ref[...]
    u = cw[0:1, :] * ext_ref[pl.ds(HALO - (SC_WIDTH - 1), tm), :]
    for kk in range(1, SC_WIDTH):
        u = u + cw[kk:kk + 1, :] * ext_ref[pl.ds(HALO - (SC_WIDTH - 1) + kk, tm), :]
    ext_ref[0:HALO, :] = ext_ref[tm:tm + HALO, :]
    gated = (_dot(xn, w_part(0)) * u).astype(BF16)
    o_ref[...] = x + _dot(gated, wo_ref[...].astype(BF16))


def _short_conv_layer(x, b, l, norm_g, w_in, conv_w, w_out, layer):
    t, d = x.shape
    tm = SC_TM
    nl = l // tm
    assert l % tm == 0
    return pl.pallas_call(
        _short_conv_kernel,
        grid=(b, nl),
        in_specs=[
            pl.BlockSpec((tm, d), lambda bi, li: (bi * nl + li, 0)),
            _resident((1, d)),
            _resident((None, d, 3 * d), (layer, 0, 0)),
            _resident((None, SC_WIDTH, d), (layer, 0, 0)),
            _resident((None, d, d), (layer, 0, 0)),
        ],
        out_specs=pl.BlockSpec((tm, d), lambda bi, li: (bi * nl + li, 0)),
        out_shape=jax.ShapeDtypeStruct((t, d), F32),
        scratch_shapes=[pltpu.VMEM((tm + HALO, d), F32)],
        compiler_params=_params(2),
        name="short_conv",
    )(x, norm_g.reshape(1, d), w_in, conv_w, w_out)


def kernel(x, ffn1_norm, ffn1_w_gu, ffn1_w_down, mix_norm, ffn2_norm, ffn2_w_gu, ffn2_w_down,
           sb_w_qkv, sb_w_o, ssd_w_in, ssd_conv_w, ssd_conv_b, ssd_dt_bias, ssd_a_log, ssd_d,
           ssd_norm, ssd_w_out, sc_w_in, sc_conv_w, sc_w_out, final_norm):
    b, l, d = x.shape
    depth = ffn1_norm.shape[0]
    h = x.reshape(b * l, d)
    for i in range(depth):
        h = _ffn(h, ffn1_norm[i], ffn1_w_gu, ffn1_w_down, i)
        kind, j = i % N_MIXERS, i // N_MIXERS
        if kind == 0:
            h = _sb_layer(h, b, l, mix_norm[i], sb_w_qkv, sb_w_o, j)
        elif kind == 1:
            h = _ssd_layer(h, b, l, mix_norm[i], ssd_w_in, ssd_conv_w[j], ssd_conv_b[j],
                           ssd_dt_bias[j], ssd_a_log[j], ssd_d[j], ssd_norm[j], ssd_w_out, j)
        else:
            h = _short_conv_layer(h, b, l, mix_norm[i], sc_w_in, sc_conv_w, sc_w_out, j)
        h = _ffn(h, ffn2_norm[i], ffn2_w_gu, ffn2_w_down, i,
                 final_g=final_norm if i == depth - 1 else None)
    return h.reshape(b, l, d)
```

```python
import functools

import jax
import jax.numpy as jnp
from jax import lax
from jax.experimental import pallas as pl
from jax.experimental.pallas import tpu as pltpu

F32 = jnp.float32
BF16 = jnp.bfloat16

RMS_EPS = 1e-6
LOG2_E = 1.4426950408889634
N_MIXERS = 3
SB_HEAD_DIM = 64
SSD_HEAD_DIM = 64
SSD_GROUPS = 8
SSD_HPG = 4
SSD_STATE = 128
SSD_CONV = 4
SSD_CHUNK = 128
SC_WIDTH = 3

LANES = 128
SUBLANES = 8
VMEM_LIMIT_BYTES = 60 * 1024 * 1024

FFN_TM = 1024
FFN_TF = 256
PROJ_TM = 512
PROJ_RES_TM = 1024
PROJ_TN = 512
ATT_T = 256
ATT_PAIRS_PER_STEP = 4
ATT_PAIRS_TOGETHER = 2
ATT_DEAD_CARRY = 160.0
SSD_GROUPS_PER_STEP = 4
SC_TM = 512
HALO = SUBLANES


def _params(n_axes):
    return pltpu.CompilerParams(
        dimension_semantics=("arbitrary",) * n_axes,
        vmem_limit_bytes=VMEM_LIMIT_BYTES)


def _resident(shape, index=None):
    index = (0,) * len(shape) if index is None else index
    return pl.BlockSpec(shape, lambda *_: index, pipeline_mode=pl.Buffered(1))


def _rms(x, g):
    ms = jnp.mean(x * x, axis=-1, keepdims=True)
    return x * lax.rsqrt(ms + RMS_EPS) * g


def _silu(x):
    h = 0.5 * x
    return h + h * jnp.tanh(h)


def _dot(a, b):
    return jnp.dot(a, b, preferred_element_type=F32)


def _dot_nt(a, b):
    return lax.dot_general(a, b, (((1,), (1,)), ((), ())), preferred_element_type=F32)


def _dot_tn(a, b):
    return lax.dot_general(a, b, (((0,), (0,)), ((), ())), preferred_element_type=F32)


def _ffn_kernel(x_ref, g_ref, wgu_ref, wd_ref, *refs, n_chunks, tf, final_norm):
    fg_ref = refs[0] if final_norm else None
    o_ref, xn_ref, acc_ref = refs[-3:]
    x = x_ref[...]
    d_ff = n_chunks * tf
    xn_ref[...] = _rms(x, g_ref[...]).astype(BF16)
    for c in range(n_chunks):
        gate = _dot(xn_ref[...], wgu_ref[:, c * tf:(c + 1) * tf].astype(BF16))
        up = _dot(xn_ref[...], wgu_ref[:, d_ff + c * tf:d_ff + (c + 1) * tf].astype(BF16))
        h = (_silu(gate) * up).astype(BF16)
        part = _dot(h, wd_ref[c * tf:(c + 1) * tf, :].astype(BF16))
        if c == 0:
            acc_ref[...] = part
        else:
            acc_ref[...] += part
    y = x + 0.5 * acc_ref[...]
    if final_norm:
        y = _rms(y, fg_ref[...])
    o_ref[...] = y


def _ffn(x, norm_g, w_gu, w_down, layer, final_g=None):
    t, d = x.shape
    d_ff = w_down.shape[1]
    tm, tf = FFN_TM, FFN_TF
    n_chunks = d_ff // tf
    assert t % tm == 0 and d_ff % tf == 0
    final_norm = final_g is not None
    extra = [final_g.reshape(1, d)] if final_norm else []
    return pl.pallas_call(
        functools.partial(_ffn_kernel, n_chunks=n_chunks, tf=tf, final_norm=final_norm),
        grid=(t // tm,),
        in_specs=[
            pl.BlockSpec((tm, d), lambda i: (i, 0)),
            _resident((1, d)),
            _resident((None, d, 2 * d_ff), (layer, 0, 0)),
            _resident((None, d_ff, d), (layer, 0, 0)),
        ] + [_resident((1, d))] * len(extra),
        out_specs=pl.BlockSpec((tm, d), lambda i: (i, 0)),
        out_shape=jax.ShapeDtypeStruct((t, d), F32),
        scratch_shapes=[pltpu.VMEM((tm, d), BF16), pltpu.VMEM((tm, d), F32)],
        compiler_params=_params(1),
        name="ffn",
    )(x, norm_g.reshape(1, d), w_gu, w_down, *extra)


def _norm_proj_kernel(x_ref, g_ref, w_ref, *o_refs, segments, tn):
    xn = _rms(x_ref[...], g_ref[...]).astype(BF16)
    for o_ref, (start, width, scale) in zip(o_refs, segments):
        for n0 in range(0, width, tn):
            n1 = min(n0 + tn, width)
            w = w_ref[:, start + n0:start + n1]
            if scale != 1.0:
                w = w * scale
            o_ref[:, n0:n1] = _dot(xn, w.astype(BF16)).astype(o_ref.dtype)


def _norm_proj(x, norm_g, w, layer, segments, out_dtypes):
    t, d = x.shape
    tm = PROJ_TM
    assert t % tm == 0 and all(s[0] % LANES == 0 for s in segments)
    return pl.pallas_call(
        functools.partial(_norm_proj_kernel, segments=tuple(segments), tn=PROJ_TN),
        grid=(t // tm,),
        in_specs=[pl.BlockSpec((tm, d), lambda i: (i, 0)), _resident((1, d)),
                  _resident((None,) + w.shape[1:], (layer, 0, 0))],
        out_specs=[pl.BlockSpec((tm, s[1]), lambda i: (i, 0)) for s in segments],
        out_shape=[jax.ShapeDtypeStruct((t, s[1]), dt) for s, dt in zip(segments, out_dtypes)],
        compiler_params=_params(1),
        name="norm_proj",
    )(x, norm_g.reshape(1, d), w)


def _proj_res_kernel(x_ref, y_ref, w_ref, o_ref):
    o_ref[...] = x_ref[...] + _dot(y_ref[...], w_ref[...].astype(BF16))


def _proj_res(x, y, w, layer):
    t, d = x.shape
    k = y.shape[1]
    tm = PROJ_RES_TM
    assert t % tm == 0
    return pl.pallas_call(
        _proj_res_kernel,
        grid=(t // tm,),
        in_specs=[
            pl.BlockSpec((tm, d), lambda i: (i, 0)),
            pl.BlockSpec((tm, k), lambda i: (i, 0)),
            _resident((None, k, d), (layer, 0, 0)),
        ],
        out_specs=pl.BlockSpec((tm, d), lambda i: (i, 0)),
        out_shape=jax.ShapeDtypeStruct((t, d), F32),
        compiler_params=_params(1),
        name="proj_res",
    )(x, y, w)


def _sb_attn_kernel(q_ref, k_ref, v_ref, o_ref, lb_a, sp_a, rs_a, lb_b, sp_b, rs_b, acc_ref, c_ref,
                    *, t, n_together):
    qi = pl.program_id(2)
    hd = SB_HEAD_DIM
    n_p = n_together
    sub = 2 * n_p * t
    buf_a, buf_b = (lb_a, sp_a, rs_a), (lb_b, sp_b, rs_b)
    both, first, second = (0, 2), (0, 1), (1, 2)
    in_a = lax.broadcasted_iota(jnp.int32, (1, 2 * hd), 1) < hd
    row = lax.broadcasted_iota(jnp.int32, (t, t), 0)
    col = lax.broadcasted_iota(jnp.int32, (t, t), 1)
    strict_lower = jnp.concatenate([col < row] * (2 * n_p), axis=0)
    tri = jnp.where(row > col, 1.0, 0.0).astype(BF16)
    sign_bit = jnp.uint32(0x80000000)
    top = 2 * qi + 1

    def mask_diagonal(x, fill):
        head = jnp.where(strict_lower, x[:sub], fill)
        return head if x.shape[0] == sub else jnp.concatenate([head, x[sub:]], axis=0)

    def pair_group(grp, _):
        lanes = [pl.ds(pl.multiple_of((grp * n_p + p) * 2 * hd, 2 * hd), 2 * hd) for p in range(n_p)]
        q_parts = []
        for s in range(2):
            for p in range(n_p):
                q2 = q_ref[0, s * t:(s + 1) * t, lanes[p]]
                q_parts += [jnp.where(in_a, q2, 0), jnp.where(in_a, 0, q2)]
        qs = jnp.concatenate(q_parts, axis=0)

        def scores(j, buf, subs, diag):
            lb_ref, sp_ref, rs_ref = buf
            r0, r1 = subs[0] * sub, subs[1] * sub
            rows_k = pl.ds(pl.multiple_of(j * t, t), t)
            k2 = [k_ref[0, rows_k, lanes[p]] for p in range(n_p)]
            z = jnp.concatenate(
                [_dot_nt(qs[(s * n_p + p) * 2 * t:(s * n_p + p + 1) * 2 * t], k2[p])
                 for s in range(*subs) for p in range(n_p)], axis=0)
            neg_abs = lax.bitcast_convert_type(lax.bitcast_convert_type(z, jnp.uint32) | sign_bit, F32)
            lb = jnp.minimum(z, 0.0) - jnp.log2(1.0 + jnp.exp2(neg_abs))
            sp = z - lb
            if diag:
                sp, lb = mask_diagonal(sp, 0.0), mask_diagonal(lb, -jnp.inf)
            lb_ref[r0:r1, :] = lb
            sp_ref[r0:r1, :] = sp.astype(BF16)
            rs_ref[r0:r1, :] = jnp.broadcast_to(jnp.sum(sp, axis=-1, keepdims=True), (r1 - r0, LANES))

        def values(j, buf, subs):
            lb_ref, sp_ref, rs_ref = buf
            r0, r1 = subs[0] * sub, subs[1] * sub
            rows_k = pl.ds(pl.multiple_of(j * t, t), t)
            c = c_ref[r0:r1, :]
            tail = _dot(sp_ref[r0:r1, :], tri)
            c_wide = jnp.concatenate([c] * (t // LANES), axis=1)
            att = jnp.exp2(lb_ref[r0:r1, :] - tail - c_wide).astype(BF16)
            c_ref[r0:r1, :] = c + rs_ref[r0:r1, :]
            for p in range(n_p):
                blocks = []
                for s in range(subs[1] - subs[0]):
                    base = (s * n_p + p) * 2 * t
                    blocks.append(jnp.concatenate([att[base:base + t], att[base + t:base + 2 * t]], axis=1))
                v2 = v_ref[0, rows_k, lanes[p]]
                vv = jnp.concatenate([jnp.where(in_a, v2, 0), jnp.where(in_a, 0, v2)], axis=0)
                acc_ref[p, subs[0] * t:subs[1] * t, :] += _dot(jnp.concatenate(blocks, axis=0), vv)

        def tile_pair(state):
            i = state[0]
            j = top - 3 - 2 * i
            values(j, buf_b, both)
            scores(j - 1, buf_a, both, False)
            values(j - 1, buf_a, both)
            scores(j - 2, buf_b, both, False)
            return i + 1, jnp.min(c_ref[...])

        def live(state):
            return (state[0] < qi - 1) & (state[1] < ATT_DEAD_CARRY)

        acc_ref[...] = jnp.zeros_like(acc_ref)
        c_ref[...] = jnp.zeros_like(c_ref)

        @pl.when(qi == 0)
        def _():
            scores(top, buf_a, second, True)
            values(top, buf_a, second)
            scores(top - 1, buf_b, both, True)
            values(top - 1, buf_b, both)

        @pl.when(qi > 0)
        def _():
            scores(top, buf_a, second, True)
            scores(top - 1, buf_b, both, True)
            values(top, buf_a, second)
            scores(top - 2, buf_a, first, False)
            values(top - 1, buf_b, both)
            c_min0 = jnp.minimum(jnp.min(c_ref[sub:, :]), jnp.min(c_ref[:sub, :] + buf_a[2][:sub, :]))
            values(top - 2, buf_a, first)

            @pl.when(c_min0 < ATT_DEAD_CARRY)
            def _():
                scores(top - 2, buf_a, second, False)
                values(top - 2, buf_a, second)
                scores(top - 3, buf_b, both, False)
                n_pairs, c_min = lax.while_loop(live, tile_pair, (jnp.int32(0), c_min0))

                @pl.when(c_min < ATT_DEAD_CARRY)
                def _():
                    values(top - 3 - 2 * n_pairs, buf_b, both)

        for p in range(n_p):
            o_ref[0, :, lanes[p]] = acc_ref[p].astype(o_ref.dtype)
        return 0

    lax.fori_loop(0, q_ref.shape[2] // (2 * hd * n_p), pair_group, 0)


def _sb_attention(q, k, v):
    b, l, d = q.shape
    t, n_p = ATT_T, ATT_PAIRS_TOGETHER
    tq = 2 * t
    hp = 2 * SB_HEAD_DIM * ATT_PAIRS_PER_STEP
    assert l % tq == 0 and d % hp == 0 and 2 * SB_HEAD_DIM == LANES and ATT_PAIRS_PER_STEP % n_p == 0
    rows = 4 * n_p * t
    stage = [pltpu.VMEM((rows, t), F32), pltpu.VMEM((rows, t), BF16), pltpu.VMEM((rows, LANES), F32)]
    return pl.pallas_call(
        functools.partial(_sb_attn_kernel, t=t, n_together=n_p),
        grid=(b, d // hp, l // tq),
        in_specs=[
            pl.BlockSpec((1, tq, hp), lambda bi, hi, qi: (bi, qi, hi)),
            pl.BlockSpec((1, l, hp), lambda bi, hi, qi: (bi, 0, hi)),
            pl.BlockSpec((1, l, hp), lambda bi, hi, qi: (bi, 0, hi)),
        ],
        out_specs=pl.BlockSpec((1, tq, hp), lambda bi, hi, qi: (bi, qi, hi)),
        out_shape=jax.ShapeDtypeStruct((b, l, d), BF16),
        scratch_shapes=stage + stage + [pltpu.VMEM((n_p, tq, LANES), F32),
                                        pltpu.VMEM((rows, LANES), F32)],
        compiler_params=_params(3),
        name="sb_attn",
    )(q, k, v)


def _sb_layer(x, b, l, norm_g, w_qkv, w_o, layer):
    t, d = x.shape
    scale = LOG2_E * SB_HEAD_DIM ** -0.5
    q, k, v = _norm_proj(x, norm_g, w_qkv, layer,
                         [(0, d, scale), (d, d, 1.0), (2 * d, d, 1.0)], [BF16] * 3)
    o = _sb_attention(q.reshape(b, l, d), k.reshape(b, l, d), v.reshape(b, l, d))
    return _proj_res(x, o.reshape(t, d), w_o, layer)


def _expand_heads(m, width):
    rows = m.shape[0]
    lane = lax.broadcasted_iota(jnp.int32, (rows, SSD_HPG * width), 1)
    out = jnp.broadcast_to(m[:, SSD_HPG - 1:SSD_HPG], (rows, SSD_HPG * width))
    for r in range(SSD_HPG - 2, -1, -1):
        out = jnp.where(lane < (r + 1) * width, m[:, r:r + 1], out)
    return out


def _to_columns(rows8):
    n = rows8.shape[1]
    padded = jnp.concatenate([rows8, jnp.zeros((n - rows8.shape[0], n), F32)], axis=0)
    return padded.T[:, :rows8.shape[0]]


def _ssd_kernel(xr_ref, br_ref, cr_ref, z_ref, dt_ref, dtn_ref, wx_ref, wb_ref, wc_ref, bx_ref,
                bb_ref, bc_ref, dtb_ref, alog_ref, dsk_ref, ng_ref, o_ref, h_ref, hist_ref,
                steps_row_ref, steps_col_ref, *, n_groups):
    c = pl.program_id(2)
    cl, hp, n = SSD_CHUNK, SSD_HPG * SSD_HEAD_DIM, SSD_STATE
    rows = n_groups * SUBLANES

    def stage_step_sizes(dt_blk):
        dt_in = (dt_blk + dtb_ref[...]).reshape(rows, cl)
        dt_all = jnp.maximum(dt_in, 0.0) + jnp.log(1.0 + jnp.exp(-jnp.abs(dt_in)))
        a_all = dt_all * (-jnp.exp(alog_ref[...].reshape(rows, 1)))
        lane_cl = lax.broadcasted_iota(jnp.int32, (rows, cl), 1)
        sh = 1
        while sh < cl:
            a_all = a_all + jnp.where(lane_cl >= sh, pltpu.roll(a_all, sh, axis=1), 0.0)
            sh *= 2
        te_all = jnp.exp(a_all[:, cl - 1:cl] - a_all) * dt_all
        steps_row_ref[0], steps_row_ref[1] = dt_all, a_all
        steps_col_ref[0], steps_col_ref[1] = _to_columns(a_all), _to_columns(te_all)

    @pl.when(c == 0)
    def _():
        h_ref[...] = jnp.zeros_like(h_ref)
        hist_ref[...] = jnp.zeros_like(hist_ref)
        stage_step_sizes(dt_ref[0])

    dt_all, a_all = steps_row_ref[0], steps_row_ref[1]
    a_col_all, te_col_all = steps_col_ref[0], steps_col_ref[1]
    stage_step_sizes(dtn_ref[0])

    raw = jnp.concatenate([xr_ref[...], br_ref[...], cr_ref[...]], axis=1)
    ext = jnp.concatenate([hist_ref[...], raw], axis=0)
    hist_ref[...] = raw[cl - HALO:, :]
    cw = jnp.concatenate([wx_ref[...], wb_ref[...], wc_ref[...]], axis=1)
    acc = cw[0:1, :] * ext
    for kk in range(1, SSD_CONV):
        acc = cw[kk:kk + 1, :] * ext + pltpu.roll(acc, 1, axis=0)
    bias = jnp.concatenate([bx_ref[...], bb_ref[...], bc_ref[...]], axis=1)
    act = _silu(acc[HALO:, :] + bias)
    xs_all = act[:, :n_groups * hp]
    bm_all = act[:, n_groups * hp:n_groups * (hp + n)].astype(BF16)
    cm_all = act[:, n_groups * (hp + n):].astype(BF16)

    row = lax.broadcasted_iota(jnp.int32, (cl, cl), 0)
    col = lax.broadcasted_iota(jnp.int32, (cl, cl), 1)
    causal = col <= row
    lane_hp = lax.broadcasted_iota(jnp.int32, (cl, hp), 1)

    for g in range(n_groups):
        xs = xs_all[:, g * hp:(g + 1) * hp]
        bm = bm_all[:, g * n:(g + 1) * n]
        cm = cm_all[:, g * n:(g + 1) * n]
        dt = dt_all[g * SUBLANES:(g + 1) * SUBLANES]
        a_cum = a_all[g * SUBLANES:(g + 1) * SUBLANES]
        a_col = a_col_all[:, g * SUBLANES:(g + 1) * SUBLANES]
        te_col = te_col_all[:, g * SUBLANES:(g + 1) * SUBLANES]

        cb = _dot_nt(cm, bm)
        ws, xblocks = [], []
        for r in range(SSD_HPG):
            seg = a_col[:, r:r + 1] - a_cum[r:r + 1, :]
            decay = jnp.exp(jnp.where(causal, seg, -jnp.inf))
            ws.append((cb * decay * dt[r:r + 1, :]).astype(BF16))
            in_head = (lane_hp >= r * SSD_HEAD_DIM) & (lane_hp < (r + 1) * SSD_HEAD_DIM)
            xblocks.append(jnp.where(in_head, xs, 0.0).astype(BF16))
        y = _dot(jnp.concatenate(ws, axis=1), jnp.concatenate(xblocks, axis=0))

        h_prev = h_ref[g]
        y = y + _dot(cm, h_prev.astype(BF16)) * _expand_heads(jnp.exp(a_col), SSD_HEAD_DIM)
        xw = (xs * _expand_heads(te_col, SSD_HEAD_DIM)).astype(BF16)
        h_decay = _expand_heads(jnp.exp(a_col[cl - 1:cl, :]), SSD_HEAD_DIM)
        h_ref[g] = h_prev * h_decay + _dot_tn(bm, xw)

        y = (y + dsk_ref[g] * xs) * _silu(z_ref[:, g * hp:(g + 1) * hp])
        o_ref[:, g * hp:(g + 1) * hp] = _rms(y, ng_ref[g]).astype(o_ref.dtype)


def _ssd_core(xbc, z, dt_t, conv_w, conv_b, dt_bias, a_log, d_skip, norm_g, b, l):
    t = xbc.shape[0]
    g, hpg, cl, n, ng = SSD_GROUPS, SSD_HPG, SSD_CHUNK, SSD_STATE, SSD_GROUPS_PER_STEP
    hp = hpg * SSD_HEAD_DIM
    d_inner = g * hp
    nc = l // cl
    assert l % cl == 0 and hp % LANES == 0 and n == LANES and g % ng == 0
    bb, cb = d_inner // (ng * n), (d_inner + g * n) // (ng * n)
    pad = lambda p: jnp.pad(p.reshape(g, hpg, 1), ((0, 0), (0, SUBLANES - hpg), (0, 0)))
    per_lane = lambda p: jnp.repeat(p.reshape(g, hpg), SSD_HEAD_DIM, axis=1).reshape(g, 1, hp)
    conv_b = conv_b.reshape(1, -1)
    tok = lambda bi, gi, ci: bi * nc + ci
    return pl.pallas_call(
        functools.partial(_ssd_kernel, n_groups=ng),
        grid=(b, g // ng, nc),
        in_specs=[
            pl.BlockSpec((cl, ng * hp), lambda bi, gi, ci: (tok(bi, gi, ci), gi)),
            pl.BlockSpec((cl, ng * n), lambda bi, gi, ci: (tok(bi, gi, ci), bb + gi)),
            pl.BlockSpec((cl, ng * n), lambda bi, gi, ci: (tok(bi, gi, ci), cb + gi)),
            pl.BlockSpec((cl, ng * hp), lambda bi, gi, ci: (tok(bi, gi, ci), gi)),
            pl.BlockSpec((1, ng, SUBLANES, cl), lambda bi, gi, ci: (bi, gi, 0, ci)),
            pl.BlockSpec((1, ng, SUBLANES, cl), lambda bi, gi, ci: (bi, gi, 0, jnp.minimum(ci + 1, nc - 1))),
            pl.BlockSpec((SSD_CONV, ng * hp), lambda bi, gi, ci: (0, gi)),
            pl.BlockSpec((SSD_CONV, ng * n), lambda bi, gi, ci: (0, bb + gi)),
            pl.BlockSpec((SSD_CONV, ng * n), lambda bi, gi, ci: (0, cb + gi)),
            pl.BlockSpec((1, ng * hp), lambda bi, gi, ci: (0, gi)),
            pl.BlockSpec((1, ng * n), lambda bi, gi, ci: (0, bb + gi)),
            pl.BlockSpec((1, ng * n), lambda bi, gi, ci: (0, cb + gi)),
            pl.BlockSpec((ng, SUBLANES, 1), lambda bi, gi, ci: (gi, 0, 0)),
            pl.BlockSpec((ng, SUBLANES, 1), lambda bi, gi, ci: (gi, 0, 0)),
            pl.BlockSpec((ng, 1, hp), lambda bi, gi, ci: (gi, 0, 0)),
            pl.BlockSpec((ng, 1, hp), lambda bi, gi, ci: (gi, 0, 0)),
        ],
        out_specs=pl.BlockSpec((cl, ng * hp), lambda bi, gi, ci: (tok(bi, gi, ci), gi)),
        out_shape=jax.ShapeDtypeStruct((t, d_inner), BF16),
        scratch_shapes=[pltpu.VMEM((ng, n, hp), F32), pltpu.VMEM((HALO, ng * (hp + 2 * n)), F32),
                        pltpu.VMEM((2, ng * SUBLANES, cl), F32), pltpu.VMEM((2, cl, ng * SUBLANES), F32)],
        compiler_params=_params(3),
        name="ssd_core",
    )(xbc, xbc, xbc, z, dt_t, dt_t, conv_w, conv_w, conv_w, conv_b, conv_b, conv_b,
      pad(dt_bias), pad(a_log), per_lane(d_skip), norm_g.reshape(g, 1, hp))


def _ssd_layer(x, b, l, norm_g, w_in, conv_w, conv_b, dt_bias, a_log, d_skip, ssd_norm, w_out, layer):
    g, hpg = SSD_GROUPS, SSD_HPG
    d_inner = w_out.shape[1]
    conv_dim = conv_w.shape[1]
    n_heads = g * hpg
    z, xbc, dt = _norm_proj(
        x, norm_g, w_in, layer,
        [(0, d_inner, 1.0), (d_inner, conv_dim, 1.0), (d_inner + conv_dim, n_heads, 1.0)], [F32] * 3)
    dt_t = dt.reshape(b, l, g, hpg).transpose(0, 2, 3, 1)
    dt_t = jnp.pad(dt_t, ((0, 0), (0, 0), (0, SUBLANES - hpg), (0, 0)))
    y = _ssd_core(xbc, z, dt_t, conv_w, conv_b, dt_bias, a_log, d_skip, ssd_norm, b, l)
    return _proj_res(x, y, w_out, layer)


def _short_conv_kernel(x_ref, g_ref, wi_ref, cw_ref, wo_ref, o_ref, ext_ref):
    tm, d = x_ref.shape

    @pl.when(pl.program_id(1) == 0)
    def _():
        ext_ref[0:HALO, :] = jnp.zeros((HALO, ext_ref.shape[1]), F32)

    x = x_ref[...]
    xn = _rms(x, g_ref[...]).astype(BF16)
    w_part = lambda i: wi_ref[:, i * d:(i + 1) * d].astype(BF16)
    ext_ref[HALO:HALO + tm, :] = _dot(xn, w_part(1)) * _dot(xn, w_part(2))
    cw = cw_ref[...]
    u = cw[0:1, :] * ext_ref[pl.ds(HALO - (SC_WIDTH - 1), tm), :]
    for kk in range(1, SC_WIDTH):
        u = u + cw[kk:kk + 1, :] * ext_ref[pl.ds(HALO - (SC_WIDTH - 1) + kk, tm), :]
    ext_ref[0:HALO, :] = ext_ref[tm:tm + HALO, :]
    gated = (_dot(xn, w_part(0)) * u).astype(BF16)
    o_ref[...] = x + _dot(gated, wo_ref[...].astype(BF16))


def _short_conv_layer(x, b, l, norm_g, w_in, conv_w, w_out, layer):
    t, d = x.shape
    tm = SC_TM
    nl = l // tm
    assert l % tm == 0
    return pl.pallas_call(
        _short_conv_kernel,
        grid=(b, nl),
        in_specs=[
            pl.BlockSpec((tm, d), lambda bi, li: (bi * nl + li, 0)),
            _resident((1, d)),
            _resident((None, d, 3 * d), (layer, 0, 0)),
            _resident((None, SC_WIDTH, d), (layer, 0, 0)),
            _resident((None, d, d), (layer, 0, 0)),
        ],
        out_specs=pl.BlockSpec((tm, d), lambda bi, li: (bi * nl + li, 0)),
        out_shape=jax.ShapeDtypeStruct((t, d), F32),
        scratch_shapes=[pltpu.VMEM((tm + HALO, d), F32)],
        compiler_params=_params(2),
        name="short_conv",
    )(x, norm_g.reshape(1, d), w_in, conv_w, w_out)


def kernel(x, ffn1_norm, ffn1_w_gu, ffn1_w_down, mix_norm, ffn2_norm, ffn2_w_gu, ffn2_w_down,
           sb_w_qkv, sb_w_o, ssd_w_in, ssd_conv_w, ssd_conv_b, ssd_dt_bias, ssd_a_log, ssd_d,
           ssd_norm, ssd_w_out, sc_w_in, sc_conv_w, sc_w_out, final_norm):
    b, l, d = x.shape
    depth = ffn1_norm.shape[0]
    h = x.reshape(b * l, d)
    for i in range(depth):
        h = _ffn(h, ffn1_norm[i], ffn1_w_gu, ffn1_w_down, i)
        kind, j = i % N_MIXERS, i // N_MIXERS
        if kind == 0:
            h = _sb_layer(h, b, l, mix_norm[i], sb_w_qkv, sb_w_o, j)
        elif kind == 1:
            h = _ssd_layer(h, b, l, mix_norm[i], ssd_w_in, ssd_conv_w[j], ssd_conv_b[j],
                           ssd_dt_bias[j], ssd_a_log[j], ssd_d[j], ssd_norm[j], ssd_w_out, j)
        else:
            h = _short_conv_layer(h, b, l, mix_norm[i], sc_w_in, sc_conv_w, sc_w_out, j)
        h = _ffn(h, ffn2_norm[i], ffn2_w_gu, ffn2_w_down, i,
                 final_g=final_norm if i == depth - 1 else None)
    return h.reshape(b, l, d)
```

```python
import functools

import jax
import jax.numpy as jnp
from jax import lax
from jax.experimental import pallas as pl
from jax.experimental.pallas import tpu as pltpu

F32 = jnp.float32
BF16 = jnp.bfloat16

RMS_EPS = 1e-6
LOG2_E = 1.4426950408889634
N_MIXERS = 3
SB_HEAD_DIM = 64
SSD_HEAD_DIM = 64
SSD_GROUPS = 8
SSD_HPG = 4
SSD_STATE = 128
SSD_CONV = 4
SSD_CHUNK = 128
SC_WIDTH = 3

LANES = 128
SUBLANES = 8
VMEM_LIMIT_BYTES = 60 * 1024 * 1024

FFN_TM = 1024
FFN_TF = 256
PROJ_TM = 512
PROJ_RES_TM = 1024
PROJ_TN = 512
ATT_T = 256
ATT_PAIRS_PER_STEP = 4
ATT_PAIRS_TOGETHER = 2
ATT_DEAD_CARRY = 160.0
SSD_GROUPS_PER_STEP = 4
SSD_CHUNKS_PER_STEP = 4
SC_TM = 512
HALO = SUBLANES


def _params(n_axes):
    return pltpu.CompilerParams(
        dimension_semantics=("arbitrary",) * n_axes,
        vmem_limit_bytes=VMEM_LIMIT_BYTES)


def _resident(shape, index=None):
    index = (0,) * len(shape) if index is None else index
    return pl.BlockSpec(shape, lambda *_: index, pipeline_mode=pl.Buffered(1))


def _rms(x, g):
    ms = jnp.mean(x * x, axis=-1, keepdims=True)
    return x * lax.rsqrt(ms + RMS_EPS) * g


def _silu(x):
    h = 0.5 * x
    return h + h * jnp.tanh(h)


def _dot(a, b):
    return jnp.dot(a, b, preferred_element_type=F32)


def _dot_nt(a, b):
    return lax.dot_general(a, b, (((1,), (1,)), ((), ())), preferred_element_type=F32)


def _dot_tn(a, b):
    return lax.dot_general(a, b, (((0,), (0,)), ((), ())), preferred_element_type=F32)


def _ffn_kernel(x_ref, g_ref, wgu_ref, wd_ref, *refs, n_chunks, tf, final_norm):
    fg_ref = refs[0] if final_norm else None
    o_ref, xn_ref, acc_ref = refs[-3:]
    x = x_ref[...]
    d_ff = n_chunks * tf
    xn_ref[...] = _rms(x, g_ref[...]).astype(BF16)
    for c in range(n_chunks):
        gate = _dot(xn_ref[...], wgu_ref[:, c * tf:(c + 1) * tf].astype(BF16))
        up = _dot(xn_ref[...], wgu_ref[:, d_ff + c * tf:d_ff + (c + 1) * tf].astype(BF16))
        h = (_silu(gate) * up).astype(BF16)
        part = _dot(h, wd_ref[c * tf:(c + 1) * tf, :].astype(BF16))
        if c == 0:
            acc_ref[...] = part
        else:
            acc_ref[...] += part
    y = x + 0.5 * acc_ref[...]
    if final_norm:
        y = _rms(y, fg_ref[...])
    o_ref[...] = y


def _ffn(x, norm_g, w_gu, w_down, layer, final_g=None):
    t, d = x.shape
    d_ff = w_down.shape[1]
    tm, tf = FFN_TM, FFN_TF
    n_chunks = d_ff // tf
    assert t % tm == 0 and d_ff % tf == 0
    final_norm = final_g is not None
    extra = [final_g.reshape(1, d)] if final_norm else []
    return pl.pallas_call(
        functools.partial(_ffn_kernel, n_chunks=n_chunks, tf=tf, final_norm=final_norm),
        grid=(t // tm,),
        in_specs=[
            pl.BlockSpec((tm, d), lambda i: (i, 0)),
            _resident((1, d)),
            _resident((None, d, 2 * d_ff), (layer, 0, 0)),
            _resident((None, d_ff, d), (layer, 0, 0)),
        ] + [_resident((1, d))] * len(extra),
        out_specs=pl.BlockSpec((tm, d), lambda i: (i, 0)),
        out_shape=jax.ShapeDtypeStruct((t, d), F32),
        scratch_shapes=[pltpu.VMEM((tm, d), BF16), pltpu.VMEM((tm, d), F32)],
        compiler_params=_params(1),
        name="ffn",
    )(x, norm_g.reshape(1, d), w_gu, w_down, *extra)


def _norm_proj_kernel(x_ref, g_ref, w_ref, *o_refs, segments, tn):
    xn = _rms(x_ref[...], g_ref[...]).astype(BF16)
    for o_ref, (start, width, scale) in zip(o_refs, segments):
        for n0 in range(0, width, tn):
            n1 = min(n0 + tn, width)
            w = w_ref[:, start + n0:start + n1]
            if scale != 1.0:
                w = w * scale
            o_ref[:, n0:n1] = _dot(xn, w.astype(BF16)).astype(o_ref.dtype)


def _norm_proj(x, norm_g, w, layer, segments, out_dtypes):
    t, d = x.shape
    tm = PROJ_TM
    assert t % tm == 0 and all(s[0] % LANES == 0 for s in segments)
    return pl.pallas_call(
        functools.partial(_norm_proj_kernel, segments=tuple(segments), tn=PROJ_TN),
        grid=(t // tm,),
        in_specs=[pl.BlockSpec((tm, d), lambda i: (i, 0)), _resident((1, d)),
                  _resident((None,) + w.shape[1:], (layer, 0, 0))],
        out_specs=[pl.BlockSpec((tm, s[1]), lambda i: (i, 0)) for s in segments],
        out_shape=[jax.ShapeDtypeStruct((t, s[1]), dt) for s, dt in zip(segments, out_dtypes)],
        compiler_params=_params(1),
        name="norm_proj",
    )(x, norm_g.reshape(1, d), w)


def _proj_res_kernel(x_ref, y_ref, w_ref, o_ref):
    o_ref[...] = x_ref[...] + _dot(y_ref[...], w_ref[...].astype(BF16))


def _proj_res(x, y, w, layer):
    t, d = x.shape
    k = y.shape[1]
    tm = PROJ_RES_TM
    assert t % tm == 0
    return pl.pallas_call(
        _proj_res_kernel,
        grid=(t // tm,),
        in_specs=[
            pl.BlockSpec((tm, d), lambda i: (i, 0)),
            pl.BlockSpec((tm, k), lambda i: (i, 0)),
            _resident((None, k, d), (layer, 0, 0)),
        ],
        out_specs=pl.BlockSpec((tm, d), lambda i: (i, 0)),
        out_shape=jax.ShapeDtypeStruct((t, d), F32),
        compiler_params=_params(1),
        name="proj_res",
    )(x, y, w)


def _sb_attn_kernel(q_ref, k_ref, v_ref, o_ref, lb_a, sp_a, rs_a, lb_b, sp_b, rs_b, acc_ref, c_ref,
                    *, t, n_together):
    qi = pl.program_id(2)
    hd = SB_HEAD_DIM
    n_p = n_together
    sub = 2 * n_p * t
    buf_a, buf_b = (lb_a, sp_a, rs_a), (lb_b, sp_b, rs_b)
    both, first, second = (0, 2), (0, 1), (1, 2)
    in_a = lax.broadcasted_iota(jnp.int32, (1, 2 * hd), 1) < hd
    row = lax.broadcasted_iota(jnp.int32, (t, t), 0)
    col = lax.broadcasted_iota(jnp.int32, (t, t), 1)
    strict_lower = jnp.concatenate([col < row] * (2 * n_p), axis=0)
    tri = jnp.where(row > col, 1.0, 0.0).astype(BF16)
    sign_bit = jnp.uint32(0x80000000)
    top = 2 * qi + 1

    def mask_diagonal(x, fill):
        head = jnp.where(strict_lower, x[:sub], fill)
        return head if x.shape[0] == sub else jnp.concatenate([head, x[sub:]], axis=0)

    def pair_group(grp, _):
        lanes = [pl.ds(pl.multiple_of((grp * n_p + p) * 2 * hd, 2 * hd), 2 * hd) for p in range(n_p)]
        q_parts = []
        for s in range(2):
            for p in range(n_p):
                q2 = q_ref[0, s * t:(s + 1) * t, lanes[p]]
                q_parts += [jnp.where(in_a, q2, 0), jnp.where(in_a, 0, q2)]
        qs = jnp.concatenate(q_parts, axis=0)

        def scores(j, buf, subs, diag):
            lb_ref, sp_ref, rs_ref = buf
            r0, r1 = subs[0] * sub, subs[1] * sub
            rows_k = pl.ds(pl.multiple_of(j * t, t), t)
            k2 = [k_ref[0, rows_k, lanes[p]] for p in range(n_p)]
            z = jnp.concatenate(
                [_dot_nt(qs[(s * n_p + p) * 2 * t:(s * n_p + p + 1) * 2 * t], k2[p])
                 for s in range(*subs) for p in range(n_p)], axis=0)
            neg_abs = lax.bitcast_convert_type(lax.bitcast_convert_type(z, jnp.uint32) | sign_bit, F32)
            lb = jnp.minimum(z, 0.0) - jnp.log2(1.0 + jnp.exp2(neg_abs))
            sp = z - lb
            if diag:
                sp, lb = mask_diagonal(sp, 0.0), mask_diagonal(lb, -jnp.inf)
            lb_ref[r0:r1, :] = lb
            sp_ref[r0:r1, :] = sp.astype(BF16)
            rs_ref[r0:r1, :] = jnp.broadcast_to(jnp.sum(sp, axis=-1, keepdims=True), (r1 - r0, LANES))

        def values(j, buf, subs):
            lb_ref, sp_ref, rs_ref = buf
            r0, r1 = subs[0] * sub, subs[1] * sub
            rows_k = pl.ds(pl.multiple_of(j * t, t), t)
            c = c_ref[r0:r1, :]
            tail = _dot(sp_ref[r0:r1, :], tri)
            c_wide = jnp.concatenate([c] * (t // LANES), axis=1)
            att = jnp.exp2(lb_ref[r0:r1, :] - tail - c_wide).astype(BF16)
            c_ref[r0:r1, :] = c + rs_ref[r0:r1, :]
            for p in range(n_p):
                blocks = []
                for s in range(subs[1] - subs[0]):
                    base = (s * n_p + p) * 2 * t
                    blocks.append(jnp.concatenate([att[base:base + t], att[base + t:base + 2 * t]], axis=1))
                v2 = v_ref[0, rows_k, lanes[p]]
                vv = jnp.concatenate([jnp.where(in_a, v2, 0), jnp.where(in_a, 0, v2)], axis=0)
                acc_ref[p, subs[0] * t:subs[1] * t, :] += _dot(jnp.concatenate(blocks, axis=0), vv)

        def tile_pair(state):
            i = state[0]
            j = top - 3 - 2 * i
            values(j, buf_b, both)
            scores(j - 1, buf_a, both, False)
            values(j - 1, buf_a, both)
            scores(j - 2, buf_b, both, False)
            return i + 1, jnp.min(c_ref[...])

        def live(state):
            return (state[0] < qi - 1) & (state[1] < ATT_DEAD_CARRY)

        acc_ref[...] = jnp.zeros_like(acc_ref)
        c_ref[...] = jnp.zeros_like(c_ref)

        @pl.when(qi == 0)
        def _():
            scores(top, buf_a, second, True)
            values(top, buf_a, second)
            scores(top - 1, buf_b, both, True)
            values(top - 1, buf_b, both)

        @pl.when(qi > 0)
        def _():
            scores(top, buf_a, second, True)
            scores(top - 1, buf_b, both, True)
            values(top, buf_a, second)
            scores(top - 2, buf_a, first, False)
            values(top - 1, buf_b, both)
            c_min0 = jnp.minimum(jnp.min(c_ref[sub:, :]), jnp.min(c_ref[:sub, :] + buf_a[2][:sub, :]))
            values(top - 2, buf_a, first)

            @pl.when(c_min0 < ATT_DEAD_CARRY)
            def _():
                scores(top - 2, buf_a, second, False)
                values(top - 2, buf_a, second)
                scores(top - 3, buf_b, both, False)
                n_pairs, c_min = lax.while_loop(live, tile_pair, (jnp.int32(0), c_min0))

                @pl.when(c_min < ATT_DEAD_CARRY)
                def _():
                    values(top - 3 - 2 * n_pairs, buf_b, both)

        for p in range(n_p):
            o_ref[0, :, lanes[p]] = acc_ref[p].astype(o_ref.dtype)
        return 0

    lax.fori_loop(0, q_ref.shape[2] // (2 * hd * n_p), pair_group, 0)


def _sb_attention(q, k, v):
    b, l, d = q.shape
    t, n_p = ATT_T, ATT_PAIRS_TOGETHER
    tq = 2 * t
    hp = 2 * SB_HEAD_DIM * ATT_PAIRS_PER_STEP
    assert l % tq == 0 and d % hp == 0 and 2 * SB_HEAD_DIM == LANES and ATT_PAIRS_PER_STEP % n_p == 0
    rows = 4 * n_p * t
    stage = [pltpu.VMEM((rows, t), F32), pltpu.VMEM((rows, t), BF16), pltpu.VMEM((rows, LANES), F32)]
    return pl.pallas_call(
        functools.partial(_sb_attn_kernel, t=t, n_together=n_p),
        grid=(b, d // hp, l // tq),
        in_specs=[
            pl.BlockSpec((1, tq, hp), lambda bi, hi, qi: (bi, qi, hi)),
            pl.BlockSpec((1, l, hp), lambda bi, hi, qi: (bi, 0, hi)),
            pl.BlockSpec((1, l, hp), lambda bi, hi, qi: (bi, 0, hi)),
        ],
        out_specs=pl.BlockSpec((1, tq, hp), lambda bi, hi, qi: (bi, qi, hi)),
        out_shape=jax.ShapeDtypeStruct((b, l, d), BF16),
        scratch_shapes=stage + stage + [pltpu.VMEM((n_p, tq, LANES), F32),
                                        pltpu.VMEM((rows, LANES), F32)],
        compiler_params=_params(3),
        name="sb_attn",
    )(q, k, v)


def _sb_layer(x, b, l, norm_g, w_qkv, w_o, layer):
    t, d = x.shape
    scale = LOG2_E * SB_HEAD_DIM ** -0.5
    q, k, v = _norm_proj(x, norm_g, w_qkv, layer,
                         [(0, d, scale), (d, d, 1.0), (2 * d, d, 1.0)], [BF16] * 3)
    o = _sb_attention(q.reshape(b, l, d), k.reshape(b, l, d), v.reshape(b, l, d))
    return _proj_res(x, o.reshape(t, d), w_o, layer)


def _expand_heads(m, width):
    rows = m.shape[0]
    lane = lax.broadcasted_iota(jnp.int32, (rows, SSD_HPG * width), 1)
    out = jnp.broadcast_to(m[:, SSD_HPG - 1:SSD_HPG], (rows, SSD_HPG * width))
    for r in range(SSD_HPG - 2, -1, -1):
        out = jnp.where(lane < (r + 1) * width, m[:, r:r + 1], out)
    return out


def _to_columns(rows8):
    n = rows8.shape[1]
    padded = jnp.concatenate([rows8, jnp.zeros((n - rows8.shape[0], n), F32)], axis=0)
    return padded.T[:, :rows8.shape[0]]


def _ssd_kernel(xr_ref, br_ref, cr_ref, z_ref, dt_ref, dtn_ref, wx_ref, wb_ref, wc_ref, bx_ref,
                bb_ref, bc_ref, dtb_ref, alog_ref, dsk_ref, ng_ref, o_ref, h_ref, hist_ref,
                steps_row_ref, steps_col_ref, *, n_groups, n_chunks):
    cl, hp, n = SSD_CHUNK, SSD_HPG * SSD_HEAD_DIM, SSD_STATE
    rows = n_groups * SUBLANES
    row = lax.broadcasted_iota(jnp.int32, (cl, cl), 0)
    col = lax.broadcasted_iota(jnp.int32, (cl, cl), 1)
    causal = col <= row
    lane_hp = lax.broadcasted_iota(jnp.int32, (cl, hp), 1)

    def stage_step_sizes(dt_blk):
        dt_in = (dt_blk + dtb_ref[...]).reshape(rows, cl)
        dt_all = jnp.maximum(dt_in, 0.0) + jnp.log(1.0 + jnp.exp(-jnp.abs(dt_in)))
        a_all = dt_all * (-jnp.exp(alog_ref[...].reshape(rows, 1)))
        lane_cl = lax.broadcasted_iota(jnp.int32, (rows, cl), 1)
        sh = 1
        while sh < cl:
            a_all = a_all + jnp.where(lane_cl >= sh, pltpu.roll(a_all, sh, axis=1), 0.0)
            sh *= 2
        te_all = jnp.exp(a_all[:, cl - 1:cl] - a_all) * dt_all
        steps_row_ref[0], steps_row_ref[1] = dt_all, a_all
        steps_col_ref[0], steps_col_ref[1] = _to_columns(a_all), _to_columns(te_all)

    def dt_chunk(ref, ci):
        return ref[0, :, :, pl.ds(pl.multiple_of(ci * cl, cl), cl)]

    def chunk(ci, _):
        tok = pl.ds(pl.multiple_of(ci * cl, cl), cl)

        @pl.when((pl.program_id(2) == 0) & (ci == 0))
        def _():
            h_ref[...] = jnp.zeros_like(h_ref)
            hist_ref[...] = jnp.zeros_like(hist_ref)
            stage_step_sizes(dt_chunk(dt_ref, 0))

        dt_all, a_all = steps_row_ref[0], steps_row_ref[1]
        a_col_all, te_col_all = steps_col_ref[0], steps_col_ref[1]
        dt_next = jnp.where(ci < n_chunks - 1, dt_chunk(dt_ref, jnp.minimum(ci + 1, n_chunks - 1)),
                            dt_chunk(dtn_ref, 0))
        stage_step_sizes(dt_next)

        raw = jnp.concatenate([xr_ref[tok, :], br_ref[tok, :], cr_ref[tok, :]], axis=1)
        ext = jnp.concatenate([hist_ref[...], raw], axis=0)
        hist_ref[...] = raw[cl - HALO:, :]
        cw = jnp.concatenate([wx_ref[...], wb_ref[...], wc_ref[...]], axis=1)
        acc = cw[0:1, :] * ext
        for kk in range(1, SSD_CONV):
            acc = cw[kk:kk + 1, :] * ext + pltpu.roll(acc, 1, axis=0)
        bias = jnp.concatenate([bx_ref[...], bb_ref[...], bc_ref[...]], axis=1)
        act = _silu(acc[HALO:, :] + bias)
        xs_all = act[:, :n_groups * hp]
        bm_all = act[:, n_groups * hp:n_groups * (hp + n)].astype(BF16)
        cm_all = act[:, n_groups * (hp + n):].astype(BF16)

        for g in range(n_groups):
            xs = xs_all[:, g * hp:(g + 1) * hp]
            bm = bm_all[:, g * n:(g + 1) * n]
            cm = cm_all[:, g * n:(g + 1) * n]
            dt = dt_all[g * SUBLANES:(g + 1) * SUBLANES]
            a_cum = a_all[g * SUBLANES:(g + 1) * SUBLANES]
            a_col = a_col_all[:, g * SUBLANES:(g + 1) * SUBLANES]
            te_col = te_col_all[:, g * SUBLANES:(g + 1) * SUBLANES]

            cb = _dot_nt(cm, bm)
            ws, xblocks = [], []
            for r in range(SSD_HPG):
                seg = a_col[:, r:r + 1] - a_cum[r:r + 1, :]
                decay = jnp.exp(jnp.where(causal, seg, -jnp.inf))
                ws.append((cb * decay * dt[r:r + 1, :]).astype(BF16))
                in_head = (lane_hp >= r * SSD_HEAD_DIM) & (lane_hp < (r + 1) * SSD_HEAD_DIM)
                xblocks.append(jnp.where(in_head, xs, 0.0).astype(BF16))
            y = _dot(jnp.concatenate(ws, axis=1), jnp.concatenate(xblocks, axis=0))

            h_prev = h_ref[g]
            y = y + _dot(cm, h_prev.astype(BF16)) * _expand_heads(jnp.exp(a_col), SSD_HEAD_DIM)
            xw = (xs * _expand_heads(te_col, SSD_HEAD_DIM)).astype(BF16)
            h_decay = _expand_heads(jnp.exp(a_col[cl - 1:cl, :]), SSD_HEAD_DIM)
            h_ref[g] = h_prev * h_decay + _dot_tn(bm, xw)

            y = (y + dsk_ref[g] * xs) * _silu(z_ref[tok, g * hp:(g + 1) * hp])
            o_ref[tok, g * hp:(g + 1) * hp] = _rms(y, ng_ref[g]).astype(o_ref.dtype)
        return 0

    lax.fori_loop(0, n_chunks, chunk, 0)


def _ssd_core(xbc, z, dt_t, conv_w, conv_b, dt_bias, a_log, d_skip, norm_g, b, l):
    t = xbc.shape[0]
    g, hpg, cl, n, ng = SSD_GROUPS, SSD_HPG, SSD_CHUNK, SSD_STATE, SSD_GROUPS_PER_STEP
    hp = hpg * SSD_HEAD_DIM
    d_inner = g * hp
    kc = SSD_CHUNKS_PER_STEP
    tl = kc * cl
    nc = l // tl
    assert l % tl == 0 and hp % LANES == 0 and n == LANES and g % ng == 0
    bb, cb = d_inner // (ng * n), (d_inner + g * n) // (ng * n)
    pad = lambda p: jnp.pad(p.reshape(g, hpg, 1), ((0, 0), (0, SUBLANES - hpg), (0, 0)))
    per_lane = lambda p: jnp.repeat(p.reshape(g, hpg), SSD_HEAD_DIM, axis=1).reshape(g, 1, hp)
    conv_b = conv_b.reshape(1, -1)
    tok = lambda bi, gi, ci: bi * nc + ci
    return pl.pallas_call(
        functools.partial(_ssd_kernel, n_groups=ng, n_chunks=kc),
        grid=(b, g // ng, nc),
        in_specs=[
            pl.BlockSpec((tl, ng * hp), lambda bi, gi, ci: (tok(bi, gi, ci), gi)),
            pl.BlockSpec((tl, ng * n), lambda bi, gi, ci: (tok(bi, gi, ci), bb + gi)),
            pl.BlockSpec((tl, ng * n), lambda bi, gi, ci: (tok(bi, gi, ci), cb + gi)),
            pl.BlockSpec((tl, ng * hp), lambda bi, gi, ci: (tok(bi, gi, ci), gi)),
            pl.BlockSpec((1, ng, SUBLANES, tl), lambda bi, gi, ci: (bi, gi, 0, ci)),
            pl.BlockSpec((1, ng, SUBLANES, tl), lambda bi, gi, ci: (bi, gi, 0, jnp.minimum(ci + 1, nc - 1))),
            pl.BlockSpec((SSD_CONV, ng * hp), lambda bi, gi, ci: (0, gi)),
            pl.BlockSpec((SSD_CONV, ng * n), lambda bi, gi, ci: (0, bb + gi)),
            pl.BlockSpec((SSD_CONV, ng * n), lambda bi, gi, ci: (0, cb + gi)),
            pl.BlockSpec((1, ng * hp), lambda bi, gi, ci: (0, gi)),
            pl.BlockSpec((1, ng * n), lambda bi, gi, ci: (0, bb + gi)),
            pl.BlockSpec((1, ng * n), lambda bi, gi, ci: (0, cb + gi)),
            pl.BlockSpec((ng, SUBLANES, 1), lambda bi, gi, ci: (gi, 0, 0)),
            pl.BlockSpec((ng, SUBLANES, 1), lambda bi, gi, ci: (gi, 0, 0)),
            pl.BlockSpec((ng, 1, hp), lambda bi, gi, ci: (gi, 0, 0)),
            pl.BlockSpec((ng, 1, hp), lambda bi, gi, ci: (gi, 0, 0)),
        ],
        out_specs=pl.BlockSpec((tl, ng * hp), lambda bi, gi, ci: (tok(bi, gi, ci), gi)),
        out_shape=jax.ShapeDtypeStruct((t, d_inner), BF16),
        scratch_shapes=[pltpu.VMEM((ng, n, hp), F32), pltpu.VMEM((HALO, ng * (hp + 2 * n)), F32),
                        pltpu.VMEM((2, ng * SUBLANES, cl), F32), pltpu.VMEM((2, cl, ng * SUBLANES), F32)],
        compiler_params=_params(3),
        name="ssd_core",
    )(xbc, xbc, xbc, z, dt_t, dt_t, conv_w, conv_w, conv_w, conv_b, conv_b, conv_b,
      pad(dt_bias), pad(a_log), per_lane(d_skip), norm_g.reshape(g, 1, hp))


def _ssd_layer(x, b, l, norm_g, w_in, conv_w, conv_b, dt_bias, a_log, d_skip, ssd_norm, w_out, layer):
    g, hpg = SSD_GROUPS, SSD_HPG
    d_inner = w_out.shape[1]
    conv_dim = conv_w.shape[1]
    n_heads = g * hpg
    z, xbc, dt = _norm_proj(
        x, norm_g, w_in, layer,
        [(0, d_inner, 1.0), (d_inner, conv_dim, 1.0), (d_inner + conv_dim, n_heads, 1.0)], [F32] * 3)
    dt_t = dt.reshape(b, l, g, hpg).transpose(0, 2, 3, 1)
    dt_t = jnp.pad(dt_t, ((0, 0), (0, 0), (0, SUBLANES - hpg), (0, 0)))
    y = _ssd_core(xbc, z, dt_t, conv_w, conv_b, dt_bias, a_log, d_skip, ssd_norm, b, l)
    return _proj_res(x, y, w_out, layer)


def _short_conv_kernel(x_ref, g_ref, wi_ref, cw_ref, wo_ref, o_ref, ext_ref):
    tm, d = x_ref.shape

    @pl.when(pl.program_id(1) == 0)
    def _():
        ext_ref[0:HALO, :] = jnp.zeros((HALO, ext_ref.shape[1]), F32)

    x = x_ref[...]
    xn = _rms(x, g_ref[...]).astype(BF16)
    w_part = lambda i: wi_ref[:, i * d:(i + 1) * d].astype(BF16)
    ext_ref[HALO:HALO + tm, :] = _dot(xn, w_part(1)) * _dot(xn, w_part(2))
    cw = cw_ref[...]
    u = cw[0:1, :] * ext_ref[pl.ds(HALO - (SC_WIDTH - 1), tm), :]
    for kk in range(1, SC_WIDTH):
        u = u + cw[kk:kk + 1, :] * ext_ref[pl.ds(HALO - (SC_WIDTH - 1) + kk, tm), :]
    ext_ref[0:HALO, :] = ext_ref[tm:tm + HALO, :]
    gated = (_dot(xn, w_part(0)) * u).astype(BF16)
    o_ref[...] = x + _dot(gated, wo_ref[...].astype(BF16))


def _short_conv_layer(x, b, l, norm_g, w_in, conv_w, w_out, layer):
    t, d = x.shape
    tm = SC_TM
    nl = l // tm
    assert l % tm == 0
    return pl.pallas_call(
        _short_conv_kernel,
        grid=(b, nl),
        in_specs=[
            pl.BlockSpec((tm, d), lambda bi, li: (bi * nl + li, 0)),
            _resident((1, d)),
            _resident((None, d, 3 * d), (layer, 0, 0)),
            _resident((None, SC_WIDTH, d), (layer, 0, 0)),
            _resident((None, d, d), (layer, 0, 0)),
        ],
        out_specs=pl.BlockSpec((tm, d), lambda bi, li: (bi * nl + li, 0)),
        out_shape=jax.ShapeDtypeStruct((t, d), F32),
        scratch_shapes=[pltpu.VMEM((tm + HALO, d), F32)],
        compiler_params=_params(2),
        name="short_conv",
    )(x, norm_g.reshape(1, d), w_in, conv_w, w_out)


def kernel(x, ffn1_norm, ffn1_w_gu, ffn1_w_down, mix_norm, ffn2_norm, ffn2_w_gu, ffn2_w_down,
           sb_w_qkv, sb_w_o, ssd_w_in, ssd_conv_w, ssd_conv_b, ssd_dt_bias, ssd_a_log, ssd_d,
           ssd_norm, ssd_w_out, sc_w_in, sc_conv_w, sc_w_out, final_norm):
    b, l, d = x.shape
    depth = ffn1_norm.shape[0]
    h = x.reshape(b * l, d)
    for i in range(depth):
        h = _ffn(h, ffn1_norm[i], ffn1_w_gu, ffn1_w_down, i)
        kind, j = i % N_MIXERS, i // N_MIXERS
        if kind == 0:
            h = _sb_layer(h, b, l, mix_norm[i], sb_w_qkv, sb_w_o, j)
        elif kind == 1:
            h = _ssd_layer(h, b, l, mix_norm[i], ssd_w_in, ssd_conv_w[j], ssd_conv_b[j],
                           ssd_dt_bias[j], ssd_a_log[j], ssd_d[j], ssd_norm[j], ssd_w_out, j)
        else:
            h = _short_conv_layer(h, b, l, mix_norm[i], sc_w_in, sc_conv_w, sc_w_out, j)
        h = _ffn(h, ffn2_norm[i], ffn2_w_gu, ffn2_w_down, i,
                 final_g=final_norm if i == depth - 1 else None)
    return h.reshape(b, l, d)
```

```python
import functools

import jax
import jax.numpy as jnp
from jax import lax
from jax.experimental import pallas as pl
from jax.experimental.pallas import tpu as pltpu

F32 = jnp.float32
BF16 = jnp.bfloat16

RMS_EPS = 1e-6
LOG2_E = 1.4426950408889634
N_MIXERS = 3
SB_HEAD_DIM = 64
SSD_HEAD_DIM = 64
SSD_GROUPS = 8
SSD_HPG = 4
SSD_STATE = 128
SSD_CONV = 4
SSD_CHUNK = 128
SC_WIDTH = 3

LANES = 128
SUBLANES = 8
VMEM_LIMIT_BYTES = 60 * 1024 * 1024

FFN_TM = 1024
FFN_TF = 256
PROJ_TM = 512
PROJ_RES_TM = 1024
PROJ_TN = 512
ATT_T = 256
ATT_PAIRS_PER_STEP = 4
ATT_PAIRS_TOGETHER = 4
ATT_DEAD_CARRY = 160.0
SSD_GROUPS_PER_STEP = 4
SSD_CHUNKS_PER_STEP = 4
SC_TM = 512
HALO = SUBLANES


def _params(n_axes):
    return pltpu.CompilerParams(
        dimension_semantics=("arbitrary",) * n_axes,
        vmem_limit_bytes=VMEM_LIMIT_BYTES)


def _resident(shape, index=None):
    index = (0,) * len(shape) if index is None else index
    return pl.BlockSpec(shape, lambda *_: index, pipeline_mode=pl.Buffered(1))


def _rms(x, g):
    ms = jnp.mean(x * x, axis=-1, keepdims=True)
    return x * lax.rsqrt(ms + RMS_EPS) * g


def _silu(x):
    h = 0.5 * x
    return h + h * jnp.tanh(h)


def _dot(a, b):
    return jnp.dot(a, b, preferred_element_type=F32)


def _dot_nt(a, b):
    return lax.dot_general(a, b, (((1,), (1,)), ((), ())), preferred_element_type=F32)


def _dot_tn(a, b):
    return lax.dot_general(a, b, (((0,), (0,)), ((), ())), preferred_element_type=F32)


def _ffn_kernel(x_ref, g_ref, wgu_hbm, wd_hbm, *refs, layer, n_chunks, tf, final_norm):
    fg_ref = refs[0] if final_norm else None
    o_ref, xn_ref, acc_ref, wgu_ref, wd_ref, sem = refs[-6:]
    d_ff = n_chunks * tf

    def chunk_copies(c):
        cols, rows = pl.ds(c * tf, tf), pl.ds(c * tf, tf)
        up_cols = pl.ds(d_ff + c * tf, tf)
        return (pltpu.make_async_copy(wgu_hbm.at[layer, :, cols], wgu_ref.at[:, cols], sem.at[0, c]),
                pltpu.make_async_copy(wgu_hbm.at[layer, :, up_cols], wgu_ref.at[:, up_cols], sem.at[1, c]),
                pltpu.make_async_copy(wd_hbm.at[layer, rows, :], wd_ref.at[rows, :], sem.at[2, c]))

    def body(weights_in_flight):
        x = x_ref[...]
        xn_ref[...] = _rms(x, g_ref[...]).astype(BF16)
        for c in range(n_chunks):
            if weights_in_flight:
                for copy in chunk_copies(c):
                    copy.wait()
            gate = _dot(xn_ref[...], wgu_ref[:, c * tf:(c + 1) * tf].astype(BF16))
            up = _dot(xn_ref[...], wgu_ref[:, d_ff + c * tf:d_ff + (c + 1) * tf].astype(BF16))
            h = (_silu(gate) * up).astype(BF16)
            part = _dot(h, wd_ref[c * tf:(c + 1) * tf, :].astype(BF16))
            if c == 0:
                acc_ref[...] = part
            else:
                acc_ref[...] += part
        y = x + 0.5 * acc_ref[...]
        if final_norm:
            y = _rms(y, fg_ref[...])
        o_ref[...] = y

    @pl.when(pl.program_id(0) == 0)
    def _():
        for c in range(n_chunks):
            for copy in chunk_copies(c):
                copy.start()
        body(True)

    @pl.when(pl.program_id(0) > 0)
    def _():
        body(False)


def _ffn(x, norm_g, w_gu, w_down, layer, final_g=None):
    t, d = x.shape
    d_ff = w_down.shape[1]
    tm, tf = FFN_TM, FFN_TF
    n_chunks = d_ff // tf
    assert t % tm == 0 and d_ff % tf == 0
    final_norm = final_g is not None
    extra = [final_g.reshape(1, d)] if final_norm else []
    return pl.pallas_call(
        functools.partial(_ffn_kernel, layer=layer, n_chunks=n_chunks, tf=tf, final_norm=final_norm),
        grid=(t // tm,),
        in_specs=[
            pl.BlockSpec((tm, d), lambda i: (i, 0)),
            _resident((1, d)),
            pl.BlockSpec(memory_space=pl.ANY),
            pl.BlockSpec(memory_space=pl.ANY),
        ] + [_resident((1, d))] * len(extra),
        out_specs=pl.BlockSpec((tm, d), lambda i: (i, 0)),
        out_shape=jax.ShapeDtypeStruct((t, d), F32),
        scratch_shapes=[pltpu.VMEM((tm, d), BF16), pltpu.VMEM((tm, d), F32),
                        pltpu.VMEM((d, 2 * d_ff), F32), pltpu.VMEM((d_ff, d), F32),
                        pltpu.SemaphoreType.DMA((3, n_chunks))],
        compiler_params=_params(1),
        name="ffn",
    )(x, norm_g.reshape(1, d), w_gu, w_down, *extra)


def _norm_proj_kernel(x_ref, g_ref, w_ref, *o_refs, segments, tn):
    xn = _rms(x_ref[...], g_ref[...]).astype(BF16)
    for o_ref, (start, width, scale) in zip(o_refs, segments):
        for n0 in range(0, width, tn):
            n1 = min(n0 + tn, width)
            w = w_ref[:, start + n0:start + n1]
            if scale != 1.0:
                w = w * scale
            o_ref[:, n0:n1] = _dot(xn, w.astype(BF16)).astype(o_ref.dtype)


def _norm_proj(x, norm_g, w, layer, segments, out_dtypes):
    t, d = x.shape
    tm = PROJ_TM
    assert t % tm == 0 and all(s[0] % LANES == 0 for s in segments)
    return pl.pallas_call(
        functools.partial(_norm_proj_kernel, segments=tuple(segments), tn=PROJ_TN),
        grid=(t // tm,),
        in_specs=[pl.BlockSpec((tm, d), lambda i: (i, 0)), _resident((1, d)),
                  _resident((None,) + w.shape[1:], (layer, 0, 0))],
        out_specs=[pl.BlockSpec((tm, s[1]), lambda i: (i, 0)) for s in segments],
        out_shape=[jax.ShapeDtypeStruct((t, s[1]), dt) for s, dt in zip(segments, out_dtypes)],
        compiler_params=_params(1),
        name="norm_proj",
    )(x, norm_g.reshape(1, d), w)


def _proj_res_kernel(x_ref, y_ref, w_ref, o_ref):
    o_ref[...] = x_ref[...] + _dot(y_ref[...], w_ref[...].astype(BF16))


def _proj_res(x, y, w, layer):
    t, d = x.shape
    k = y.shape[1]
    tm = PROJ_RES_TM
    assert t % tm == 0
    return pl.pallas_call(
        _proj_res_kernel,
        grid=(t // tm,),
        in_specs=[
            pl.BlockSpec((tm, d), lambda i: (i, 0)),
            pl.BlockSpec((tm, k), lambda i: (i, 0)),
            _resident((None, k, d), (layer, 0, 0)),
        ],
        out_specs=pl.BlockSpec((tm, d), lambda i: (i, 0)),
        out_shape=jax.ShapeDtypeStruct((t, d), F32),
        compiler_params=_params(1),
        name="proj_res",
    )(x, y, w)


def _sb_attn_kernel(q_ref, k_ref, v_ref, o_ref, lb_a, sp_a, rs_a, lb_b, sp_b, rs_b, acc_ref, c_ref,
                    *, t, n_together):
    qi = pl.program_id(2)
    hd = SB_HEAD_DIM
    n_p = n_together
    sub = 2 * n_p * t
    buf_a, buf_b = (lb_a, sp_a, rs_a), (lb_b, sp_b, rs_b)
    both, first, second = (0, 2), (0, 1), (1, 2)
    in_a = lax.broadcasted_iota(jnp.int32, (1, 2 * hd), 1) < hd
    row = lax.broadcasted_iota(jnp.int32, (t, t), 0)
    col = lax.broadcasted_iota(jnp.int32, (t, t), 1)
    strict_lower = jnp.concatenate([col < row] * (2 * n_p), axis=0)
    tri = jnp.where(row > col, 1.0, 0.0).astype(BF16)
    sign_bit = jnp.uint32(0x80000000)
    top = 2 * qi + 1

    def mask_diagonal(x, fill):
        head = jnp.where(strict_lower, x[:sub], fill)
        return head if x.shape[0] == sub else jnp.concatenate([head, x[sub:]], axis=0)

    def pair_group(grp, _):
        lanes = [pl.ds(pl.multiple_of((grp * n_p + p) * 2 * hd, 2 * hd), 2 * hd) for p in range(n_p)]
        q_parts = []
        for s in range(2):
            for p in range(n_p):
                q2 = q_ref[0, s * t:(s + 1) * t, lanes[p]]
                q_parts += [jnp.where(in_a, q2, 0), jnp.where(in_a, 0, q2)]
        qs = jnp.concatenate(q_parts, axis=0)

        def scores(j, buf, subs, diag):
            lb_ref, sp_ref, rs_ref = buf
            r0, r1 = subs[0] * sub, subs[1] * sub
            rows_k = pl.ds(pl.multiple_of(j * t, t), t)
            k2 = [k_ref[0, rows_k, lanes[p]] for p in range(n_p)]
            z = jnp.concatenate(
                [_dot_nt(qs[(s * n_p + p) * 2 * t:(s * n_p + p + 1) * 2 * t], k2[p])
                 for s in range(*subs) for p in range(n_p)], axis=0)
            neg_abs = lax.bitcast_convert_type(lax.bitcast_convert_type(z, jnp.uint32) | sign_bit, F32)
            lb = jnp.minimum(z, 0.0) - jnp.log2(1.0 + jnp.exp2(neg_abs))
            sp = z - lb
            if diag:
                sp, lb = mask_diagonal(sp, 0.0), mask_diagonal(lb, -jnp.inf)
            lb_ref[r0:r1, :] = lb
            sp_ref[r0:r1, :] = sp.astype(BF16)
            rs_ref[r0:r1, :] = jnp.broadcast_to(jnp.sum(sp, axis=-1, keepdims=True), (r1 - r0, LANES))

        def values(j, buf, subs):
            lb_ref, sp_ref, rs_ref = buf
            r0, r1 = subs[0] * sub, subs[1] * sub
            rows_k = pl.ds(pl.multiple_of(j * t, t), t)
            c = c_ref[r0:r1, :]
            tail = _dot(sp_ref[r0:r1, :], tri)
            c_wide = jnp.concatenate([c] * (t // LANES), axis=1)
            att = jnp.exp2(lb_ref[r0:r1, :] - tail - c_wide).astype(BF16)
            c_ref[r0:r1, :] = c + rs_ref[r0:r1, :]
            for p in range(n_p):
                blocks = []
                for s in range(subs[1] - subs[0]):
                    base = (s * n_p + p) * 2 * t
                    blocks.append(jnp.concatenate([att[base:base + t], att[base + t:base + 2 * t]], axis=1))
                v2 = v_ref[0, rows_k, lanes[p]]
                vv = jnp.concatenate([jnp.where(in_a, v2, 0), jnp.where(in_a, 0, v2)], axis=0)
                acc_ref[p, subs[0] * t:subs[1] * t, :] += _dot(jnp.concatenate(blocks, axis=0), vv)

        def tile_pair(state):
            i = state[0]
            j = top - 3 - 2 * i
            values(j, buf_b, both)
            scores(j - 1, buf_a, both, False)
            values(j - 1, buf_a, both)
            scores(j - 2, buf_b, both, False)
            return i + 1, jnp.min(c_ref[...])

        def live(state):
            return (state[0] < qi - 1) & (state[1] < ATT_DEAD_CARRY)

        acc_ref[...] = jnp.zeros_like(acc_ref)
        c_ref[...] = jnp.zeros_like(c_ref)

        @pl.when(qi == 0)
        def _():
            scores(top, buf_a, second, True)
            values(top, buf_a, second)
            scores(top - 1, buf_b, both, True)
            values(top - 1, buf_b, both)

        @pl.when(qi > 0)
        def _():
            scores(top, buf_a, second, True)
            scores(top - 1, buf_b, both, True)
            values(top, buf_a, second)
            scores(top - 2, buf_a, first, False)
            values(top - 1, buf_b, both)
            c_min0 = jnp.minimum(jnp.min(c_ref[sub:, :]), jnp.min(c_ref[:sub, :] + buf_a[2][:sub, :]))
            values(top - 2, buf_a, first)

            @pl.when(c_min0 < ATT_DEAD_CARRY)
            def _():
                scores(top - 2, buf_a, second, False)
                values(top - 2, buf_a, second)
                scores(top - 3, buf_b, both, False)
                n_pairs, c_min = lax.while_loop(live, tile_pair, (jnp.int32(0), c_min0))

                @pl.when(c_min < ATT_DEAD_CARRY)
                def _():
                    values(top - 3 - 2 * n_pairs, buf_b, both)

        for p in range(n_p):
            o_ref[0, :, lanes[p]] = acc_ref[p].astype(o_ref.dtype)
        return 0

    lax.fori_loop(0, q_ref.shape[2] // (2 * hd * n_p), pair_group, 0)


def _sb_attention(q, k, v):
    b, l, d = q.shape
    t, n_p = ATT_T, ATT_PAIRS_TOGETHER
    tq = 2 * t
    hp = 2 * SB_HEAD_DIM * ATT_PAIRS_PER_STEP
    assert l % tq == 0 and d % hp == 0 and 2 * SB_HEAD_DIM == LANES and ATT_PAIRS_PER_STEP % n_p == 0
    rows = 4 * n_p * t
    stage = [pltpu.VMEM((rows, t), F32), pltpu.VMEM((rows, t), BF16), pltpu.VMEM((rows, LANES), F32)]
    return pl.pallas_call(
        functools.partial(_sb_attn_kernel, t=t, n_together=n_p),
        grid=(b, d // hp, l // tq),
        in_specs=[
            pl.BlockSpec((1, tq, hp), lambda bi, hi, qi: (bi, qi, hi)),
            pl.BlockSpec((1, l, hp), lambda bi, hi, qi: (bi, 0, hi)),
            pl.BlockSpec((1, l, hp), lambda bi, hi, qi: (bi, 0, hi)),
        ],
        out_specs=pl.BlockSpec((1, tq, hp), lambda bi, hi, qi: (bi, qi, hi)),
        out_shape=jax.ShapeDtypeStruct((b, l, d), BF16),
        scratch_shapes=stage + stage + [pltpu.VMEM((n_p, tq, LANES), F32),
                                        pltpu.VMEM((rows, LANES), F32)],
        compiler_params=_params(3),
        name="sb_attn",
    )(q, k, v)


def _sb_layer(x, b, l, norm_g, w_qkv, w_o, layer):
    t, d = x.shape
    scale = LOG2_E * SB_HEAD_DIM ** -0.5
    q, k, v = _norm_proj(x, norm_g, w_qkv, layer,
                         [(0, d, scale), (d, d, 1.0), (2 * d, d, 1.0)], [BF16] * 3)
    o = _sb_attention(q.reshape(b, l, d), k.reshape(b, l, d), v.reshape(b, l, d))
    return _proj_res(x, o.reshape(t, d), w_o, layer)


def _expand_heads(m, width):
    rows = m.shape[0]
    lane = lax.broadcasted_iota(jnp.int32, (rows, SSD_HPG * width), 1)
    out = jnp.broadcast_to(m[:, SSD_HPG - 1:SSD_HPG], (rows, SSD_HPG * width))
    for r in range(SSD_HPG - 2, -1, -1):
        out = jnp.where(lane < (r + 1) * width, m[:, r:r + 1], out)
    return out


def _to_columns(rows8):
    n = rows8.shape[1]
    padded = jnp.concatenate([rows8, jnp.zeros((n - rows8.shape[0], n), F32)], axis=0)
    return padded.T[:, :rows8.shape[0]]


def _ssd_kernel(xr_ref, br_ref, cr_ref, z_ref, dt_ref, dtn_ref, wx_ref, wb_ref, wc_ref, bx_ref,
                bb_ref, bc_ref, dtb_ref, alog_ref, dsk_ref, ng_ref, o_ref, h_ref, hist_ref,
                steps_row_ref, steps_col_ref, *, n_groups, n_chunks):
    cl, hp, n = SSD_CHUNK, SSD_HPG * SSD_HEAD_DIM, SSD_STATE
    rows = n_groups * SUBLANES
    row = lax.broadcasted_iota(jnp.int32, (cl, cl), 0)
    col = lax.broadcasted_iota(jnp.int32, (cl, cl), 1)
    causal = col <= row
    lane_hp = lax.broadcasted_iota(jnp.int32, (cl, hp), 1)

    def stage_step_sizes(dt_blk):
        dt_in = (dt_blk + dtb_ref[...]).reshape(rows, cl)
        dt_all = jnp.maximum(dt_in, 0.0) + jnp.log(1.0 + jnp.exp(-jnp.abs(dt_in)))
        a_all = dt_all * (-jnp.exp(alog_ref[...].reshape(rows, 1)))
        lane_cl = lax.broadcasted_iota(jnp.int32, (rows, cl), 1)
        sh = 1
        while sh < cl:
            a_all = a_all + jnp.where(lane_cl >= sh, pltpu.roll(a_all, sh, axis=1), 0.0)
            sh *= 2
        te_all = jnp.exp(a_all[:, cl - 1:cl] - a_all) * dt_all
        steps_row_ref[0], steps_row_ref[1] = dt_all, a_all
        steps_col_ref[0], steps_col_ref[1] = _to_columns(a_all), _to_columns(te_all)

    def dt_chunk(ref, ci):
        return ref[0, :, :, pl.ds(pl.multiple_of(ci * cl, cl), cl)]

    def chunk(ci, _):
        tok = pl.ds(pl.multiple_of(ci * cl, cl), cl)

        @pl.when((pl.program_id(2) == 0) & (ci == 0))
        def _():
            h_ref[...] = jnp.zeros_like(h_ref)
            hist_ref[...] = jnp.zeros_like(hist_ref)
            stage_step_sizes(dt_chunk(dt_ref, 0))

        dt_all, a_all = steps_row_ref[0], steps_row_ref[1]
        a_col_all, te_col_all = steps_col_ref[0], steps_col_ref[1]
        dt_next = jnp.where(ci < n_chunks - 1, dt_chunk(dt_ref, jnp.minimum(ci + 1, n_chunks - 1)),
                            dt_chunk(dtn_ref, 0))
        stage_step_sizes(dt_next)

        raw = jnp.concatenate([xr_ref[tok, :], br_ref[tok, :], cr_ref[tok, :]], axis=1)
        ext = jnp.concatenate([hist_ref[...], raw], axis=0)
        hist_ref[...] = raw[cl - HALO:, :]
        cw = jnp.concatenate([wx_ref[...], wb_ref[...], wc_ref[...]], axis=1)
        acc = cw[0:1, :] * ext
        for kk in range(1, SSD_CONV):
            acc = cw[kk:kk + 1, :] * ext + pltpu.roll(acc, 1, axis=0)
        bias = jnp.concatenate([bx_ref[...], bb_ref[...], bc_ref[...]], axis=1)
        act = _silu(acc[HALO:, :] + bias)
        xs_all = act[:, :n_groups * hp]
        bm_all = act[:, n_groups * hp:n_groups * (hp + n)].astype(BF16)
        cm_all = act[:, n_groups * (hp + n):].astype(BF16)

        for g in range(n_groups):
            xs = xs_all[:, g * hp:(g + 1) * hp]
            bm = bm_all[:, g * n:(g + 1) * n]
            cm = cm_all[:, g * n:(g + 1) * n]
            dt = dt_all[g * SUBLANES:(g + 1) * SUBLANES]
            a_cum = a_all[g * SUBLANES:(g + 1) * SUBLANES]
            a_col = a_col_all[:, g * SUBLANES:(g + 1) * SUBLANES]
            te_col = te_col_all[:, g * SUBLANES:(g + 1) * SUBLANES]

            cb = _dot_nt(cm, bm)
            ws, xblocks = [], []
            for r in range(SSD_HPG):
                seg = a_col[:, r:r + 1] - a_cum[r:r + 1, :]
                decay = jnp.exp(jnp.where(causal, seg, -jnp.inf))
                ws.append((cb * decay * dt[r:r + 1, :]).astype(BF16))
                in_head = (lane_hp >= r * SSD_HEAD_DIM) & (lane_hp < (r + 1) * SSD_HEAD_DIM)
                xblocks.append(jnp.where(in_head, xs, 0.0).astype(BF16))
            y = _dot(jnp.concatenate(ws, axis=1), jnp.concatenate(xblocks, axis=0))

            h_prev = h_ref[g]
            y = y + _dot(cm, h_prev.astype(BF16)) * _expand_heads(jnp.exp(a_col), SSD_HEAD_DIM)
            xw = (xs * _expand_heads(te_col, SSD_HEAD_DIM)).astype(BF16)
            h_decay = _expand_heads(jnp.exp(a_col[cl - 1:cl, :]), SSD_HEAD_DIM)
            h_ref[g] = h_prev * h_decay + _dot_tn(bm, xw)

            y = (y + dsk_ref[g] * xs) * _silu(z_ref[tok, g * hp:(g + 1) * hp])
            o_ref[tok, g * hp:(g + 1) * hp] = _rms(y, ng_ref[g]).astype(o_ref.dtype)
        return 0

    lax.fori_loop(0, n_chunks, chunk, 0)


def _ssd_core(xbc, z, dt_t, conv_w, conv_b, dt_bias, a_log, d_skip, norm_g, b, l):
    t = xbc.shape[0]
    g, hpg, cl, n, ng = SSD_GROUPS, SSD_HPG, SSD_CHUNK, SSD_STATE, SSD_GROUPS_PER_STEP
    hp = hpg * SSD_HEAD_DIM
    d_inner = g * hp
    kc = SSD_CHUNKS_PER_STEP
    tl = kc * cl
    nc = l // tl
    assert l % tl == 0 and hp % LANES == 0 and n == LANES and g % ng == 0
    bb, cb = d_inner // (ng * n), (d_inner + g * n) // (ng * n)
    pad = lambda p: jnp.pad(p.reshape(g, hpg, 1), ((0, 0), (0, SUBLANES - hpg), (0, 0)))
    per_lane = lambda p: jnp.repeat(p.reshape(g, hpg), SSD_HEAD_DIM, axis=1).reshape(g, 1, hp)
    conv_b = conv_b.reshape(1, -1)
    tok = lambda bi, gi, ci: bi * nc + ci
    return pl.pallas_call(
        functools.partial(_ssd_kernel, n_groups=ng, n_chunks=kc),
        grid=(b, g // ng, nc),
        in_specs=[
            pl.BlockSpec((tl, ng * hp), lambda bi, gi, ci: (tok(bi, gi, ci), gi)),
            pl.BlockSpec((tl, ng * n), lambda bi, gi, ci: (tok(bi, gi, ci), bb + gi)),
            pl.BlockSpec((tl, ng * n), lambda bi, gi, ci: (tok(bi, gi, ci), cb + gi)),
            pl.BlockSpec((tl, ng * hp), lambda bi, gi, ci: (tok(bi, gi, ci), gi)),
            pl.BlockSpec((1, ng, SUBLANES, tl), lambda bi, gi, ci: (bi, gi, 0, ci)),
            pl.BlockSpec((1, ng, SUBLANES, tl), lambda bi, gi, ci: (bi, gi, 0, jnp.minimum(ci + 1, nc - 1))),
            pl.BlockSpec((SSD_CONV, ng * hp), lambda bi, gi, ci: (0, gi)),
            pl.BlockSpec((SSD_CONV, ng * n), lambda bi, gi, ci: (0, bb + gi)),
            pl.BlockSpec((SSD_CONV, ng * n), lambda bi, gi, ci: (0, cb + gi)),
            pl.BlockSpec((1, ng * hp), lambda bi, gi, ci: (0, gi)),
            pl.BlockSpec((1, ng * n), lambda bi, gi, ci: (0, bb + gi)),
            pl.BlockSpec((1, ng * n), lambda bi, gi, ci: (0, cb + gi)),
            pl.BlockSpec((ng, SUBLANES, 1), lambda bi, gi, ci: (gi, 0, 0)),
            pl.BlockSpec((ng, SUBLANES, 1), lambda bi, gi, ci: (gi, 0, 0)),
            pl.BlockSpec((ng, 1, hp), lambda bi, gi, ci: (gi, 0, 0)),
            pl.BlockSpec((ng, 1, hp), lambda bi, gi, ci: (gi, 0, 0)),
        ],
        out_specs=pl.BlockSpec((tl, ng * hp), lambda bi, gi, ci: (tok(bi, gi, ci), gi)),
        out_shape=jax.ShapeDtypeStruct((t, d_inner), BF16),
        scratch_shapes=[pltpu.VMEM((ng, n, hp), F32), pltpu.VMEM((HALO, ng * (hp + 2 * n)), F32),
                        pltpu.VMEM((2, ng * SUBLANES, cl), F32), pltpu.VMEM((2, cl, ng * SUBLANES), F32)],
        compiler_params=_params(3),
        name="ssd_core",
    )(xbc, xbc, xbc, z, dt_t, dt_t, conv_w, conv_w, conv_w, conv_b, conv_b, conv_b,
      pad(dt_bias), pad(a_log), per_lane(d_skip), norm_g.reshape(g, 1, hp))


def _ssd_layer(x, b, l, norm_g, w_in, conv_w, conv_b, dt_bias, a_log, d_skip, ssd_norm, w_out, layer):
    g, hpg = SSD_GROUPS, SSD_HPG
    d_inner = w_out.shape[1]
    conv_dim = conv_w.shape[1]
    n_heads = g * hpg
    z, xbc, dt = _norm_proj(
        x, norm_g, w_in, layer,
        [(0, d_inner, 1.0), (d_inner, conv_dim, 1.0), (d_inner + conv_dim, n_heads, 1.0)], [F32] * 3)
    dt_t = dt.reshape(b, l, g, hpg).transpose(0, 2, 3, 1)
    dt_t = jnp.pad(dt_t, ((0, 0), (0, 0), (0, SUBLANES - hpg), (0, 0)))
    y = _ssd_core(xbc, z, dt_t, conv_w, conv_b, dt_bias, a_log, d_skip, ssd_norm, b, l)
    return _proj_res(x, y, w_out, layer)


def _short_conv_kernel(x_ref, g_ref, wi_ref, cw_ref, wo_ref, o_ref, ext_ref):
    tm, d = x_ref.shape

    @pl.when(pl.program_id(1) == 0)
    def _():
        ext_ref[0:HALO, :] = jnp.zeros((HALO, ext_ref.shape[1]), F32)

    x = x_ref[...]
    xn = _rms(x, g_ref[...]).astype(BF16)
    w_part = lambda i: wi_ref[:, i * d:(i + 1) * d].astype(BF16)
    ext_ref[HALO:HALO + tm, :] = _dot(xn, w_part(1)) * _dot(xn, w_part(2))
    cw = cw_ref[...]
    u = cw[0:1, :] * ext_ref[pl.ds(HALO - (SC_WIDTH - 1), tm), :]
    for kk in range(1, SC_WIDTH):
        u = u + cw[kk:kk + 1, :] * ext_ref[pl.ds(HALO - (SC_WIDTH - 1) + kk, tm), :]
    ext_ref[0:HALO, :] = ext_ref[tm:tm + HALO, :]
    gated = (_dot(xn, w_part(0)) * u).astype(BF16)
    o_ref[...] = x + _dot(gated, wo_ref[...].astype(BF16))


def _short_conv_layer(x, b, l, norm_g, w_in, conv_w, w_out, layer):
    t, d = x.shape
    tm = SC_TM
    nl = l // tm
    assert l % tm == 0
    return pl.pallas_call(
        _short_conv_kernel,
        grid=(b, nl),
        in_specs=[
            pl.BlockSpec((tm, d), lambda bi, li: (bi * nl + li, 0)),
            _resident((1, d)),
            _resident((None, d, 3 * d), (layer, 0, 0)),
            _resident((None, SC_WIDTH, d), (layer, 0, 0)),
            _resident((None, d, d), (layer, 0, 0)),
        ],
        out_specs=pl.BlockSpec((tm, d), lambda bi, li: (bi * nl + li, 0)),
        out_shape=jax.ShapeDtypeStruct((t, d), F32),
        scratch_shapes=[pltpu.VMEM((tm + HALO, d), F32)],
        compiler_params=_params(2),
        name="short_conv",
    )(x, norm_g.reshape(1, d), w_in, conv_w, w_out)


def kernel(x, ffn1_norm, ffn1_w_gu, ffn1_w_down, mix_norm, ffn2_norm, ffn2_w_gu, ffn2_w_down,
           sb_w_qkv, sb_w_o, ssd_w_in, ssd_conv_w, ssd_conv_b, ssd_dt_bias, ssd_a_log, ssd_d,
           ssd_norm, ssd_w_out, sc_w_in, sc_conv_w, sc_w_out, final_norm):
    b, l, d = x.shape
    depth = ffn1_norm.shape[0]
    h = x.reshape(b * l, d)
    for i in range(depth):
        h = _ffn(h, ffn1_norm[i], ffn1_w_gu, ffn1_w_down, i)
        kind, j = i % N_MIXERS, i // N_MIXERS
        if kind == 0:
            h = _sb_layer(h, b, l, mix_norm[i], sb_w_qkv, sb_w_o, j)
        elif kind == 1:
            h = _ssd_layer(h, b, l, mix_norm[i], ssd_w_in, ssd_conv_w[j], ssd_conv_b[j],
                           ssd_dt_bias[j], ssd_a_log[j], ssd_d[j], ssd_norm[j], ssd_w_out, j)
        else:
            h = _short_conv_layer(h, b, l, mix_norm[i], sc_w_in, sc_conv_w, sc_w_out, j)
        h = _ffn(h, ffn2_norm[i], ffn2_w_gu, ffn2_w_down, i,
                 final_g=final_norm if i == depth - 1 else None)
    return h.reshape(b, l, d)
```

```python
import functools

import jax
import jax.numpy as jnp
from jax import lax
from jax.experimental import pallas as pl
from jax.experimental.pallas import tpu as pltpu

F32 = jnp.float32
BF16 = jnp.bfloat16

RMS_EPS = 1e-6
LOG2_E = 1.4426950408889634
N_MIXERS = 3
SB_HEAD_DIM = 64
SSD_HEAD_DIM = 64
SSD_GROUPS = 8
SSD_HPG = 4
SSD_STATE = 128
SSD_CONV = 4
SSD_CHUNK = 128
SC_WIDTH = 3

LANES = 128
SUBLANES = 8
VMEM_LIMIT_BYTES = 60 * 1024 * 1024

FFN_TM = 1024
FFN_TF = 256
PROJ_TM = 512
PROJ_RES_TM = 1024
PROJ_TN = 512
ATT_T = 256
ATT_PAIRS_PER_STEP = 4
ATT_PAIRS_TOGETHER = 4
ATT_DEAD_CARRY = 160.0
SSD_GROUPS_PER_STEP = 4
SSD_CHUNKS_PER_STEP = 4
SC_TM = 512
HALO = SUBLANES


def _params(n_axes):
    return pltpu.CompilerParams(
        dimension_semantics=("arbitrary",) * n_axes,
        vmem_limit_bytes=VMEM_LIMIT_BYTES)


def _resident(shape, index=None):
    index = (0,) * len(shape) if index is None else index
    return pl.BlockSpec(shape, lambda *_: index, pipeline_mode=pl.Buffered(1))


def _rms(x, g):
    ms = jnp.mean(x * x, axis=-1, keepdims=True)
    return x * lax.rsqrt(ms + RMS_EPS) * g


def _silu(x):
    h = 0.5 * x
    return h + h * jnp.tanh(h)


def _dot(a, b):
    return jnp.dot(a, b, preferred_element_type=F32)


def _dot_nt(a, b):
    return lax.dot_general(a, b, (((1,), (1,)), ((), ())), preferred_element_type=F32)


def _dot_tn(a, b):
    return lax.dot_general(a, b, (((0,), (0,)), ((), ())), preferred_element_type=F32)


def _ffn_kernel(x_ref, g_ref, wgu_ref, wd_ref, *refs, n_chunks, tf, final_norm):
    fg_ref = refs[0] if final_norm else None
    o_ref, xn_ref, acc_ref = refs[-3:]
    x = x_ref[...]
    d_ff = n_chunks * tf
    xn_ref[...] = _rms(x, g_ref[...]).astype(BF16)
    for c in range(n_chunks):
        gate = _dot(xn_ref[...], wgu_ref[:, c * tf:(c + 1) * tf].astype(BF16))
        up = _dot(xn_ref[...], wgu_ref[:, d_ff + c * tf:d_ff + (c + 1) * tf].astype(BF16))
        h = (_silu(gate) * up).astype(BF16)
        part = _dot(h, wd_ref[c * tf:(c + 1) * tf, :].astype(BF16))
        if c == 0:
            acc_ref[...] = part
        else:
            acc_ref[...] += part
    y = x + 0.5 * acc_ref[...]
    if final_norm:
        y = _rms(y, fg_ref[...])
    o_ref[...] = y


def _ffn(x, norm_g, w_gu, w_down, layer, final_g=None):
    t, d = x.shape
    d_ff = w_down.shape[1]
    tm, tf = FFN_TM, FFN_TF
    n_chunks = d_ff // tf
    assert t % tm == 0 and d_ff % tf == 0
    final_norm = final_g is not None
    extra = [final_g.reshape(1, d)] if final_norm else []
    return pl.pallas_call(
        functools.partial(_ffn_kernel, n_chunks=n_chunks, tf=tf, final_norm=final_norm),
        grid=(t // tm,),
        in_specs=[
            pl.BlockSpec((tm, d), lambda i: (i, 0)),
            _resident((1, d)),
            _resident((None, d, 2 * d_ff), (layer, 0, 0)),
            _resident((None, d_ff, d), (layer, 0, 0)),
        ] + [_resident((1, d))] * len(extra),
        out_specs=pl.BlockSpec((tm, d), lambda i: (i, 0)),
        out_shape=jax.ShapeDtypeStruct((t, d), F32),
        scratch_shapes=[pltpu.VMEM((tm, d), BF16), pltpu.VMEM((tm, d), F32)],
        compiler_params=_params(1),
        name="ffn",
    )(x, norm_g.reshape(1, d), w_gu, w_down, *extra)


def _norm_proj_kernel(x_ref, g_ref, w_ref, *o_refs, segments, tn):
    xn = _rms(x_ref[...], g_ref[...]).astype(BF16)
    for o_ref, (start, width, scale) in zip(o_refs, segments):
        for n0 in range(0, width, tn):
            n1 = min(n0 + tn, width)
            w = w_ref[:, start + n0:start + n1]
            if scale != 1.0:
                w = w * scale
            o_ref[:, n0:n1] = _dot(xn, w.astype(BF16)).astype(o_ref.dtype)


def _norm_proj(x, norm_g, w, layer, segments, out_dtypes):
    t, d = x.shape
    tm = PROJ_TM
    assert t % tm == 0 and all(s[0] % LANES == 0 for s in segments)
    return pl.pallas_call(
        functools.partial(_norm_proj_kernel, segments=tuple(segments), tn=PROJ_TN),
        grid=(t // tm,),
        in_specs=[pl.BlockSpec((tm, d), lambda i: (i, 0)), _resident((1, d)),
                  _resident((None,) + w.shape[1:], (layer, 0, 0))],
        out_specs=[pl.BlockSpec((tm, s[1]), lambda i: (i, 0)) for s in segments],
        out_shape=[jax.ShapeDtypeStruct((t, s[1]), dt) for s, dt in zip(segments, out_dtypes)],
        compiler_params=_params(1),
        name="norm_proj",
    )(x, norm_g.reshape(1, d), w)


def _proj_res_kernel(x_ref, y_ref, w_ref, o_ref):
    o_ref[...] = x_ref[...] + _dot(y_ref[...], w_ref[...].astype(BF16))


def _proj_res(x, y, w, layer):
    t, d = x.shape
    k = y.shape[1]
    tm = PROJ_RES_TM
    assert t % tm == 0
    return pl.pallas_call(
        _proj_res_kernel,
        grid=(t // tm,),
        in_specs=[
            pl.BlockSpec((tm, d), lambda i: (i, 0)),
            pl.BlockSpec((tm, k), lambda i: (i, 0)),
            _resident((None, k, d), (layer, 0, 0)),
        ],
        out_specs=pl.BlockSpec((tm, d), lambda i: (i, 0)),
        out_shape=jax.ShapeDtypeStruct((t, d), F32),
        compiler_params=_params(1),
        name="proj_res",
    )(x, y, w)


def _sb_attn_kernel(q_ref, k_ref, v_ref, o_ref, lb_a, sp_a, rs_a, lb_b, sp_b, rs_b, acc_ref, c_ref,
                    *, t, n_together):
    qi = pl.program_id(2)
    hd = SB_HEAD_DIM
    n_p = n_together
    sub = 2 * n_p * t
    buf_a, buf_b = (lb_a, sp_a, rs_a), (lb_b, sp_b, rs_b)
    both, first, second = (0, 2), (0, 1), (1, 2)
    in_a = lax.broadcasted_iota(jnp.int32, (1, 2 * hd), 1) < hd
    row = lax.broadcasted_iota(jnp.int32, (t, t), 0)
    col = lax.broadcasted_iota(jnp.int32, (t, t), 1)
    strict_lower = jnp.concatenate([col < row] * (2 * n_p), axis=0)
    tri = jnp.where(row > col, 1.0, 0.0).astype(BF16)
    sign_bit = jnp.uint32(0x80000000)
    top = 2 * qi + 1

    def mask_diagonal(x, fill):
        head = jnp.where(strict_lower, x[:sub], fill)
        return head if x.shape[0] == sub else jnp.concatenate([head, x[sub:]], axis=0)

    def pair_group(grp, _):
        lanes = [pl.ds(pl.multiple_of((grp * n_p + p) * 2 * hd, 2 * hd), 2 * hd) for p in range(n_p)]
        q_parts = []
        for s in range(2):
            for p in range(n_p):
                q2 = q_ref[0, s * t:(s + 1) * t, lanes[p]]
                q_parts += [jnp.where(in_a, q2, 0), jnp.where(in_a, 0, q2)]
        qs = jnp.concatenate(q_parts, axis=0)

        def scores(j, buf, subs, diag):
            lb_ref, sp_ref, rs_ref = buf
            r0, r1 = subs[0] * sub, subs[1] * sub
            rows_k = pl.ds(pl.multiple_of(j * t, t), t)
            k2 = [k_ref[0, rows_k, lanes[p]] for p in range(n_p)]
            z = jnp.concatenate(
                [_dot_nt(qs[(s * n_p + p) * 2 * t:(s * n_p + p + 1) * 2 * t], k2[p])
                 for s in range(*subs) for p in range(n_p)], axis=0)
            neg_abs = lax.bitcast_convert_type(lax.bitcast_convert_type(z, jnp.uint32) | sign_bit, F32)
            lb = jnp.minimum(z, 0.0) - jnp.log2(1.0 + jnp.exp2(neg_abs))
            sp = z - lb
            if diag:
                sp, lb = mask_diagonal(sp, 0.0), mask_diagonal(lb, -jnp.inf)
            lb_ref[r0:r1, :] = lb
            sp_ref[r0:r1, :] = sp.astype(BF16)
            rs_ref[r0:r1, :] = jnp.broadcast_to(jnp.sum(sp, axis=-1, keepdims=True), (r1 - r0, LANES))

        def values(j, buf, subs):
            lb_ref, sp_ref, rs_ref = buf
            r0, r1 = subs[0] * sub, subs[1] * sub
            rows_k = pl.ds(pl.multiple_of(j * t, t), t)
            c = c_ref[r0:r1, :]
            tail = _dot(sp_ref[r0:r1, :], tri)
            c_wide = jnp.concatenate([c] * (t // LANES), axis=1)
            att = jnp.exp2(lb_ref[r0:r1, :] - tail - c_wide).astype(BF16)
            c_ref[r0:r1, :] = c + rs_ref[r0:r1, :]
            for p in range(n_p):
                blocks = []
                for s in range(subs[1] - subs[0]):
                    base = (s * n_p + p) * 2 * t
                    blocks.append(jnp.concatenate([att[base:base + t], att[base + t:base + 2 * t]], axis=1))
                v2 = v_ref[0, rows_k, lanes[p]]
                vv = jnp.concatenate([jnp.where(in_a, v2, 0), jnp.where(in_a, 0, v2)], axis=0)
                acc_ref[p, subs[0] * t:subs[1] * t, :] += _dot(jnp.concatenate(blocks, axis=0), vv)

        def tile_pair(state):
            i = state[0]
            j = top - 3 - 2 * i
            values(j, buf_b, both)
            scores(j - 1, buf_a, both, False)
            values(j - 1, buf_a, both)
            scores(j - 2, buf_b, both, False)
            return i + 1, jnp.min(c_ref[...])

        def live(state):
            return (state[0] < qi - 1) & (state[1] < ATT_DEAD_CARRY)

        acc_ref[...] = jnp.zeros_like(acc_ref)
        c_ref[...] = jnp.zeros_like(c_ref)

        @pl.when(qi == 0)
        def _():
            scores(top, buf_a, second, True)
            values(top, buf_a, second)
            scores(top - 1, buf_b, both, True)
            values(top - 1, buf_b, both)

        @pl.when(qi > 0)
        def _():
            scores(top, buf_a, second, True)
            scores(top - 1, buf_b, both, True)
            values(top, buf_a, second)
            scores(top - 2, buf_a, first, False)
            values(top - 1, buf_b, both)
            c_min0 = jnp.minimum(jnp.min(c_ref[sub:, :]), jnp.min(c_ref[:sub, :] + buf_a[2][:sub, :]))
            values(top - 2, buf_a, first)

            @pl.when(c_min0 < ATT_DEAD_CARRY)
            def _():
                scores(top - 2, buf_a, second, False)
                values(top - 2, buf_a, second)
                scores(top - 3, buf_b, both, False)
                n_pairs, c_min = lax.while_loop(live, tile_pair, (jnp.int32(0), c_min0))

                @pl.when(c_min < ATT_DEAD_CARRY)
                def _():
                    values(top - 3 - 2 * n_pairs, buf_b, both)

        for p in range(n_p):
            o_ref[0, :, lanes[p]] = acc_ref[p].astype(o_ref.dtype)
        return 0

    lax.fori_loop(0, q_ref.shape[2] // (2 * hd * n_p), pair_group, 0)


def _sb_attention(q, k, v):
    b, l, d = q.shape
    t, n_p = ATT_T, ATT_PAIRS_TOGETHER
    tq = 2 * t
    hp = 2 * SB_HEAD_DIM * ATT_PAIRS_PER_STEP
    assert l % tq == 0 and d % hp == 0 and 2 * SB_HEAD_DIM == LANES and ATT_PAIRS_PER_STEP % n_p == 0
    rows = 4 * n_p * t
    stage = [pltpu.VMEM((rows, t), F32), pltpu.VMEM((rows, t), BF16), pltpu.VMEM((rows, LANES), F32)]
    return pl.pallas_call(
        functools.partial(_sb_attn_kernel, t=t, n_together=n_p),
        grid=(b, d // hp, l // tq),
        in_specs=[
            pl.BlockSpec((1, tq, hp), lambda bi, hi, qi: (bi, qi, hi)),
            pl.BlockSpec((1, l, hp), lambda bi, hi, qi: (bi, 0, hi)),
            pl.BlockSpec((1, l, hp), lambda bi, hi, qi: (bi, 0, hi)),
        ],
        out_specs=pl.BlockSpec((1, tq, hp), lambda bi, hi, qi: (bi, qi, hi)),
        out_shape=jax.ShapeDtypeStruct((b, l, d), BF16),
        scratch_shapes=stage + stage + [pltpu.VMEM((n_p, tq, LANES), F32),
                                        pltpu.VMEM((rows, LANES), F32)],
        compiler_params=_params(3),
        name="sb_attn",
    )(q, k, v)


def _sb_layer(x, b, l, norm_g, w_qkv, w_o, layer):
    t, d = x.shape
    scale = LOG2_E * SB_HEAD_DIM ** -0.5
    q, k, v = _norm_proj(x, norm_g, w_qkv, layer,
                         [(0, d, scale), (d, d, 1.0), (2 * d, d, 1.0)], [BF16] * 3)
    o = _sb_attention(q.reshape(b, l, d), k.reshape(b, l, d), v.reshape(b, l, d))
    return _proj_res(x, o.reshape(t, d), w_o, layer)


def _expand_heads(m, width):
    rows = m.shape[0]
    lane = lax.broadcasted_iota(jnp.int32, (rows, SSD_HPG * width), 1)
    out = jnp.broadcast_to(m[:, SSD_HPG - 1:SSD_HPG], (rows, SSD_HPG * width))
    for r in range(SSD_HPG - 2, -1, -1):
        out = jnp.where(lane < (r + 1) * width, m[:, r:r + 1], out)
    return out


def _to_columns(rows8):
    n = rows8.shape[1]
    padded = jnp.concatenate([rows8, jnp.zeros((n - rows8.shape[0], n), F32)], axis=0)
    return padded.T[:, :rows8.shape[0]]


def _ssd_kernel(xr_ref, br_ref, cr_ref, z_ref, dt_ref, dtn_ref, wx_ref, wb_ref, wc_ref, bx_ref,
                bb_ref, bc_ref, dtb_ref, alog_ref, dsk_ref, ng_ref, o_ref, h_ref, hist_ref,
                steps_row_ref, steps_col_ref, *, n_groups, n_chunks):
    cl, hp, n = SSD_CHUNK, SSD_HPG * SSD_HEAD_DIM, SSD_STATE
    rows = n_groups * SUBLANES
    row = lax.broadcasted_iota(jnp.int32, (cl, cl), 0)
    col = lax.broadcasted_iota(jnp.int32, (cl, cl), 1)
    causal = col <= row
    lane_hp = lax.broadcasted_iota(jnp.int32, (cl, hp), 1)

    def stage_step_sizes(dt_blk):
        dt_in = (dt_blk + dtb_ref[...]).reshape(rows, cl)
        dt_all = jnp.maximum(dt_in, 0.0) + jnp.log(1.0 + jnp.exp(-jnp.abs(dt_in)))
        a_all = dt_all * (-jnp.exp(alog_ref[...].reshape(rows, 1)))
        lane_cl = lax.broadcasted_iota(jnp.int32, (rows, cl), 1)
        sh = 1
        while sh < cl:
            a_all = a_all + jnp.where(lane_cl >= sh, pltpu.roll(a_all, sh, axis=1), 0.0)
            sh *= 2
        te_all = jnp.exp(a_all[:, cl - 1:cl] - a_all) * dt_all
        steps_row_ref[0], steps_row_ref[1] = dt_all, a_all
        steps_col_ref[0], steps_col_ref[1] = _to_columns(a_all), _to_columns(te_all)

    def dt_chunk(ref, ci):
        return ref[0, :, :, pl.ds(pl.multiple_of(ci * cl, cl), cl)]

    def chunk(ci, _):
        tok = pl.ds(pl.multiple_of(ci * cl, cl), cl)

        @pl.when((pl.program_id(2) == 0) & (ci == 0))
        def _():
            h_ref[...] = jnp.zeros_like(h_ref)
            hist_ref[...] = jnp.zeros_like(hist_ref)
            stage_step_sizes(dt_chunk(dt_ref, 0))

        dt_all, a_all = steps_row_ref[0], steps_row_ref[1]
        a_col_all, te_col_all = steps_col_ref[0], steps_col_ref[1]
        dt_next = jnp.where(ci < n_chunks - 1, dt_chunk(dt_ref, jnp.minimum(ci + 1, n_chunks - 1)),
                            dt_chunk(dtn_ref, 0))
        stage_step_sizes(dt_next)

        raw = jnp.concatenate([xr_ref[tok, :], br_ref[tok, :], cr_ref[tok, :]], axis=1)
        ext = jnp.concatenate([hist_ref[...], raw], axis=0)
        hist_ref[...] = raw[cl - HALO:, :]
        cw = jnp.concatenate([wx_ref[...], wb_ref[...], wc_ref[...]], axis=1)
        acc = cw[0:1, :] * ext
        for kk in range(1, SSD_CONV):
            acc = cw[kk:kk + 1, :] * ext + pltpu.roll(acc, 1, axis=0)
        bias = jnp.concatenate([bx_ref[...], bb_ref[...], bc_ref[...]], axis=1)
        act = _silu(acc[HALO:, :] + bias)
        xs_all = act[:, :n_groups * hp]
        bm_all = act[:, n_groups * hp:n_groups * (hp + n)].astype(BF16)
        cm_all = act[:, n_groups * (hp + n):].astype(BF16)

        for g in range(n_groups):
            xs = xs_all[:, g * hp:(g + 1) * hp]
            bm = bm_all[:, g * n:(g + 1) * n]
            cm = cm_all[:, g * n:(g + 1) * n]
            dt = dt_all[g * SUBLANES:(g + 1) * SUBLANES]
            a_cum = a_all[g * SUBLANES:(g + 1) * SUBLANES]
            a_col = a_col_all[:, g * SUBLANES:(g + 1) * SUBLANES]
            te_col = te_col_all[:, g * SUBLANES:(g + 1) * SUBLANES]

            cb = _dot_nt(cm, bm)
            ws, xblocks = [], []
            for r in range(SSD_HPG):
                seg = a_col[:, r:r + 1] - a_cum[r:r + 1, :]
                decay = jnp.exp(jnp.where(causal, seg, -jnp.inf))
                ws.append((cb * decay * dt[r:r + 1, :]).astype(BF16))
                in_head = (lane_hp >= r * SSD_HEAD_DIM) & (lane_hp < (r + 1) * SSD_HEAD_DIM)
                xblocks.append(jnp.where(in_head, xs, 0.0).astype(BF16))
            y = _dot(jnp.concatenate(ws, axis=1), jnp.concatenate(xblocks, axis=0))

            h_prev = h_ref[g]
            y = y + _dot(cm, h_prev.astype(BF16)) * _expand_heads(jnp.exp(a_col), SSD_HEAD_DIM)
            xw = (xs * _expand_heads(te_col, SSD_HEAD_DIM)).astype(BF16)
            h_decay = _expand_heads(jnp.exp(a_col[cl - 1:cl, :]), SSD_HEAD_DIM)
            h_ref[g] = h_prev * h_decay + _dot_tn(bm, xw)

            y = (y + dsk_ref[g] * xs) * _silu(z_ref[tok, g * hp:(g + 1) * hp])
            o_ref[tok, g * hp:(g + 1) * hp] = _rms(y, ng_ref[g]).astype(o_ref.dtype)
        return 0

    lax.fori_loop(0, n_chunks, chunk, 0)


def _ssd_core(xbc, z, dt_t, conv_w, conv_b, dt_bias, a_log, d_skip, norm_g, b, l):
    t = xbc.shape[0]
    g, hpg, cl, n, ng = SSD_GROUPS, SSD_HPG, SSD_CHUNK, SSD_STATE, SSD_GROUPS_PER_STEP
    hp = hpg * SSD_HEAD_DIM
    d_inner = g * hp
    kc = SSD_CHUNKS_PER_STEP
    tl = kc * cl
    nc = l // tl
    assert l % tl == 0 and hp % LANES == 0 and n == LANES and g % ng == 0
    bb, cb = d_inner // (ng * n), (d_inner + g * n) // (ng * n)
    pad = lambda p: jnp.pad(p.reshape(g, hpg, 1), ((0, 0), (0, SUBLANES - hpg), (0, 0)))
    per_lane = lambda p: jnp.repeat(p.reshape(g, hpg), SSD_HEAD_DIM, axis=1).reshape(g, 1, hp)
    conv_b = conv_b.reshape(1, -1)
    tok = lambda bi, gi, ci: bi * nc + ci
    return pl.pallas_call(
        functools.partial(_ssd_kernel, n_groups=ng, n_chunks=kc),
        grid=(b, g // ng, nc),
        in_specs=[
            pl.BlockSpec((tl, ng * hp), lambda bi, gi, ci: (tok(bi, gi, ci), gi)),
            pl.BlockSpec((tl, ng * n), lambda bi, gi, ci: (tok(bi, gi, ci), bb + gi)),
            pl.BlockSpec((tl, ng * n), lambda bi, gi, ci: (tok(bi, gi, ci), cb + gi)),
            pl.BlockSpec((tl, ng * hp), lambda bi, gi, ci: (tok(bi, gi, ci), gi)),
            pl.BlockSpec((1, ng, SUBLANES, tl), lambda bi, gi, ci: (bi, gi, 0, ci)),
            pl.BlockSpec((1, ng, SUBLANES, tl), lambda bi, gi, ci: (bi, gi, 0, jnp.minimum(ci + 1, nc - 1))),
            pl.BlockSpec((SSD_CONV, ng * hp), lambda bi, gi, ci: (0, gi)),
            pl.BlockSpec((SSD_CONV, ng * n), lambda bi, gi, ci: (0, bb + gi)),
            pl.BlockSpec((SSD_CONV, ng * n), lambda bi, gi, ci: (0, cb + gi)),
            pl.BlockSpec((1, ng * hp), lambda bi, gi, ci: (0, gi)),
            pl.BlockSpec((1, ng * n), lambda bi, gi, ci: (0, bb + gi)),
            pl.BlockSpec((1, ng * n), lambda bi, gi, ci: (0, cb + gi)),
            pl.BlockSpec((ng, SUBLANES, 1), lambda bi, gi, ci: (gi, 0, 0)),
            pl.BlockSpec((ng, SUBLANES, 1), lambda bi, gi, ci: (gi, 0, 0)),
            pl.BlockSpec((ng, 1, hp), lambda bi, gi, ci: (gi, 0, 0)),
            pl.BlockSpec((ng, 1, hp), lambda bi, gi, ci: (gi, 0, 0)),
        ],
        out_specs=pl.BlockSpec((tl, ng * hp), lambda bi, gi, ci: (tok(bi, gi, ci), gi)),
        out_shape=jax.ShapeDtypeStruct((t, d_inner), BF16),
        scratch_shapes=[pltpu.VMEM((ng, n, hp), F32), pltpu.VMEM((HALO, ng * (hp + 2 * n)), F32),
                        pltpu.VMEM((2, ng * SUBLANES, cl), F32), pltpu.VMEM((2, cl, ng * SUBLANES), F32)],
        compiler_params=_params(3),
        name="ssd_core",
    )(xbc, xbc, xbc, z, dt_t, dt_t, conv_w, conv_w, conv_w, conv_b, conv_b, conv_b,
      pad(dt_bias), pad(a_log), per_lane(d_skip), norm_g.reshape(g, 1, hp))


def _ssd_layer(x, b, l, norm_g, w_in, conv_w, conv_b, dt_bias, a_log, d_skip, ssd_norm, w_out, layer):
    g, hpg = SSD_GROUPS, SSD_HPG
    d_inner = w_out.shape[1]
    conv_dim = conv_w.shape[1]
    n_heads = g * hpg
    z, xbc, dt = _norm_proj(
        x, norm_g, w_in, layer,
        [(0, d_inner, 1.0), (d_inner, conv_dim, 1.0), (d_inner + conv_dim, n_heads, 1.0)], [F32] * 3)
    dt_t = dt.reshape(b, l, g, hpg).transpose(0, 2, 3, 1)
    dt_t = jnp.pad(dt_t, ((0, 0), (0, 0), (0, SUBLANES - hpg), (0, 0)))
    y = _ssd_core(xbc, z, dt_t, conv_w, conv_b, dt_bias, a_log, d_skip, ssd_norm, b, l)
    return _proj_res(x, y, w_out, layer)


def _short_conv_kernel(x_ref, g_ref, wi_ref, cw_ref, wo_ref, o_ref, ext_ref):
    tm, d = x_ref.shape

    @pl.when(pl.program_id(1) == 0)
    def _():
        ext_ref[0:HALO, :] = jnp.zeros((HALO, ext_ref.shape[1]), F32)

    x = x_ref[...]
    xn = _rms(x, g_ref[...]).astype(BF16)
    w_part = lambda i: wi_ref[:, i * d:(i + 1) * d].astype(BF16)
    ext_ref[HALO:HALO + tm, :] = _dot(xn, w_part(1)) * _dot(xn, w_part(2))
    cw = cw_ref[...]
    u = cw[0:1, :] * ext_ref[pl.ds(HALO - (SC_WIDTH - 1), tm), :]
    for kk in range(1, SC_WIDTH):
        u = u + cw[kk:kk + 1, :] * ext_ref[pl.ds(HALO - (SC_WIDTH - 1) + kk, tm), :]
    ext_ref[0:HALO, :] = ext_ref[tm:tm + HALO, :]
    gated = (_dot(xn, w_part(0)) * u).astype(BF16)
    o_ref[...] = x + _dot(gated, wo_ref[...].astype(BF16))


def _short_conv_layer(x, b, l, norm_g, w_in, conv_w, w_out, layer):
    t, d = x.shape
    tm = SC_TM
    nl = l // tm
    assert l % tm == 0
    return pl.pallas_call(
        _short_conv_kernel,
        grid=(b, nl),
        in_specs=[
            pl.BlockSpec((tm, d), lambda bi, li: (bi * nl + li, 0)),
            _resident((1, d)),
            _resident((None, d, 3 * d), (layer, 0, 0)),
            _resident((None, SC_WIDTH, d), (layer, 0, 0)),
            _resident((None, d, d), (layer, 0, 0)),
        ],
        out_specs=pl.BlockSpec((tm, d), lambda bi, li: (bi * nl + li, 0)),
        out_shape=jax.ShapeDtypeStruct((t, d), F32),
        scratch_shapes=[pltpu.VMEM((tm + HALO, d), F32)],
        compiler_params=_params(2),
        name="short_conv",
    )(x, norm_g.reshape(1, d), w_in, conv_w, w_out)


def kernel(x, ffn1_norm, ffn1_w_gu, ffn1_w_down, mix_norm, ffn2_norm, ffn2_w_gu, ffn2_w_down,
           sb_w_qkv, sb_w_o, ssd_w_in, ssd_conv_w, ssd_conv_b, ssd_dt_bias, ssd_a_log, ssd_d,
           ssd_norm, ssd_w_out, sc_w_in, sc_conv_w, sc_w_out, final_norm):
    b, l, d = x.shape
    depth = ffn1_norm.shape[0]
    h = x.reshape(b * l, d)
    for i in range(depth):
        h = _ffn(h, ffn1_norm[i], ffn1_w_gu, ffn1_w_down, i)
        kind, j = i % N_MIXERS, i // N_MIXERS
        if kind == 0:
            h = _sb_layer(h, b, l, mix_norm[i], sb_w_qkv, sb_w_o, j)
        elif kind == 1:
            h = _ssd_layer(h, b, l, mix_norm[i], ssd_w_in, ssd_conv_w[j], ssd_conv_b[j],
                           ssd_dt_bias[j], ssd_a_log[j], ssd_d[j], ssd_norm[j], ssd_w_out, j)
        else:
            h = _short_conv_layer(h, b, l, mix_norm[i], sc_w_in, sc_conv_w, sc_w_out, j)
        h = _ffn(h, ffn2_norm[i], ffn2_w_gu, ffn2_w_down, i,
                 final_g=final_norm if i == depth - 1 else None)
    return h.reshape(b, l, d)
```

```python
import functools

import jax
import jax.numpy as jnp
from jax import lax
from jax.experimental import pallas as pl
from jax.experimental.pallas import tpu as pltpu

F32 = jnp.float32
BF16 = jnp.bfloat16

RMS_EPS = 1e-6
LOG2_E = 1.4426950408889634
N_MIXERS = 3
SB_HEAD_DIM = 64
SSD_HEAD_DIM = 64
SSD_GROUPS = 8
SSD_HPG = 4
SSD_STATE = 128
SSD_CONV = 4
SSD_CHUNK = 128
SC_WIDTH = 3

LANES = 128
SUBLANES = 8
VMEM_LIMIT_BYTES = 60 * 1024 * 1024

FFN_TM = 1024
FFN_TF = 256
PROJ_TM = 512
PROJ_RES_TM = 1024
PROJ_TN = 512
ATT_T = 256
ATT_PAIRS_PER_STEP = 8
ATT_PAIRS_TOGETHER = 2
ATT_DEAD_CARRY = 160.0
SSD_GROUPS_PER_STEP = 4
SSD_CHUNKS_PER_STEP = 8
SC_TM = 512
HALO = SUBLANES


def _params(n_axes):
    return pltpu.CompilerParams(
        dimension_semantics=("arbitrary",) * n_axes,
        vmem_limit_bytes=VMEM_LIMIT_BYTES)


def _resident(shape, index=None):
    index = (0,) * len(shape) if index is None else index
    return pl.BlockSpec(shape, lambda *_: index, pipeline_mode=pl.Buffered(1))


def _rms(x, g):
    ms = jnp.mean(x * x, axis=-1, keepdims=True)
    return x * lax.rsqrt(ms + RMS_EPS) * g


def _silu(x):
    h = 0.5 * x
    return h + h * jnp.tanh(h)


def _dot(a, b):
    return jnp.dot(a, b, preferred_element_type=F32)


def _dot_nt(a, b):
    return lax.dot_general(a, b, (((1,), (1,)), ((), ())), preferred_element_type=F32)


def _dot_tn(a, b):
    return lax.dot_general(a, b, (((0,), (0,)), ((), ())), preferred_element_type=F32)


def _ffn_kernel(x_ref, g_ref, wgu_ref, wd_ref, *refs, n_chunks, tf, final_norm):
    fg_ref = refs[0] if final_norm else None
    o_ref, xn_ref, acc_ref = refs[-3:]
    x = x_ref[...]
    d_ff = n_chunks * tf
    xn_ref[...] = _rms(x, g_ref[...]).astype(BF16)
    for c in range(n_chunks):
        gate = _dot(xn_ref[...], wgu_ref[:, c * tf:(c + 1) * tf].astype(BF16))
        up = _dot(xn_ref[...], wgu_ref[:, d_ff + c * tf:d_ff + (c + 1) * tf].astype(BF16))
        h = (_silu(gate) * up).astype(BF16)
        part = _dot(h, wd_ref[c * tf:(c + 1) * tf, :].astype(BF16))
        if c == 0:
            acc_ref[...] = part
        else:
            acc_ref[...] += part
    y = x + 0.5 * acc_ref[...]
    if final_norm:
        y = _rms(y, fg_ref[...])
    o_ref[...] = y


def _ffn(x, norm_g, w_gu, w_down, layer, final_g=None):
    t, d = x.shape
    d_ff = w_down.shape[1]
    tm, tf = FFN_TM, FFN_TF
    n_chunks = d_ff // tf
    assert t % tm == 0 and d_ff % tf == 0
    final_norm = final_g is not None
    extra = [final_g.reshape(1, d)] if final_norm else []
    return pl.pallas_call(
        functools.partial(_ffn_kernel, n_chunks=n_chunks, tf=tf, final_norm=final_norm),
        grid=(t // tm,),
        in_specs=[
            pl.BlockSpec((tm, d), lambda i: (i, 0)),
            _resident((1, d)),
            _resident((None, d, 2 * d_ff), (layer, 0, 0)),
            _resident((None, d_ff, d), (layer, 0, 0)),
        ] + [_resident((1, d))] * len(extra),
        out_specs=pl.BlockSpec((tm, d), lambda i: (i, 0)),
        out_shape=jax.ShapeDtypeStruct((t, d), F32),
        scratch_shapes=[pltpu.VMEM((tm, d), BF16), pltpu.VMEM((tm, d), F32)],
        compiler_params=_params(1),
        name="ffn",
    )(x, norm_g.reshape(1, d), w_gu, w_down, *extra)


def _norm_proj_kernel(x_ref, g_ref, w_ref, *o_refs, segments, tn):
    xn = _rms(x_ref[...], g_ref[...]).astype(BF16)
    for o_ref, (start, width, scale) in zip(o_refs, segments):
        for n0 in range(0, width, tn):
            n1 = min(n0 + tn, width)
            w = w_ref[:, start + n0:start + n1]
            if scale != 1.0:
                w = w * scale
            o_ref[:, n0:n1] = _dot(xn, w.astype(BF16)).astype(o_ref.dtype)


def _norm_proj(x, norm_g, w, layer, segments, out_dtypes):
    t, d = x.shape
    tm = PROJ_TM
    assert t % tm == 0 and all(s[0] % LANES == 0 for s in segments)
    return pl.pallas_call(
        functools.partial(_norm_proj_kernel, segments=tuple(segments), tn=PROJ_TN),
        grid=(t // tm,),
        in_specs=[pl.BlockSpec((tm, d), lambda i: (i, 0)), _resident((1, d)),
                  _resident((None,) + w.shape[1:], (layer, 0, 0))],
        out_specs=[pl.BlockSpec((tm, s[1]), lambda i: (i, 0)) for s in segments],
        out_shape=[jax.ShapeDtypeStruct((t, s[1]), dt) for s, dt in zip(segments, out_dtypes)],
        compiler_params=_params(1),
        name="norm_proj",
    )(x, norm_g.reshape(1, d), w)


def _proj_res_kernel(x_ref, y_ref, w_ref, o_ref):
    o_ref[...] = x_ref[...] + _dot(y_ref[...], w_ref[...].astype(BF16))


def _proj_res(x, y, w, layer):
    t, d = x.shape
    k = y.shape[1]
    tm = PROJ_RES_TM
    assert t % tm == 0
    return pl.pallas_call(
        _proj_res_kernel,
        grid=(t // tm,),
        in_specs=[
            pl.BlockSpec((tm, d), lambda i: (i, 0)),
            pl.BlockSpec((tm, k), lambda i: (i, 0)),
            _resident((None, k, d), (layer, 0, 0)),
        ],
        out_specs=pl.BlockSpec((tm, d), lambda i: (i, 0)),
        out_shape=jax.ShapeDtypeStruct((t, d), F32),
        compiler_params=_params(1),
        name="proj_res",
    )(x, y, w)


def _sb_attn_kernel(q_ref, k_ref, v_ref, o_ref, lb_a, sp_a, rs_a, lb_b, sp_b, rs_b, acc_ref, c_ref,
                    *, t, n_together):
    qi = pl.program_id(2)
    hd = SB_HEAD_DIM
    n_p = n_together
    sub = 2 * n_p * t
    buf_a, buf_b = (lb_a, sp_a, rs_a), (lb_b, sp_b, rs_b)
    both, first, second = (0, 2), (0, 1), (1, 2)
    in_a = lax.broadcasted_iota(jnp.int32, (1, 2 * hd), 1) < hd
    row = lax.broadcasted_iota(jnp.int32, (t, t), 0)
    col = lax.broadcasted_iota(jnp.int32, (t, t), 1)
    strict_lower = jnp.concatenate([col < row] * (2 * n_p), axis=0)
    tri = jnp.where(row > col, 1.0, 0.0).astype(BF16)
    sign_bit = jnp.uint32(0x80000000)
    top = 2 * qi + 1

    def mask_diagonal(x, fill):
        head = jnp.where(strict_lower, x[:sub], fill)
        return head if x.shape[0] == sub else jnp.concatenate([head, x[sub:]], axis=0)

    def pair_group(grp, _):
        lanes = [pl.ds(pl.multiple_of((grp * n_p + p) * 2 * hd, 2 * hd), 2 * hd) for p in range(n_p)]
        q_parts = []
        for s in range(2):
            for p in range(n_p):
                q2 = q_ref[0, s * t:(s + 1) * t, lanes[p]]
                q_parts += [jnp.where(in_a, q2, 0), jnp.where(in_a, 0, q2)]
        qs = jnp.concatenate(q_parts, axis=0)

        def scores(j, buf, subs, diag):
            lb_ref, sp_ref, rs_ref = buf
            r0, r1 = subs[0] * sub, subs[1] * sub
            rows_k = pl.ds(pl.multiple_of(j * t, t), t)
            k2 = [k_ref[0, rows_k, lanes[p]] for p in range(n_p)]
            z = jnp.concatenate(
                [_dot_nt(qs[(s * n_p + p) * 2 * t:(s * n_p + p + 1) * 2 * t], k2[p])
                 for s in range(*subs) for p in range(n_p)], axis=0)
            neg_abs = lax.bitcast_convert_type(lax.bitcast_convert_type(z, jnp.uint32) | sign_bit, F32)
            lb = jnp.minimum(z, 0.0) - jnp.log2(1.0 + jnp.exp2(neg_abs))
            sp = z - lb
            if diag:
                sp, lb = mask_diagonal(sp, 0.0), mask_diagonal(lb, -jnp.inf)
            lb_ref[r0:r1, :] = lb
            sp_ref[r0:r1, :] = sp.astype(BF16)
            rs_ref[r0:r1, :] = jnp.broadcast_to(jnp.sum(sp, axis=-1, keepdims=True), (r1 - r0, LANES))

        def values(j, buf, subs):
            lb_ref, sp_ref, rs_ref = buf
            r0, r1 = subs[0] * sub, subs[1] * sub
            rows_k = pl.ds(pl.multiple_of(j * t, t), t)
            c = c_ref[r0:r1, :]
            tail = _dot(sp_ref[r0:r1, :], tri)
            c_wide = jnp.concatenate([c] * (t // LANES), axis=1)
            att = jnp.exp2(lb_ref[r0:r1, :] - tail - c_wide).astype(BF16)
            c_ref[r0:r1, :] = c + rs_ref[r0:r1, :]
            for p in range(n_p):
                blocks = []
                for s in range(subs[1] - subs[0]):
                    base = (s * n_p + p) * 2 * t
                    blocks.append(jnp.concatenate([att[base:base + t], att[base + t:base + 2 * t]], axis=1))
                v2 = v_ref[0, rows_k, lanes[p]]
                vv = jnp.concatenate([jnp.where(in_a, v2, 0), jnp.where(in_a, 0, v2)], axis=0)
                acc_ref[p, subs[0] * t:subs[1] * t, :] += _dot(jnp.concatenate(blocks, axis=0), vv)

        def tile_pair(state):
            i = state[0]
            j = top - 3 - 2 * i
            values(j, buf_b, both)
            scores(j - 1, buf_a, both, False)
            values(j - 1, buf_a, both)
            scores(j - 2, buf_b, both, False)
            return i + 1, jnp.min(c_ref[...])

        def live(state):
            return (state[0] < qi - 1) & (state[1] < ATT_DEAD_CARRY)

        acc_ref[...] = jnp.zeros_like(acc_ref)
        c_ref[...] = jnp.zeros_like(c_ref)

        @pl.when(qi == 0)
        def _():
            scores(top, buf_a, second, True)
            values(top, buf_a, second)
            scores(top - 1, buf_b, both, True)
            values(top - 1, buf_b, both)

        @pl.when(qi > 0)
        def _():
            scores(top, buf_a, second, True)
            scores(top - 1, buf_b, both, True)
            values(top, buf_a, second)
            scores(top - 2, buf_a, first, False)
            values(top - 1, buf_b, both)
            c_min0 = jnp.minimum(jnp.min(c_ref[sub:, :]), jnp.min(c_ref[:sub, :] + buf_a[2][:sub, :]))
            values(top - 2, buf_a, first)

            @pl.when(c_min0 < ATT_DEAD_CARRY)
            def _():
                scores(top - 2, buf_a, second, False)
                values(top - 2, buf_a, second)
                scores(top - 3, buf_b, both, False)
                n_pairs, c_min = lax.while_loop(live, tile_pair, (jnp.int32(0), c_min0))

                @pl.when(c_min < ATT_DEAD_CARRY)
                def _():
                    values(top - 3 - 2 * n_pairs, buf_b, both)

        for p in range(n_p):
            o_ref[0, :, lanes[p]] = acc_ref[p].astype(o_ref.dtype)
        return 0

    lax.fori_loop(0, q_ref.shape[2] // (2 * hd * n_p), pair_group, 0)


def _sb_attention(q, k, v):
    b, l, d = q.shape
    t, n_p = ATT_T, ATT_PAIRS_TOGETHER
    tq = 2 * t
    hp = 2 * SB_HEAD_DIM * ATT_PAIRS_PER_STEP
    assert l % tq == 0 and d % hp == 0 and 2 * SB_HEAD_DIM == LANES and ATT_PAIRS_PER_STEP % n_p == 0
    rows = 4 * n_p * t
    stage = [pltpu.VMEM((rows, t), F32), pltpu.VMEM((rows, t), BF16), pltpu.VMEM((rows, LANES), F32)]
    return pl.pallas_call(
        functools.partial(_sb_attn_kernel, t=t, n_together=n_p),
        grid=(b, d // hp, l // tq),
        in_specs=[
            pl.BlockSpec((1, tq, hp), lambda bi, hi, qi: (bi, qi, hi)),
            pl.BlockSpec((1, l, hp), lambda bi, hi, qi: (bi, 0, hi)),
            pl.BlockSpec((1, l, hp), lambda bi, hi, qi: (bi, 0, hi)),
        ],
        out_specs=pl.BlockSpec((1, tq, hp), lambda bi, hi, qi: (bi, qi, hi)),
        out_shape=jax.ShapeDtypeStruct((b, l, d), BF16),
        scratch_shapes=stage + stage + [pltpu.VMEM((n_p, tq, LANES), F32),
                                        pltpu.VMEM((rows, LANES), F32)],
        compiler_params=_params(3),
        name="sb_attn",
    )(q, k, v)


def _sb_layer(x, b, l, norm_g, w_qkv, w_o, layer):
    t, d = x.shape
    scale = LOG2_E * SB_HEAD_DIM ** -0.5
    q, k, v = _norm_proj(x, norm_g, w_qkv, layer,
                         [(0, d, scale), (d, d, 1.0), (2 * d, d, 1.0)], [BF16] * 3)
    o = _sb_attention(q.reshape(b, l, d), k.reshape(b, l, d), v.reshape(b, l, d))
    return _proj_res(x, o.reshape(t, d), w_o, layer)


def _expand_heads(m, width):
    rows = m.shape[0]
    lane = lax.broadcasted_iota(jnp.int32, (rows, SSD_HPG * width), 1)
    out = jnp.broadcast_to(m[:, SSD_HPG - 1:SSD_HPG], (rows, SSD_HPG * width))
    for r in range(SSD_HPG - 2, -1, -1):
        out = jnp.where(lane < (r + 1) * width, m[:, r:r + 1], out)
    return out


def _to_columns(rows8):
    n = rows8.shape[1]
    padded = jnp.concatenate([rows8, jnp.zeros((n - rows8.shape[0], n), F32)], axis=0)
    return padded.T[:, :rows8.shape[0]]


def _ssd_kernel(xr_ref, br_ref, cr_ref, z_ref, dt_ref, dtn_ref, wx_ref, wb_ref, wc_ref, bx_ref,
                bb_ref, bc_ref, dtb_ref, alog_ref, dsk_ref, ng_ref, o_ref, h_ref, hist_ref,
                steps_row_ref, steps_col_ref, *, n_groups, n_chunks):
    cl, hp, n = SSD_CHUNK, SSD_HPG * SSD_HEAD_DIM, SSD_STATE
    rows = n_groups * SUBLANES
    row = lax.broadcasted_iota(jnp.int32, (cl, cl), 0)
    col = lax.broadcasted_iota(jnp.int32, (cl, cl), 1)
    causal = col <= row
    lane_hp = lax.broadcasted_iota(jnp.int32, (cl, hp), 1)

    def stage_step_sizes(dt_blk):
        dt_in = (dt_blk + dtb_ref[...]).reshape(rows, cl)
        dt_all = jnp.maximum(dt_in, 0.0) + jnp.log(1.0 + jnp.exp(-jnp.abs(dt_in)))
        a_all = dt_all * (-jnp.exp(alog_ref[...].reshape(rows, 1)))
        lane_cl = lax.broadcasted_iota(jnp.int32, (rows, cl), 1)
        sh = 1
        while sh < cl:
            a_all = a_all + jnp.where(lane_cl >= sh, pltpu.roll(a_all, sh, axis=1), 0.0)
            sh *= 2
        te_all = jnp.exp(a_all[:, cl - 1:cl] - a_all) * dt_all
        steps_row_ref[0], steps_row_ref[1] = dt_all, a_all
        steps_col_ref[0], steps_col_ref[1] = _to_columns(a_all), _to_columns(te_all)

    def dt_chunk(ref, ci):
        return ref[0, :, :, pl.ds(pl.multiple_of(ci * cl, cl), cl)]

    def chunk(ci, _):
        tok = pl.ds(pl.multiple_of(ci * cl, cl), cl)

        @pl.when((pl.program_id(2) == 0) & (ci == 0))
        def _():
            h_ref[...] = jnp.zeros_like(h_ref)
            hist_ref[...] = jnp.zeros_like(hist_ref)
            stage_step_sizes(dt_chunk(dt_ref, 0))

        dt_all, a_all = steps_row_ref[0], steps_row_ref[1]
        a_col_all, te_col_all = steps_col_ref[0], steps_col_ref[1]
        dt_next = jnp.where(ci < n_chunks - 1, dt_chunk(dt_ref, jnp.minimum(ci + 1, n_chunks - 1)),
                            dt_chunk(dtn_ref, 0))
        stage_step_sizes(dt_next)

        raw = jnp.concatenate([xr_ref[tok, :], br_ref[tok, :], cr_ref[tok, :]], axis=1)
        ext = jnp.concatenate([hist_ref[...], raw], axis=0)
        hist_ref[...] = raw[cl - HALO:, :]
        cw = jnp.concatenate([wx_ref[...], wb_ref[...], wc_ref[...]], axis=1)
        acc = cw[0:1, :] * ext
        for kk in range(1, SSD_CONV):
            acc = cw[kk:kk + 1, :] * ext + pltpu.roll(acc, 1, axis=0)
        bias = jnp.concatenate([bx_ref[...], bb_ref[...], bc_ref[...]], axis=1)
        act = _silu(acc[HALO:, :] + bias)
        xs_all = act[:, :n_groups * hp]
        bm_all = act[:, n_groups * hp:n_groups * (hp + n)].astype(BF16)
        cm_all = act[:, n_groups * (hp + n):].astype(BF16)

        for g in range(n_groups):
            xs = xs_all[:, g * hp:(g + 1) * hp]
            bm = bm_all[:, g * n:(g + 1) * n]
            cm = cm_all[:, g * n:(g + 1) * n]
            dt = dt_all[g * SUBLANES:(g + 1) * SUBLANES]
            a_cum = a_all[g * SUBLANES:(g + 1) * SUBLANES]
            a_col = a_col_all[:, g * SUBLANES:(g + 1) * SUBLANES]
            te_col = te_col_all[:, g * SUBLANES:(g + 1) * SUBLANES]

            cb = _dot_nt(cm, bm)
            ws, xblocks = [], []
            for r in range(SSD_HPG):
                seg = a_col[:, r:r + 1] - a_cum[r:r + 1, :]
                decay = jnp.exp(jnp.where(causal, seg, -jnp.inf))
                ws.append((cb * decay * dt[r:r + 1, :]).astype(BF16))
                in_head = (lane_hp >= r * SSD_HEAD_DIM) & (lane_hp < (r + 1) * SSD_HEAD_DIM)
                xblocks.append(jnp.where(in_head, xs, 0.0).astype(BF16))
            y = _dot(jnp.concatenate(ws, axis=1), jnp.concatenate(xblocks, axis=0))

            h_prev = h_ref[g]
            y = y + _dot(cm, h_prev.astype(BF16)) * _expand_heads(jnp.exp(a_col), SSD_HEAD_DIM)
            xw = (xs * _expand_heads(te_col, SSD_HEAD_DIM)).astype(BF16)
            h_decay = _expand_heads(jnp.exp(a_col[cl - 1:cl, :]), SSD_HEAD_DIM)
            h_ref[g] = h_prev * h_decay + _dot_tn(bm, xw)

            y = (y + dsk_ref[g] * xs) * _silu(z_ref[tok, g * hp:(g + 1) * hp])
            o_ref[tok, g * hp:(g + 1) * hp] = _rms(y, ng_ref[g]).astype(o_ref.dtype)
        return 0

    lax.fori_loop(0, n_chunks, chunk, 0)


def _ssd_core(xbc, z, dt_t, conv_w, conv_b, dt_bias, a_log, d_skip, norm_g, b, l):
    t = xbc.shape[0]
    g, hpg, cl, n, ng = SSD_GROUPS, SSD_HPG, SSD_CHUNK, SSD_STATE, SSD_GROUPS_PER_STEP
    hp = hpg * SSD_HEAD_DIM
    d_inner = g * hp
    kc = SSD_CHUNKS_PER_STEP
    tl = kc * cl
    nc = l // tl
    assert l % tl == 0 and hp % LANES == 0 and n == LANES and g % ng == 0
    bb, cb = d_inner // (ng * n), (d_inner + g * n) // (ng * n)
    pad = lambda p: jnp.pad(p.reshape(g, hpg, 1), ((0, 0), (0, SUBLANES - hpg), (0, 0)))
    per_lane = lambda p: jnp.repeat(p.reshape(g, hpg), SSD_HEAD_DIM, axis=1).reshape(g, 1, hp)
    conv_b = conv_b.reshape(1, -1)
    tok = lambda bi, gi, ci: bi * nc + ci
    return pl.pallas_call(
        functools.partial(_ssd_kernel, n_groups=ng, n_chunks=kc),
        grid=(b, g // ng, nc),
        in_specs=[
            pl.BlockSpec((tl, ng * hp), lambda bi, gi, ci: (tok(bi, gi, ci), gi)),
            pl.BlockSpec((tl, ng * n), lambda bi, gi, ci: (tok(bi, gi, ci), bb + gi)),
            pl.BlockSpec((tl, ng * n), lambda bi, gi, ci: (tok(bi, gi, ci), cb + gi)),
            pl.BlockSpec((tl, ng * hp), lambda bi, gi, ci: (tok(bi, gi, ci), gi)),
            pl.BlockSpec((1, ng, SUBLANES, tl), lambda bi, gi, ci: (bi, gi, 0, ci)),
            pl.BlockSpec((1, ng, SUBLANES, tl), lambda bi, gi, ci: (bi, gi, 0, jnp.minimum(ci + 1, nc - 1))),
            pl.BlockSpec((SSD_CONV, ng * hp), lambda bi, gi, ci: (0, gi)),
            pl.BlockSpec((SSD_CONV, ng * n), lambda bi, gi, ci: (0, bb + gi)),
            pl.BlockSpec((SSD_CONV, ng * n), lambda bi, gi, ci: (0, cb + gi)),
            pl.BlockSpec((1, ng * hp), lambda bi, gi, ci: (0, gi)),
            pl.BlockSpec((1, ng * n), lambda bi, gi, ci: (0, bb + gi)),
            pl.BlockSpec((1, ng * n), lambda bi, gi, ci: (0, cb + gi)),
            pl.BlockSpec((ng, SUBLANES, 1), lambda bi, gi, ci: (gi, 0, 0)),
            pl.BlockSpec((ng, SUBLANES, 1), lambda bi, gi, ci: (gi, 0, 0)),
            pl.BlockSpec((ng, 1, hp), lambda bi, gi, ci: (gi, 0, 0)),
            pl.BlockSpec((ng, 1, hp), lambda bi, gi, ci: (gi, 0, 0)),
        ],
        out_specs=pl.BlockSpec((tl, ng * hp), lambda bi, gi, ci: (tok(bi, gi, ci), gi)),
        out_shape=jax.ShapeDtypeStruct((t, d_inner), BF16),
        scratch_shapes=[pltpu.VMEM((ng, n, hp), F32), pltpu.VMEM((HALO, ng * (hp + 2 * n)), F32),
                        pltpu.VMEM((2, ng * SUBLANES, cl), F32), pltpu.VMEM((2, cl, ng * SUBLANES), F32)],
        compiler_params=_params(3),
        name="ssd_core",
    )(xbc, xbc, xbc, z, dt_t, dt_t, conv_w, conv_w, conv_w, conv_b, conv_b, conv_b,
      pad(dt_bias), pad(a_log), per_lane(d_skip), norm_g.reshape(g, 1, hp))


def _ssd_layer(x, b, l, norm_g, w_in, conv_w, conv_b, dt_bias, a_log, d_skip, ssd_norm, w_out, layer):
    g, hpg = SSD_GROUPS, SSD_HPG
    d_inner = w_out.shape[1]
    conv_dim = conv_w.shape[1]
    n_heads = g * hpg
    z, xbc, dt = _norm_proj(
        x, norm_g, w_in, layer,
        [(0, d_inner, 1.0), (d_inner, conv_dim, 1.0), (d_inner + conv_dim, n_heads, 1.0)], [F32] * 3)
    dt_t = dt.reshape(b, l, g, hpg).transpose(0, 2, 3, 1)
    dt_t = jnp.pad(dt_t, ((0, 0), (0, 0), (0, SUBLANES - hpg), (0, 0)))
    y = _ssd_core(xbc, z, dt_t, conv_w, conv_b, dt_bias, a_log, d_skip, ssd_norm, b, l)
    return _proj_res(x, y, w_out, layer)


def _short_conv_kernel(x_ref, g_ref, wi_ref, cw_ref, wo_ref, o_ref, ext_ref):
    tm, d = x_ref.shape

    @pl.when(pl.program_id(1) == 0)
    def _():
        ext_ref[0:HALO, :] = jnp.zeros((HALO, ext_ref.shape[1]), F32)

    x = x_ref[...]
    xn = _rms(x, g_ref[...]).astype(BF16)
    w_part = lambda i: wi_ref[:, i * d:(i + 1) * d].astype(BF16)
    ext_ref[HALO:HALO + tm, :] = _dot(xn, w_part(1)) * _dot(xn, w_part(2))
    cw = cw_ref[...]
    u = cw[0:1, :] * ext_ref[pl.ds(HALO - (SC_WIDTH - 1), tm), :]
    for kk in range(1, SC_WIDTH):
        u = u + cw[kk:kk + 1, :] * ext_ref[pl.ds(HALO - (SC_WIDTH - 1) + kk, tm), :]
    ext_ref[0:HALO, :] = ext_ref[tm:tm + HALO, :]
    gated = (_dot(xn, w_part(0)) * u).astype(BF16)
    o_ref[...] = x + _dot(gated, wo_ref[...].astype(BF16))


def _short_conv_layer(x, b, l, norm_g, w_in, conv_w, w_out, layer):
    t, d = x.shape
    tm = SC_TM
    nl = l // tm
    assert l % tm == 0
    return pl.pallas_call(
        _short_conv_kernel,
        grid=(b, nl),
        in_specs=[
            pl.BlockSpec((tm, d), lambda bi, li: (bi * nl + li, 0)),
            _resident((1, d)),
            _resident((None, d, 3 * d), (layer, 0, 0)),
            _resident((None, SC_WIDTH, d), (layer, 0, 0)),
            _resident((None, d, d), (layer, 0, 0)),
        ],
        out_specs=pl.BlockSpec((tm, d), lambda bi, li: (bi * nl + li, 0)),
        out_shape=jax.ShapeDtypeStruct((t, d), F32),
        scratch_shapes=[pltpu.VMEM((tm + HALO, d), F32)],
        compiler_params=_params(2),
        name="short_conv",
    )(x, norm_g.reshape(1, d), w_in, conv_w, w_out)


def kernel(x, ffn1_norm, ffn1_w_gu, ffn1_w_down, mix_norm, ffn2_norm, ffn2_w_gu, ffn2_w_down,
           sb_w_qkv, sb_w_o, ssd_w_in, ssd_conv_w, ssd_conv_b, ssd_dt_bias, ssd_a_log, ssd_d,
           ssd_norm, ssd_w_out, sc_w_in, sc_conv_w, sc_w_out, final_norm):
    b, l, d = x.shape
    depth = ffn1_norm.shape[0]
    h = x.reshape(b * l, d)
    for i in range(depth):
        h = _ffn(h, ffn1_norm[i], ffn1_w_gu, ffn1_w_down, i)
        kind, j = i % N_MIXERS, i // N_MIXERS
        if kind == 0:
            h = _sb_layer(h, b, l, mix_norm[i], sb_w_qkv, sb_w_o, j)
        elif kind == 1:
            h = _ssd_layer(h, b, l, mix_norm[i], ssd_w_in, ssd_conv_w[j], ssd_conv_b[j],
                           ssd_dt_bias[j], ssd_a_log[j], ssd_d[j], ssd_norm[j], ssd_w_out, j)
        else:
            h = _short_conv_layer(h, b, l, mix_norm[i], sc_w_in, sc_conv_w, sc_w_out, j)
        h = _ffn(h, ffn2_norm[i], ffn2_w_gu, ffn2_w_down, i,
                 final_g=final_norm if i == depth - 1 else None)
    return h.reshape(b, l, d)
```

```python
import functools

import jax
import jax.numpy as jnp
from jax import lax
from jax.experimental import pallas as pl
from jax.experimental.pallas import tpu as pltpu

F32 = jnp.float32
BF16 = jnp.bfloat16

RMS_EPS = 1e-6
LOG2_E = 1.4426950408889634
N_MIXERS = 3
SB_HEAD_DIM = 64
SSD_HEAD_DIM = 64
SSD_GROUPS = 8
SSD_HPG = 4
SSD_STATE = 128
SSD_CONV = 4
SSD_CHUNK = 128
SC_WIDTH = 3

LANES = 128
SUBLANES = 8
VMEM_LIMIT_BYTES = 60 * 1024 * 1024

FFN_TM = 1024
FFN_FUSED_TM = 512
FFN_TF = 256
PROJ_TM = 512
PROJ_TN = 512
ATT_T = 256
ATT_PAIRS_PER_STEP = 4
ATT_PAIRS_TOGETHER = 2
ATT_DEAD_CARRY = 160.0
SSD_GROUPS_PER_STEP = 4
SSD_CHUNKS_PER_STEP = 4
SC_TM = 512
HALO = SUBLANES


def _params(n_axes):
    return pltpu.CompilerParams(
        dimension_semantics=("arbitrary",) * n_axes,
        vmem_limit_bytes=VMEM_LIMIT_BYTES)


def _resident(shape, index=None):
    index = (0,) * len(shape) if index is None else index
    return pl.BlockSpec(shape, lambda *_: index, pipeline_mode=pl.Buffered(1))


def _rms(x, g):
    ms = jnp.mean(x * x, axis=-1, keepdims=True)
    return x * lax.rsqrt(ms + RMS_EPS) * g


def _silu(x):
    h = 0.5 * x
    return h + h * jnp.tanh(h)


def _dot(a, b):
    return jnp.dot(a, b, preferred_element_type=F32)


def _dot_nt(a, b):
    return lax.dot_general(a, b, (((1,), (1,)), ((), ())), preferred_element_type=F32)


def _dot_tn(a, b):
    return lax.dot_general(a, b, (((0,), (0,)), ((), ())), preferred_element_type=F32)


def _ffn_kernel(x_ref, g_ref, wgu_ref, wd_ref, *refs, n_chunks, tf, final_norm, mixer_out):
    refs = list(refs)
    y_ref, wo_ref = (refs.pop(0), refs.pop(0)) if mixer_out else (None, None)
    fg_ref = refs[0] if final_norm else None
    o_ref, xn_ref, acc_ref = refs[-3:]
    x = x_ref[...]
    if mixer_out:
        x = x + _dot(y_ref[...], wo_ref[...].astype(BF16))
    d_ff = n_chunks * tf
    xn_ref[...] = _rms(x, g_ref[...]).astype(BF16)
    for c in range(n_chunks):
        gate = _dot(xn_ref[...], wgu_ref[:, c * tf:(c + 1) * tf].astype(BF16))
        up = _dot(xn_ref[...], wgu_ref[:, d_ff + c * tf:d_ff + (c + 1) * tf].astype(BF16))
        h = (_silu(gate) * up).astype(BF16)
        part = _dot(h, wd_ref[c * tf:(c + 1) * tf, :].astype(BF16))
        if c == 0:
            acc_ref[...] = part
        else:
            acc_ref[...] += part
    y = x + 0.5 * acc_ref[...]
    if final_norm:
        y = _rms(y, fg_ref[...])
    o_ref[...] = y


def _ffn(x, norm_g, w_gu, w_down, layer, final_g=None, mixer_out=None):
    t, d = x.shape
    d_ff = w_down.shape[1]
    tm, tf = (FFN_TM if mixer_out is None else FFN_FUSED_TM), FFN_TF
    n_chunks = d_ff // tf
    assert t % tm == 0 and d_ff % tf == 0
    final_norm = final_g is not None
    extra = [final_g.reshape(1, d)] if final_norm else []
    pre, pre_specs = [], []
    if mixer_out is not None:
        y, w_o, j = mixer_out
        k = y.shape[1]
        pre = [y, w_o]
        pre_specs = [pl.BlockSpec((tm, k), lambda i: (i, 0)), _resident((None, k, d), (j, 0, 0))]
    return pl.pallas_call(
        functools.partial(_ffn_kernel, n_chunks=n_chunks, tf=tf, final_norm=final_norm,
                          mixer_out=mixer_out is not None),
        grid=(t // tm,),
        in_specs=[
            pl.BlockSpec((tm, d), lambda i: (i, 0)),
            _resident((1, d)),
            _resident((None, d, 2 * d_ff), (layer, 0, 0)),
            _resident((None, d_ff, d), (layer, 0, 0)),
        ] + pre_specs + [_resident((1, d))] * len(extra),
        out_specs=pl.BlockSpec((tm, d), lambda i: (i, 0)),
        out_shape=jax.ShapeDtypeStruct((t, d), F32),
        scratch_shapes=[pltpu.VMEM((tm, d), BF16), pltpu.VMEM((tm, d), F32)],
        compiler_params=_params(1),
        name="ffn",
    )(x, norm_g.reshape(1, d), w_gu, w_down, *pre, *extra)


def _norm_proj_kernel(x_ref, g_ref, w_ref, *o_refs, segments, tn):
    xn = _rms(x_ref[...], g_ref[...]).astype(BF16)
    for o_ref, (start, width, scale) in zip(o_refs, segments):
        for n0 in range(0, width, tn):
            n1 = min(n0 + tn, width)
            w = w_ref[:, start + n0:start + n1]
            if scale != 1.0:
                w = w * scale
            o_ref[:, n0:n1] = _dot(xn, w.astype(BF16)).astype(o_ref.dtype)


def _norm_proj(x, norm_g, w, layer, segments, out_dtypes):
    t, d = x.shape
    tm = PROJ_TM
    assert t % tm == 0 and all(s[0] % LANES == 0 for s in segments)
    return pl.pallas_call(
        functools.partial(_norm_proj_kernel, segments=tuple(segments), tn=PROJ_TN),
        grid=(t // tm,),
        in_specs=[pl.BlockSpec((tm, d), lambda i: (i, 0)), _resident((1, d)),
                  _resident((None,) + w.shape[1:], (layer, 0, 0))],
        out_specs=[pl.BlockSpec((tm, s[1]), lambda i: (i, 0)) for s in segments],
        out_shape=[jax.ShapeDtypeStruct((t, s[1]), dt) for s, dt in zip(segments, out_dtypes)],
        compiler_params=_params(1),
        name="norm_proj",
    )(x, norm_g.reshape(1, d), w)


def _sb_attn_kernel(q_ref, k_ref, v_ref, o_ref, lb_a, sp_a, rs_a, lb_b, sp_b, rs_b, acc_ref, c_ref,
                    *, t, n_together):
    qi = pl.program_id(2)
    hd = SB_HEAD_DIM
    n_p = n_together
    sub = 2 * n_p * t
    buf_a, buf_b = (lb_a, sp_a, rs_a), (lb_b, sp_b, rs_b)
    both, first, second = (0, 2), (0, 1), (1, 2)
    in_a = lax.broadcasted_iota(jnp.int32, (1, 2 * hd), 1) < hd
    row = lax.broadcasted_iota(jnp.int32, (t, t), 0)
    col = lax.broadcasted_iota(jnp.int32, (t, t), 1)
    strict_lower = jnp.concatenate([col < row] * (2 * n_p), axis=0)
    tri = jnp.where(row > col, 1.0, 0.0).astype(BF16)
    sign_bit = jnp.uint32(0x80000000)
    top = 2 * qi + 1

    def mask_diagonal(x, fill):
        head = jnp.where(strict_lower, x[:sub], fill)
        return head if x.shape[0] == sub else jnp.concatenate([head, x[sub:]], axis=0)

    def pair_group(grp, _):
        lanes = [pl.ds(pl.multiple_of((grp * n_p + p) * 2 * hd, 2 * hd), 2 * hd) for p in range(n_p)]
        q_parts = []
        for s in range(2):
            for p in range(n_p):
                q2 = q_ref[0, s * t:(s + 1) * t, lanes[p]]
                q_parts += [jnp.where(in_a, q2, 0), jnp.where(in_a, 0, q2)]
        qs = jnp.concatenate(q_parts, axis=0)

        def scores(j, buf, subs, diag):
            lb_ref, sp_ref, rs_ref = buf
            r0, r1 = subs[0] * sub, subs[1] * sub
            rows_k = pl.ds(pl.multiple_of(j * t, t), t)
            k2 = [k_ref[0, rows_k, lanes[p]] for p in range(n_p)]
            z = jnp.concatenate(
                [_dot_nt(qs[(s * n_p + p) * 2 * t:(s * n_p + p + 1) * 2 * t], k2[p])
                 for s in range(*subs) for p in range(n_p)], axis=0)
            neg_abs = lax.bitcast_convert_type(lax.bitcast_convert_type(z, jnp.uint32) | sign_bit, F32)
            lb = jnp.minimum(z, 0.0) - jnp.log2(1.0 + jnp.exp2(neg_abs))
            sp = z - lb
            if diag:
                sp, lb = mask_diagonal(sp, 0.0), mask_diagonal(lb, -jnp.inf)
            lb_ref[r0:r1, :] = lb
            sp_ref[r0:r1, :] = sp.astype(BF16)
            rs_ref[r0:r1, :] = jnp.broadcast_to(jnp.sum(sp, axis=-1, keepdims=True), (r1 - r0, LANES))

        def values(j, buf, subs):
            lb_ref, sp_ref, rs_ref = buf
            r0, r1 = subs[0] * sub, subs[1] * sub
            rows_k = pl.ds(pl.multiple_of(j * t, t), t)
            c = c_ref[r0:r1, :]
            tail = _dot(sp_ref[r0:r1, :], tri)
            c_wide = jnp.concatenate([c] * (t // LANES), axis=1)
            att = jnp.exp2(lb_ref[r0:r1, :] - tail - c_wide).astype(BF16)
            c_ref[r0:r1, :] = c + rs_ref[r0:r1, :]
            for p in range(n_p):
                blocks = []
                for s in range(subs[1] - subs[0]):
                    base = (s * n_p + p) * 2 * t
                    blocks.append(jnp.concatenate([att[base:base + t], att[base + t:base + 2 * t]], axis=1))
                v2 = v_ref[0, rows_k, lanes[p]]
                vv = jnp.concatenate([jnp.where(in_a, v2, 0), jnp.where(in_a, 0, v2)], axis=0)
                acc_ref[p, subs[0] * t:subs[1] * t, :] += _dot(jnp.concatenate(blocks, axis=0), vv)

        def tile_pair(state):
            i = state[0]
            j = top - 3 - 2 * i
            values(j, buf_b, both)
            scores(j - 1, buf_a, both, False)
            values(j - 1, buf_a, both)
            scores(j - 2, buf_b, both, False)
            return i + 1, jnp.min(c_ref[...])

        def live(state):
            return (state[0] < qi - 1) & (state[1] < ATT_DEAD_CARRY)

        acc_ref[...] = jnp.zeros_like(acc_ref)
        c_ref[...] = jnp.zeros_like(c_ref)

        @pl.when(qi == 0)
        def _():
            scores(top, buf_a, second, True)
            values(top, buf_a, second)
            scores(top - 1, buf_b, both, True)
            values(top - 1, buf_b, both)

        @pl.when(qi > 0)
        def _():
            scores(top, buf_a, second, True)
            scores(top - 1, buf_b, both, True)
            values(top, buf_a, second)
            scores(top - 2, buf_a, first, False)
            values(top - 1, buf_b, both)
            c_min0 = jnp.minimum(jnp.min(c_ref[sub:, :]), jnp.min(c_ref[:sub, :] + buf_a[2][:sub, :]))
            values(top - 2, buf_a, first)

            @pl.when(c_min0 < ATT_DEAD_CARRY)
            def _():
                scores(top - 2, buf_a, second, False)
                values(top - 2, buf_a, second)
                scores(top - 3, buf_b, both, False)
                n_pairs, c_min = lax.while_loop(live, tile_pair, (jnp.int32(0), c_min0))

                @pl.when(c_min < ATT_DEAD_CARRY)
                def _():
                    values(top - 3 - 2 * n_pairs, buf_b, both)

        for p in range(n_p):
            o_ref[0, :, lanes[p]] = acc_ref[p].astype(o_ref.dtype)
        return 0

    lax.fori_loop(0, q_ref.shape[2] // (2 * hd * n_p), pair_group, 0)


def _sb_attention(q, k, v):
    b, l, d = q.shape
    t, n_p = ATT_T, ATT_PAIRS_TOGETHER
    tq = 2 * t
    hp = 2 * SB_HEAD_DIM * ATT_PAIRS_PER_STEP
    assert l % tq == 0 and d % hp == 0 and 2 * SB_HEAD_DIM == LANES and ATT_PAIRS_PER_STEP % n_p == 0
    rows = 4 * n_p * t
    stage = [pltpu.VMEM((rows, t), F32), pltpu.VMEM((rows, t), BF16), pltpu.VMEM((rows, LANES), F32)]
    return pl.pallas_call(
        functools.partial(_sb_attn_kernel, t=t, n_together=n_p),
        grid=(b, d // hp, l // tq),
        in_specs=[
            pl.BlockSpec((1, tq, hp), lambda bi, hi, qi: (bi, qi, hi)),
            pl.BlockSpec((1, l, hp), lambda bi, hi, qi: (bi, 0, hi)),
            pl.BlockSpec((1, l, hp), lambda bi, hi, qi: (bi, 0, hi)),
        ],
        out_specs=pl.BlockSpec((1, tq, hp), lambda bi, hi, qi: (bi, qi, hi)),
        out_shape=jax.ShapeDtypeStruct((b, l, d), BF16),
        scratch_shapes=stage + stage + [pltpu.VMEM((n_p, tq, LANES), F32),
                                        pltpu.VMEM((rows, LANES), F32)],
        compiler_params=_params(3),
        name="sb_attn",
    )(q, k, v)


def _sb_mixer(x, b, l, norm_g, w_qkv, layer):
    t, d = x.shape
    scale = LOG2_E * SB_HEAD_DIM ** -0.5
    q, k, v = _norm_proj(x, norm_g, w_qkv, layer,
                         [(0, d, scale), (d, d, 1.0), (2 * d, d, 1.0)], [BF16] * 3)
    o = _sb_attention(q.reshape(b, l, d), k.reshape(b, l, d), v.reshape(b, l, d))
    return o.reshape(t, d)


def _expand_heads(m, width):
    rows = m.shape[0]
    lane = lax.broadcasted_iota(jnp.int32, (rows, SSD_HPG * width), 1)
    out = jnp.broadcast_to(m[:, SSD_HPG - 1:SSD_HPG], (rows, SSD_HPG * width))
    for r in range(SSD_HPG - 2, -1, -1):
        out = jnp.where(lane < (r + 1) * width, m[:, r:r + 1], out)
    return out


def _to_columns(rows8):
    n = rows8.shape[1]
    padded = jnp.concatenate([rows8, jnp.zeros((n - rows8.shape[0], n), F32)], axis=0)
    return padded.T[:, :rows8.shape[0]]


def _ssd_kernel(xr_ref, br_ref, cr_ref, z_ref, dt_ref, dtn_ref, wx_ref, wb_ref, wc_ref, bx_ref,
                bb_ref, bc_ref, dtb_ref, alog_ref, dsk_ref, ng_ref, o_ref, h_ref, hist_ref,
                steps_row_ref, steps_col_ref, *, n_groups, n_chunks):
    cl, hp, n = SSD_CHUNK, SSD_HPG * SSD_HEAD_DIM, SSD_STATE
    rows = n_groups * SUBLANES
    row = lax.broadcasted_iota(jnp.int32, (cl, cl), 0)
    col = lax.broadcasted_iota(jnp.int32, (cl, cl), 1)
    causal = col <= row
    lane_hp = lax.broadcasted_iota(jnp.int32, (cl, hp), 1)

    def stage_step_sizes(dt_blk):
        dt_in = (dt_blk + dtb_ref[...]).reshape(rows, cl)
        dt_all = jnp.maximum(dt_in, 0.0) + jnp.log(1.0 + jnp.exp(-jnp.abs(dt_in)))
        a_all = dt_all * (-jnp.exp(alog_ref[...].reshape(rows, 1)))
        lane_cl = lax.broadcasted_iota(jnp.int32, (rows, cl), 1)
        sh = 1
        while sh < cl:
            a_all = a_all + jnp.where(lane_cl >= sh, pltpu.roll(a_all, sh, axis=1), 0.0)
            sh *= 2
        te_all = jnp.exp(a_all[:, cl - 1:cl] - a_all) * dt_all
        steps_row_ref[0], steps_row_ref[1] = dt_all, a_all
        steps_col_ref[0], steps_col_ref[1] = _to_columns(a_all), _to_columns(te_all)

    def dt_chunk(ref, ci):
        return ref[0, :, :, pl.ds(pl.multiple_of(ci * cl, cl), cl)]

    def chunk(ci, _):
        tok = pl.ds(pl.multiple_of(ci * cl, cl), cl)

        @pl.when((pl.program_id(2) == 0) & (ci == 0))
        def _():
            h_ref[...] = jnp.zeros_like(h_ref)
            hist_ref[...] = jnp.zeros_like(hist_ref)
            stage_step_sizes(dt_chunk(dt_ref, 0))

        dt_all, a_all = steps_row_ref[0], steps_row_ref[1]
        a_col_all, te_col_all = steps_col_ref[0], steps_col_ref[1]
        dt_next = jnp.where(ci < n_chunks - 1, dt_chunk(dt_ref, jnp.minimum(ci + 1, n_chunks - 1)),
                            dt_chunk(dtn_ref, 0))
        stage_step_sizes(dt_next)

        raw = jnp.concatenate([xr_ref[tok, :], br_ref[tok, :], cr_ref[tok, :]], axis=1)
        ext = jnp.concatenate([hist_ref[...], raw], axis=0)
        hist_ref[...] = raw[cl - HALO:, :]
        cw = jnp.concatenate([wx_ref[...], wb_ref[...], wc_ref[...]], axis=1)
        acc = cw[0:1, :] * ext
        for kk in range(1, SSD_CONV):
            acc = cw[kk:kk + 1, :] * ext + pltpu.roll(acc, 1, axis=0)
        bias = jnp.concatenate([bx_ref[...], bb_ref[...], bc_ref[...]], axis=1)
        act = _silu(acc[HALO:, :] + bias)
        xs_all = act[:, :n_groups * hp]
        bm_all = act[:, n_groups * hp:n_groups * (hp + n)].astype(BF16)
        cm_all = act[:, n_groups * (hp + n):].astype(BF16)

        for g in range(n_groups):
            xs = xs_all[:, g * hp:(g + 1) * hp]
            bm = bm_all[:, g * n:(g + 1) * n]
            cm = cm_all[:, g * n:(g + 1) * n]
            dt = dt_all[g * SUBLANES:(g + 1) * SUBLANES]
            a_cum = a_all[g * SUBLANES:(g + 1) * SUBLANES]
            a_col = a_col_all[:, g * SUBLANES:(g + 1) * SUBLANES]
            te_col = te_col_all[:, g * SUBLANES:(g + 1) * SUBLANES]

            cb = _dot_nt(cm, bm)
            ws, xblocks = [], []
            for r in range(SSD_HPG):
                seg = a_col[:, r:r + 1] - a_cum[r:r + 1, :]
                decay = jnp.exp(jnp.where(causal, seg, -jnp.inf))
                ws.append((cb * decay * dt[r:r + 1, :]).astype(BF16))
                in_head = (lane_hp >= r * SSD_HEAD_DIM) & (lane_hp < (r + 1) * SSD_HEAD_DIM)
                xblocks.append(jnp.where(in_head, xs, 0.0).astype(BF16))
            y = _dot(jnp.concatenate(ws, axis=1), jnp.concatenate(xblocks, axis=0))

            h_prev = h_ref[g]
            y = y + _dot(cm, h_prev.astype(BF16)) * _expand_heads(jnp.exp(a_col), SSD_HEAD_DIM)
            xw = (xs * _expand_heads(te_col, SSD_HEAD_DIM)).astype(BF16)
            h_decay = _expand_heads(jnp.exp(a_col[cl - 1:cl, :]), SSD_HEAD_DIM)
            h_ref[g] = h_prev * h_decay + _dot_tn(bm, xw)

            y = (y + dsk_ref[g] * xs) * _silu(z_ref[tok, g * hp:(g + 1) * hp])
            o_ref[tok, g * hp:(g + 1) * hp] = _rms(y, ng_ref[g]).astype(o_ref.dtype)
        return 0

    lax.fori_loop(0, n_chunks, chunk, 0)


def _ssd_core(xbc, z, dt_t, conv_w, conv_b, dt_bias, a_log, d_skip, norm_g, b, l):
    t = xbc.shape[0]
    g, hpg, cl, n, ng = SSD_GROUPS, SSD_HPG, SSD_CHUNK, SSD_STATE, SSD_GROUPS_PER_STEP
    hp = hpg * SSD_HEAD_DIM
    d_inner = g * hp
    kc = SSD_CHUNKS_PER_STEP
    tl = kc * cl
    nc = l // tl
    assert l % tl == 0 and hp % LANES == 0 and n == LANES and g % ng == 0
    bb, cb = d_inner // (ng * n), (d_inner + g * n) // (ng * n)
    pad = lambda p: jnp.pad(p.reshape(g, hpg, 1), ((0, 0), (0, SUBLANES - hpg), (0, 0)))
    per_lane = lambda p: jnp.repeat(p.reshape(g, hpg), SSD_HEAD_DIM, axis=1).reshape(g, 1, hp)
    conv_b = conv_b.reshape(1, -1)
    tok = lambda bi, gi, ci: bi * nc + ci
    return pl.pallas_call(
        functools.partial(_ssd_kernel, n_groups=ng, n_chunks=kc),
        grid=(b, g // ng, nc),
        in_specs=[
            pl.BlockSpec((tl, ng * hp), lambda bi, gi, ci: (tok(bi, gi, ci), gi)),
            pl.BlockSpec((tl, ng * n), lambda bi, gi, ci: (tok(bi, gi, ci), bb + gi)),
            pl.BlockSpec((tl, ng * n), lambda bi, gi, ci: (tok(bi, gi, ci), cb + gi)),
            pl.BlockSpec((tl, ng * hp), lambda bi, gi, ci: (tok(bi, gi, ci), gi)),
            pl.BlockSpec((1, ng, SUBLANES, tl), lambda bi, gi, ci: (bi, gi, 0, ci)),
            pl.BlockSpec((1, ng, SUBLANES, tl), lambda bi, gi, ci: (bi, gi, 0, jnp.minimum(ci + 1, nc - 1))),
            pl.BlockSpec((SSD_CONV, ng * hp), lambda bi, gi, ci: (0, gi)),
            pl.BlockSpec((SSD_CONV, ng * n), lambda bi, gi, ci: (0, bb + gi)),
            pl.BlockSpec((SSD_CONV, ng * n), lambda bi, gi, ci: (0, cb + gi)),
            pl.BlockSpec((1, ng * hp), lambda bi, gi, ci: (0, gi)),
            pl.BlockSpec((1, ng * n), lambda bi, gi, ci: (0, bb + gi)),
            pl.BlockSpec((1, ng * n), lambda bi, gi, ci: (0, cb + gi)),
            pl.BlockSpec((ng, SUBLANES, 1), lambda bi, gi, ci: (gi, 0, 0)),
            pl.BlockSpec((ng, SUBLANES, 1), lambda bi, gi, ci: (gi, 0, 0)),
            pl.BlockSpec((ng, 1, hp), lambda bi, gi, ci: (gi, 0, 0)),
            pl.BlockSpec((ng, 1, hp), lambda bi, gi, ci: (gi, 0, 0)),
        ],
        out_specs=pl.BlockSpec((tl, ng * hp), lambda bi, gi, ci: (tok(bi, gi, ci), gi)),
        out_shape=jax.ShapeDtypeStruct((t, d_inner), BF16),
        scratch_shapes=[pltpu.VMEM((ng, n, hp), F32), pltpu.VMEM((HALO, ng * (hp + 2 * n)), F32),
                        pltpu.VMEM((2, ng * SUBLANES, cl), F32), pltpu.VMEM((2, cl, ng * SUBLANES), F32)],
        compiler_params=_params(3),
        name="ssd_core",
    )(xbc, xbc, xbc, z, dt_t, dt_t, conv_w, conv_w, conv_w, conv_b, conv_b, conv_b,
      pad(dt_bias), pad(a_log), per_lane(d_skip), norm_g.reshape(g, 1, hp))


def _ssd_mixer(x, b, l, norm_g, w_in, conv_w, conv_b, dt_bias, a_log, d_skip, ssd_norm, layer):
    g, hpg = SSD_GROUPS, SSD_HPG
    d_inner = ssd_norm.shape[0]
    conv_dim = conv_w.shape[1]
    n_heads = g * hpg
    z, xbc, dt = _norm_proj(
        x, norm_g, w_in, layer,
        [(0, d_inner, 1.0), (d_inner, conv_dim, 1.0), (d_inner + conv_dim, n_heads, 1.0)], [F32] * 3)
    dt_t = dt.reshape(b, l, g, hpg).transpose(0, 2, 3, 1)
    dt_t = jnp.pad(dt_t, ((0, 0), (0, 0), (0, SUBLANES - hpg), (0, 0)))
    return _ssd_core(xbc, z, dt_t, conv_w, conv_b, dt_bias, a_log, d_skip, ssd_norm, b, l)


def _short_conv_kernel(x_ref, g_ref, wi_ref, cw_ref, wo_ref, o_ref, ext_ref):
    tm, d = x_ref.shape

    @pl.when(pl.program_id(1) == 0)
    def _():
        ext_ref[0:HALO, :] = jnp.zeros((HALO, ext_ref.shape[1]), F32)

    x = x_ref[...]
    xn = _rms(x, g_ref[...]).astype(BF16)
    w_part = lambda i: wi_ref[:, i * d:(i + 1) * d].astype(BF16)
    ext_ref[HALO:HALO + tm, :] = _dot(xn, w_part(1)) * _dot(xn, w_part(2))
    cw = cw_ref[...]
    u = cw[0:1, :] * ext_ref[pl.ds(HALO - (SC_WIDTH - 1), tm), :]
    for kk in range(1, SC_WIDTH):
        u = u + cw[kk:kk + 1, :] * ext_ref[pl.ds(HALO - (SC_WIDTH - 1) + kk, tm), :]
    ext_ref[0:HALO, :] = ext_ref[tm:tm + HALO, :]
    gated = (_dot(xn, w_part(0)) * u).astype(BF16)
    o_ref[...] = x + _dot(gated, wo_ref[...].astype(BF16))


def _short_conv_layer(x, b, l, norm_g, w_in, conv_w, w_out, layer):
    t, d = x.shape
    tm = SC_TM
    nl = l // tm
    assert l % tm == 0
    return pl.pallas_call(
        _short_conv_kernel,
        grid=(b, nl),
        in_specs=[
            pl.BlockSpec((tm, d), lambda bi, li: (bi * nl + li, 0)),
            _resident((1, d)),
            _resident((None, d, 3 * d), (layer, 0, 0)),
            _resident((None, SC_WIDTH, d), (layer, 0, 0)),
            _resident((None, d, d), (layer, 0, 0)),
        ],
        out_specs=pl.BlockSpec((tm, d), lambda bi, li: (bi * nl + li, 0)),
        out_shape=jax.ShapeDtypeStruct((t, d), F32),
        scratch_shapes=[pltpu.VMEM((tm + HALO, d), F32)],
        compiler_params=_params(2),
        name="short_conv",
    )(x, norm_g.reshape(1, d), w_in, conv_w, w_out)


def kernel(x, ffn1_norm, ffn1_w_gu, ffn1_w_down, mix_norm, ffn2_norm, ffn2_w_gu, ffn2_w_down,
           sb_w_qkv, sb_w_o, ssd_w_in, ssd_conv_w, ssd_conv_b, ssd_dt_bias, ssd_a_log, ssd_d,
           ssd_norm, ssd_w_out, sc_w_in, sc_conv_w, sc_w_out, final_norm):
    b, l, d = x.shape
    depth = ffn1_norm.shape[0]
    h = x.reshape(b * l, d)
    for i in range(depth):
        h = _ffn(h, ffn1_norm[i], ffn1_w_gu, ffn1_w_down, i)
        kind, j = i % N_MIXERS, i // N_MIXERS
        mixer_out = None
        if kind == 0:
            mixer_out = (_sb_mixer(h, b, l, mix_norm[i], sb_w_qkv, j), sb_w_o, j)
        elif kind == 1:
            y = _ssd_mixer(h, b, l, mix_norm[i], ssd_w_in, ssd_conv_w[j], ssd_conv_b[j],
                           ssd_dt_bias[j], ssd_a_log[j], ssd_d[j], ssd_norm[j], j)
            mixer_out = (y, ssd_w_out, j)
        else:
            h = _short_conv_layer(h, b, l, mix_norm[i], sc_w_in, sc_conv_w, sc_w_out, j)
        h = _ffn(h, ffn2_norm[i], ffn2_w_gu, ffn2_w_down, i,
                 final_g=final_norm if i == depth - 1 else None, mixer_out=mixer_out)
    return h.reshape(b, l, d)
```

```python
import functools

import jax
import jax.numpy as jnp
from jax import lax
from jax.experimental import pallas as pl
from jax.experimental.pallas import tpu as pltpu

F32 = jnp.float32
BF16 = jnp.bfloat16

RMS_EPS = 1e-6
LOG2_E = 1.4426950408889634
N_MIXERS = 3
SB_HEAD_DIM = 64
SSD_HEAD_DIM = 64
SSD_GROUPS = 8
SSD_HPG = 4
SSD_STATE = 128
SSD_CONV = 4
SSD_CHUNK = 128
SC_WIDTH = 3

LANES = 128
SUBLANES = 8
VMEM_LIMIT_BYTES = 60 * 1024 * 1024

FFN_TM = 1024
FFN_FUSED_TM = 512
FFN_TF = 256
PROJ_TM = 512
PROJ_TN = 512
ATT_T = 256
ATT_PAIRS_PER_STEP = 4
ATT_PAIRS_TOGETHER = 2
ATT_DEAD_CARRY = 160.0
SSD_GROUPS_PER_STEP = 4
SSD_CHUNKS_PER_STEP = 4
SC_TM = 512
HALO = SUBLANES


def _params(n_axes):
    return pltpu.CompilerParams(
        dimension_semantics=("arbitrary",) * n_axes,
        vmem_limit_bytes=VMEM_LIMIT_BYTES)


def _resident(shape, index=None):
    index = (0,) * len(shape) if index is None else index
    return pl.BlockSpec(shape, lambda *_: index, pipeline_mode=pl.Buffered(1))


def _rms(x, g):
    ms = jnp.mean(x * x, axis=-1, keepdims=True)
    return x * lax.rsqrt(ms + RMS_EPS) * g


def _silu(x):
    h = 0.5 * x
    return h + h * jnp.tanh(h)


def _dot(a, b):
    return jnp.dot(a, b, preferred_element_type=F32)


def _dot_nt(a, b):
    return lax.dot_general(a, b, (((1,), (1,)), ((), ())), preferred_element_type=F32)


def _dot_tn(a, b):
    return lax.dot_general(a, b, (((0,), (0,)), ((), ())), preferred_element_type=F32)


def _ffn_kernel(x_ref, g_ref, wgu_ref, wd_ref, *refs, n_chunks, tf, final_norm, mixer_out):
    refs = list(refs)
    y_ref, wo_ref = (refs.pop(0), refs.pop(0)) if mixer_out else (None, None)
    fg_ref = refs[0] if final_norm else None
    o_ref, xn_ref, acc_ref = refs[-3:]
    x = x_ref[...]
    if mixer_out:
        x = x + _dot(y_ref[...], wo_ref[...].astype(BF16))
    d_ff = n_chunks * tf
    xn_ref[...] = _rms(x, g_ref[...]).astype(BF16)
    for c in range(n_chunks):
        gate = _dot(xn_ref[...], wgu_ref[:, c * tf:(c + 1) * tf].astype(BF16))
        up = _dot(xn_ref[...], wgu_ref[:, d_ff + c * tf:d_ff + (c + 1) * tf].astype(BF16))
        h = (_silu(gate) * up).astype(BF16)
        part = _dot(h, wd_ref[c * tf:(c + 1) * tf, :].astype(BF16))
        if c == 0:
            acc_ref[...] = part
        else:
            acc_ref[...] += part
    y = x + 0.5 * acc_ref[...]
    if final_norm:
        y = _rms(y, fg_ref[...])
    o_ref[...] = y


def _ffn(x, norm_g, w_gu, w_down, layer, final_g=None, mixer_out=None):
    t, d = x.shape
    d_ff = w_down.shape[1]
    tm, tf = (FFN_TM if mixer_out is None else FFN_FUSED_TM), FFN_TF
    n_chunks = d_ff // tf
    assert t % tm == 0 and d_ff % tf == 0
    final_norm = final_g is not None
    extra = [final_g.reshape(1, d)] if final_norm else []
    pre, pre_specs = [], []
    if mixer_out is not None:
        y, w_o, j = mixer_out
        k = y.shape[1]
        pre = [y, w_o]
        pre_specs = [pl.BlockSpec((tm, k), lambda i: (i, 0)), _resident((None, k, d), (j, 0, 0))]
    return pl.pallas_call(
        functools.partial(_ffn_kernel, n_chunks=n_chunks, tf=tf, final_norm=final_norm,
                          mixer_out=mixer_out is not None),
        grid=(t // tm,),
        in_specs=[
            pl.BlockSpec((tm, d), lambda i: (i, 0)),
            _resident((1, d)),
            _resident((None, d, 2 * d_ff), (layer, 0, 0)),
            _resident((None, d_ff, d), (layer, 0, 0)),
        ] + pre_specs + [_resident((1, d))] * len(extra),
        out_specs=pl.BlockSpec((tm, d), lambda i: (i, 0)),
        out_shape=jax.ShapeDtypeStruct((t, d), F32),
        scratch_shapes=[pltpu.VMEM((tm, d), BF16), pltpu.VMEM((tm, d), F32)],
        compiler_params=_params(1),
        name="ffn",
    )(x, norm_g.reshape(1, d), w_gu, w_down, *pre, *extra)


def _norm_proj_kernel(x_ref, g_ref, w_ref, *o_refs, segments, tn):
    xn = _rms(x_ref[...], g_ref[...]).astype(BF16)
    for o_ref, (start, width, scale) in zip(o_refs, segments):
        for n0 in range(0, width, tn):
            n1 = min(n0 + tn, width)
            w = w_ref[:, start + n0:start + n1]
            if scale != 1.0:
                w = w * scale
            o_ref[:, n0:n1] = _dot(xn, w.astype(BF16)).astype(o_ref.dtype)


def _norm_proj(x, norm_g, w, layer, segments, out_dtypes):
    t, d = x.shape
    tm = PROJ_TM
    assert t % tm == 0 and all(s[0] % LANES == 0 for s in segments)
    return pl.pallas_call(
        functools.partial(_norm_proj_kernel, segments=tuple(segments), tn=PROJ_TN),
        grid=(t // tm,),
        in_specs=[pl.BlockSpec((tm, d), lambda i: (i, 0)), _resident((1, d)),
                  _resident((None,) + w.shape[1:], (layer, 0, 0))],
        out_specs=[pl.BlockSpec((tm, s[1]), lambda i: (i, 0)) for s in segments],
        out_shape=[jax.ShapeDtypeStruct((t, s[1]), dt) for s, dt in zip(segments, out_dtypes)],
        compiler_params=_params(1),
        name="norm_proj",
    )(x, norm_g.reshape(1, d), w)


def _sb_attn_kernel(q_ref, k_ref, v_ref, o_ref, lb_a, sp_a, rs_a, lb_b, sp_b, rs_b, acc_ref, c_ref,
                    *, t, n_together):
    qi = pl.program_id(2)
    hd = SB_HEAD_DIM
    n_p = n_together
    sub = 2 * n_p * t
    buf_a, buf_b = (lb_a, sp_a, rs_a), (lb_b, sp_b, rs_b)
    both, first, second = (0, 2), (0, 1), (1, 2)
    in_a = lax.broadcasted_iota(jnp.int32, (1, 2 * hd), 1) < hd
    row = lax.broadcasted_iota(jnp.int32, (t, t), 0)
    col = lax.broadcasted_iota(jnp.int32, (t, t), 1)
    strict_lower = jnp.concatenate([col < row] * (2 * n_p), axis=0)
    tri = jnp.where(row > col, 1.0, 0.0).astype(BF16)
    sign_bit = jnp.uint32(0x80000000)
    top = 2 * qi + 1

    def mask_diagonal(x, fill):
        head = jnp.where(strict_lower, x[:sub], fill)
        return head if x.shape[0] == sub else jnp.concatenate([head, x[sub:]], axis=0)

    def pair_group(grp, _):
        lanes = [pl.ds(pl.multiple_of((grp * n_p + p) * 2 * hd, 2 * hd), 2 * hd) for p in range(n_p)]
        q_parts = []
        for s in range(2):
            for p in range(n_p):
                q2 = q_ref[0, s * t:(s + 1) * t, lanes[p]]
                q_parts += [jnp.where(in_a, q2, 0), jnp.where(in_a, 0, q2)]
        qs = jnp.concatenate(q_parts, axis=0)

        def scores(j, buf, subs, diag):
            lb_ref, sp_ref, rs_ref = buf
            r0, r1 = subs[0] * sub, subs[1] * sub
            rows_k = pl.ds(pl.multiple_of(j * t, t), t)
            k2 = [k_ref[0, rows_k, lanes[p]] for p in range(n_p)]
            z = jnp.concatenate(
                [_dot_nt(qs[(s * n_p + p) * 2 * t:(s * n_p + p + 1) * 2 * t], k2[p])
                 for s in range(*subs) for p in range(n_p)], axis=0)
            neg_abs = lax.bitcast_convert_type(lax.bitcast_convert_type(z, jnp.uint32) | sign_bit, F32)
            lb = jnp.minimum(z, 0.0) - jnp.log2(1.0 + jnp.exp2(neg_abs))
            sp = z - lb
            if diag:
                sp, lb = mask_diagonal(sp, 0.0), mask_diagonal(lb, -jnp.inf)
            lb_ref[r0:r1, :] = lb
            sp_ref[r0:r1, :] = sp.astype(BF16)
            rs_ref[r0:r1, :] = jnp.broadcast_to(jnp.sum(sp, axis=-1, keepdims=True), (r1 - r0, LANES))

        def values(j, buf, subs):
            lb_ref, sp_ref, rs_ref = buf
            r0, r1 = subs[0] * sub, subs[1] * sub
            rows_k = pl.ds(pl.multiple_of(j * t, t), t)
            c = c_ref[r0:r1, :]
            tail = _dot(sp_ref[r0:r1, :], tri)
            c_wide = jnp.concatenate([c] * (t // LANES), axis=1)
            att = jnp.exp2(lb_ref[r0:r1, :] - tail - c_wide).astype(BF16)
            c_ref[r0:r1, :] = c + rs_ref[r0:r1, :]
            for p in range(n_p):
                blocks = []
                for s in range(subs[1] - subs[0]):
                    base = (s * n_p + p) * 2 * t
                    blocks.append(jnp.concatenate([att[base:base + t], att[base + t:base + 2 * t]], axis=1))
                v2 = v_ref[0, rows_k, lanes[p]]
                vv = jnp.concatenate([jnp.where(in_a, v2, 0), jnp.where(in_a, 0, v2)], axis=0)
                acc_ref[p, subs[0] * t:subs[1] * t, :] += _dot(jnp.concatenate(blocks, axis=0), vv)

        def tile_pair(state):
            i = state[0]
            j = top - 3 - 2 * i
            values(j, buf_b, both)
            scores(j - 1, buf_a, both, False)
            values(j - 1, buf_a, both)
            scores(j - 2, buf_b, both, False)
            return i + 1, jnp.min(c_ref[...])

        def live(state):
            return (state[0] < qi - 1) & (state[1] < ATT_DEAD_CARRY)

        acc_ref[...] = jnp.zeros_like(acc_ref)
        c_ref[...] = jnp.zeros_like(c_ref)

        @pl.when(qi == 0)
        def _():
            scores(top, buf_a, second, True)
            values(top, buf_a, second)
            scores(top - 1, buf_b, both, True)
            values(top - 1, buf_b, both)

        @pl.when(qi > 0)
        def _():
            scores(top, buf_a, second, True)
            scores(top - 1, buf_b, both, True)
            values(top, buf_a, second)
            scores(top - 2, buf_a, first, False)
            values(top - 1, buf_b, both)
            c_min0 = jnp.minimum(jnp.min(c_ref[sub:, :]), jnp.min(c_ref[:sub, :] + buf_a[2][:sub, :]))
            values(top - 2, buf_a, first)

            @pl.when(c_min0 < ATT_DEAD_CARRY)
            def _():
                scores(top - 2, buf_a, second, False)
                values(top - 2, buf_a, second)
                scores(top - 3, buf_b, both, False)
                n_pairs, c_min = lax.while_loop(live, tile_pair, (jnp.int32(0), c_min0))

                @pl.when(c_min < ATT_DEAD_CARRY)
                def _():
                    values(top - 3 - 2 * n_pairs, buf_b, both)

        for p in range(n_p):
            o_ref[0, :, lanes[p]] = acc_ref[p].astype(o_ref.dtype)
        return 0

    lax.fori_loop(0, q_ref.shape[2] // (2 * hd * n_p), pair_group, 0)


def _sb_attention(q, k, v):
    b, l, d = q.shape
    t, n_p = ATT_T, ATT_PAIRS_TOGETHER
    tq = 2 * t
    hp = 2 * SB_HEAD_DIM * ATT_PAIRS_PER_STEP
    assert l % tq == 0 and d % hp == 0 and 2 * SB_HEAD_DIM == LANES and ATT_PAIRS_PER_STEP % n_p == 0
    rows = 4 * n_p * t
    stage = [pltpu.VMEM((rows, t), F32), pltpu.VMEM((rows, t), BF16), pltpu.VMEM((rows, LANES), F32)]
    return pl.pallas_call(
        functools.partial(_sb_attn_kernel, t=t, n_together=n_p),
        grid=(b, d // hp, l // tq),
        in_specs=[
            pl.BlockSpec((1, tq, hp), lambda bi, hi, qi: (bi, qi, hi)),
            pl.BlockSpec((1, l, hp), lambda bi, hi, qi: (bi, 0, hi)),
            pl.BlockSpec((1, l, hp), lambda bi, hi, qi: (bi, 0, hi)),
        ],
        out_specs=pl.BlockSpec((1, tq, hp), lambda bi, hi, qi: (bi, qi, hi)),
        out_shape=jax.ShapeDtypeStruct((b, l, d), BF16),
        scratch_shapes=stage + stage + [pltpu.VMEM((n_p, tq, LANES), F32),
                                        pltpu.VMEM((rows, LANES), F32)],
        compiler_params=_params(3),
        name="sb_attn",
    )(q, k, v)


def _sb_mixer(x, b, l, norm_g, w_qkv, layer):
    t, d = x.shape
    scale = LOG2_E * SB_HEAD_DIM ** -0.5
    q, k, v = _norm_proj(x, norm_g, w_qkv, layer,
                         [(0, d, scale), (d, d, 1.0), (2 * d, d, 1.0)], [BF16] * 3)
    o = _sb_attention(q.reshape(b, l, d), k.reshape(b, l, d), v.reshape(b, l, d))
    return o.reshape(t, d)


def _expand_heads(m, width):
    rows = m.shape[0]
    lane = lax.broadcasted_iota(jnp.int32, (rows, SSD_HPG * width), 1)
    out = jnp.broadcast_to(m[:, SSD_HPG - 1:SSD_HPG], (rows, SSD_HPG * width))
    for r in range(SSD_HPG - 2, -1, -1):
        out = jnp.where(lane < (r + 1) * width, m[:, r:r + 1], out)
    return out


def _to_columns(rows):
    r, n = rows.shape
    padded = jnp.concatenate([rows, jnp.zeros((n - r, n), F32)], axis=0)
    return padded.T[:, :r]


def _ssd_kernel(xr_ref, br_ref, cr_ref, z_ref, dt_ref, dtn_ref, wx_ref, wb_ref, wc_ref, bx_ref,
                bb_ref, bc_ref, dtb_ref, alog_ref, dsk_ref, ng_ref, o_ref, h_ref, hist_ref,
                steps_row_ref, steps_col_ref, *, n_groups, n_chunks):
    cl, hp, n = SSD_CHUNK, SSD_HPG * SSD_HEAD_DIM, SSD_STATE
    rows = n_groups * SUBLANES
    row = lax.broadcasted_iota(jnp.int32, (cl, cl), 0)
    col = lax.broadcasted_iota(jnp.int32, (cl, cl), 1)
    causal = col <= row
    lane_hp = lax.broadcasted_iota(jnp.int32, (cl, hp), 1)

    def stage_step_sizes(dt_blk):
        dt_in = (dt_blk + dtb_ref[...]).reshape(rows, cl)
        dt_all = jnp.maximum(dt_in, 0.0) + jnp.log(1.0 + jnp.exp(-jnp.abs(dt_in)))
        a_all = dt_all * (-jnp.exp(alog_ref[...].reshape(rows, 1)))
        lane_cl = lax.broadcasted_iota(jnp.int32, (rows, cl), 1)
        sh = 1
        while sh < cl:
            a_all = a_all + jnp.where(lane_cl >= sh, pltpu.roll(a_all, sh, axis=1), 0.0)
            sh *= 2
        te_all = jnp.exp(a_all[:, cl - 1:cl] - a_all) * dt_all
        steps_row_ref[0], steps_row_ref[1] = dt_all, a_all
        steps_col_ref[0], steps_col_ref[1] = _to_columns(a_all), _to_columns(te_all)

    def dt_chunk(ref, ci):
        return ref[0, :, :, pl.ds(pl.multiple_of(ci * cl, cl), cl)]

    def chunk(ci, _):
        tok = pl.ds(pl.multiple_of(ci * cl, cl), cl)

        @pl.when((pl.program_id(2) == 0) & (ci == 0))
        def _():
            h_ref[...] = jnp.zeros_like(h_ref)
            hist_ref[...] = jnp.zeros_like(hist_ref)
            stage_step_sizes(dt_chunk(dt_ref, 0))

        dt_all, a_all = steps_row_ref[0], steps_row_ref[1]
        a_col_all, te_col_all = steps_col_ref[0], steps_col_ref[1]
        dt_next = jnp.where(ci < n_chunks - 1, dt_chunk(dt_ref, jnp.minimum(ci + 1, n_chunks - 1)),
                            dt_chunk(dtn_ref, 0))
        stage_step_sizes(dt_next)

        raw = jnp.concatenate([xr_ref[tok, :], br_ref[tok, :], cr_ref[tok, :]], axis=1)
        ext = jnp.concatenate([hist_ref[...], raw], axis=0)
        hist_ref[...] = raw[cl - HALO:, :]
        cw = jnp.concatenate([wx_ref[...], wb_ref[...], wc_ref[...]], axis=1)
        acc = cw[0:1, :] * ext
        for kk in range(1, SSD_CONV):
            acc = cw[kk:kk + 1, :] * ext + pltpu.roll(acc, 1, axis=0)
        bias = jnp.concatenate([bx_ref[...], bb_ref[...], bc_ref[...]], axis=1)
        act = _silu(acc[HALO:, :] + bias)
        xs_all = act[:, :n_groups * hp]
        bm_all = act[:, n_groups * hp:n_groups * (hp + n)].astype(BF16)
        cm_all = act[:, n_groups * (hp + n):].astype(BF16)

        for g in range(n_groups):
            xs = xs_all[:, g * hp:(g + 1) * hp]
            bm = bm_all[:, g * n:(g + 1) * n]
            cm = cm_all[:, g * n:(g + 1) * n]
            dt = dt_all[g * SUBLANES:(g + 1) * SUBLANES]
            a_cum = a_all[g * SUBLANES:(g + 1) * SUBLANES]
            a_col = a_col_all[:, g * SUBLANES:(g + 1) * SUBLANES]
            te_col = te_col_all[:, g * SUBLANES:(g + 1) * SUBLANES]

            cb = _dot_nt(cm, bm)
            ws, xblocks = [], []
            for r in range(SSD_HPG):
                seg = a_col[:, r:r + 1] - a_cum[r:r + 1, :]
                decay = jnp.exp(jnp.where(causal, seg, -jnp.inf))
                ws.append((cb * decay * dt[r:r + 1, :]).astype(BF16))
                in_head = (lane_hp >= r * SSD_HEAD_DIM) & (lane_hp < (r + 1) * SSD_HEAD_DIM)
                xblocks.append(jnp.where(in_head, xs, 0.0).astype(BF16))
            y = _dot(jnp.concatenate(ws, axis=1), jnp.concatenate(xblocks, axis=0))

            h_prev = h_ref[g]
            y = y + _dot(cm, h_prev.astype(BF16)) * _expand_heads(jnp.exp(a_col), SSD_HEAD_DIM)
            xw = (xs * _expand_heads(te_col, SSD_HEAD_DIM)).astype(BF16)
            h_decay = _expand_heads(jnp.exp(a_col[cl - 1:cl, :]), SSD_HEAD_DIM)
            h_ref[g] = h_prev * h_decay + _dot_tn(bm, xw)

            y = (y + dsk_ref[g] * xs) * _silu(z_ref[tok, g * hp:(g + 1) * hp])
            o_ref[tok, g * hp:(g + 1) * hp] = _rms(y, ng_ref[g]).astype(o_ref.dtype)
        return 0

    lax.fori_loop(0, n_chunks, chunk, 0)


def _ssd_core(xbc, z, dt_t, conv_w, conv_b, dt_bias, a_log, d_skip, norm_g, b, l):
    t = xbc.shape[0]
    g, hpg, cl, n, ng = SSD_GROUPS, SSD_HPG, SSD_CHUNK, SSD_STATE, SSD_GROUPS_PER_STEP
    hp = hpg * SSD_HEAD_DIM
    d_inner = g * hp
    kc = SSD_CHUNKS_PER_STEP
    tl = kc * cl
    nc = l // tl
    assert l % tl == 0 and hp % LANES == 0 and n == LANES and g % ng == 0
    bb, cb = d_inner // (ng * n), (d_inner + g * n) // (ng * n)
    pad = lambda p: jnp.pad(p.reshape(g, hpg, 1), ((0, 0), (0, SUBLANES - hpg), (0, 0)))
    per_lane = lambda p: jnp.repeat(p.reshape(g, hpg), SSD_HEAD_DIM, axis=1).reshape(g, 1, hp)
    conv_b = conv_b.reshape(1, -1)
    tok = lambda bi, gi, ci: bi * nc + ci
    return pl.pallas_call(
        functools.partial(_ssd_kernel, n_groups=ng, n_chunks=kc),
        grid=(b, g // ng, nc),
        in_specs=[
            pl.BlockSpec((tl, ng * hp), lambda bi, gi, ci: (tok(bi, gi, ci), gi)),
            pl.BlockSpec((tl, ng * n), lambda bi, gi, ci: (tok(bi, gi, ci), bb + gi)),
            pl.BlockSpec((tl, ng * n), lambda bi, gi, ci: (tok(bi, gi, ci), cb + gi)),
            pl.BlockSpec((tl, ng * hp), lambda bi, gi, ci: (tok(bi, gi, ci), gi)),
            pl.BlockSpec((1, ng, SUBLANES, tl), lambda bi, gi, ci: (bi, gi, 0, ci)),
            pl.BlockSpec((1, ng, SUBLANES, tl), lambda bi, gi, ci: (bi, gi, 0, jnp.minimum(ci + 1, nc - 1))),
            pl.BlockSpec((SSD_CONV, ng * hp), lambda bi, gi, ci: (0, gi)),
            pl.BlockSpec((SSD_CONV, ng * n), lambda bi, gi, ci: (0, bb + gi)),
            pl.BlockSpec((SSD_CONV, ng * n), lambda bi, gi, ci: (0, cb + gi)),
            pl.BlockSpec((1, ng * hp), lambda bi, gi, ci: (0, gi)),
            pl.BlockSpec((1, ng * n), lambda bi, gi, ci: (0, bb + gi)),
            pl.BlockSpec((1, ng * n), lambda bi, gi, ci: (0, cb + gi)),
            pl.BlockSpec((ng, SUBLANES, 1), lambda bi, gi, ci: (gi, 0, 0)),
            pl.BlockSpec((ng, SUBLANES, 1), lambda bi, gi, ci: (gi, 0, 0)),
            pl.BlockSpec((ng, 1, hp), lambda bi, gi, ci: (gi, 0, 0)),
            pl.BlockSpec((ng, 1, hp), lambda bi, gi, ci: (gi, 0, 0)),
        ],
        out_specs=pl.BlockSpec((tl, ng * hp), lambda bi, gi, ci: (tok(bi, gi, ci), gi)),
        out_shape=jax.ShapeDtypeStruct((t, d_inner), BF16),
        scratch_shapes=[pltpu.VMEM((ng, n, hp), F32), pltpu.VMEM((HALO, ng * (hp + 2 * n)), F32),
                        pltpu.VMEM((2, ng * SUBLANES, cl), F32), pltpu.VMEM((2, cl, ng * SUBLANES), F32)],
        compiler_params=_params(3),
        name="ssd_core",
    )(xbc, xbc, xbc, z, dt_t, dt_t, conv_w, conv_w, conv_w, conv_b, conv_b, conv_b,
      pad(dt_bias), pad(a_log), per_lane(d_skip), norm_g.reshape(g, 1, hp))


def _ssd_mixer(x, b, l, norm_g, w_in, conv_w, conv_b, dt_bias, a_log, d_skip, ssd_norm, layer):
    g, hpg = SSD_GROUPS, SSD_HPG
    d_inner = ssd_norm.shape[0]
    conv_dim = conv_w.shape[1]
    n_heads = g * hpg
    z, xbc, dt = _norm_proj(
        x, norm_g, w_in, layer,
        [(0, d_inner, 1.0), (d_inner, conv_dim, 1.0), (d_inner + conv_dim, n_heads, 1.0)], [F32] * 3)
    dt_t = dt.reshape(b, l, g, hpg).transpose(0, 2, 3, 1)
    dt_t = jnp.pad(dt_t, ((0, 0), (0, 0), (0, SUBLANES - hpg), (0, 0)))
    return _ssd_core(xbc, z, dt_t, conv_w, conv_b, dt_bias, a_log, d_skip, ssd_norm, b, l)


def _short_conv_kernel(x_ref, g_ref, wi_ref, cw_ref, wo_ref, o_ref, ext_ref):
    tm, d = x_ref.shape

    @pl.when(pl.program_id(1) == 0)
    def _():
        ext_ref[0:HALO, :] = jnp.zeros((HALO, ext_ref.shape[1]), F32)

    x = x_ref[...]
    xn = _rms(x, g_ref[...]).astype(BF16)
    w_part = lambda i: wi_ref[:, i * d:(i + 1) * d].astype(BF16)
    ext_ref[HALO:HALO + tm, :] = _dot(xn, w_part(1)) * _dot(xn, w_part(2))
    cw = cw_ref[...]
    u = cw[0:1, :] * ext_ref[pl.ds(HALO - (SC_WIDTH - 1), tm), :]
    for kk in range(1, SC_WIDTH):
        u = u + cw[kk:kk + 1, :] * ext_ref[pl.ds(HALO - (SC_WIDTH - 1) + kk, tm), :]
    ext_ref[0:HALO, :] = ext_ref[tm:tm + HALO, :]
    gated = (_dot(xn, w_part(0)) * u).astype(BF16)
    o_ref[...] = x + _dot(gated, wo_ref[...].astype(BF16))


def _short_conv_layer(x, b, l, norm_g, w_in, conv_w, w_out, layer):
    t, d = x.shape
    tm = SC_TM
    nl = l // tm
    assert l % tm == 0
    return pl.pallas_call(
        _short_conv_kernel,
        grid=(b, nl),
        in_specs=[
            pl.BlockSpec((tm, d), lambda bi, li: (bi * nl + li, 0)),
            _resident((1, d)),
            _resident((None, d, 3 * d), (layer, 0, 0)),
            _resident((None, SC_WIDTH, d), (layer, 0, 0)),
            _resident((None, d, d), (layer, 0, 0)),
        ],
        out_specs=pl.BlockSpec((tm, d), lambda bi, li: (bi * nl + li, 0)),
        out_shape=jax.ShapeDtypeStruct((t, d), F32),
        scratch_shapes=[pltpu.VMEM((tm + HALO, d), F32)],
        compiler_params=_params(2),
        name="short_conv",
    )(x, norm_g.reshape(1, d), w_in, conv_w, w_out)


def kernel(x, ffn1_norm, ffn1_w_gu, ffn1_w_down, mix_norm, ffn2_norm, ffn2_w_gu, ffn2_w_down,
           sb_w_qkv, sb_w_o, ssd_w_in, ssd_conv_w, ssd_conv_b, ssd_dt_bias, ssd_a_log, ssd_d,
           ssd_norm, ssd_w_out, sc_w_in, sc_conv_w, sc_w_out, final_norm):
    b, l, d = x.shape
    depth = ffn1_norm.shape[0]
    h = x.reshape(b * l, d)
    for i in range(depth):
        h = _ffn(h, ffn1_norm[i], ffn1_w_gu, ffn1_w_down, i)
        kind, j = i % N_MIXERS, i // N_MIXERS
        mixer_out = None
        if kind == 0:
            mixer_out = (_sb_mixer(h, b, l, mix_norm[i], sb_w_qkv, j), sb_w_o, j)
        elif kind == 1:
            y = _ssd_mixer(h, b, l, mix_norm[i], ssd_w_in, ssd_conv_w[j], ssd_conv_b[j],
                           ssd_dt_bias[j], ssd_a_log[j], ssd_d[j], ssd_norm[j], j)
            mixer_out = (y, ssd_w_out, j)
        else:
            h = _short_conv_layer(h, b, l, mix_norm[i], sc_w_in, sc_conv_w, sc_w_out, j)
        h = _ffn(h, ffn2_norm[i], ffn2_w_gu, ffn2_w_down, i,
                 final_g=final_norm if i == depth - 1 else None, mixer_out=mixer_out)
    return h.reshape(b, l, d)
```

```python
import functools

import jax
import jax.numpy as jnp
from jax import lax
from jax.experimental import pallas as pl
from jax.experimental.pallas import tpu as pltpu

F32 = jnp.float32
BF16 = jnp.bfloat16

RMS_EPS = 1e-6
LOG2_E = 1.4426950408889634
N_MIXERS = 3
SB_HEAD_DIM = 64
SSD_HEAD_DIM = 64
SSD_GROUPS = 8
SSD_HPG = 4
SSD_STATE = 128
SSD_CONV = 4
SSD_CHUNK = 128
SC_WIDTH = 3

LANES = 128
SUBLANES = 8
VMEM_LIMIT_BYTES = 60 * 1024 * 1024

FFN_TM = 1024
FFN_FUSED_TM = 512
FFN_TF = 256
PROJ_TM = 512
PROJ_TN = 512
ATT_T = 256
ATT_PAIRS_PER_STEP = 4
ATT_PAIRS_TOGETHER = 2
ATT_DEAD_CARRY = 160.0
SSD_GROUPS_PER_STEP = 4
SSD_CHUNKS_PER_STEP = 4
SC_TM = 512
HALO = SUBLANES


def _params(n_axes):
    return pltpu.CompilerParams(
        dimension_semantics=("arbitrary",) * n_axes,
        vmem_limit_bytes=VMEM_LIMIT_BYTES)


def _resident(shape, index=None):
    index = (0,) * len(shape) if index is None else index
    return pl.BlockSpec(shape, lambda *_: index, pipeline_mode=pl.Buffered(1))


def _rms(x, g):
    ms = jnp.mean(x * x, axis=-1, keepdims=True)
    return x * lax.rsqrt(ms + RMS_EPS) * g


def _silu(x):
    h = 0.5 * x
    return h + h * jnp.tanh(h)


def _dot(a, b):
    return jnp.dot(a, b, preferred_element_type=F32)


def _dot_nt(a, b):
    return lax.dot_general(a, b, (((1,), (1,)), ((), ())), preferred_element_type=F32)


def _dot_tn(a, b):
    return lax.dot_general(a, b, (((0,), (0,)), ((), ())), preferred_element_type=F32)


def _ffn_kernel(x_ref, g_ref, wgu_ref, wd_ref, *refs, n_chunks, tf, final_norm, mixer_out):
    refs = list(refs)
    y_ref, wo_ref = (refs.pop(0), refs.pop(0)) if mixer_out else (None, None)
    fg_ref = refs[0] if final_norm else None
    o_ref, xn_ref, acc_ref = refs[-3:]
    x = x_ref[...]
    if mixer_out:
        x = x + _dot(y_ref[...], wo_ref[...].astype(BF16))
    d_ff = n_chunks * tf
    xn_ref[...] = _rms(x, g_ref[...]).astype(BF16)
    for c in range(n_chunks):
        gate = _dot(xn_ref[...], wgu_ref[:, c * tf:(c + 1) * tf].astype(BF16))
        up = _dot(xn_ref[...], wgu_ref[:, d_ff + c * tf:d_ff + (c + 1) * tf].astype(BF16))
        h = (_silu(gate) * up).astype(BF16)
        part = _dot(h, wd_ref[c * tf:(c + 1) * tf, :].astype(BF16))
        if c == 0:
            acc_ref[...] = part
        else:
            acc_ref[...] += part
    y = x + 0.5 * acc_ref[...]
    if final_norm:
        y = _rms(y, fg_ref[...])
    o_ref[...] = y


def _ffn(x, norm_g, w_gu, w_down, layer, final_g=None, mixer_out=None):
    t, d = x.shape
    d_ff = w_down.shape[1]
    tm, tf = (FFN_TM if mixer_out is None else FFN_FUSED_TM), FFN_TF
    n_chunks = d_ff // tf
    assert t % tm == 0 and d_ff % tf == 0
    final_norm = final_g is not None
    extra = [final_g.reshape(1, d)] if final_norm else []
    pre, pre_specs = [], []
    if mixer_out is not None:
        y, w_o, j = mixer_out
        k = y.shape[1]
        pre = [y, w_o]
        pre_specs = [pl.BlockSpec((tm, k), lambda i: (i, 0)), _resident((None, k, d), (j, 0, 0))]
    return pl.pallas_call(
        functools.partial(_ffn_kernel, n_chunks=n_chunks, tf=tf, final_norm=final_norm,
                          mixer_out=mixer_out is not None),
        grid=(t // tm,),
        in_specs=[
            pl.BlockSpec((tm, d), lambda i: (i, 0)),
            _resident((1, d)),
            _resident((None, d, 2 * d_ff), (layer, 0, 0)),
            _resident((None, d_ff, d), (layer, 0, 0)),
        ] + pre_specs + [_resident((1, d))] * len(extra),
        out_specs=pl.BlockSpec((tm, d), lambda i: (i, 0)),
        out_shape=jax.ShapeDtypeStruct((t, d), F32),
        scratch_shapes=[pltpu.VMEM((tm, d), BF16), pltpu.VMEM((tm, d), F32)],
        compiler_params=_params(1),
        name="ffn",
    )(x, norm_g.reshape(1, d), w_gu, w_down, *pre, *extra)


def _norm_proj_kernel(x_ref, g_ref, w_ref, *o_refs, segments, tn):
    xn = _rms(x_ref[...], g_ref[...]).astype(BF16)
    for o_ref, (start, width, scale) in zip(o_refs, segments):
        for n0 in range(0, width, tn):
            n1 = min(n0 + tn, width)
            w = w_ref[:, start + n0:start + n1]
            if scale != 1.0:
                w = w * scale
            o_ref[:, n0:n1] = _dot(xn, w.astype(BF16)).astype(o_ref.dtype)


def _norm_proj(x, norm_g, w, layer, segments, out_dtypes):
    t, d = x.shape
    tm = PROJ_TM
    assert t % tm == 0 and all(s[0] % LANES == 0 for s in segments)
    return pl.pallas_call(
        functools.partial(_norm_proj_kernel, segments=tuple(segments), tn=PROJ_TN),
        grid=(t // tm,),
        in_specs=[pl.BlockSpec((tm, d), lambda i: (i, 0)), _resident((1, d)),
                  _resident((None,) + w.shape[1:], (layer, 0, 0))],
        out_specs=[pl.BlockSpec((tm, s[1]), lambda i: (i, 0)) for s in segments],
        out_shape=[jax.ShapeDtypeStruct((t, s[1]), dt) for s, dt in zip(segments, out_dtypes)],
        compiler_params=_params(1),
        name="norm_proj",
    )(x, norm_g.reshape(1, d), w)


def _sb_attn_kernel(q_ref, k_ref, v_ref, o_ref, lb_a, sp_a, rs_a, lb_b, sp_b, rs_b, acc_ref, c_ref,
                    *, t, n_together):
    qi = pl.program_id(2)
    hd = SB_HEAD_DIM
    n_p = n_together
    sub = 2 * n_p * t
    buf_a, buf_b = (lb_a, sp_a, rs_a), (lb_b, sp_b, rs_b)
    both, first, second = (0, 2), (0, 1), (1, 2)
    in_a = lax.broadcasted_iota(jnp.int32, (1, 2 * hd), 1) < hd
    row = lax.broadcasted_iota(jnp.int32, (t, t), 0)
    col = lax.broadcasted_iota(jnp.int32, (t, t), 1)
    strict_lower = jnp.concatenate([col < row] * (2 * n_p), axis=0)
    tri = jnp.where(row > col, 1.0, 0.0).astype(BF16)
    sign_bit = jnp.uint32(0x80000000)
    top = 2 * qi + 1

    def mask_diagonal(x, fill):
        head = jnp.where(strict_lower, x[:sub], fill)
        return head if x.shape[0] == sub else jnp.concatenate([head, x[sub:]], axis=0)

    def pair_group(grp, _):
        lanes = [pl.ds(pl.multiple_of((grp * n_p + p) * 2 * hd, 2 * hd), 2 * hd) for p in range(n_p)]
        q_parts = []
        for s in range(2):
            for p in range(n_p):
                q2 = q_ref[0, s * t:(s + 1) * t, lanes[p]]
                q_parts += [jnp.where(in_a, q2, 0), jnp.where(in_a, 0, q2)]
        qs = jnp.concatenate(q_parts, axis=0)

        def scores(j, buf, subs, diag):
            lb_ref, sp_ref, rs_ref = buf
            r0, r1 = subs[0] * sub, subs[1] * sub
            rows_k = pl.ds(pl.multiple_of(j * t, t), t)
            k2 = [k_ref[0, rows_k, lanes[p]] for p in range(n_p)]
            z = jnp.concatenate(
                [_dot_nt(qs[(s * n_p + p) * 2 * t:(s * n_p + p + 1) * 2 * t], k2[p])
                 for s in range(*subs) for p in range(n_p)], axis=0)
            neg_abs = lax.bitcast_convert_type(lax.bitcast_convert_type(z, jnp.uint32) | sign_bit, F32)
            lb = jnp.minimum(z, 0.0) - jnp.log2(1.0 + jnp.exp2(neg_abs))
            sp = z - lb
            if diag:
                sp, lb = mask_diagonal(sp, 0.0), mask_diagonal(lb, -jnp.inf)
            lb_ref[r0:r1, :] = lb
            sp_ref[r0:r1, :] = sp.astype(BF16)
            rs_ref[r0:r1, :] = jnp.broadcast_to(jnp.sum(sp, axis=-1, keepdims=True), (r1 - r0, LANES))

        def values(j, buf, subs):
            lb_ref, sp_ref, rs_ref = buf
            r0, r1 = subs[0] * sub, subs[1] * sub
            rows_k = pl.ds(pl.multiple_of(j * t, t), t)
            c = c_ref[r0:r1, :]
            tail = _dot(sp_ref[r0:r1, :], tri)
            c_wide = jnp.concatenate([c] * (t // LANES), axis=1)
            att = jnp.exp2(lb_ref[r0:r1, :] - tail - c_wide).astype(BF16)
            c_ref[r0:r1, :] = c + rs_ref[r0:r1, :]
            for p in range(n_p):
                blocks = []
                for s in range(subs[1] - subs[0]):
                    base = (s * n_p + p) * 2 * t
                    blocks.append(jnp.concatenate([att[base:base + t], att[base + t:base + 2 * t]], axis=1))
                v2 = v_ref[0, rows_k, lanes[p]]
                vv = jnp.concatenate([jnp.where(in_a, v2, 0), jnp.where(in_a, 0, v2)], axis=0)
                acc_ref[p, subs[0] * t:subs[1] * t, :] += _dot(jnp.concatenate(blocks, axis=0), vv)

        def tile_pair(state):
            i = state[0]
            j = top - 3 - 2 * i
            values(j, buf_b, both)
            scores(j - 1, buf_a, both, False)
            values(j - 1, buf_a, both)
            scores(j - 2, buf_b, both, False)
            return i + 1, jnp.min(c_ref[...])

        def live(state):
            return (state[0] < qi - 1) & (state[1] < ATT_DEAD_CARRY)

        acc_ref[...] = jnp.zeros_like(acc_ref)
        c_ref[...] = jnp.zeros_like(c_ref)

        @pl.when(qi == 0)
        def _():
            scores(top, buf_a, second, True)
            values(top, buf_a, second)
            scores(top - 1, buf_b, both, True)
            values(top - 1, buf_b, both)

        @pl.when(qi > 0)
        def _():
            scores(top, buf_a, second, True)
            scores(top - 1, buf_b, both, True)
            values(top, buf_a, second)
            scores(top - 2, buf_a, first, False)
            values(top - 1, buf_b, both)
            c_min0 = jnp.minimum(jnp.min(c_ref[sub:, :]), jnp.min(c_ref[:sub, :] + buf_a[2][:sub, :]))
            values(top - 2, buf_a, first)

            @pl.when(c_min0 < ATT_DEAD_CARRY)
            def _():
                scores(top - 2, buf_a, second, False)
                values(top - 2, buf_a, second)
                scores(top - 3, buf_b, both, False)
                n_pairs, c_min = lax.while_loop(live, tile_pair, (jnp.int32(0), c_min0))

                @pl.when(c_min < ATT_DEAD_CARRY)
                def _():
                    values(top - 3 - 2 * n_pairs, buf_b, both)

        for p in range(n_p):
            o_ref[0, :, lanes[p]] = acc_ref[p].astype(o_ref.dtype)
        return 0

    lax.fori_loop(0, q_ref.shape[2] // (2 * hd * n_p), pair_group, 0)


def _sb_attention(q, k, v):
    b, l, d = q.shape
    t, n_p = ATT_T, ATT_PAIRS_TOGETHER
    tq = 2 * t
    hp = 2 * SB_HEAD_DIM * ATT_PAIRS_PER_STEP
    assert l % tq == 0 and d % hp == 0 and 2 * SB_HEAD_DIM == LANES and ATT_PAIRS_PER_STEP % n_p == 0
    rows = 4 * n_p * t
    stage = [pltpu.VMEM((rows, t), F32), pltpu.VMEM((rows, t), BF16), pltpu.VMEM((rows, LANES), F32)]
    return pl.pallas_call(
        functools.partial(_sb_attn_kernel, t=t, n_together=n_p),
        grid=(b, d // hp, l // tq),
        in_specs=[
            pl.BlockSpec((1, tq, hp), lambda bi, hi, qi: (bi, qi, hi)),
            pl.BlockSpec((1, l, hp), lambda bi, hi, qi: (bi, 0, hi)),
            pl.BlockSpec((1, l, hp), lambda bi, hi, qi: (bi, 0, hi)),
        ],
        out_specs=pl.BlockSpec((1, tq, hp), lambda bi, hi, qi: (bi, qi, hi)),
        out_shape=jax.ShapeDtypeStruct((b, l, d), BF16),
        scratch_shapes=stage + stage + [pltpu.VMEM((n_p, tq, LANES), F32),
                                        pltpu.VMEM((rows, LANES), F32)],
        compiler_params=_params(3),
        name="sb_attn",
    )(q, k, v)


def _sb_mixer(x, b, l, norm_g, w_qkv, layer):
    t, d = x.shape
    scale = LOG2_E * SB_HEAD_DIM ** -0.5
    q, k, v = _norm_proj(x, norm_g, w_qkv, layer,
                         [(0, d, scale), (d, d, 1.0), (2 * d, d, 1.0)], [BF16] * 3)
    o = _sb_attention(q.reshape(b, l, d), k.reshape(b, l, d), v.reshape(b, l, d))
    return o.reshape(t, d)


def _split3(x):
    hi = x.astype(BF16).astype(F32)
    rest = x - hi
    mid = rest.astype(BF16).astype(F32)
    return hi, mid, (rest - mid).astype(BF16).astype(F32)


def _head_selector(width):
    k = lax.broadcasted_iota(jnp.int32, (3 * SUBLANES, SSD_HPG * width), 0) % SUBLANES
    lane = lax.broadcasted_iota(jnp.int32, (3 * SUBLANES, SSD_HPG * width), 1)
    return jnp.where((lane >= k * width) & (lane < (k + 1) * width), 1.0, 0.0).astype(BF16)


def _ssd_kernel(xr_ref, br_ref, cr_ref, z_ref, dt_ref, dtn_ref, wx_ref, wb_ref, wc_ref, bx_ref,
                bb_ref, bc_ref, dtb_ref, alog_ref, dsk_ref, ng_ref, o_ref, h_ref, hist_ref,
                steps_row_ref, steps_parts_ref, *, n_groups, n_chunks):
    cl, hp, n = SSD_CHUNK, SSD_HPG * SSD_HEAD_DIM, SSD_STATE
    rows = n_groups * SUBLANES
    row = lax.broadcasted_iota(jnp.int32, (cl, cl), 0)
    col = lax.broadcasted_iota(jnp.int32, (cl, cl), 1)
    causal = col <= row
    lane_hp = lax.broadcasted_iota(jnp.int32, (cl, hp), 1)
    to_chunk_lanes = _head_selector(cl)
    to_head_lanes = _head_selector(SSD_HEAD_DIM)

    def stage_step_sizes(dt_blk):
        dt_in = (dt_blk + dtb_ref[...]).reshape(rows, cl)
        dt_all = jnp.maximum(dt_in, 0.0) + jnp.log(1.0 + jnp.exp(-jnp.abs(dt_in)))
        a_all = dt_all * (-jnp.exp(alog_ref[...].reshape(rows, 1)))
        lane_cl = lax.broadcasted_iota(jnp.int32, (rows, cl), 1)
        sh = 1
        while sh < cl:
            a_all = a_all + jnp.where(lane_cl >= sh, pltpu.roll(a_all, sh, axis=1), 0.0)
            sh *= 2
        te_all = jnp.exp(a_all[:, cl - 1:cl] - a_all) * dt_all
        steps_row_ref[0], steps_row_ref[1] = dt_all, a_all
        for i, part in enumerate(_split3(a_all) + _split3(jnp.exp(a_all)) + _split3(te_all)):
            steps_parts_ref[i] = part

    def per_position(first_part, g, selector):
        parts = [steps_parts_ref[first_part + i, g * SUBLANES:(g + 1) * SUBLANES, :] for i in range(3)]
        return _dot_tn(jnp.concatenate(parts, axis=0).astype(BF16), selector)

    def dt_chunk(ref, ci):
        return ref[0, :, :, pl.ds(pl.multiple_of(ci * cl, cl), cl)]

    def chunk(ci, _):
        tok = pl.ds(pl.multiple_of(ci * cl, cl), cl)

        @pl.when((pl.program_id(2) == 0) & (ci == 0))
        def _():
            h_ref[...] = jnp.zeros_like(h_ref)
            hist_ref[...] = jnp.zeros_like(hist_ref)
            stage_step_sizes(dt_chunk(dt_ref, 0))

        dt_all, a_all = steps_row_ref[0], steps_row_ref[1]
        a_cols = [per_position(0, g, to_chunk_lanes) for g in range(n_groups)]
        decay_in = [per_position(3, g, to_head_lanes) for g in range(n_groups)]
        decay_out = [per_position(6, g, to_head_lanes) for g in range(n_groups)]
        dt_next = jnp.where(ci < n_chunks - 1, dt_chunk(dt_ref, jnp.minimum(ci + 1, n_chunks - 1)),
                            dt_chunk(dtn_ref, 0))
        stage_step_sizes(dt_next)

        raw = jnp.concatenate([xr_ref[tok, :], br_ref[tok, :], cr_ref[tok, :]], axis=1)
        ext = jnp.concatenate([hist_ref[...], raw], axis=0)
        hist_ref[...] = raw[cl - HALO:, :]
        cw = jnp.concatenate([wx_ref[...], wb_ref[...], wc_ref[...]], axis=1)
        acc = cw[0:1, :] * ext
        for kk in range(1, SSD_CONV):
            acc = cw[kk:kk + 1, :] * ext + pltpu.roll(acc, 1, axis=0)
        bias = jnp.concatenate([bx_ref[...], bb_ref[...], bc_ref[...]], axis=1)
        act = _silu(acc[HALO:, :] + bias)
        xs_all = act[:, :n_groups * hp]
        bm_all = act[:, n_groups * hp:n_groups * (hp + n)].astype(BF16)
        cm_all = act[:, n_groups * (hp + n):].astype(BF16)

        for g in range(n_groups):
            xs = xs_all[:, g * hp:(g + 1) * hp]
            bm = bm_all[:, g * n:(g + 1) * n]
            cm = cm_all[:, g * n:(g + 1) * n]
            dt = dt_all[g * SUBLANES:(g + 1) * SUBLANES]
            a_cum = a_all[g * SUBLANES:(g + 1) * SUBLANES]

            cb = _dot_nt(cm, bm)
            ws, xblocks = [], []
            for r in range(SSD_HPG):
                seg = a_cols[g][:, r * cl:(r + 1) * cl] - a_cum[r:r + 1, :]
                decay = jnp.exp(jnp.where(causal, seg, -jnp.inf))
                ws.append((cb * decay * dt[r:r + 1, :]).astype(BF16))
                in_head = (lane_hp >= r * SSD_HEAD_DIM) & (lane_hp < (r + 1) * SSD_HEAD_DIM)
                xblocks.append(jnp.where(in_head, xs, 0.0).astype(BF16))
            y = _dot(jnp.concatenate(ws, axis=1), jnp.concatenate(xblocks, axis=0))

            h_prev = h_ref[g]
            y = y + _dot(cm, h_prev.astype(BF16)) * decay_in[g]
            xw = (xs * decay_out[g]).astype(BF16)
            h_ref[g] = h_prev * decay_in[g][cl - 1:cl, :] + _dot_tn(bm, xw)

            y = (y + dsk_ref[g] * xs) * _silu(z_ref[tok, g * hp:(g + 1) * hp])
            o_ref[tok, g * hp:(g + 1) * hp] = _rms(y, ng_ref[g]).astype(o_ref.dtype)
        return 0

    lax.fori_loop(0, n_chunks, chunk, 0)


def _ssd_core(xbc, z, dt_t, conv_w, conv_b, dt_bias, a_log, d_skip, norm_g, b, l):
    t = xbc.shape[0]
    g, hpg, cl, n, ng = SSD_GROUPS, SSD_HPG, SSD_CHUNK, SSD_STATE, SSD_GROUPS_PER_STEP
    hp = hpg * SSD_HEAD_DIM
    d_inner = g * hp
    kc = SSD_CHUNKS_PER_STEP
    tl = kc * cl
    nc = l // tl
    assert l % tl == 0 and hp % LANES == 0 and n == LANES and g % ng == 0
    bb, cb = d_inner // (ng * n), (d_inner + g * n) // (ng * n)
    pad = lambda p: jnp.pad(p.reshape(g, hpg, 1), ((0, 0), (0, SUBLANES - hpg), (0, 0)))
    per_lane = lambda p: jnp.repeat(p.reshape(g, hpg), SSD_HEAD_DIM, axis=1).reshape(g, 1, hp)
    conv_b = conv_b.reshape(1, -1)
    tok = lambda bi, gi, ci: bi * nc + ci
    return pl.pallas_call(
        functools.partial(_ssd_kernel, n_groups=ng, n_chunks=kc),
        grid=(b, g // ng, nc),
        in_specs=[
            pl.BlockSpec((tl, ng * hp), lambda bi, gi, ci: (tok(bi, gi, ci), gi)),
            pl.BlockSpec((tl, ng * n), lambda bi, gi, ci: (tok(bi, gi, ci), bb + gi)),
            pl.BlockSpec((tl, ng * n), lambda bi, gi, ci: (tok(bi, gi, ci), cb + gi)),
            pl.BlockSpec((tl, ng * hp), lambda bi, gi, ci: (tok(bi, gi, ci), gi)),
            pl.BlockSpec((1, ng, SUBLANES, tl), lambda bi, gi, ci: (bi, gi, 0, ci)),
            pl.BlockSpec((1, ng, SUBLANES, tl), lambda bi, gi, ci: (bi, gi, 0, jnp.minimum(ci + 1, nc - 1))),
            pl.BlockSpec((SSD_CONV, ng * hp), lambda bi, gi, ci: (0, gi)),
            pl.BlockSpec((SSD_CONV, ng * n), lambda bi, gi, ci: (0, bb + gi)),
            pl.BlockSpec((SSD_CONV, ng * n), lambda bi, gi, ci: (0, cb + gi)),
            pl.BlockSpec((1, ng * hp), lambda bi, gi, ci: (0, gi)),
            pl.BlockSpec((1, ng * n), lambda bi, gi, ci: (0, bb + gi)),
            pl.BlockSpec((1, ng * n), lambda bi, gi, ci: (0, cb + gi)),
            pl.BlockSpec((ng, SUBLANES, 1), lambda bi, gi, ci: (gi, 0, 0)),
            pl.BlockSpec((ng, SUBLANES, 1), lambda bi, gi, ci: (gi, 0, 0)),
            pl.BlockSpec((ng, 1, hp), lambda bi, gi, ci: (gi, 0, 0)),
            pl.BlockSpec((ng, 1, hp), lambda bi, gi, ci: (gi, 0, 0)),
        ],
        out_specs=pl.BlockSpec((tl, ng * hp), lambda bi, gi, ci: (tok(bi, gi, ci), gi)),
        out_shape=jax.ShapeDtypeStruct((t, d_inner), BF16),
        scratch_shapes=[pltpu.VMEM((ng, n, hp), F32), pltpu.VMEM((HALO, ng * (hp + 2 * n)), F32),
                        pltpu.VMEM((2, ng * SUBLANES, cl), F32), pltpu.VMEM((9, ng * SUBLANES, cl), F32)],
        compiler_params=_params(3),
        name="ssd_core",
    )(xbc, xbc, xbc, z, dt_t, dt_t, conv_w, conv_w, conv_w, conv_b, conv_b, conv_b,
      pad(dt_bias), pad(a_log), per_lane(d_skip), norm_g.reshape(g, 1, hp))


def _ssd_mixer(x, b, l, norm_g, w_in, conv_w, conv_b, dt_bias, a_log, d_skip, ssd_norm, layer):
    g, hpg = SSD_GROUPS, SSD_HPG
    d_inner = ssd_norm.shape[0]
    conv_dim = conv_w.shape[1]
    n_heads = g * hpg
    z, xbc, dt = _norm_proj(
        x, norm_g, w_in, layer,
        [(0, d_inner, 1.0), (d_inner, conv_dim, 1.0), (d_inner + conv_dim, n_heads, 1.0)], [F32] * 3)
    dt_t = dt.reshape(b, l, g, hpg).transpose(0, 2, 3, 1)
    dt_t = jnp.pad(dt_t, ((0, 0), (0, 0), (0, SUBLANES - hpg), (0, 0)))
    return _ssd_core(xbc, z, dt_t, conv_w, conv_b, dt_bias, a_log, d_skip, ssd_norm, b, l)


def _short_conv_kernel(x_ref, g_ref, wi_ref, cw_ref, wo_ref, o_ref, ext_ref):
    tm, d = x_ref.shape

    @pl.when(pl.program_id(1) == 0)
    def _():
        ext_ref[0:HALO, :] = jnp.zeros((HALO, ext_ref.shape[1]), F32)

    x = x_ref[...]
    xn = _rms(x, g_ref[...]).astype(BF16)
    w_part = lambda i: wi_ref[:, i * d:(i + 1) * d].astype(BF16)
    ext_ref[HALO:HALO + tm, :] = _dot(xn, w_part(1)) * _dot(xn, w_part(2))
    cw = cw_ref[...]
    u = cw[0:1, :] * ext_ref[pl.ds(HALO - (SC_WIDTH - 1), tm), :]
    for kk in range(1, SC_WIDTH):
        u = u + cw[kk:kk + 1, :] * ext_ref[pl.ds(HALO - (SC_WIDTH - 1) + kk, tm), :]
    ext_ref[0:HALO, :] = ext_ref[tm:tm + HALO, :]
    gated = (_dot(xn, w_part(0)) * u).astype(BF16)
    o_ref[...] = x + _dot(gated, wo_ref[...].astype(BF16))


def _short_conv_layer(x, b, l, norm_g, w_in, conv_w, w_out, layer):
    t, d = x.shape
    tm = SC_TM
    nl = l // tm
    assert l % tm == 0
    return pl.pallas_call(
        _short_conv_kernel,
        grid=(b, nl),
        in_specs=[
            pl.BlockSpec((tm, d), lambda bi, li: (bi * nl + li, 0)),
            _resident((1, d)),
            _resident((None, d, 3 * d), (layer, 0, 0)),
            _resident((None, SC_WIDTH, d), (layer, 0, 0)),
            _resident((None, d, d), (layer, 0, 0)),
        ],
        out_specs=pl.BlockSpec((tm, d), lambda bi, li: (bi * nl + li, 0)),
        out_shape=jax.ShapeDtypeStruct((t, d), F32),
        scratch_shapes=[pltpu.VMEM((tm + HALO, d), F32)],
        compiler_params=_params(2),
        name="short_conv",
    )(x, norm_g.reshape(1, d), w_in, conv_w, w_out)


def kernel(x, ffn1_norm, ffn1_w_gu, ffn1_w_down, mix_norm, ffn2_norm, ffn2_w_gu, ffn2_w_down,
           sb_w_qkv, sb_w_o, ssd_w_in, ssd_conv_w, ssd_conv_b, ssd_dt_bias, ssd_a_log, ssd_d,
           ssd_norm, ssd_w_out, sc_w_in, sc_conv_w, sc_w_out, final_norm):
    b, l, d = x.shape
    depth = ffn1_norm.shape[0]
    h = x.reshape(b * l, d)
    for i in range(depth):
        h = _ffn(h, ffn1_norm[i], ffn1_w_gu, ffn1_w_down, i)
        kind, j = i % N_MIXERS, i // N_MIXERS
        mixer_out = None
        if kind == 0:
            mixer_out = (_sb_mixer(h, b, l, mix_norm[i], sb_w_qkv, j), sb_w_o, j)
        elif kind == 1:
            y = _ssd_mixer(h, b, l, mix_norm[i], ssd_w_in, ssd_conv_w[j], ssd_conv_b[j],
                           ssd_dt_bias[j], ssd_a_log[j], ssd_d[j], ssd_norm[j], j)
            mixer_out = (y, ssd_w_out, j)
        else:
            h = _short_conv_layer(h, b, l, mix_norm[i], sc_w_in, sc_conv_w, sc_w_out, j)
        h = _ffn(h, ffn2_norm[i], ffn2_w_gu, ffn2_w_down, i,
                 final_g=final_norm if i == depth - 1 else None, mixer_out=mixer_out)
    return h.reshape(b, l, d)
```

```python
import functools

import jax
import jax.numpy as jnp
from jax import lax
from jax.experimental import pallas as pl
from jax.experimental.pallas import tpu as pltpu

F32 = jnp.float32
BF16 = jnp.bfloat16

RMS_EPS = 1e-6
LOG2_E = 1.4426950408889634
N_MIXERS = 3
SB_HEAD_DIM = 64
SSD_HEAD_DIM = 64
SSD_GROUPS = 8
SSD_HPG = 4
SSD_STATE = 128
SSD_CONV = 4
SSD_CHUNK = 128
SC_WIDTH = 3

LANES = 128
SUBLANES = 8
VMEM_LIMIT_BYTES = 60 * 1024 * 1024

FFN_TM = 1024
FFN_FUSED_TM = 512
FFN_TF = 256
PROJ_TM = 512
PROJ_TN = 512
ATT_T = 256
ATT_PAIRS_PER_STEP = 4
ATT_PAIRS_TOGETHER = 2
ATT_DEAD_CARRY = 160.0
SSD_GROUPS_PER_STEP = 4
SSD_CHUNKS_PER_STEP = 4
SC_TM = 512
HALO = SUBLANES


def _params(n_axes):
    return pltpu.CompilerParams(
        dimension_semantics=("arbitrary",) * n_axes,
        vmem_limit_bytes=VMEM_LIMIT_BYTES)


def _resident(shape, index=None):
    index = (0,) * len(shape) if index is None else index
    return pl.BlockSpec(shape, lambda *_: index, pipeline_mode=pl.Buffered(1))


def _rms(x, g):
    ms = jnp.mean(x * x, axis=-1, keepdims=True)
    return x * lax.rsqrt(ms + RMS_EPS) * g


def _silu(x):
    h = 0.5 * x
    return h + h * jnp.tanh(h)


def _dot(a, b):
    return jnp.dot(a, b, preferred_element_type=F32)


def _dot_nt(a, b):
    return lax.dot_general(a, b, (((1,), (1,)), ((), ())), preferred_element_type=F32)


def _dot_tn(a, b):
    return lax.dot_general(a, b, (((0,), (0,)), ((), ())), preferred_element_type=F32)


def _ffn_kernel(x_ref, g_ref, wgu_ref, wd_ref, *refs, n_chunks, tf, final_norm, mixer_out):
    refs = list(refs)
    y_ref, wo_ref = (refs.pop(0), refs.pop(0)) if mixer_out else (None, None)
    fg_ref = refs[0] if final_norm else None
    o_ref, xn_ref, acc_ref = refs[-3:]
    x = x_ref[...]
    if mixer_out:
        x = x + _dot(y_ref[...], wo_ref[...].astype(BF16))
    d_ff = n_chunks * tf
    xn_ref[...] = _rms(x, g_ref[...]).astype(BF16)
    for c in range(n_chunks):
        gate = _dot(xn_ref[...], wgu_ref[:, c * tf:(c + 1) * tf].astype(BF16))
        up = _dot(xn_ref[...], wgu_ref[:, d_ff + c * tf:d_ff + (c + 1) * tf].astype(BF16))
        h = (_silu(gate) * up).astype(BF16)
        part = _dot(h, wd_ref[c * tf:(c + 1) * tf, :].astype(BF16))
        if c == 0:
            acc_ref[...] = part
        else:
            acc_ref[...] += part
    y = x + 0.5 * acc_ref[...]
    if final_norm:
        y = _rms(y, fg_ref[...])
    o_ref[...] = y


def _ffn(x, norm_g, w_gu, w_down, layer, final_g=None, mixer_out=None):
    t, d = x.shape
    d_ff = w_down.shape[1]
    tm, tf = (FFN_TM if mixer_out is None else FFN_FUSED_TM), FFN_TF
    n_chunks = d_ff // tf
    assert t % tm == 0 and d_ff % tf == 0
    final_norm = final_g is not None
    extra = [final_g.reshape(1, d)] if final_norm else []
    pre, pre_specs = [], []
    if mixer_out is not None:
        y, w_o, j = mixer_out
        k = y.shape[1]
        pre = [y, w_o]
        pre_specs = [pl.BlockSpec((tm, k), lambda i: (i, 0)), _resident((None, k, d), (j, 0, 0))]
    return pl.pallas_call(
        functools.partial(_ffn_kernel, n_chunks=n_chunks, tf=tf, final_norm=final_norm,
                          mixer_out=mixer_out is not None),
        grid=(t // tm,),
        in_specs=[
            pl.BlockSpec((tm, d), lambda i: (i, 0)),
            _resident((1, d)),
            _resident((None, d, 2 * d_ff), (layer, 0, 0)),
            _resident((None, d_ff, d), (layer, 0, 0)),
        ] + pre_specs + [_resident((1, d))] * len(extra),
        out_specs=pl.BlockSpec((tm, d), lambda i: (i, 0)),
        out_shape=jax.ShapeDtypeStruct((t, d), F32),
        scratch_shapes=[pltpu.VMEM((tm, d), BF16), pltpu.VMEM((tm, d), F32)],
        compiler_params=_params(1),
        name="ffn",
    )(x, norm_g.reshape(1, d), w_gu, w_down, *pre, *extra)


def _norm_proj_kernel(x_ref, g_ref, w_ref, *o_refs, segments, tn):
    xn = _rms(x_ref[...], g_ref[...]).astype(BF16)
    for o_ref, (start, width, scale) in zip(o_refs, segments):
        for n0 in range(0, width, tn):
            n1 = min(n0 + tn, width)
            w = w_ref[:, start + n0:start + n1]
            if scale != 1.0:
                w = w * scale
            o_ref[:, n0:n1] = _dot(xn, w.astype(BF16)).astype(o_ref.dtype)


def _norm_proj(x, norm_g, w, layer, segments, out_dtypes):
    t, d = x.shape
    tm = PROJ_TM
    assert t % tm == 0 and all(s[0] % LANES == 0 for s in segments)
    return pl.pallas_call(
        functools.partial(_norm_proj_kernel, segments=tuple(segments), tn=PROJ_TN),
        grid=(t // tm,),
        in_specs=[pl.BlockSpec((tm, d), lambda i: (i, 0)), _resident((1, d)),
                  _resident((None,) + w.shape[1:], (layer, 0, 0))],
        out_specs=[pl.BlockSpec((tm, s[1]), lambda i: (i, 0)) for s in segments],
        out_shape=[jax.ShapeDtypeStruct((t, s[1]), dt) for s, dt in zip(segments, out_dtypes)],
        compiler_params=_params(1),
        name="norm_proj",
    )(x, norm_g.reshape(1, d), w)


def _sb_attn_kernel(q_ref, k_ref, v_ref, o_ref, lb_a, sp_a, rs_a, lb_b, sp_b, rs_b, acc_ref, c_ref,
                    *, t, n_together):
    qi = pl.program_id(2)
    hd = SB_HEAD_DIM
    n_p = n_together
    sub = 2 * n_p * t
    buf_a, buf_b = (lb_a, sp_a, rs_a), (lb_b, sp_b, rs_b)
    both, first, second = (0, 2), (0, 1), (1, 2)
    in_a = lax.broadcasted_iota(jnp.int32, (1, 2 * hd), 1) < hd
    row = lax.broadcasted_iota(jnp.int32, (t, t), 0)
    col = lax.broadcasted_iota(jnp.int32, (t, t), 1)
    strict_lower = jnp.concatenate([col < row] * (2 * n_p), axis=0)
    tri = jnp.where(row > col, 1.0, 0.0).astype(BF16)
    sign_bit = jnp.uint32(0x80000000)
    top = 2 * qi + 1

    def mask_diagonal(x, fill):
        head = jnp.where(strict_lower, x[:sub], fill)
        return head if x.shape[0] == sub else jnp.concatenate([head, x[sub:]], axis=0)

    def pair_group(grp, _):
        lanes = [pl.ds(pl.multiple_of((grp * n_p + p) * 2 * hd, 2 * hd), 2 * hd) for p in range(n_p)]
        q_parts = []
        for s in range(2):
            for p in range(n_p):
                q2 = q_ref[0, s * t:(s + 1) * t, lanes[p]]
                q_parts += [jnp.where(in_a, q2, 0), jnp.where(in_a, 0, q2)]
        qs = jnp.concatenate(q_parts, axis=0)

        def scores(j, buf, subs, diag):
            lb_ref, sp_ref, rs_ref = buf
            r0, r1 = subs[0] * sub, subs[1] * sub
            rows_k = pl.ds(pl.multiple_of(j * t, t), t)
            k2 = [k_ref[0, rows_k, lanes[p]] for p in range(n_p)]
            z = jnp.concatenate(
                [_dot_nt(qs[(s * n_p + p) * 2 * t:(s * n_p + p + 1) * 2 * t], k2[p])
                 for s in range(*subs) for p in range(n_p)], axis=0)
            neg_abs = lax.bitcast_convert_type(lax.bitcast_convert_type(z, jnp.uint32) | sign_bit, F32)
            lb = jnp.minimum(z, 0.0) - jnp.log2(1.0 + jnp.exp2(neg_abs))
            sp = z - lb
            if diag:
                sp, lb = mask_diagonal(sp, 0.0), mask_diagonal(lb, -jnp.inf)
            lb_ref[r0:r1, :] = lb
            sp_ref[r0:r1, :] = sp.astype(BF16)
            rs_ref[r0:r1, :] = jnp.broadcast_to(jnp.sum(sp, axis=-1, keepdims=True), (r1 - r0, LANES))

        def values(j, buf, subs):
            lb_ref, sp_ref, rs_ref = buf
            r0, r1 = subs[0] * sub, subs[1] * sub
            rows_k = pl.ds(pl.multiple_of(j * t, t), t)
            c = c_ref[r0:r1, :]
            tail = _dot(sp_ref[r0:r1, :], tri)
            c_wide = jnp.concatenate([c] * (t // LANES), axis=1)
            att = jnp.exp2(lb_ref[r0:r1, :] - tail - c_wide).astype(BF16)
            c_ref[r0:r1, :] = c + rs_ref[r0:r1, :]
            for p in range(n_p):
                blocks = []
                for s in range(subs[1] - subs[0]):
                    base = (s * n_p + p) * 2 * t
                    blocks.append(jnp.concatenate([att[base:base + t], att[base + t:base + 2 * t]], axis=1))
                v2 = v_ref[0, rows_k, lanes[p]]
                vv = jnp.concatenate([jnp.where(in_a, v2, 0), jnp.where(in_a, 0, v2)], axis=0)
                acc_ref[p, subs[0] * t:subs[1] * t, :] += _dot(jnp.concatenate(blocks, axis=0), vv)

        def tile_pair(state):
            i = state[0]
            j = top - 3 - 2 * i
            values(j, buf_b, both)
            scores(j - 1, buf_a, both, False)
            values(j - 1, buf_a, both)
            scores(j - 2, buf_b, both, False)
            return i + 1, jnp.min(c_ref[...])

        def live(state):
            return (state[0] < qi - 1) & (state[1] < ATT_DEAD_CARRY)

        acc_ref[...] = jnp.zeros_like(acc_ref)
        c_ref[...] = jnp.zeros_like(c_ref)

        @pl.when(qi == 0)
        def _():
            scores(top, buf_a, second, True)
            values(top, buf_a, second)
            scores(top - 1, buf_b, both, True)
            values(top - 1, buf_b, both)

        @pl.when(qi > 0)
        def _():
            scores(top, buf_a, second, True)
            scores(top - 1, buf_b, both, True)
            values(top, buf_a, second)
            scores(top - 2, buf_a, first, False)
            values(top - 1, buf_b, both)
            c_min0 = jnp.minimum(jnp.min(c_ref[sub:, :]), jnp.min(c_ref[:sub, :] + buf_a[2][:sub, :]))
            values(top - 2, buf_a, first)

            @pl.when(c_min0 < ATT_DEAD_CARRY)
            def _():
                scores(top - 2, buf_a, second, False)
                values(top - 2, buf_a, second)
                scores(top - 3, buf_b, both, False)
                n_pairs, c_min = lax.while_loop(live, tile_pair, (jnp.int32(0), c_min0))

                @pl.when(c_min < ATT_DEAD_CARRY)
                def _():
                    values(top - 3 - 2 * n_pairs, buf_b, both)

        for p in range(n_p):
            o_ref[0, :, lanes[p]] = acc_ref[p].astype(o_ref.dtype)
        return 0

    lax.fori_loop(0, q_ref.shape[2] // (2 * hd * n_p), pair_group, 0)


def _sb_attention(q, k, v):
    b, l, d = q.shape
    t, n_p = ATT_T, ATT_PAIRS_TOGETHER
    tq = 2 * t
    hp = 2 * SB_HEAD_DIM * ATT_PAIRS_PER_STEP
    assert l % tq == 0 and d % hp == 0 and 2 * SB_HEAD_DIM == LANES and ATT_PAIRS_PER_STEP % n_p == 0
    rows = 4 * n_p * t
    stage = [pltpu.VMEM((rows, t), F32), pltpu.VMEM((rows, t), BF16), pltpu.VMEM((rows, LANES), F32)]
    return pl.pallas_call(
        functools.partial(_sb_attn_kernel, t=t, n_together=n_p),
        grid=(b, d // hp, l // tq),
        in_specs=[
            pl.BlockSpec((1, tq, hp), lambda bi, hi, qi: (bi, qi, hi)),
            pl.BlockSpec((1, l, hp), lambda bi, hi, qi: (bi, 0, hi)),
            pl.BlockSpec((1, l, hp), lambda bi, hi, qi: (bi, 0, hi)),
        ],
        out_specs=pl.BlockSpec((1, tq, hp), lambda bi, hi, qi: (bi, qi, hi)),
        out_shape=jax.ShapeDtypeStruct((b, l, d), BF16),
        scratch_shapes=stage + stage + [pltpu.VMEM((n_p, tq, LANES), F32),
                                        pltpu.VMEM((rows, LANES), F32)],
        compiler_params=_params(3),
        name="sb_attn",
    )(q, k, v)


def _sb_mixer(x, b, l, norm_g, w_qkv, layer):
    t, d = x.shape
    scale = LOG2_E * SB_HEAD_DIM ** -0.5
    q, k, v = _norm_proj(x, norm_g, w_qkv, layer,
                         [(0, d, scale), (d, d, 1.0), (2 * d, d, 1.0)], [BF16] * 3)
    o = _sb_attention(q.reshape(b, l, d), k.reshape(b, l, d), v.reshape(b, l, d))
    return o.reshape(t, d)


def _split3(x):
    hi = x.astype(BF16).astype(F32)
    rest = x - hi
    mid = rest.astype(BF16).astype(F32)
    return hi, mid, (rest - mid).astype(BF16).astype(F32)


def _head_selector(width):
    k = lax.broadcasted_iota(jnp.int32, (3 * SUBLANES, SSD_HPG * width), 0) % SUBLANES
    lane = lax.broadcasted_iota(jnp.int32, (3 * SUBLANES, SSD_HPG * width), 1)
    return jnp.where((lane >= k * width) & (lane < (k + 1) * width), 1.0, 0.0).astype(BF16)


def _ssd_kernel(xr_ref, br_ref, cr_ref, z_ref, dt_ref, dtn_ref, wx_ref, wb_ref, wc_ref, bx_ref,
                bb_ref, bc_ref, dtb_ref, alog_ref, dsk_ref, ng_ref, o_ref, h_ref, hist_ref,
                steps_row_ref, steps_parts_ref, *, n_groups, n_chunks):
    cl, hp, n = SSD_CHUNK, SSD_HPG * SSD_HEAD_DIM, SSD_STATE
    rows = n_groups * SUBLANES
    row = lax.broadcasted_iota(jnp.int32, (cl, cl), 0)
    col = lax.broadcasted_iota(jnp.int32, (cl, cl), 1)
    causal = col <= row
    lane_hp = lax.broadcasted_iota(jnp.int32, (cl, hp), 1)
    to_chunk_lanes = _head_selector(cl)
    to_head_lanes = _head_selector(SSD_HEAD_DIM)

    def stage_step_sizes(dt_blk):
        dt_in = (dt_blk + dtb_ref[...]).reshape(rows, cl)
        dt_all = jnp.maximum(dt_in, 0.0) + jnp.log(1.0 + jnp.exp(-jnp.abs(dt_in)))
        a_all = dt_all * (-jnp.exp(alog_ref[...].reshape(rows, 1)))
        lane_cl = lax.broadcasted_iota(jnp.int32, (rows, cl), 1)
        sh = 1
        while sh < cl:
            a_all = a_all + jnp.where(lane_cl >= sh, pltpu.roll(a_all, sh, axis=1), 0.0)
            sh *= 2
        te_all = jnp.exp(a_all[:, cl - 1:cl] - a_all) * dt_all
        a2 = a_all * LOG2_E
        steps_row_ref[0] = a2 - jnp.log2(dt_all)
        for i, part in enumerate(_split3(a2) + _split3(jnp.exp(a_all)) + _split3(te_all)):
            steps_parts_ref[i] = part

    def per_position(first_part, g, selector):
        parts = [steps_parts_ref[first_part + i, g * SUBLANES:(g + 1) * SUBLANES, :] for i in range(3)]
        return _dot_tn(jnp.concatenate(parts, axis=0).astype(BF16), selector)

    def dt_chunk(ref, ci):
        return ref[0, :, :, pl.ds(pl.multiple_of(ci * cl, cl), cl)]

    def chunk(ci, _):
        tok = pl.ds(pl.multiple_of(ci * cl, cl), cl)

        @pl.when((pl.program_id(2) == 0) & (ci == 0))
        def _():
            h_ref[...] = jnp.zeros_like(h_ref)
            hist_ref[...] = jnp.zeros_like(hist_ref)
            stage_step_sizes(dt_chunk(dt_ref, 0))

        src2_all = steps_row_ref[0]
        a_cols = [per_position(0, g, to_chunk_lanes) for g in range(n_groups)]
        decay_in = [per_position(3, g, to_head_lanes) for g in range(n_groups)]
        decay_out = [per_position(6, g, to_head_lanes) for g in range(n_groups)]
        dt_next = jnp.where(ci < n_chunks - 1, dt_chunk(dt_ref, jnp.minimum(ci + 1, n_chunks - 1)),
                            dt_chunk(dtn_ref, 0))
        stage_step_sizes(dt_next)

        raw = jnp.concatenate([xr_ref[tok, :], br_ref[tok, :], cr_ref[tok, :]], axis=1)
        ext = jnp.concatenate([hist_ref[...], raw], axis=0)
        hist_ref[...] = raw[cl - HALO:, :]
        cw = jnp.concatenate([wx_ref[...], wb_ref[...], wc_ref[...]], axis=1)
        acc = cw[0:1, :] * ext
        for kk in range(1, SSD_CONV):
            acc = cw[kk:kk + 1, :] * ext + pltpu.roll(acc, 1, axis=0)
        bias = jnp.concatenate([bx_ref[...], bb_ref[...], bc_ref[...]], axis=1)
        act = _silu(acc[HALO:, :] + bias)
        xs_all = act[:, :n_groups * hp]
        bm_all = act[:, n_groups * hp:n_groups * (hp + n)].astype(BF16)
        cm_all = act[:, n_groups * (hp + n):].astype(BF16)

        for g in range(n_groups):
            xs = xs_all[:, g * hp:(g + 1) * hp]
            bm = bm_all[:, g * n:(g + 1) * n]
            cm = cm_all[:, g * n:(g + 1) * n]
            src2 = src2_all[g * SUBLANES:(g + 1) * SUBLANES]

            cb = _dot_nt(cm, bm)
            ws, xblocks = [], []
            for r in range(SSD_HPG):
                seg = a_cols[g][:, r * cl:(r + 1) * cl] - src2[r:r + 1, :]
                ws.append((cb * jnp.exp2(jnp.where(causal, seg, -jnp.inf))).astype(BF16))
                in_head = (lane_hp >= r * SSD_HEAD_DIM) & (lane_hp < (r + 1) * SSD_HEAD_DIM)
                xblocks.append(jnp.where(in_head, xs, 0.0).astype(BF16))
            y = _dot(jnp.concatenate(ws, axis=1), jnp.concatenate(xblocks, axis=0))

            h_prev = h_ref[g]
            y = y + _dot(cm, h_prev.astype(BF16)) * decay_in[g]
            xw = (xs * decay_out[g]).astype(BF16)
            h_ref[g] = h_prev * decay_in[g][cl - 1:cl, :] + _dot_tn(bm, xw)

            y = (y + dsk_ref[g] * xs) * _silu(z_ref[tok, g * hp:(g + 1) * hp])
            o_ref[tok, g * hp:(g + 1) * hp] = _rms(y, ng_ref[g]).astype(o_ref.dtype)
        return 0

    lax.fori_loop(0, n_chunks, chunk, 0)


def _ssd_core(xbc, z, dt_t, conv_w, conv_b, dt_bias, a_log, d_skip, norm_g, b, l):
    t = xbc.shape[0]
    g, hpg, cl, n, ng = SSD_GROUPS, SSD_HPG, SSD_CHUNK, SSD_STATE, SSD_GROUPS_PER_STEP
    hp = hpg * SSD_HEAD_DIM
    d_inner = g * hp
    kc = SSD_CHUNKS_PER_STEP
    tl = kc * cl
    nc = l // tl
    assert l % tl == 0 and hp % LANES == 0 and n == LANES and g % ng == 0
    bb, cb = d_inner // (ng * n), (d_inner + g * n) // (ng * n)
    pad = lambda p: jnp.pad(p.reshape(g, hpg, 1), ((0, 0), (0, SUBLANES - hpg), (0, 0)))
    per_lane = lambda p: jnp.repeat(p.reshape(g, hpg), SSD_HEAD_DIM, axis=1).reshape(g, 1, hp)
    conv_b = conv_b.reshape(1, -1)
    tok = lambda bi, gi, ci: bi * nc + ci
    return pl.pallas_call(
        functools.partial(_ssd_kernel, n_groups=ng, n_chunks=kc),
        grid=(b, g // ng, nc),
        in_specs=[
            pl.BlockSpec((tl, ng * hp), lambda bi, gi, ci: (tok(bi, gi, ci), gi)),
            pl.BlockSpec((tl, ng * n), lambda bi, gi, ci: (tok(bi, gi, ci), bb + gi)),
            pl.BlockSpec((tl, ng * n), lambda bi, gi, ci: (tok(bi, gi, ci), cb + gi)),
            pl.BlockSpec((tl, ng * hp), lambda bi, gi, ci: (tok(bi, gi, ci), gi)),
            pl.BlockSpec((1, ng, SUBLANES, tl), lambda bi, gi, ci: (bi, gi, 0, ci)),
            pl.BlockSpec((1, ng, SUBLANES, tl), lambda bi, gi, ci: (bi, gi, 0, jnp.minimum(ci + 1, nc - 1))),
            pl.BlockSpec((SSD_CONV, ng * hp), lambda bi, gi, ci: (0, gi)),
            pl.BlockSpec((SSD_CONV, ng * n), lambda bi, gi, ci: (0, bb + gi)),
            pl.BlockSpec((SSD_CONV, ng * n), lambda bi, gi, ci: (0, cb + gi)),
            pl.BlockSpec((1, ng * hp), lambda bi, gi, ci: (0, gi)),
            pl.BlockSpec((1, ng * n), lambda bi, gi, ci: (0, bb + gi)),
            pl.BlockSpec((1, ng * n), lambda bi, gi, ci: (0, cb + gi)),
            pl.BlockSpec((ng, SUBLANES, 1), lambda bi, gi, ci: (gi, 0, 0)),
            pl.BlockSpec((ng, SUBLANES, 1), lambda bi, gi, ci: (gi, 0, 0)),
            pl.BlockSpec((ng, 1, hp), lambda bi, gi, ci: (gi, 0, 0)),
            pl.BlockSpec((ng, 1, hp), lambda bi, gi, ci: (gi, 0, 0)),
        ],
        out_specs=pl.BlockSpec((tl, ng * hp), lambda bi, gi, ci: (tok(bi, gi, ci), gi)),
        out_shape=jax.ShapeDtypeStruct((t, d_inner), BF16),
        scratch_shapes=[pltpu.VMEM((ng, n, hp), F32), pltpu.VMEM((HALO, ng * (hp + 2 * n)), F32),
                        pltpu.VMEM((1, ng * SUBLANES, cl), F32), pltpu.VMEM((9, ng * SUBLANES, cl), F32)],
        compiler_params=_params(3),
        name="ssd_core",
    )(xbc, xbc, xbc, z, dt_t, dt_t, conv_w, conv_w, conv_w, conv_b, conv_b, conv_b,
      pad(dt_bias), pad(a_log), per_lane(d_skip), norm_g.reshape(g, 1, hp))


def _ssd_mixer(x, b, l, norm_g, w_in, conv_w, conv_b, dt_bias, a_log, d_skip, ssd_norm, layer):
    g, hpg = SSD_GROUPS, SSD_HPG
    d_inner = ssd_norm.shape[0]
    conv_dim = conv_w.shape[1]
    n_heads = g * hpg
    z, xbc, dt = _norm_proj(
        x, norm_g, w_in, layer,
        [(0, d_inner, 1.0), (d_inner, conv_dim, 1.0), (d_inner + conv_dim, n_heads, 1.0)], [F32] * 3)
    dt_t = dt.reshape(b, l, g, hpg).transpose(0, 2, 3, 1)
    dt_t = jnp.pad(dt_t, ((0, 0), (0, 0), (0, SUBLANES - hpg), (0, 0)))
    return _ssd_core(xbc, z, dt_t, conv_w, conv_b, dt_bias, a_log, d_skip, ssd_norm, b, l)


def _short_conv_kernel(x_ref, g_ref, wi_ref, cw_ref, wo_ref, o_ref, ext_ref):
    tm, d = x_ref.shape

    @pl.when(pl.program_id(1) == 0)
    def _():
        ext_ref[0:HALO, :] = jnp.zeros((HALO, ext_ref.shape[1]), F32)

    x = x_ref[...]
    xn = _rms(x, g_ref[...]).astype(BF16)
    w_part = lambda i: wi_ref[:, i * d:(i + 1) * d].astype(BF16)
    ext_ref[HALO:HALO + tm, :] = _dot(xn, w_part(1)) * _dot(xn, w_part(2))
    cw = cw_ref[...]
    u = cw[0:1, :] * ext_ref[pl.ds(HALO - (SC_WIDTH - 1), tm), :]
    for kk in range(1, SC_WIDTH):
        u = u + cw[kk:kk + 1, :] * ext_ref[pl.ds(HALO - (SC_WIDTH - 1) + kk, tm), :]
    ext_ref[0:HALO, :] = ext_ref[tm:tm + HALO, :]
    gated = (_dot(xn, w_part(0)) * u).astype(BF16)
    o_ref[...] = x + _dot(gated, wo_ref[...].astype(BF16))


def _short_conv_layer(x, b, l, norm_g, w_in, conv_w, w_out, layer):
    t, d = x.shape
    tm = SC_TM
    nl = l // tm
    assert l % tm == 0
    return pl.pallas_call(
        _short_conv_kernel,
        grid=(b, nl),
        in_specs=[
            pl.BlockSpec((tm, d), lambda bi, li: (bi * nl + li, 0)),
            _resident((1, d)),
            _resident((None, d, 3 * d), (layer, 0, 0)),
            _resident((None, SC_WIDTH, d), (layer, 0, 0)),
            _resident((None, d, d), (layer, 0, 0)),
        ],
        out_specs=pl.BlockSpec((tm, d), lambda bi, li: (bi * nl + li, 0)),
        out_shape=jax.ShapeDtypeStruct((t, d), F32),
        scratch_shapes=[pltpu.VMEM((tm + HALO, d), F32)],
        compiler_params=_params(2),
        name="short_conv",
    )(x, norm_g.reshape(1, d), w_in, conv_w, w_out)


def kernel(x, ffn1_norm, ffn1_w_gu, ffn1_w_down, mix_norm, ffn2_norm, ffn2_w_gu, ffn2_w_down,
           sb_w_qkv, sb_w_o, ssd_w_in, ssd_conv_w, ssd_conv_b, ssd_dt_bias, ssd_a_log, ssd_d,
           ssd_norm, ssd_w_out, sc_w_in, sc_conv_w, sc_w_out, final_norm):
    b, l, d = x.shape
    depth = ffn1_norm.shape[0]
    h = x.reshape(b * l, d)
    for i in range(depth):
        h = _ffn(h, ffn1_norm[i], ffn1_w_gu, ffn1_w_down, i)
        kind, j = i % N_MIXERS, i // N_MIXERS
        mixer_out = None
        if kind == 0:
            mixer_out = (_sb_mixer(h, b, l, mix_norm[i], sb_w_qkv, j), sb_w_o, j)
        elif kind == 1:
            y = _ssd_mixer(h, b, l, mix_norm[i], ssd_w_in, ssd_conv_w[j], ssd_conv_b[j],
                           ssd_dt_bias[j], ssd_a_log[j], ssd_d[j], ssd_norm[j], j)
            mixer_out = (y, ssd_w_out, j)
        else:
            h = _short_conv_layer(h, b, l, mix_norm[i], sc_w_in, sc_conv_w, sc_w_out, j)
        h = _ffn(h, ffn2_norm[i], ffn2_w_gu, ffn2_w_down, i,
                 final_g=final_norm if i == depth - 1 else None, mixer_out=mixer_out)
    return h.reshape(b, l, d)
```

```python
import functools

import jax
import jax.numpy as jnp
from jax import lax
from jax.experimental import pallas as pl
from jax.experimental.pallas import tpu as pltpu

F32 = jnp.float32
BF16 = jnp.bfloat16

RMS_EPS = 1e-6
LOG2_E = 1.4426950408889634
N_MIXERS = 3
SB_HEAD_DIM = 64
SSD_HEAD_DIM = 64
SSD_GROUPS = 8
SSD_HPG = 4
SSD_STATE = 128
SSD_CONV = 4
SSD_CHUNK = 128
SC_WIDTH = 3

LANES = 128
SUBLANES = 8
VMEM_LIMIT_BYTES = 60 * 1024 * 1024

FFN_TM = 1024
FFN_FUSED_TM = 512
FFN_TF = 256
PROJ_TM = 512
PROJ_TN = 512
ATT_T = 256
ATT_PAIRS_PER_STEP = 4
ATT_PAIRS_TOGETHER = 2
ATT_DEAD_CARRY = 160.0
SSD_GROUPS_PER_STEP = 4
SSD_CHUNKS_PER_STEP = 4
SC_TM = 512
HALO = SUBLANES


def _params(n_axes):
    return pltpu.CompilerParams(
        dimension_semantics=("arbitrary",) * n_axes,
        vmem_limit_bytes=VMEM_LIMIT_BYTES)


def _resident(shape, index=None):
    index = (0,) * len(shape) if index is None else index
    return pl.BlockSpec(shape, lambda *_: index, pipeline_mode=pl.Buffered(1))


def _rms(x, g):
    ms = jnp.mean(x * x, axis=-1, keepdims=True)
    return x * lax.rsqrt(ms + RMS_EPS) * g


def _silu_of_half(h):
    return h + h * jnp.tanh(h)


def _silu(x):
    return _silu_of_half(0.5 * x)


def _dot(a, b):
    return jnp.dot(a, b, preferred_element_type=F32)


def _dot_nt(a, b):
    return lax.dot_general(a, b, (((1,), (1,)), ((), ())), preferred_element_type=F32)


def _dot_tn(a, b):
    return lax.dot_general(a, b, (((0,), (0,)), ((), ())), preferred_element_type=F32)


def _ffn_kernel(x_ref, g_ref, wgu_ref, wd_ref, *refs, n_chunks, tf, final_norm, mixer_out):
    refs = list(refs)
    y_ref, wo_ref = (refs.pop(0), refs.pop(0)) if mixer_out else (None, None)
    fg_ref = refs[0] if final_norm else None
    o_ref, xn_ref, acc_ref = refs[-3:]
    x = x_ref[...]
    if mixer_out:
        x = x + _dot(y_ref[...], wo_ref[...].astype(BF16))
    d_ff = n_chunks * tf
    xn_ref[...] = _rms(x, g_ref[...]).astype(BF16)
    for c in range(n_chunks):
        gate = _dot(xn_ref[...], wgu_ref[:, c * tf:(c + 1) * tf].astype(BF16))
        up = _dot(xn_ref[...], wgu_ref[:, d_ff + c * tf:d_ff + (c + 1) * tf].astype(BF16))
        h = (_silu(gate) * up).astype(BF16)
        part = _dot(h, wd_ref[c * tf:(c + 1) * tf, :].astype(BF16))
        if c == 0:
            acc_ref[...] = part
        else:
            acc_ref[...] += part
    y = x + 0.5 * acc_ref[...]
    if final_norm:
        y = _rms(y, fg_ref[...])
    o_ref[...] = y


def _ffn(x, norm_g, w_gu, w_down, layer, final_g=None, mixer_out=None):
    t, d = x.shape
    d_ff = w_down.shape[1]
    tm, tf = (FFN_TM if mixer_out is None else FFN_FUSED_TM), FFN_TF
    n_chunks = d_ff // tf
    assert t % tm == 0 and d_ff % tf == 0
    final_norm = final_g is not None
    extra = [final_g.reshape(1, d)] if final_norm else []
    pre, pre_specs = [], []
    if mixer_out is not None:
        y, w_o, j = mixer_out
        k = y.shape[1]
        pre = [y, w_o]
        pre_specs = [pl.BlockSpec((tm, k), lambda i: (i, 0)), _resident((None, k, d), (j, 0, 0))]
    return pl.pallas_call(
        functools.partial(_ffn_kernel, n_chunks=n_chunks, tf=tf, final_norm=final_norm,
                          mixer_out=mixer_out is not None),
        grid=(t // tm,),
        in_specs=[
            pl.BlockSpec((tm, d), lambda i: (i, 0)),
            _resident((1, d)),
            _resident((None, d, 2 * d_ff), (layer, 0, 0)),
            _resident((None, d_ff, d), (layer, 0, 0)),
        ] + pre_specs + [_resident((1, d))] * len(extra),
        out_specs=pl.BlockSpec((tm, d), lambda i: (i, 0)),
        out_shape=jax.ShapeDtypeStruct((t, d), F32),
        scratch_shapes=[pltpu.VMEM((tm, d), BF16), pltpu.VMEM((tm, d), F32)],
        compiler_params=_params(1),
        name="ffn",
    )(x, norm_g.reshape(1, d), w_gu, w_down, *pre, *extra)


def _norm_proj_kernel(x_ref, g_ref, w_ref, *o_refs, segments, tn):
    xn = _rms(x_ref[...], g_ref[...]).astype(BF16)
    for o_ref, (start, width, scale) in zip(o_refs, segments):
        for n0 in range(0, width, tn):
            n1 = min(n0 + tn, width)
            w = w_ref[:, start + n0:start + n1]
            if scale != 1.0:
                w = w * scale
            o_ref[:, n0:n1] = _dot(xn, w.astype(BF16)).astype(o_ref.dtype)


def _norm_proj(x, norm_g, w, layer, segments, out_dtypes):
    t, d = x.shape
    tm = PROJ_TM
    assert t % tm == 0 and all(s[0] % LANES == 0 for s in segments)
    return pl.pallas_call(
        functools.partial(_norm_proj_kernel, segments=tuple(segments), tn=PROJ_TN),
        grid=(t // tm,),
        in_specs=[pl.BlockSpec((tm, d), lambda i: (i, 0)), _resident((1, d)),
                  _resident((None,) + w.shape[1:], (layer, 0, 0))],
        out_specs=[pl.BlockSpec((tm, s[1]), lambda i: (i, 0)) for s in segments],
        out_shape=[jax.ShapeDtypeStruct((t, s[1]), dt) for s, dt in zip(segments, out_dtypes)],
        compiler_params=_params(1),
        name="norm_proj",
    )(x, norm_g.reshape(1, d), w)


def _sb_attn_kernel(q_ref, k_ref, v_ref, o_ref, lb_a, sp_a, rs_a, lb_b, sp_b, rs_b, acc_ref, c_ref,
                    *, t, n_together):
    qi = pl.program_id(2)
    hd = SB_HEAD_DIM
    n_p = n_together
    sub = 2 * n_p * t
    buf_a, buf_b = (lb_a, sp_a, rs_a), (lb_b, sp_b, rs_b)
    both, first, second = (0, 2), (0, 1), (1, 2)
    in_a = lax.broadcasted_iota(jnp.int32, (1, 2 * hd), 1) < hd
    row = lax.broadcasted_iota(jnp.int32, (t, t), 0)
    col = lax.broadcasted_iota(jnp.int32, (t, t), 1)
    strict_lower = jnp.concatenate([col < row] * (2 * n_p), axis=0)
    tri = jnp.where(row > col, 1.0, 0.0).astype(BF16)
    sign_bit = jnp.uint32(0x80000000)
    top = 2 * qi + 1

    def mask_diagonal(x, fill):
        head = jnp.where(strict_lower, x[:sub], fill)
        return head if x.shape[0] == sub else jnp.concatenate([head, x[sub:]], axis=0)

    def pair_group(grp, _):
        lanes = [pl.ds(pl.multiple_of((grp * n_p + p) * 2 * hd, 2 * hd), 2 * hd) for p in range(n_p)]
        q_parts = []
        for s in range(2):
            for p in range(n_p):
                q2 = q_ref[0, s * t:(s + 1) * t, lanes[p]]
                q_parts += [jnp.where(in_a, q2, 0), jnp.where(in_a, 0, q2)]
        qs = jnp.concatenate(q_parts, axis=0)

        def scores(j, buf, subs, diag):
            lb_ref, sp_ref, rs_ref = buf
            r0, r1 = subs[0] * sub, subs[1] * sub
            rows_k = pl.ds(pl.multiple_of(j * t, t), t)
            k2 = [k_ref[0, rows_k, lanes[p]] for p in range(n_p)]
            z = jnp.concatenate(
                [_dot_nt(qs[(s * n_p + p) * 2 * t:(s * n_p + p + 1) * 2 * t], k2[p])
                 for s in range(*subs) for p in range(n_p)], axis=0)
            neg_abs = lax.bitcast_convert_type(lax.bitcast_convert_type(z, jnp.uint32) | sign_bit, F32)
            lb = jnp.minimum(z, 0.0) - jnp.log2(1.0 + jnp.exp2(neg_abs))
            sp = z - lb
            if diag:
                sp, lb = mask_diagonal(sp, 0.0), mask_diagonal(lb, -jnp.inf)
            lb_ref[r0:r1, :] = lb
            sp_ref[r0:r1, :] = sp.astype(BF16)
            rs_ref[r0:r1, :] = jnp.broadcast_to(jnp.sum(sp, axis=-1, keepdims=True), (r1 - r0, LANES))

        def values(j, buf, subs):
            lb_ref, sp_ref, rs_ref = buf
            r0, r1 = subs[0] * sub, subs[1] * sub
            rows_k = pl.ds(pl.multiple_of(j * t, t), t)
            c = c_ref[r0:r1, :]
            tail = _dot(sp_ref[r0:r1, :], tri)
            c_wide = jnp.concatenate([c] * (t // LANES), axis=1)
            att = jnp.exp2(lb_ref[r0:r1, :] - tail - c_wide).astype(BF16)
            c_ref[r0:r1, :] = c + rs_ref[r0:r1, :]
            for p in range(n_p):
                blocks = []
                for s in range(subs[1] - subs[0]):
                    base = (s * n_p + p) * 2 * t
                    blocks.append(jnp.concatenate([att[base:base + t], att[base + t:base + 2 * t]], axis=1))
                v2 = v_ref[0, rows_k, lanes[p]]
                vv = jnp.concatenate([jnp.where(in_a, v2, 0), jnp.where(in_a, 0, v2)], axis=0)
                acc_ref[p, subs[0] * t:subs[1] * t, :] += _dot(jnp.concatenate(blocks, axis=0), vv)

        def tile_pair(state):
            i = state[0]
            j = top - 3 - 2 * i
            values(j, buf_b, both)
            scores(j - 1, buf_a, both, False)
            values(j - 1, buf_a, both)
            scores(j - 2, buf_b, both, False)
            return i + 1, jnp.min(c_ref[...])

        def live(state):
            return (state[0] < qi - 1) & (state[1] < ATT_DEAD_CARRY)

        acc_ref[...] = jnp.zeros_like(acc_ref)
        c_ref[...] = jnp.zeros_like(c_ref)

        @pl.when(qi == 0)
        def _():
            scores(top, buf_a, second, True)
            values(top, buf_a, second)
            scores(top - 1, buf_b, both, True)
            values(top - 1, buf_b, both)

        @pl.when(qi > 0)
        def _():
            scores(top, buf_a, second, True)
            scores(top - 1, buf_b, both, True)
            values(top, buf_a, second)
            scores(top - 2, buf_a, first, False)
            values(top - 1, buf_b, both)
            c_min0 = jnp.minimum(jnp.min(c_ref[sub:, :]), jnp.min(c_ref[:sub, :] + buf_a[2][:sub, :]))
            values(top - 2, buf_a, first)

            @pl.when(c_min0 < ATT_DEAD_CARRY)
            def _():
                scores(top - 2, buf_a, second, False)
                values(top - 2, buf_a, second)
                scores(top - 3, buf_b, both, False)
                n_pairs, c_min = lax.while_loop(live, tile_pair, (jnp.int32(0), c_min0))

                @pl.when(c_min < ATT_DEAD_CARRY)
                def _():
                    values(top - 3 - 2 * n_pairs, buf_b, both)

        for p in range(n_p):
            o_ref[0, :, lanes[p]] = acc_ref[p].astype(o_ref.dtype)
        return 0

    lax.fori_loop(0, q_ref.shape[2] // (2 * hd * n_p), pair_group, 0)


def _sb_attention(q, k, v):
    b, l, d = q.shape
    t, n_p = ATT_T, ATT_PAIRS_TOGETHER
    tq = 2 * t
    hp = 2 * SB_HEAD_DIM * ATT_PAIRS_PER_STEP
    assert l % tq == 0 and d % hp == 0 and 2 * SB_HEAD_DIM == LANES and ATT_PAIRS_PER_STEP % n_p == 0
    rows = 4 * n_p * t
    stage = [pltpu.VMEM((rows, t), F32), pltpu.VMEM((rows, t), BF16), pltpu.VMEM((rows, LANES), F32)]
    return pl.pallas_call(
        functools.partial(_sb_attn_kernel, t=t, n_together=n_p),
        grid=(b, d // hp, l // tq),
        in_specs=[
            pl.BlockSpec((1, tq, hp), lambda bi, hi, qi: (bi, qi, hi)),
            pl.BlockSpec((1, l, hp), lambda bi, hi, qi: (bi, 0, hi)),
            pl.BlockSpec((1, l, hp), lambda bi, hi, qi: (bi, 0, hi)),
        ],
        out_specs=pl.BlockSpec((1, tq, hp), lambda bi, hi, qi: (bi, qi, hi)),
        out_shape=jax.ShapeDtypeStruct((b, l, d), BF16),
        scratch_shapes=stage + stage + [pltpu.VMEM((n_p, tq, LANES), F32),
                                        pltpu.VMEM((rows, LANES), F32)],
        compiler_params=_params(3),
        name="sb_attn",
    )(q, k, v)


def _sb_mixer(x, b, l, norm_g, w_qkv, layer):
    t, d = x.shape
    scale = LOG2_E * SB_HEAD_DIM ** -0.5
    q, k, v = _norm_proj(x, norm_g, w_qkv, layer,
                         [(0, d, scale), (d, d, 1.0), (2 * d, d, 1.0)], [BF16] * 3)
    o = _sb_attention(q.reshape(b, l, d), k.reshape(b, l, d), v.reshape(b, l, d))
    return o.reshape(t, d)


def _split3(x):
    hi = x.astype(BF16).astype(F32)
    rest = x - hi
    mid = rest.astype(BF16).astype(F32)
    return hi, mid, (rest - mid).astype(BF16).astype(F32)


def _head_selector(width):
    k = lax.broadcasted_iota(jnp.int32, (3 * SUBLANES, SSD_HPG * width), 0) % SUBLANES
    lane = lax.broadcasted_iota(jnp.int32, (3 * SUBLANES, SSD_HPG * width), 1)
    return jnp.where((lane >= k * width) & (lane < (k + 1) * width), 1.0, 0.0).astype(BF16)


def _ssd_kernel(xr_ref, br_ref, cr_ref, z_ref, dt_ref, dtn_ref, wx_ref, wb_ref, wc_ref, bx_ref,
                bb_ref, bc_ref, dtb_ref, alog_ref, dsk_ref, ng_ref, o_ref, h_ref, hist_ref,
                steps_row_ref, steps_parts_ref, *, n_groups, n_chunks):
    cl, hp, n = SSD_CHUNK, SSD_HPG * SSD_HEAD_DIM, SSD_STATE
    rows = n_groups * SUBLANES
    row = lax.broadcasted_iota(jnp.int32, (cl, cl), 0)
    col = lax.broadcasted_iota(jnp.int32, (cl, cl), 1)
    causal = col <= row
    lane_hp = lax.broadcasted_iota(jnp.int32, (cl, hp), 1)
    to_chunk_lanes = _head_selector(cl)
    to_head_lanes = _head_selector(SSD_HEAD_DIM)

    def stage_step_sizes(dt_blk):
        dt_in = (dt_blk + dtb_ref[...]).reshape(rows, cl)
        dt_all = jnp.maximum(dt_in, 0.0) + jnp.log(1.0 + jnp.exp(-jnp.abs(dt_in)))
        a_all = dt_all * (-jnp.exp(alog_ref[...].reshape(rows, 1)))
        lane_cl = lax.broadcasted_iota(jnp.int32, (rows, cl), 1)
        sh = 1
        while sh < cl:
            a_all = a_all + jnp.where(lane_cl >= sh, pltpu.roll(a_all, sh, axis=1), 0.0)
            sh *= 2
        te_all = jnp.exp(a_all[:, cl - 1:cl] - a_all) * dt_all
        a2 = a_all * LOG2_E
        steps_row_ref[0] = a2 - jnp.log2(dt_all)
        for i, part in enumerate(_split3(a2) + _split3(jnp.exp(a_all)) + _split3(te_all)):
            steps_parts_ref[i] = part

    def per_position(first_part, g, selector):
        parts = [steps_parts_ref[first_part + i, g * SUBLANES:(g + 1) * SUBLANES, :] for i in range(3)]
        return _dot_tn(jnp.concatenate(parts, axis=0).astype(BF16), selector)

    def dt_chunk(ref, ci):
        return ref[0, :, :, pl.ds(pl.multiple_of(ci * cl, cl), cl)]

    def chunk(ci, _):
        tok = pl.ds(pl.multiple_of(ci * cl, cl), cl)

        @pl.when((pl.program_id(2) == 0) & (ci == 0))
        def _():
            h_ref[...] = jnp.zeros_like(h_ref)
            hist_ref[...] = jnp.zeros_like(hist_ref)
            stage_step_sizes(dt_chunk(dt_ref, 0))

        src2_all = steps_row_ref[0]
        a_cols = [per_position(0, g, to_chunk_lanes) for g in range(n_groups)]
        decay_in = [per_position(3, g, to_head_lanes) for g in range(n_groups)]
        decay_out = [per_position(6, g, to_head_lanes) for g in range(n_groups)]
        dt_next = jnp.where(ci < n_chunks - 1, dt_chunk(dt_ref, jnp.minimum(ci + 1, n_chunks - 1)),
                            dt_chunk(dtn_ref, 0))
        stage_step_sizes(dt_next)

        raw = jnp.concatenate([xr_ref[tok, :], br_ref[tok, :], cr_ref[tok, :]], axis=1)
        ext = jnp.concatenate([hist_ref[...], raw], axis=0)
        hist_ref[...] = raw[cl - HALO:, :]
        cw = 0.5 * jnp.concatenate([wx_ref[...], wb_ref[...], wc_ref[...]], axis=1)
        acc = cw[0:1, :] * ext
        for kk in range(1, SSD_CONV):
            acc = cw[kk:kk + 1, :] * ext + pltpu.roll(acc, 1, axis=0)
        bias = 0.5 * jnp.concatenate([bx_ref[...], bb_ref[...], bc_ref[...]], axis=1)
        act = _silu_of_half(acc[HALO:, :] + bias)
        xs_all = act[:, :n_groups * hp]
        bm_all = act[:, n_groups * hp:n_groups * (hp + n)].astype(BF16)
        cm_all = act[:, n_groups * (hp + n):].astype(BF16)

        for g in range(n_groups):
            xs = xs_all[:, g * hp:(g + 1) * hp]
            bm = bm_all[:, g * n:(g + 1) * n]
            cm = cm_all[:, g * n:(g + 1) * n]
            src2 = src2_all[g * SUBLANES:(g + 1) * SUBLANES]

            cb = _dot_nt(cm, bm)
            ws, xblocks = [], []
            for r in range(SSD_HPG):
                seg = a_cols[g][:, r * cl:(r + 1) * cl] - src2[r:r + 1, :]
                ws.append((cb * jnp.exp2(jnp.where(causal, seg, -jnp.inf))).astype(BF16))
                in_head = (lane_hp >= r * SSD_HEAD_DIM) & (lane_hp < (r + 1) * SSD_HEAD_DIM)
                xblocks.append(jnp.where(in_head, xs, 0.0).astype(BF16))
            y = _dot(jnp.concatenate(ws, axis=1), jnp.concatenate(xblocks, axis=0))

            h_prev = h_ref[g]
            y = y + _dot(cm, h_prev.astype(BF16)) * decay_in[g]
            xw = (xs * decay_out[g]).astype(BF16)
            h_ref[g] = h_prev * decay_in[g][cl - 1:cl, :] + _dot_tn(bm, xw)

            y = (y + dsk_ref[g] * xs) * _silu_of_half(z_ref[tok, g * hp:(g + 1) * hp])
            o_ref[tok, g * hp:(g + 1) * hp] = _rms(y, ng_ref[g]).astype(o_ref.dtype)
        return 0

    lax.fori_loop(0, n_chunks, chunk, 0)


def _ssd_core(xbc, z, dt_t, conv_w, conv_b, dt_bias, a_log, d_skip, norm_g, b, l):
    t = xbc.shape[0]
    g, hpg, cl, n, ng = SSD_GROUPS, SSD_HPG, SSD_CHUNK, SSD_STATE, SSD_GROUPS_PER_STEP
    hp = hpg * SSD_HEAD_DIM
    d_inner = g * hp
    kc = SSD_CHUNKS_PER_STEP
    tl = kc * cl
    nc = l // tl
    assert l % tl == 0 and hp % LANES == 0 and n == LANES and g % ng == 0
    bb, cb = d_inner // (ng * n), (d_inner + g * n) // (ng * n)
    pad = lambda p: jnp.pad(p.reshape(g, hpg, 1), ((0, 0), (0, SUBLANES - hpg), (0, 0)))
    per_lane = lambda p: jnp.repeat(p.reshape(g, hpg), SSD_HEAD_DIM, axis=1).reshape(g, 1, hp)
    conv_b = conv_b.reshape(1, -1)
    tok = lambda bi, gi, ci: bi * nc + ci
    return pl.pallas_call(
        functools.partial(_ssd_kernel, n_groups=ng, n_chunks=kc),
        grid=(b, g // ng, nc),
        in_specs=[
            pl.BlockSpec((tl, ng * hp), lambda bi, gi, ci: (tok(bi, gi, ci), gi)),
            pl.BlockSpec((tl, ng * n), lambda bi, gi, ci: (tok(bi, gi, ci), bb + gi)),
            pl.BlockSpec((tl, ng * n), lambda bi, gi, ci: (tok(bi, gi, ci), cb + gi)),
            pl.BlockSpec((tl, ng * hp), lambda bi, gi, ci: (tok(bi, gi, ci), gi)),
            pl.BlockSpec((1, ng, SUBLANES, tl), lambda bi, gi, ci: (bi, gi, 0, ci)),
            pl.BlockSpec((1, ng, SUBLANES, tl), lambda bi, gi, ci: (bi, gi, 0, jnp.minimum(ci + 1, nc - 1))),
            pl.BlockSpec((SSD_CONV, ng * hp), lambda bi, gi, ci: (0, gi)),
            pl.BlockSpec((SSD_CONV, ng * n), lambda bi, gi, ci: (0, bb + gi)),
            pl.BlockSpec((SSD_CONV, ng * n), lambda bi, gi, ci: (0, cb + gi)),
            pl.BlockSpec((1, ng * hp), lambda bi, gi, ci: (0, gi)),
            pl.BlockSpec((1, ng * n), lambda bi, gi, ci: (0, bb + gi)),
            pl.BlockSpec((1, ng * n), lambda bi, gi, ci: (0, cb + gi)),
            pl.BlockSpec((ng, SUBLANES, 1), lambda bi, gi, ci: (gi, 0, 0)),
            pl.BlockSpec((ng, SUBLANES, 1), lambda bi, gi, ci: (gi, 0, 0)),
            pl.BlockSpec((ng, 1, hp), lambda bi, gi, ci: (gi, 0, 0)),
            pl.BlockSpec((ng, 1, hp), lambda bi, gi, ci: (gi, 0, 0)),
        ],
        out_specs=pl.BlockSpec((tl, ng * hp), lambda bi, gi, ci: (tok(bi, gi, ci), gi)),
        out_shape=jax.ShapeDtypeStruct((t, d_inner), BF16),
        scratch_shapes=[pltpu.VMEM((ng, n, hp), F32), pltpu.VMEM((HALO, ng * (hp + 2 * n)), F32),
                        pltpu.VMEM((1, ng * SUBLANES, cl), F32), pltpu.VMEM((9, ng * SUBLANES, cl), F32)],
        compiler_params=_params(3),
        name="ssd_core",
    )(xbc, xbc, xbc, z, dt_t, dt_t, conv_w, conv_w, conv_w, conv_b, conv_b, conv_b,
      pad(dt_bias), pad(a_log), per_lane(d_skip), norm_g.reshape(g, 1, hp))


def _ssd_mixer(x, b, l, norm_g, w_in, conv_w, conv_b, dt_bias, a_log, d_skip, ssd_norm, layer):
    g, hpg = SSD_GROUPS, SSD_HPG
    d_inner = ssd_norm.shape[0]
    conv_dim = conv_w.shape[1]
    n_heads = g * hpg
    z, xbc, dt = _norm_proj(
        x, norm_g, w_in, layer,
        [(0, d_inner, 0.5), (d_inner, conv_dim, 1.0), (d_inner + conv_dim, n_heads, 1.0)], [F32] * 3)
    dt_t = dt.reshape(b, l, g, hpg).transpose(0, 2, 3, 1)
    dt_t = jnp.pad(dt_t, ((0, 0), (0, 0), (0, SUBLANES - hpg), (0, 0)))
    return _ssd_core(xbc, z, dt_t, conv_w, conv_b, dt_bias, a_log, d_skip, ssd_norm, b, l)


def _short_conv_kernel(x_ref, g_ref, wi_ref, cw_ref, wo_ref, o_ref, ext_ref):
    tm, d = x_ref.shape

    @pl.when(pl.program_id(1) == 0)
    def _():
        ext_ref[0:HALO, :] = jnp.zeros((HALO, ext_ref.shape[1]), F32)

    x = x_ref[...]
    xn = _rms(x, g_ref[...]).astype(BF16)
    w_part = lambda i: wi_ref[:, i * d:(i + 1) * d].astype(BF16)
    ext_ref[HALO:HALO + tm, :] = _dot(xn, w_part(1)) * _dot(xn, w_part(2))
    cw = cw_ref[...]
    u = cw[0:1, :] * ext_ref[pl.ds(HALO - (SC_WIDTH - 1), tm), :]
    for kk in range(1, SC_WIDTH):
        u = u + cw[kk:kk + 1, :] * ext_ref[pl.ds(HALO - (SC_WIDTH - 1) + kk, tm), :]
    ext_ref[0:HALO, :] = ext_ref[tm:tm + HALO, :]
    gated = (_dot(xn, w_part(0)) * u).astype(BF16)
    o_ref[...] = x + _dot(gated, wo_ref[...].astype(BF16))


def _short_conv_layer(x, b, l, norm_g, w_in, conv_w, w_out, layer):
    t, d = x.shape
    tm = SC_TM
    nl = l // tm
    assert l % tm == 0
    return pl.pallas_call(
        _short_conv_kernel,
        grid=(b, nl),
        in_specs=[
            pl.BlockSpec((tm, d), lambda bi, li: (bi * nl + li, 0)),
            _resident((1, d)),
            _resident((None, d, 3 * d), (layer, 0, 0)),
            _resident((None, SC_WIDTH, d), (layer, 0, 0)),
            _resident((None, d, d), (layer, 0, 0)),
        ],
        out_specs=pl.BlockSpec((tm, d), lambda bi, li: (bi * nl + li, 0)),
        out_shape=jax.ShapeDtypeStruct((t, d), F32),
        scratch_shapes=[pltpu.VMEM((tm + HALO, d), F32)],
        compiler_params=_params(2),
        name="short_conv",
    )(x, norm_g.reshape(1, d), w_in, conv_w, w_out)


def kernel(x, ffn1_norm, ffn1_w_gu, ffn1_w_down, mix_norm, ffn2_norm, ffn2_w_gu, ffn2_w_down,
           sb_w_qkv, sb_w_o, ssd_w_in, ssd_conv_w, ssd_conv_b, ssd_dt_bias, ssd_a_log, ssd_d,
           ssd_norm, ssd_w_out, sc_w_in, sc_conv_w, sc_w_out, final_norm):
    b, l, d = x.shape
    depth = ffn1_norm.shape[0]
    h = x.reshape(b * l, d)
    for i in range(depth):
        h = _ffn(h, ffn1_norm[i], ffn1_w_gu, ffn1_w_down, i)
        kind, j = i % N_MIXERS, i // N_MIXERS
        mixer_out = None
        if kind == 0:
            mixer_out = (_sb_mixer(h, b, l, mix_norm[i], sb_w_qkv, j), sb_w_o, j)
        elif kind == 1:
            y = _ssd_mixer(h, b, l, mix_norm[i], ssd_w_in, ssd_conv_w[j], ssd_conv_b[j],
                           ssd_dt_bias[j], ssd_a_log[j], ssd_d[j], ssd_norm[j], j)
            mixer_out = (y, ssd_w_out, j)
        else:
            h = _short_conv_layer(h, b, l, mix_norm[i], sc_w_in, sc_conv_w, sc_w_out, j)
        h = _ffn(h, ffn2_norm[i], ffn2_w_gu, ffn2_w_down, i,
                 final_g=final_norm if i == depth - 1 else None, mixer_out=mixer_out)
    return h.reshape(b, l, d)
```

```python
import functools

import jax
import jax.numpy as jnp
from jax import lax
from jax.experimental import pallas as pl
from jax.experimental.pallas import tpu as pltpu

F32 = jnp.float32
BF16 = jnp.bfloat16

RMS_EPS = 1e-6
LOG2_E = 1.4426950408889634
N_MIXERS = 3
SB_HEAD_DIM = 64
SSD_HEAD_DIM = 64
SSD_GROUPS = 8
SSD_HPG = 4
SSD_STATE = 128
SSD_CONV = 4
SSD_CHUNK = 128
SC_WIDTH = 3

LANES = 128
SUBLANES = 8
VMEM_LIMIT_BYTES = 60 * 1024 * 1024

FFN_TM = 1024
FFN_FUSED_TM = 512
FFN_TF = 256
PROJ_TM = 512
PROJ_TN = 512
ATT_T = 256
ATT_PAIRS_PER_STEP = 4
ATT_PAIRS_TOGETHER = 2
ATT_DEAD_CARRY = 160.0
SSD_GROUPS_PER_STEP = 4
SSD_CHUNKS_PER_STEP = 4
SC_TM = 512
HALO = SUBLANES


def _params(n_axes):
    return pltpu.CompilerParams(
        dimension_semantics=("arbitrary",) * n_axes,
        vmem_limit_bytes=VMEM_LIMIT_BYTES)


def _resident(shape, index=None):
    index = (0,) * len(shape) if index is None else index
    return pl.BlockSpec(shape, lambda *_: index, pipeline_mode=pl.Buffered(1))


def _rms(x, g):
    ms = jnp.mean(x * x, axis=-1, keepdims=True)
    return x * lax.rsqrt(ms + RMS_EPS) * g


def _silu(x):
    h = 0.5 * x
    return h + h * jnp.tanh(h)


def _dot(a, b):
    return jnp.dot(a, b, preferred_element_type=F32)


def _dot_nt(a, b):
    return lax.dot_general(a, b, (((1,), (1,)), ((), ())), preferred_element_type=F32)


def _dot_tn(a, b):
    return lax.dot_general(a, b, (((0,), (0,)), ((), ())), preferred_element_type=F32)


def _ffn_kernel(x_ref, g_ref, wgu_ref, wd_ref, *refs, n_chunks, tf, final_norm, mixer_out):
    refs = list(refs)
    y_ref, wo_ref = (refs.pop(0), refs.pop(0)) if mixer_out else (None, None)
    fg_ref = refs[0] if final_norm else None
    o_ref, xn_ref, acc_ref = refs[-3:]
    x = x_ref[...]
    if mixer_out:
        x = x + _dot(y_ref[...], wo_ref[...].astype(BF16))
    d_ff = n_chunks * tf
    xn_ref[...] = _rms(x, g_ref[...]).astype(BF16)
    for c in range(n_chunks):
        gate = _dot(xn_ref[...], wgu_ref[:, c * tf:(c + 1) * tf].astype(BF16))
        up = _dot(xn_ref[...], wgu_ref[:, d_ff + c * tf:d_ff + (c + 1) * tf].astype(BF16))
        h = (_silu(gate) * up).astype(BF16)
        part = _dot(h, wd_ref[c * tf:(c + 1) * tf, :].astype(BF16))
        if c == 0:
            acc_ref[...] = part
        else:
            acc_ref[...] += part
    y = x + 0.5 * acc_ref[...]
    if final_norm:
        y = _rms(y, fg_ref[...])
    o_ref[...] = y


def _ffn(x, norm_g, w_gu, w_down, layer, final_g=None, mixer_out=None):
    t, d = x.shape
    d_ff = w_down.shape[1]
    tm, tf = (FFN_TM if mixer_out is None else FFN_FUSED_TM), FFN_TF
    n_chunks = d_ff // tf
    assert t % tm == 0 and d_ff % tf == 0
    final_norm = final_g is not None
    extra = [final_g.reshape(1, d)] if final_norm else []
    pre, pre_specs = [], []
    if mixer_out is not None:
        y, w_o, j = mixer_out
        k = y.shape[1]
        pre = [y, w_o]
        pre_specs = [pl.BlockSpec((tm, k), lambda i: (i, 0)), _resident((None, k, d), (j, 0, 0))]
    return pl.pallas_call(
        functools.partial(_ffn_kernel, n_chunks=n_chunks, tf=tf, final_norm=final_norm,
                          mixer_out=mixer_out is not None),
        grid=(t // tm,),
        in_specs=[
            pl.BlockSpec((tm, d), lambda i: (i, 0)),
            _resident((1, d)),
            _resident((None, d, 2 * d_ff), (layer, 0, 0)),
            _resident((None, d_ff, d), (layer, 0, 0)),
        ] + pre_specs + [_resident((1, d))] * len(extra),
        out_specs=pl.BlockSpec((tm, d), lambda i: (i, 0)),
        out_shape=jax.ShapeDtypeStruct((t, d), F32),
        scratch_shapes=[pltpu.VMEM((tm, d), BF16), pltpu.VMEM((tm, d), F32)],
        compiler_params=_params(1),
        name="ffn",
    )(x, norm_g.reshape(1, d), w_gu, w_down, *pre, *extra)


def _norm_proj_kernel(x_ref, g_ref, w_ref, *o_refs, segments, tn):
    xn = _rms(x_ref[...], g_ref[...]).astype(BF16)
    for o_ref, (start, width, scale) in zip(o_refs, segments):
        for n0 in range(0, width, tn):
            n1 = min(n0 + tn, width)
            w = w_ref[:, start + n0:start + n1]
            if scale != 1.0:
                w = w * scale
            o_ref[:, n0:n1] = _dot(xn, w.astype(BF16)).astype(o_ref.dtype)


def _norm_proj(x, norm_g, w, layer, segments, out_dtypes):
    t, d = x.shape
    tm = PROJ_TM
    assert t % tm == 0 and all(s[0] % LANES == 0 for s in segments)
    return pl.pallas_call(
        functools.partial(_norm_proj_kernel, segments=tuple(segments), tn=PROJ_TN),
        grid=(t // tm,),
        in_specs=[pl.BlockSpec((tm, d), lambda i: (i, 0)), _resident((1, d)),
                  _resident((None,) + w.shape[1:], (layer, 0, 0))],
        out_specs=[pl.BlockSpec((tm, s[1]), lambda i: (i, 0)) for s in segments],
        out_shape=[jax.ShapeDtypeStruct((t, s[1]), dt) for s, dt in zip(segments, out_dtypes)],
        compiler_params=_params(1),
        name="norm_proj",
    )(x, norm_g.reshape(1, d), w)


def _sb_attn_kernel(q_ref, k_ref, v_ref, o_ref, lb_a, sp_a, rs_a, lb_b, sp_b, rs_b, acc_ref, c_ref,
                    *, t, n_together):
    qi = pl.program_id(2)
    hd = SB_HEAD_DIM
    n_p = n_together
    sub = 2 * n_p * t
    buf_a, buf_b = (lb_a, sp_a, rs_a), (lb_b, sp_b, rs_b)
    both, first, second = (0, 2), (0, 1), (1, 2)
    in_a = lax.broadcasted_iota(jnp.int32, (1, 2 * hd), 1) < hd
    row = lax.broadcasted_iota(jnp.int32, (t, t), 0)
    col = lax.broadcasted_iota(jnp.int32, (t, t), 1)
    strict_lower = jnp.concatenate([col < row] * (2 * n_p), axis=0)
    tri = jnp.where(row > col, 1.0, 0.0).astype(BF16)
    sign_bit = jnp.uint32(0x80000000)
    top = 2 * qi + 1

    def mask_diagonal(x, fill):
        head = jnp.where(strict_lower, x[:sub], fill)
        return head if x.shape[0] == sub else jnp.concatenate([head, x[sub:]], axis=0)

    def pair_group(grp, _):
        lanes = [pl.ds(pl.multiple_of((grp * n_p + p) * 2 * hd, 2 * hd), 2 * hd) for p in range(n_p)]
        q_parts = []
        for s in range(2):
            for p in range(n_p):
                q2 = q_ref[0, s * t:(s + 1) * t, lanes[p]]
                q_parts += [jnp.where(in_a, q2, 0), jnp.where(in_a, 0, q2)]
        qs = jnp.concatenate(q_parts, axis=0)

        def scores(j, buf, subs, diag):
            lb_ref, sp_ref, rs_ref = buf
            r0, r1 = subs[0] * sub, subs[1] * sub
            rows_k = pl.ds(pl.multiple_of(j * t, t), t)
            k2 = [k_ref[0, rows_k, lanes[p]] for p in range(n_p)]
            z = jnp.concatenate(
                [_dot_nt(qs[(s * n_p + p) * 2 * t:(s * n_p + p + 1) * 2 * t], k2[p])
                 for s in range(*subs) for p in range(n_p)], axis=0)
            neg_abs = lax.bitcast_convert_type(lax.bitcast_convert_type(z, jnp.uint32) | sign_bit, F32)
            lb = jnp.minimum(z, 0.0) - jnp.log2(1.0 + jnp.exp2(neg_abs))
            sp = z - lb
            if diag:
                sp, lb = mask_diagonal(sp, 0.0), mask_diagonal(lb, -jnp.inf)
            lb_ref[r0:r1, :] = lb
            sp_ref[r0:r1, :] = sp.astype(BF16)
            rs_ref[r0:r1, :] = jnp.broadcast_to(jnp.sum(sp, axis=-1, keepdims=True), (r1 - r0, LANES))

        def values(j, buf, subs):
            lb_ref, sp_ref, rs_ref = buf
            r0, r1 = subs[0] * sub, subs[1] * sub
            rows_k = pl.ds(pl.multiple_of(j * t, t), t)
            c = c_ref[r0:r1, :]
            tail = _dot(sp_ref[r0:r1, :], tri)
            c_wide = jnp.concatenate([c] * (t // LANES), axis=1)
            att = jnp.exp2(lb_ref[r0:r1, :] - tail - c_wide).astype(BF16)
            c_ref[r0:r1, :] = c + rs_ref[r0:r1, :]
            for p in range(n_p):
                blocks = []
                for s in range(subs[1] - subs[0]):
                    base = (s * n_p + p) * 2 * t
                    blocks.append(jnp.concatenate([att[base:base + t], att[base + t:base + 2 * t]], axis=1))
                v2 = v_ref[0, rows_k, lanes[p]]
                vv = jnp.concatenate([jnp.where(in_a, v2, 0), jnp.where(in_a, 0, v2)], axis=0)
                acc_ref[p, subs[0] * t:subs[1] * t, :] += _dot(jnp.concatenate(blocks, axis=0), vv)

        def tile_pair(state):
            i = state[0]
            j = top - 3 - 2 * i
            values(j, buf_b, both)
            scores(j - 1, buf_a, both, False)
            values(j - 1, buf_a, both)
            scores(j - 2, buf_b, both, False)
            return i + 1, jnp.min(c_ref[...])

        def live(state):
            return (state[0] < qi - 1) & (state[1] < ATT_DEAD_CARRY)

        acc_ref[...] = jnp.zeros_like(acc_ref)
        c_ref[...] = jnp.zeros_like(c_ref)

        @pl.when(qi == 0)
        def _():
            scores(top, buf_a, second, True)
            values(top, buf_a, second)
            scores(top - 1, buf_b, both, True)
            values(top - 1, buf_b, both)

        @pl.when(qi > 0)
        def _():
            scores(top, buf_a, second, True)
            scores(top - 1, buf_b, both, True)
            values(top, buf_a, second)
            scores(top - 2, buf_a, first, False)
            values(top - 1, buf_b, both)
            c_min0 = jnp.minimum(jnp.min(c_ref[sub:, :]), jnp.min(c_ref[:sub, :] + buf_a[2][:sub, :]))
            values(top - 2, buf_a, first)

            @pl.when(c_min0 < ATT_DEAD_CARRY)
            def _():
                scores(top - 2, buf_a, second, False)
                values(top - 2, buf_a, second)
                scores(top - 3, buf_b, both, False)
                n_pairs, c_min = lax.while_loop(live, tile_pair, (jnp.int32(0), c_min0))

                @pl.when(c_min < ATT_DEAD_CARRY)
                def _():
                    values(top - 3 - 2 * n_pairs, buf_b, both)

        for p in range(n_p):
            o_ref[0, :, lanes[p]] = acc_ref[p].astype(o_ref.dtype)
        return 0

    lax.fori_loop(0, q_ref.shape[2] // (2 * hd * n_p), pair_group, 0)


def _sb_attention(q, k, v):
    b, l, d = q.shape
    t, n_p = ATT_T, ATT_PAIRS_TOGETHER
    tq = 2 * t
    hp = 2 * SB_HEAD_DIM * ATT_PAIRS_PER_STEP
    assert l % tq == 0 and d % hp == 0 and 2 * SB_HEAD_DIM == LANES and ATT_PAIRS_PER_STEP % n_p == 0
    rows = 4 * n_p * t
    stage = [pltpu.VMEM((rows, t), F32), pltpu.VMEM((rows, t), BF16), pltpu.VMEM((rows, LANES), F32)]
    return pl.pallas_call(
        functools.partial(_sb_attn_kernel, t=t, n_together=n_p),
        grid=(b, d // hp, l // tq),
        in_specs=[
            pl.BlockSpec((1, tq, hp), lambda bi, hi, qi: (bi, qi, hi)),
            pl.BlockSpec((1, l, hp), lambda bi, hi, qi: (bi, 0, hi)),
            pl.BlockSpec((1, l, hp), lambda bi, hi, qi: (bi, 0, hi)),
        ],
        out_specs=pl.BlockSpec((1, tq, hp), lambda bi, hi, qi: (bi, qi, hi)),
        out_shape=jax.ShapeDtypeStruct((b, l, d), BF16),
        scratch_shapes=stage + stage + [pltpu.VMEM((n_p, tq, LANES), F32),
                                        pltpu.VMEM((rows, LANES), F32)],
        compiler_params=_params(3),
        name="sb_attn",
    )(q, k, v)


def _sb_mixer(x, b, l, norm_g, w_qkv, layer):
    t, d = x.shape
    scale = LOG2_E * SB_HEAD_DIM ** -0.5
    q, k, v = _norm_proj(x, norm_g, w_qkv, layer,
                         [(0, d, scale), (d, d, 1.0), (2 * d, d, 1.0)], [BF16] * 3)
    o = _sb_attention(q.reshape(b, l, d), k.reshape(b, l, d), v.reshape(b, l, d))
    return o.reshape(t, d)


def _split3(x):
    hi = x.astype(BF16).astype(F32)
    rest = x - hi
    mid = rest.astype(BF16).astype(F32)
    return hi, mid, (rest - mid).astype(BF16).astype(F32)


def _head_selector(width):
    k = lax.broadcasted_iota(jnp.int32, (3 * SUBLANES, SSD_HPG * width), 0) % SUBLANES
    lane = lax.broadcasted_iota(jnp.int32, (3 * SUBLANES, SSD_HPG * width), 1)
    return jnp.where((lane >= k * width) & (lane < (k + 1) * width), 1.0, 0.0).astype(BF16)


def _ssd_kernel(xr_ref, br_ref, cr_ref, z_ref, dt_ref, dtn_ref, wx_ref, wb_ref, wc_ref, bx_ref,
                bb_ref, bc_ref, dtb_ref, alog_ref, dsk_ref, ng_ref, o_ref, h_ref, hist_ref,
                steps_row_ref, steps_parts_ref, *, n_groups, n_chunks):
    cl, hp, n = SSD_CHUNK, SSD_HPG * SSD_HEAD_DIM, SSD_STATE
    rows = n_groups * SUBLANES
    row = lax.broadcasted_iota(jnp.int32, (cl, cl), 0)
    col = lax.broadcasted_iota(jnp.int32, (cl, cl), 1)
    causal = col <= row
    lane_hp = lax.broadcasted_iota(jnp.int32, (cl, hp), 1)
    to_chunk_lanes = _head_selector(cl)
    to_head_lanes = _head_selector(SSD_HEAD_DIM)

    def stage_step_sizes(dt_blk):
        dt_in = (dt_blk + dtb_ref[...]).reshape(rows, cl)
        dt_all = jnp.maximum(dt_in, 0.0) + jnp.log(1.0 + jnp.exp(-jnp.abs(dt_in)))
        a_all = dt_all * (-jnp.exp(alog_ref[...].reshape(rows, 1)))
        lane_cl = lax.broadcasted_iota(jnp.int32, (rows, cl), 1)
        sh = 1
        while sh < cl:
            a_all = a_all + jnp.where(lane_cl >= sh, pltpu.roll(a_all, sh, axis=1), 0.0)
            sh *= 2
        te_all = jnp.exp(a_all[:, cl - 1:cl] - a_all) * dt_all
        a2 = a_all * LOG2_E
        steps_row_ref[0] = a2 - jnp.log2(dt_all)
        for i, part in enumerate(_split3(a2) + _split3(jnp.exp(a_all)) + _split3(te_all)):
            steps_parts_ref[i] = part

    def per_position(first_part, g, selector):
        parts = [steps_parts_ref[first_part + i, g * SUBLANES:(g + 1) * SUBLANES, :] for i in range(3)]
        return _dot_tn(jnp.concatenate(parts, axis=0).astype(BF16), selector)

    def dt_chunk(ref, ci):
        return ref[0, :, :, pl.ds(pl.multiple_of(ci * cl, cl), cl)]

    def chunk(ci, _):
        tok = pl.ds(pl.multiple_of(ci * cl, cl), cl)

        @pl.when((pl.program_id(2) == 0) & (ci == 0))
        def _():
            h_ref[...] = jnp.zeros_like(h_ref)
            hist_ref[...] = jnp.zeros_like(hist_ref)
            stage_step_sizes(dt_chunk(dt_ref, 0))

        src2_all = steps_row_ref[0]
        a_cols = [per_position(0, g, to_chunk_lanes) for g in range(n_groups)]
        decay_in = [per_position(3, g, to_head_lanes) for g in range(n_groups)]
        decay_out = [per_position(6, g, to_head_lanes) for g in range(n_groups)]
        dt_next = jnp.where(ci < n_chunks - 1, dt_chunk(dt_ref, jnp.minimum(ci + 1, n_chunks - 1)),
                            dt_chunk(dtn_ref, 0))
        stage_step_sizes(dt_next)

        raw = jnp.concatenate([xr_ref[tok, :], br_ref[tok, :], cr_ref[tok, :]], axis=1)
        ext = jnp.concatenate([hist_ref[...], raw], axis=0)
        hist_ref[...] = raw[cl - HALO:, :]
        cw = jnp.concatenate([wx_ref[...], wb_ref[...], wc_ref[...]], axis=1)
        acc = cw[0:1, :] * ext
        for kk in range(1, SSD_CONV):
            acc = cw[kk:kk + 1, :] * ext + pltpu.roll(acc, 1, axis=0)
        bias = jnp.concatenate([bx_ref[...], bb_ref[...], bc_ref[...]], axis=1)
        act = _silu(acc[HALO:, :] + bias)
        xs_all = act[:, :n_groups * hp]
        bm_all = act[:, n_groups * hp:n_groups * (hp + n)].astype(BF16)
        cm_all = act[:, n_groups * (hp + n):].astype(BF16)

        for g in range(n_groups):
            xs = xs_all[:, g * hp:(g + 1) * hp]
            bm = bm_all[:, g * n:(g + 1) * n]
            cm = cm_all[:, g * n:(g + 1) * n]
            src2 = src2_all[g * SUBLANES:(g + 1) * SUBLANES]

            cb = _dot_nt(cm, bm)
            ws, xblocks = [], []
            for r in range(SSD_HPG):
                seg = a_cols[g][:, r * cl:(r + 1) * cl] - src2[r:r + 1, :]
                ws.append((cb * jnp.exp2(jnp.where(causal, seg, -jnp.inf))).astype(BF16))
                in_head = (lane_hp >= r * SSD_HEAD_DIM) & (lane_hp < (r + 1) * SSD_HEAD_DIM)
                xblocks.append(jnp.where(in_head, xs, 0.0).astype(BF16))
            y = _dot(jnp.concatenate(ws, axis=1), jnp.concatenate(xblocks, axis=0))

            h_prev = h_ref[g]
            y = y + _dot(cm, h_prev.astype(BF16)) * decay_in[g]
            xw = (xs * decay_out[g]).astype(BF16)
            h_ref[g] = h_prev * decay_in[g][cl - 1:cl, :] + _dot_tn(bm, xw)

            y = (y + dsk_ref[g] * xs) * _silu(z_ref[tok, g * hp:(g + 1) * hp])
            o_ref[tok, g * hp:(g + 1) * hp] = _rms(y, ng_ref[g]).astype(o_ref.dtype)
        return 0

    lax.fori_loop(0, n_chunks, chunk, 0, unroll=True)


def _ssd_core(xbc, z, dt_t, conv_w, conv_b, dt_bias, a_log, d_skip, norm_g, b, l):
    t = xbc.shape[0]
    g, hpg, cl, n, ng = SSD_GROUPS, SSD_HPG, SSD_CHUNK, SSD_STATE, SSD_GROUPS_PER_STEP
    hp = hpg * SSD_HEAD_DIM
    d_inner = g * hp
    kc = SSD_CHUNKS_PER_STEP
    tl = kc * cl
    nc = l // tl
    assert l % tl == 0 and hp % LANES == 0 and n == LANES and g % ng == 0
    bb, cb = d_inner // (ng * n), (d_inner + g * n) // (ng * n)
    pad = lambda p: jnp.pad(p.reshape(g, hpg, 1), ((0, 0), (0, SUBLANES - hpg), (0, 0)))
    per_lane = lambda p: jnp.repeat(p.reshape(g, hpg), SSD_HEAD_DIM, axis=1).reshape(g, 1, hp)
    conv_b = conv_b.reshape(1, -1)
    tok = lambda bi, gi, ci: bi * nc + ci
    return pl.pallas_call(
        functools.partial(_ssd_kernel, n_groups=ng, n_chunks=kc),
        grid=(b, g // ng, nc),
        in_specs=[
            pl.BlockSpec((tl, ng * hp), lambda bi, gi, ci: (tok(bi, gi, ci), gi)),
            pl.BlockSpec((tl, ng * n), lambda bi, gi, ci: (tok(bi, gi, ci), bb + gi)),
            pl.BlockSpec((tl, ng * n), lambda bi, gi, ci: (tok(bi, gi, ci), cb + gi)),
            pl.BlockSpec((tl, ng * hp), lambda bi, gi, ci: (tok(bi, gi, ci), gi)),
            pl.BlockSpec((1, ng, SUBLANES, tl), lambda bi, gi, ci: (bi, gi, 0, ci)),
            pl.BlockSpec((1, ng, SUBLANES, tl), lambda bi, gi, ci: (bi, gi, 0, jnp.minimum(ci + 1, nc - 1))),
            pl.BlockSpec((SSD_CONV, ng * hp), lambda bi, gi, ci: (0, gi)),
            pl.BlockSpec((SSD_CONV, ng * n), lambda bi, gi, ci: (0, bb + gi)),
            pl.BlockSpec((SSD_CONV, ng * n), lambda bi, gi, ci: (0, cb + gi)),
            pl.BlockSpec((1, ng * hp), lambda bi, gi, ci: (0, gi)),
            pl.BlockSpec((1, ng * n), lambda bi, gi, ci: (0, bb + gi)),
            pl.BlockSpec((1, ng * n), lambda bi, gi, ci: (0, cb + gi)),
            pl.BlockSpec((ng, SUBLANES, 1), lambda bi, gi, ci: (gi, 0, 0)),
            pl.BlockSpec((ng, SUBLANES, 1), lambda bi, gi, ci: (gi, 0, 0)),
            pl.BlockSpec((ng, 1, hp), lambda bi, gi, ci: (gi, 0, 0)),
            pl.BlockSpec((ng, 1, hp), lambda bi, gi, ci: (gi, 0, 0)),
        ],
        out_specs=pl.BlockSpec((tl, ng * hp), lambda bi, gi, ci: (tok(bi, gi, ci), gi)),
        out_shape=jax.ShapeDtypeStruct((t, d_inner), BF16),
        scratch_shapes=[pltpu.VMEM((ng, n, hp), F32), pltpu.VMEM((HALO, ng * (hp + 2 * n)), F32),
                        pltpu.VMEM((1, ng * SUBLANES, cl), F32), pltpu.VMEM((9, ng * SUBLANES, cl), F32)],
        compiler_params=_params(3),
        name="ssd_core",
    )(xbc, xbc, xbc, z, dt_t, dt_t, conv_w, conv_w, conv_w, conv_b, conv_b, conv_b,
      pad(dt_bias), pad(a_log), per_lane(d_skip), norm_g.reshape(g, 1, hp))


def _ssd_mixer(x, b, l, norm_g, w_in, conv_w, conv_b, dt_bias, a_log, d_skip, ssd_norm, layer):
    g, hpg = SSD_GROUPS, SSD_HPG
    d_inner = ssd_norm.shape[0]
    conv_dim = conv_w.shape[1]
    n_heads = g * hpg
    z, xbc, dt = _norm_proj(
        x, norm_g, w_in, layer,
        [(0, d_inner, 1.0), (d_inner, conv_dim, 1.0), (d_inner + conv_dim, n_heads, 1.0)], [F32] * 3)
    dt_t = dt.reshape(b, l, g, hpg).transpose(0, 2, 3, 1)
    dt_t = jnp.pad(dt_t, ((0, 0), (0, 0), (0, SUBLANES - hpg), (0, 0)))
    return _ssd_core(xbc, z, dt_t, conv_w, conv_b, dt_bias, a_log, d_skip, ssd_norm, b, l)


def _short_conv_kernel(x_ref, g_ref, wi_ref, cw_ref, wo_ref, o_ref, ext_ref):
    tm, d = x_ref.shape

    @pl.when(pl.program_id(1) == 0)
    def _():
        ext_ref[0:HALO, :] = jnp.zeros((HALO, ext_ref.shape[1]), F32)

    x = x_ref[...]
    xn = _rms(x, g_ref[...]).astype(BF16)
    w_part = lambda i: wi_ref[:, i * d:(i + 1) * d].astype(BF16)
    ext_ref[HALO:HALO + tm, :] = _dot(xn, w_part(1)) * _dot(xn, w_part(2))
    cw = cw_ref[...]
    u = cw[0:1, :] * ext_ref[pl.ds(HALO - (SC_WIDTH - 1), tm), :]
    for kk in range(1, SC_WIDTH):
        u = u + cw[kk:kk + 1, :] * ext_ref[pl.ds(HALO - (SC_WIDTH - 1) + kk, tm), :]
    ext_ref[0:HALO, :] = ext_ref[tm:tm + HALO, :]
    gated = (_dot(xn, w_part(0)) * u).astype(BF16)
    o_ref[...] = x + _dot(gated, wo_ref[...].astype(BF16))


def _short_conv_layer(x, b, l, norm_g, w_in, conv_w, w_out, layer):
    t, d = x.shape
    tm = SC_TM
    nl = l // tm
    assert l % tm == 0
    return pl.pallas_call(
        _short_conv_kernel,
        grid=(b, nl),
        in_specs=[
            pl.BlockSpec((tm, d), lambda bi, li: (bi * nl + li, 0)),
            _resident((1, d)),
            _resident((None, d, 3 * d), (layer, 0, 0)),
            _resident((None, SC_WIDTH, d), (layer, 0, 0)),
            _resident((None, d, d), (layer, 0, 0)),
        ],
        out_specs=pl.BlockSpec((tm, d), lambda bi, li: (bi * nl + li, 0)),
        out_shape=jax.ShapeDtypeStruct((t, d), F32),
        scratch_shapes=[pltpu.VMEM((tm + HALO, d), F32)],
        compiler_params=_params(2),
        name="short_conv",
    )(x, norm_g.reshape(1, d), w_in, conv_w, w_out)


def kernel(x, ffn1_norm, ffn1_w_gu, ffn1_w_down, mix_norm, ffn2_norm, ffn2_w_gu, ffn2_w_down,
           sb_w_qkv, sb_w_o, ssd_w_in, ssd_conv_w, ssd_conv_b, ssd_dt_bias, ssd_a_log, ssd_d,
           ssd_norm, ssd_w_out, sc_w_in, sc_conv_w, sc_w_out, final_norm):
    b, l, d = x.shape
    depth = ffn1_norm.shape[0]
    h = x.reshape(b * l, d)
    for i in range(depth):
        h = _ffn(h, ffn1_norm[i], ffn1_w_gu, ffn1_w_down, i)
        kind, j = i % N_MIXERS, i // N_MIXERS
        mixer_out = None
        if kind == 0:
            mixer_out = (_sb_mixer(h, b, l, mix_norm[i], sb_w_qkv, j), sb_w_o, j)
        elif kind == 1:
            y = _ssd_mixer(h, b, l, mix_norm[i], ssd_w_in, ssd_conv_w[j], ssd_conv_b[j],
                           ssd_dt_bias[j], ssd_a_log[j], ssd_d[j], ssd_norm[j], j)
            mixer_out = (y, ssd_w_out, j)
        else:
            h = _short_conv_layer(h, b, l, mix_norm[i], sc_w_in, sc_conv_w, sc_w_out, j)
        h = _ffn(h, ffn2_norm[i], ffn2_w_gu, ffn2_w_down, i,
                 final_g=final_norm if i == depth - 1 else None, mixer_out=mixer_out)
    return h.reshape(b, l, d)
```

```python
import functools

import jax
import jax.numpy as jnp
from jax import lax
from jax.experimental import pallas as pl
from jax.experimental.pallas import tpu as pltpu

F32 = jnp.float32
BF16 = jnp.bfloat16

RMS_EPS = 1e-6
LOG2_E = 1.4426950408889634
N_MIXERS = 3
SB_HEAD_DIM = 64
SSD_HEAD_DIM = 64
SSD_GROUPS = 8
SSD_HPG = 4
SSD_STATE = 128
SSD_CONV = 4
SSD_CHUNK = 128
SC_WIDTH = 3

LANES = 128
SUBLANES = 8
VMEM_LIMIT_BYTES = 60 * 1024 * 1024

FFN_TM = 1024
FFN_FUSED_TM = 512
FFN_TF = 256
PROJ_TM = 512
PROJ_TN = 512
ATT_T = 256
ATT_PAIRS_PER_STEP = 4
ATT_PAIRS_TOGETHER = 2
ATT_DEAD_CARRY = 160.0
SSD_GROUPS_PER_STEP = 4
SSD_CHUNKS_PER_STEP = 4
SC_TM = 512
HALO = SUBLANES


def _params(n_axes):
    return pltpu.CompilerParams(
        dimension_semantics=("arbitrary",) * n_axes,
        vmem_limit_bytes=VMEM_LIMIT_BYTES)


def _resident(shape, index=None):
    index = (0,) * len(shape) if index is None else index
    return pl.BlockSpec(shape, lambda *_: index, pipeline_mode=pl.Buffered(1))


def _rms(x, g):
    ms = jnp.mean(x * x, axis=-1, keepdims=True)
    return x * lax.rsqrt(ms + RMS_EPS) * g


def _silu(x):
    h = 0.5 * x
    return h + h * jnp.tanh(h)


def _dot(a, b):
    return jnp.dot(a, b, preferred_element_type=F32)


def _dot_nt(a, b):
    return lax.dot_general(a, b, (((1,), (1,)), ((), ())), preferred_element_type=F32)


def _dot_tn(a, b):
    return lax.dot_general(a, b, (((0,), (0,)), ((), ())), preferred_element_type=F32)


def _ffn_kernel(x_ref, g_ref, wgu_ref, wd_ref, *refs, n_chunks, tf, final_norm, mixer_out):
    refs = list(refs)
    y_ref, wo_ref = (refs.pop(0), refs.pop(0)) if mixer_out else (None, None)
    fg_ref = refs[0] if final_norm else None
    o_ref, xn_ref, acc_ref = refs[-3:]
    x = x_ref[...]
    if mixer_out:
        x = x + _dot(y_ref[...], wo_ref[...].astype(BF16))
    d_ff = n_chunks * tf
    xn_ref[...] = _rms(x, g_ref[...]).astype(BF16)
    for c in range(n_chunks):
        gate = _dot(xn_ref[...], wgu_ref[:, c * tf:(c + 1) * tf].astype(BF16))
        up = _dot(xn_ref[...], wgu_ref[:, d_ff + c * tf:d_ff + (c + 1) * tf].astype(BF16))
        h = (_silu(gate) * up).astype(BF16)
        part = _dot(h, wd_ref[c * tf:(c + 1) * tf, :].astype(BF16))
        if c == 0:
            acc_ref[...] = part
        else:
            acc_ref[...] += part
    y = x + 0.5 * acc_ref[...]
    if final_norm:
        y = _rms(y, fg_ref[...])
    o_ref[...] = y


def _ffn(x, norm_g, w_gu, w_down, layer, final_g=None, mixer_out=None):
    t, d = x.shape
    d_ff = w_down.shape[1]
    tm, tf = (FFN_TM if mixer_out is None else FFN_FUSED_TM), FFN_TF
    n_chunks = d_ff // tf
    assert t % tm == 0 and d_ff % tf == 0
    final_norm = final_g is not None
    extra = [final_g.reshape(1, d)] if final_norm else []
    pre, pre_specs = [], []
    if mixer_out is not None:
        y, w_o, j = mixer_out
        k = y.shape[1]
        pre = [y, w_o]
        pre_specs = [pl.BlockSpec((tm, k), lambda i: (i, 0)), _resident((None, k, d), (j, 0, 0))]
    return pl.pallas_call(
        functools.partial(_ffn_kernel, n_chunks=n_chunks, tf=tf, final_norm=final_norm,
                          mixer_out=mixer_out is not None),
        grid=(t // tm,),
        in_specs=[
            pl.BlockSpec((tm, d), lambda i: (i, 0)),
            _resident((1, d)),
            _resident((None, d, 2 * d_ff), (layer, 0, 0)),
            _resident((None, d_ff, d), (layer, 0, 0)),
        ] + pre_specs + [_resident((1, d))] * len(extra),
        out_specs=pl.BlockSpec((tm, d), lambda i: (i, 0)),
        out_shape=jax.ShapeDtypeStruct((t, d), F32),
        scratch_shapes=[pltpu.VMEM((tm, d), BF16), pltpu.VMEM((tm, d), F32)],
        compiler_params=_params(1),
        name="ffn",
    )(x, norm_g.reshape(1, d), w_gu, w_down, *pre, *extra)


def _norm_proj_kernel(x_ref, g_ref, w_ref, *o_refs, segments, tn):
    xn = _rms(x_ref[...], g_ref[...]).astype(BF16)
    for o_ref, (start, width, scale) in zip(o_refs, segments):
        for n0 in range(0, width, tn):
            n1 = min(n0 + tn, width)
            w = w_ref[:, start + n0:start + n1]
            if scale != 1.0:
                w = w * scale
            o_ref[:, n0:n1] = _dot(xn, w.astype(BF16)).astype(o_ref.dtype)


def _norm_proj(x, norm_g, w, layer, segments, out_dtypes):
    t, d = x.shape
    tm = PROJ_TM
    assert t % tm == 0 and all(s[0] % LANES == 0 for s in segments)
    return pl.pallas_call(
        functools.partial(_norm_proj_kernel, segments=tuple(segments), tn=PROJ_TN),
        grid=(t // tm,),
        in_specs=[pl.BlockSpec((tm, d), lambda i: (i, 0)), _resident((1, d)),
                  _resident((None,) + w.shape[1:], (layer, 0, 0))],
        out_specs=[pl.BlockSpec((tm, s[1]), lambda i: (i, 0)) for s in segments],
        out_shape=[jax.ShapeDtypeStruct((t, s[1]), dt) for s, dt in zip(segments, out_dtypes)],
        compiler_params=_params(1),
        name="norm_proj",
    )(x, norm_g.reshape(1, d), w)


def _sb_attn_kernel(q_ref, k_ref, v_ref, o_ref, lb_a, sp_a, rs_a, lb_b, sp_b, rs_b, acc_all, c_all,
                    *, t, n_together):
    qi = pl.program_id(2)
    hd = SB_HEAD_DIM
    n_p = n_together
    sub = 2 * n_p * t
    both, first, second = (0, 2), (0, 1), (1, 2)
    in_a = lax.broadcasted_iota(jnp.int32, (1, 2 * hd), 1) < hd
    row = lax.broadcasted_iota(jnp.int32, (t, t), 0)
    col = lax.broadcasted_iota(jnp.int32, (t, t), 1)
    strict_lower = jnp.concatenate([col < row] * (2 * n_p), axis=0)
    tri = jnp.where(row > col, 1.0, 0.0).astype(BF16)
    sign_bit = jnp.uint32(0x80000000)
    top = 2 * qi + 1

    def mask_diagonal(x, fill):
        head = jnp.where(strict_lower, x[:sub], fill)
        return head if x.shape[0] == sub else jnp.concatenate([head, x[sub:]], axis=0)

    def pair_group(grp):
        lanes = [pl.ds((grp * n_p + p) * 2 * hd, 2 * hd) for p in range(n_p)]
        buf_a = (lb_a.at[grp], sp_a.at[grp], rs_a.at[grp])
        buf_b = (lb_b.at[grp], sp_b.at[grp], rs_b.at[grp])
        acc_ref, c_ref = acc_all.at[grp], c_all.at[grp]
        q_parts = []
        for s in range(2):
            for p in range(n_p):
                q2 = q_ref[0, s * t:(s + 1) * t, lanes[p]]
                q_parts += [jnp.where(in_a, q2, 0), jnp.where(in_a, 0, q2)]
        qs = jnp.concatenate(q_parts, axis=0)

        def scores(j, buf, subs, diag):
            lb_ref, sp_ref, rs_ref = buf
            r0, r1 = subs[0] * sub, subs[1] * sub
            rows_k = pl.ds(pl.multiple_of(j * t, t), t)
            k2 = [k_ref[0, rows_k, lanes[p]] for p in range(n_p)]
            z = jnp.concatenate(
                [_dot_nt(qs[(s * n_p + p) * 2 * t:(s * n_p + p + 1) * 2 * t], k2[p])
                 for s in range(*subs) for p in range(n_p)], axis=0)
            neg_abs = lax.bitcast_convert_type(lax.bitcast_convert_type(z, jnp.uint32) | sign_bit, F32)
            lb = jnp.minimum(z, 0.0) - jnp.log2(1.0 + jnp.exp2(neg_abs))
            sp = z - lb
            if diag:
                sp, lb = mask_diagonal(sp, 0.0), mask_diagonal(lb, -jnp.inf)
            lb_ref[r0:r1, :] = lb
            sp_ref[r0:r1, :] = sp.astype(BF16)
            rs_ref[r0:r1, :] = jnp.broadcast_to(jnp.sum(sp, axis=-1, keepdims=True), (r1 - r0, LANES))

        def values(j, buf, subs):
            lb_ref, sp_ref, rs_ref = buf
            r0, r1 = subs[0] * sub, subs[1] * sub
            rows_k = pl.ds(pl.multiple_of(j * t, t), t)
            c = c_ref[r0:r1, :]
            tail = _dot(sp_ref[r0:r1, :], tri)
            c_wide = jnp.concatenate([c] * (t // LANES), axis=1)
            att = jnp.exp2(lb_ref[r0:r1, :] - tail - c_wide).astype(BF16)
            c_ref[r0:r1, :] = c + rs_ref[r0:r1, :]
            for p in range(n_p):
                blocks = []
                for s in range(subs[1] - subs[0]):
                    base = (s * n_p + p) * 2 * t
                    blocks.append(jnp.concatenate([att[base:base + t], att[base + t:base + 2 * t]], axis=1))
                v2 = v_ref[0, rows_k, lanes[p]]
                vv = jnp.concatenate([jnp.where(in_a, v2, 0), jnp.where(in_a, 0, v2)], axis=0)
                acc_ref[p, subs[0] * t:subs[1] * t, :] += _dot(jnp.concatenate(blocks, axis=0), vv)

        def tile_pair(state):
            i = state[0]
            j = top - 3 - 2 * i
            values(j, buf_b, both)
            scores(j - 1, buf_a, both, False)
            values(j - 1, buf_a, both)
            scores(j - 2, buf_b, both, False)
            return i + 1, jnp.min(c_ref[...])

        def live(state):
            return (state[0] < qi - 1) & (state[1] < ATT_DEAD_CARRY)

        def first_tile_head():
            scores(top, buf_a, second, True)
            values(top, buf_a, second)
            scores(top - 1, buf_b, both, True)
            values(top - 1, buf_b, both)

        def head():
            scores(top, buf_a, second, True)
            scores(top - 1, buf_b, both, True)
            values(top, buf_a, second)
            scores(top - 2, buf_a, first, False)
            values(top - 1, buf_b, both)
            c_min0 = jnp.minimum(jnp.min(c_ref[sub:, :]), jnp.min(c_ref[:sub, :] + buf_a[2][:sub, :]))
            values(top - 2, buf_a, first)
            return c_min0

        def older_tiles(c_min0):
            @pl.when(c_min0 < ATT_DEAD_CARRY)
            def _():
                scores(top - 2, buf_a, second, False)
                values(top - 2, buf_a, second)
                scores(top - 3, buf_b, both, False)
                n_pairs, c_min = lax.while_loop(live, tile_pair, (jnp.int32(0), c_min0))

                @pl.when(c_min < ATT_DEAD_CARRY)
                def _():
                    values(top - 3 - 2 * n_pairs, buf_b, both)

        def write_out():
            for p in range(n_p):
                o_ref[0, :, lanes[p]] = acc_ref[p].astype(o_ref.dtype)

        return first_tile_head, head, older_tiles, write_out

    groups = [pair_group(grp) for grp in range(q_ref.shape[2] // (2 * hd * n_p))]
    acc_all[...] = jnp.zeros_like(acc_all)
    c_all[...] = jnp.zeros_like(c_all)

    @pl.when(qi == 0)
    def _():
        for first_tile_head, _, _, _ in groups:
            first_tile_head()

    @pl.when(qi > 0)
    def _():
        carries = [head() for _, head, _, _ in groups]
        for (_, _, older_tiles, _), c_min0 in zip(groups, carries):
            older_tiles(c_min0)

    for _, _, _, write_out in groups:
        write_out()


def _sb_attention(q, k, v):
    b, l, d = q.shape
    t, n_p = ATT_T, ATT_PAIRS_TOGETHER
    tq = 2 * t
    hp = 2 * SB_HEAD_DIM * ATT_PAIRS_PER_STEP
    assert l % tq == 0 and d % hp == 0 and 2 * SB_HEAD_DIM == LANES and ATT_PAIRS_PER_STEP % n_p == 0
    rows = 4 * n_p * t
    n_g = ATT_PAIRS_PER_STEP // n_p
    stage = [pltpu.VMEM((n_g, rows, t), F32), pltpu.VMEM((n_g, rows, t), BF16),
             pltpu.VMEM((n_g, rows, LANES), F32)]
    return pl.pallas_call(
        functools.partial(_sb_attn_kernel, t=t, n_together=n_p),
        grid=(b, d // hp, l // tq),
        in_specs=[
            pl.BlockSpec((1, tq, hp), lambda bi, hi, qi: (bi, qi, hi)),
            pl.BlockSpec((1, l, hp), lambda bi, hi, qi: (bi, 0, hi)),
            pl.BlockSpec((1, l, hp), lambda bi, hi, qi: (bi, 0, hi)),
        ],
        out_specs=pl.BlockSpec((1, tq, hp), lambda bi, hi, qi: (bi, qi, hi)),
        out_shape=jax.ShapeDtypeStruct((b, l, d), BF16),
        scratch_shapes=stage + stage + [pltpu.VMEM((n_g, n_p, tq, LANES), F32),
                                        pltpu.VMEM((n_g, rows, LANES), F32)],
        compiler_params=_params(3),
        name="sb_attn",
    )(q, k, v)


def _sb_mixer(x, b, l, norm_g, w_qkv, layer):
    t, d = x.shape
    scale = LOG2_E * SB_HEAD_DIM ** -0.5
    q, k, v = _norm_proj(x, norm_g, w_qkv, layer,
                         [(0, d, scale), (d, d, 1.0), (2 * d, d, 1.0)], [BF16] * 3)
    o = _sb_attention(q.reshape(b, l, d), k.reshape(b, l, d), v.reshape(b, l, d))
    return o.reshape(t, d)


def _split3(x):
    hi = x.astype(BF16).astype(F32)
    rest = x - hi
    mid = rest.astype(BF16).astype(F32)
    return hi, mid, (rest - mid).astype(BF16).astype(F32)


def _head_selector(width):
    k = lax.broadcasted_iota(jnp.int32, (3 * SUBLANES, SSD_HPG * width), 0) % SUBLANES
    lane = lax.broadcasted_iota(jnp.int32, (3 * SUBLANES, SSD_HPG * width), 1)
    return jnp.where((lane >= k * width) & (lane < (k + 1) * width), 1.0, 0.0).astype(BF16)


def _ssd_kernel(xr_ref, br_ref, cr_ref, z_ref, dt_ref, dtn_ref, wx_ref, wb_ref, wc_ref, bx_ref,
                bb_ref, bc_ref, dtb_ref, alog_ref, dsk_ref, ng_ref, o_ref, h_ref, hist_ref,
                steps_row_ref, steps_parts_ref, *, n_groups, n_chunks):
    cl, hp, n = SSD_CHUNK, SSD_HPG * SSD_HEAD_DIM, SSD_STATE
    rows = n_groups * SUBLANES
    row = lax.broadcasted_iota(jnp.int32, (cl, cl), 0)
    col = lax.broadcasted_iota(jnp.int32, (cl, cl), 1)
    causal = col <= row
    lane_hp = lax.broadcasted_iota(jnp.int32, (cl, hp), 1)
    to_chunk_lanes = _head_selector(cl)
    to_head_lanes = _head_selector(SSD_HEAD_DIM)

    def stage_step_sizes(dt_blk):
        dt_in = (dt_blk + dtb_ref[...]).reshape(rows, cl)
        dt_all = jnp.maximum(dt_in, 0.0) + jnp.log(1.0 + jnp.exp(-jnp.abs(dt_in)))
        a_all = dt_all * (-jnp.exp(alog_ref[...].reshape(rows, 1)))
        lane_cl = lax.broadcasted_iota(jnp.int32, (rows, cl), 1)
        sh = 1
        while sh < cl:
            a_all = a_all + jnp.where(lane_cl >= sh, pltpu.roll(a_all, sh, axis=1), 0.0)
            sh *= 2
        te_all = jnp.exp(a_all[:, cl - 1:cl] - a_all) * dt_all
        a2 = a_all * LOG2_E
        steps_row_ref[0] = a2 - jnp.log2(dt_all)
        for i, part in enumerate(_split3(a2) + _split3(jnp.exp(a_all)) + _split3(te_all)):
            steps_parts_ref[i] = part

    def per_position(first_part, g, selector):
        parts = [steps_parts_ref[first_part + i, g * SUBLANES:(g + 1) * SUBLANES, :] for i in range(3)]
        return _dot_tn(jnp.concatenate(parts, axis=0).astype(BF16), selector)

    def dt_chunk(ref, ci):
        return ref[0, :, :, pl.ds(pl.multiple_of(ci * cl, cl), cl)]

    def chunk(ci, _):
        tok = pl.ds(pl.multiple_of(ci * cl, cl), cl)

        @pl.when((pl.program_id(2) == 0) & (ci == 0))
        def _():
            h_ref[...] = jnp.zeros_like(h_ref)
            hist_ref[...] = jnp.zeros_like(hist_ref)
            stage_step_sizes(dt_chunk(dt_ref, 0))

        src2_all = steps_row_ref[0]
        a_cols = [per_position(0, g, to_chunk_lanes) for g in range(n_groups)]
        decay_in = [per_position(3, g, to_head_lanes) for g in range(n_groups)]
        decay_out = [per_position(6, g, to_head_lanes) for g in range(n_groups)]
        dt_next = jnp.where(ci < n_chunks - 1, dt_chunk(dt_ref, jnp.minimum(ci + 1, n_chunks - 1)),
                            dt_chunk(dtn_ref, 0))
        stage_step_sizes(dt_next)

        raw = jnp.concatenate([xr_ref[tok, :], br_ref[tok, :], cr_ref[tok, :]], axis=1)
        ext = jnp.concatenate([hist_ref[...], raw], axis=0)
        hist_ref[...] = raw[cl - HALO:, :]
        cw = jnp.concatenate([wx_ref[...], wb_ref[...], wc_ref[...]], axis=1)
        acc = cw[0:1, :] * ext
        for kk in range(1, SSD_CONV):
            acc = cw[kk:kk + 1, :] * ext + pltpu.roll(acc, 1, axis=0)
        bias = jnp.concatenate([bx_ref[...], bb_ref[...], bc_ref[...]], axis=1)
        act = _silu(acc[HALO:, :] + bias)
        xs_all = act[:, :n_groups * hp]
        bm_all = act[:, n_groups * hp:n_groups * (hp + n)].astype(BF16)
        cm_all = act[:, n_groups * (hp + n):].astype(BF16)

        for g in range(n_groups):
            xs = xs_all[:, g * hp:(g + 1) * hp]
            bm = bm_all[:, g * n:(g + 1) * n]
            cm = cm_all[:, g * n:(g + 1) * n]
            src2 = src2_all[g * SUBLANES:(g + 1) * SUBLANES]

            cb = _dot_nt(cm, bm)
            ws, xblocks = [], []
            for r in range(SSD_HPG):
                seg = a_cols[g][:, r * cl:(r + 1) * cl] - src2[r:r + 1, :]
                ws.append((cb * jnp.exp2(jnp.where(causal, seg, -jnp.inf))).astype(BF16))
                in_head = (lane_hp >= r * SSD_HEAD_DIM) & (lane_hp < (r + 1) * SSD_HEAD_DIM)
                xblocks.append(jnp.where(in_head, xs, 0.0).astype(BF16))
            y = _dot(jnp.concatenate(ws, axis=1), jnp.concatenate(xblocks, axis=0))

            h_prev = h_ref[g]
            y = y + _dot(cm, h_prev.astype(BF16)) * decay_in[g]
            xw = (xs * decay_out[g]).astype(BF16)
            h_ref[g] = h_prev * decay_in[g][cl - 1:cl, :] + _dot_tn(bm, xw)

            y = (y + dsk_ref[g] * xs) * _silu(z_ref[tok, g * hp:(g + 1) * hp])
            o_ref[tok, g * hp:(g + 1) * hp] = _rms(y, ng_ref[g]).astype(o_ref.dtype)
        return 0

    lax.fori_loop(0, n_chunks, chunk, 0, unroll=True)


def _ssd_core(xbc, z, dt_t, conv_w, conv_b, dt_bias, a_log, d_skip, norm_g, b, l):
    t = xbc.shape[0]
    g, hpg, cl, n, ng = SSD_GROUPS, SSD_HPG, SSD_CHUNK, SSD_STATE, SSD_GROUPS_PER_STEP
    hp = hpg * SSD_HEAD_DIM
    d_inner = g * hp
    kc = SSD_CHUNKS_PER_STEP
    tl = kc * cl
    nc = l // tl
    assert l % tl == 0 and hp % LANES == 0 and n == LANES and g % ng == 0
    bb, cb = d_inner // (ng * n), (d_inner + g * n) // (ng * n)
    pad = lambda p: jnp.pad(p.reshape(g, hpg, 1), ((0, 0), (0, SUBLANES - hpg), (0, 0)))
    per_lane = lambda p: jnp.repeat(p.reshape(g, hpg), SSD_HEAD_DIM, axis=1).reshape(g, 1, hp)
    conv_b = conv_b.reshape(1, -1)
    tok = lambda bi, gi, ci: bi * nc + ci
    return pl.pallas_call(
        functools.partial(_ssd_kernel, n_groups=ng, n_chunks=kc),
        grid=(b, g // ng, nc),
        in_specs=[
            pl.BlockSpec((tl, ng * hp), lambda bi, gi, ci: (tok(bi, gi, ci), gi)),
            pl.BlockSpec((tl, ng * n), lambda bi, gi, ci: (tok(bi, gi, ci), bb + gi)),
            pl.BlockSpec((tl, ng * n), lambda bi, gi, ci: (tok(bi, gi, ci), cb + gi)),
            pl.BlockSpec((tl, ng * hp), lambda bi, gi, ci: (tok(bi, gi, ci), gi)),
            pl.BlockSpec((1, ng, SUBLANES, tl), lambda bi, gi, ci: (bi, gi, 0, ci)),
            pl.BlockSpec((1, ng, SUBLANES, tl), lambda bi, gi, ci: (bi, gi, 0, jnp.minimum(ci + 1, nc - 1))),
            pl.BlockSpec((SSD_CONV, ng * hp), lambda bi, gi, ci: (0, gi)),
            pl.BlockSpec((SSD_CONV, ng * n), lambda bi, gi, ci: (0, bb + gi)),
            pl.BlockSpec((SSD_CONV, ng * n), lambda bi, gi, ci: (0, cb + gi)),
            pl.BlockSpec((1, ng * hp), lambda bi, gi, ci: (0, gi)),
            pl.BlockSpec((1, ng * n), lambda bi, gi, ci: (0, bb + gi)),
            pl.BlockSpec((1, ng * n), lambda bi, gi, ci: (0, cb + gi)),
            pl.BlockSpec((ng, SUBLANES, 1), lambda bi, gi, ci: (gi, 0, 0)),
            pl.BlockSpec((ng, SUBLANES, 1), lambda bi, gi, ci: (gi, 0, 0)),
            pl.BlockSpec((ng, 1, hp), lambda bi, gi, ci: (gi, 0, 0)),
            pl.BlockSpec((ng, 1, hp), lambda bi, gi, ci: (gi, 0, 0)),
        ],
        out_specs=pl.BlockSpec((tl, ng * hp), lambda bi, gi, ci: (tok(bi, gi, ci), gi)),
        out_shape=jax.ShapeDtypeStruct((t, d_inner), BF16),
        scratch_shapes=[pltpu.VMEM((ng, n, hp), F32), pltpu.VMEM((HALO, ng * (hp + 2 * n)), F32),
                        pltpu.VMEM((1, ng * SUBLANES, cl), F32), pltpu.VMEM((9, ng * SUBLANES, cl), F32)],
        compiler_params=_params(3),
        name="ssd_core",
    )(xbc, xbc, xbc, z, dt_t, dt_t, conv_w, conv_w, conv_w, conv_b, conv_b, conv_b,
      pad(dt_bias), pad(a_log), per_lane(d_skip), norm_g.reshape(g, 1, hp))


def _ssd_mixer(x, b, l, norm_g, w_in, conv_w, conv_b, dt_bias, a_log, d_skip, ssd_norm, layer):
    g, hpg = SSD_GROUPS, SSD_HPG
    d_inner = ssd_norm.shape[0]
    conv_dim = conv_w.shape[1]
    n_heads = g * hpg
    z, xbc, dt = _norm_proj(
        x, norm_g, w_in, layer,
        [(0, d_inner, 1.0), (d_inner, conv_dim, 1.0), (d_inner + conv_dim, n_heads, 1.0)], [F32] * 3)
    dt_t = dt.reshape(b, l, g, hpg).transpose(0, 2, 3, 1)
    dt_t = jnp.pad(dt_t, ((0, 0), (0, 0), (0, SUBLANES - hpg), (0, 0)))
    return _ssd_core(xbc, z, dt_t, conv_w, conv_b, dt_bias, a_log, d_skip, ssd_norm, b, l)


def _short_conv_kernel(x_ref, g_ref, wi_ref, cw_ref, wo_ref, o_ref, ext_ref):
    tm, d = x_ref.shape

    @pl.when(pl.program_id(1) == 0)
    def _():
        ext_ref[0:HALO, :] = jnp.zeros((HALO, ext_ref.shape[1]), F32)

    x = x_ref[...]
    xn = _rms(x, g_ref[...]).astype(BF16)
    w_part = lambda i: wi_ref[:, i * d:(i + 1) * d].astype(BF16)
    ext_ref[HALO:HALO + tm, :] = _dot(xn, w_part(1)) * _dot(xn, w_part(2))
    cw = cw_ref[...]
    u = cw[0:1, :] * ext_ref[pl.ds(HALO - (SC_WIDTH - 1), tm), :]
    for kk in range(1, SC_WIDTH):
        u = u + cw[kk:kk + 1, :] * ext_ref[pl.ds(HALO - (SC_WIDTH - 1) + kk, tm), :]
    ext_ref[0:HALO, :] = ext_ref[tm:tm + HALO, :]
    gated = (_dot(xn, w_part(0)) * u).astype(BF16)
    o_ref[...] = x + _dot(gated, wo_ref[...].astype(BF16))


def _short_conv_layer(x, b, l, norm_g, w_in, conv_w, w_out, layer):
    t, d = x.shape
    tm = SC_TM
    nl = l // tm
    assert l % tm == 0
    return pl.pallas_call(
        _short_conv_kernel,
        grid=(b, nl),
        in_specs=[
            pl.BlockSpec((tm, d), lambda bi, li: (bi * nl + li, 0)),
            _resident((1, d)),
            _resident((None, d, 3 * d), (layer, 0, 0)),
            _resident((None, SC_WIDTH, d), (layer, 0, 0)),
            _resident((None, d, d), (layer, 0, 0)),
        ],
        out_specs=pl.BlockSpec((tm, d), lambda bi, li: (bi * nl + li, 0)),
        out_shape=jax.ShapeDtypeStruct((t, d), F32),
        scratch_shapes=[pltpu.VMEM((tm + HALO, d), F32)],
        compiler_params=_params(2),
        name="short_conv",
    )(x, norm_g.reshape(1, d), w_in, conv_w, w_out)


def kernel(x, ffn1_norm, ffn1_w_gu, ffn1_w_down, mix_norm, ffn2_norm, ffn2_w_gu, ffn2_w_down,
           sb_w_qkv, sb_w_o, ssd_w_in, ssd_conv_w, ssd_conv_b, ssd_dt_bias, ssd_a_log, ssd_d,
           ssd_norm, ssd_w_out, sc_w_in, sc_conv_w, sc_w_out, final_norm):
    b, l, d = x.shape
    depth = ffn1_norm.shape[0]
    h = x.reshape(b * l, d)
    for i in range(depth):
        h = _ffn(h, ffn1_norm[i], ffn1_w_gu, ffn1_w_down, i)
        kind, j = i % N_MIXERS, i // N_MIXERS
        mixer_out = None
        if kind == 0:
            mixer_out = (_sb_mixer(h, b, l, mix_norm[i], sb_w_qkv, j), sb_w_o, j)
        elif kind == 1:
            y = _ssd_mixer(h, b, l, mix_norm[i], ssd_w_in, ssd_conv_w[j], ssd_conv_b[j],
                           ssd_dt_bias[j], ssd_a_log[j], ssd_d[j], ssd_norm[j], j)
            mixer_out = (y, ssd_w_out, j)
        else:
            h = _short_conv_layer(h, b, l, mix_norm[i], sc_w_in, sc_conv_w, sc_w_out, j)
        h = _ffn(h, ffn2_norm[i], ffn2_w_gu, ffn2_w_down, i,
                 final_g=final_norm if i == depth - 1 else None, mixer_out=mixer_out)
    return h.reshape(b, l, d)
```

```python
import functools

import jax
import jax.numpy as jnp
from jax import lax
from jax.experimental import pallas as pl
from jax.experimental.pallas import tpu as pltpu

F32 = jnp.float32
BF16 = jnp.bfloat16

RMS_EPS = 1e-6
LOG2_E = 1.4426950408889634
N_MIXERS = 3
SB_HEAD_DIM = 64
SSD_HEAD_DIM = 64
SSD_GROUPS = 8
SSD_HPG = 4
SSD_STATE = 128
SSD_CONV = 4
SSD_CHUNK = 128
SC_WIDTH = 3

LANES = 128
SUBLANES = 8
VMEM_LIMIT_BYTES = 60 * 1024 * 1024

FFN_TM = 1024
FFN_FUSED_TM = 512
FFN_TF = 256
PROJ_TM = 512
PROJ_VMEM_BUDGET = 56 * 1024 * 1024
PROJ_TN = 512
ATT_T = 256
ATT_PAIRS_PER_STEP = 4
ATT_PAIRS_TOGETHER = 2
ATT_DEAD_CARRY = 160.0
SSD_GROUPS_PER_STEP = 4
SSD_CHUNKS_PER_STEP = 4
SC_TM = 1024
HALO = SUBLANES


def _params(n_axes):
    return pltpu.CompilerParams(
        dimension_semantics=("arbitrary",) * n_axes,
        vmem_limit_bytes=VMEM_LIMIT_BYTES)


def _resident(shape, index=None):
    index = (0,) * len(shape) if index is None else index
    return pl.BlockSpec(shape, lambda *_: index, pipeline_mode=pl.Buffered(1))


def _rms(x, g):
    ms = jnp.mean(x * x, axis=-1, keepdims=True)
    return x * lax.rsqrt(ms + RMS_EPS) * g


def _silu(x):
    h = 0.5 * x
    return h + h * jnp.tanh(h)


def _dot(a, b):
    return jnp.dot(a, b, preferred_element_type=F32)


def _dot_nt(a, b):
    return lax.dot_general(a, b, (((1,), (1,)), ((), ())), preferred_element_type=F32)


def _dot_tn(a, b):
    return lax.dot_general(a, b, (((0,), (0,)), ((), ())), preferred_element_type=F32)


def _ffn_kernel(x_ref, g_ref, wgu_ref, wd_ref, *refs, n_chunks, tf, final_norm, mixer_out):
    refs = list(refs)
    y_ref, wo_ref = (refs.pop(0), refs.pop(0)) if mixer_out else (None, None)
    fg_ref = refs[0] if final_norm else None
    o_ref, xn_ref, acc_ref = refs[-3:]
    x = x_ref[...]
    if mixer_out:
        x = x + _dot(y_ref[...], wo_ref[...].astype(BF16))
    d_ff = n_chunks * tf
    xn_ref[...] = _rms(x, g_ref[...]).astype(BF16)
    for c in range(n_chunks):
        gate = _dot(xn_ref[...], wgu_ref[:, c * tf:(c + 1) * tf].astype(BF16))
        up = _dot(xn_ref[...], wgu_ref[:, d_ff + c * tf:d_ff + (c + 1) * tf].astype(BF16))
        h = (_silu(gate) * up).astype(BF16)
        part = _dot(h, wd_ref[c * tf:(c + 1) * tf, :].astype(BF16))
        if c == 0:
            acc_ref[...] = part
        else:
            acc_ref[...] += part
    y = x + 0.5 * acc_ref[...]
    if final_norm:
        y = _rms(y, fg_ref[...])
    o_ref[...] = y


def _ffn(x, norm_g, w_gu, w_down, layer, final_g=None, mixer_out=None):
    t, d = x.shape
    d_ff = w_down.shape[1]
    tm, tf = (FFN_TM if mixer_out is None else FFN_FUSED_TM), FFN_TF
    n_chunks = d_ff // tf
    assert t % tm == 0 and d_ff % tf == 0
    final_norm = final_g is not None
    extra = [final_g.reshape(1, d)] if final_norm else []
    pre, pre_specs = [], []
    if mixer_out is not None:
        y, w_o, j = mixer_out
        k = y.shape[1]
        pre = [y, w_o]
        pre_specs = [pl.BlockSpec((tm, k), lambda i: (i, 0)), _resident((None, k, d), (j, 0, 0))]
    return pl.pallas_call(
        functools.partial(_ffn_kernel, n_chunks=n_chunks, tf=tf, final_norm=final_norm,
                          mixer_out=mixer_out is not None),
        grid=(t // tm,),
        in_specs=[
            pl.BlockSpec((tm, d), lambda i: (i, 0)),
            _resident((1, d)),
            _resident((None, d, 2 * d_ff), (layer, 0, 0)),
            _resident((None, d_ff, d), (layer, 0, 0)),
        ] + pre_specs + [_resident((1, d))] * len(extra),
        out_specs=pl.BlockSpec((tm, d), lambda i: (i, 0)),
        out_shape=jax.ShapeDtypeStruct((t, d), F32),
        scratch_shapes=[pltpu.VMEM((tm, d), BF16), pltpu.VMEM((tm, d), F32)],
        compiler_params=_params(1),
        name="ffn",
    )(x, norm_g.reshape(1, d), w_gu, w_down, *pre, *extra)


def _norm_proj_kernel(x_ref, g_ref, w_ref, *o_refs, segments, tn):
    xn = _rms(x_ref[...], g_ref[...]).astype(BF16)
    for o_ref, (start, width, scale) in zip(o_refs, segments):
        for n0 in range(0, width, tn):
            n1 = min(n0 + tn, width)
            w = w_ref[:, start + n0:start + n1]
            if scale != 1.0:
                w = w * scale
            o_ref[:, n0:n1] = _dot(xn, w.astype(BF16)).astype(o_ref.dtype)


def _norm_proj(x, norm_g, w, layer, segments, out_dtypes):
    t, d = x.shape
    row_bytes = 4 * d + sum(s[1] * jnp.dtype(dt).itemsize for s, dt in zip(segments, out_dtypes))
    tm = next(c for c in (2 * PROJ_TM, PROJ_TM)
              if t % c == 0 and 4 * w[0].size + 2 * c * row_bytes <= PROJ_VMEM_BUDGET)
    assert all(s[0] % LANES == 0 for s in segments)
    return pl.pallas_call(
        functools.partial(_norm_proj_kernel, segments=tuple(segments), tn=PROJ_TN),
        grid=(t // tm,),
        in_specs=[pl.BlockSpec((tm, d), lambda i: (i, 0)), _resident((1, d)),
                  _resident((None,) + w.shape[1:], (layer, 0, 0))],
        out_specs=[pl.BlockSpec((tm, s[1]), lambda i: (i, 0)) for s in segments],
        out_shape=[jax.ShapeDtypeStruct((t, s[1]), dt) for s, dt in zip(segments, out_dtypes)],
        compiler_params=_params(1),
        name="norm_proj",
    )(x, norm_g.reshape(1, d), w)


def _sb_attn_kernel(q_ref, k_ref, v_ref, o_ref, lb_a, sp_a, rs_a, lb_b, sp_b, rs_b, acc_all, c_all,
                    *, t, n_together):
    qi = pl.program_id(2)
    hd = SB_HEAD_DIM
    n_p = n_together
    sub = 2 * n_p * t
    both, first, second = (0, 2), (0, 1), (1, 2)
    in_a = lax.broadcasted_iota(jnp.int32, (1, 2 * hd), 1) < hd
    row = lax.broadcasted_iota(jnp.int32, (t, t), 0)
    col = lax.broadcasted_iota(jnp.int32, (t, t), 1)
    strict_lower = jnp.concatenate([col < row] * (2 * n_p), axis=0)
    tri = jnp.where(row > col, 1.0, 0.0).astype(BF16)
    sign_bit = jnp.uint32(0x80000000)
    top = 2 * qi + 1

    def mask_diagonal(x, fill):
        head = jnp.where(strict_lower, x[:sub], fill)
        return head if x.shape[0] == sub else jnp.concatenate([head, x[sub:]], axis=0)

    def pair_group(grp):
        lanes = [pl.ds((grp * n_p + p) * 2 * hd, 2 * hd) for p in range(n_p)]
        buf_a = (lb_a.at[grp], sp_a.at[grp], rs_a.at[grp])
        buf_b = (lb_b.at[grp], sp_b.at[grp], rs_b.at[grp])
        acc_ref, c_ref = acc_all.at[grp], c_all.at[grp]
        q_parts = []
        for s in range(2):
            for p in range(n_p):
                q2 = q_ref[0, s * t:(s + 1) * t, lanes[p]]
                q_parts += [jnp.where(in_a, q2, 0), jnp.where(in_a, 0, q2)]
        qs = jnp.concatenate(q_parts, axis=0)

        def scores(j, buf, subs, diag):
            lb_ref, sp_ref, rs_ref = buf
            r0, r1 = subs[0] * sub, subs[1] * sub
            rows_k = pl.ds(pl.multiple_of(j * t, t), t)
            k2 = [k_ref[0, rows_k, lanes[p]] for p in range(n_p)]
            z = jnp.concatenate(
                [_dot_nt(qs[(s * n_p + p) * 2 * t:(s * n_p + p + 1) * 2 * t], k2[p])
                 for s in range(*subs) for p in range(n_p)], axis=0)
            neg_abs = lax.bitcast_convert_type(lax.bitcast_convert_type(z, jnp.uint32) | sign_bit, F32)
            lb = jnp.minimum(z, 0.0) - jnp.log2(1.0 + jnp.exp2(neg_abs))
            sp = z - lb
            if diag:
                sp, lb = mask_diagonal(sp, 0.0), mask_diagonal(lb, -jnp.inf)
            lb_ref[r0:r1, :] = lb
            sp_ref[r0:r1, :] = sp.astype(BF16)
            rs_ref[r0:r1, :] = jnp.broadcast_to(jnp.sum(sp, axis=-1, keepdims=True), (r1 - r0, LANES))

        def values(j, buf, subs):
            lb_ref, sp_ref, rs_ref = buf
            r0, r1 = subs[0] * sub, subs[1] * sub
            rows_k = pl.ds(pl.multiple_of(j * t, t), t)
            c = c_ref[r0:r1, :]
            tail = _dot(sp_ref[r0:r1, :], tri)
            c_wide = jnp.concatenate([c] * (t // LANES), axis=1)
            att = jnp.exp2(lb_ref[r0:r1, :] - tail - c_wide).astype(BF16)
            c_ref[r0:r1, :] = c + rs_ref[r0:r1, :]
            for p in range(n_p):
                blocks = []
                for s in range(subs[1] - subs[0]):
                    base = (s * n_p + p) * 2 * t
                    blocks.append(jnp.concatenate([att[base:base + t], att[base + t:base + 2 * t]], axis=1))
                v2 = v_ref[0, rows_k, lanes[p]]
                vv = jnp.concatenate([jnp.where(in_a, v2, 0), jnp.where(in_a, 0, v2)], axis=0)
                acc_ref[p, subs[0] * t:subs[1] * t, :] += _dot(jnp.concatenate(blocks, axis=0), vv)

        def tile_pair(state):
            i = state[0]
            j = top - 3 - 2 * i
            values(j, buf_b, both)
            scores(j - 1, buf_a, both, False)
            values(j - 1, buf_a, both)
            scores(j - 2, buf_b, both, False)
            return i + 1, jnp.min(c_ref[...])

        def live(state):
            return (state[0] < qi - 1) & (state[1] < ATT_DEAD_CARRY)

        def first_tile_head():
            scores(top, buf_a, second, True)
            values(top, buf_a, second)
            scores(top - 1, buf_b, both, True)
            values(top - 1, buf_b, both)

        def head():
            scores(top, buf_a, second, True)
            scores(top - 1, buf_b, both, True)
            values(top, buf_a, second)
            scores(top - 2, buf_a, first, False)
            values(top - 1, buf_b, both)
            c_min0 = jnp.minimum(jnp.min(c_ref[sub:, :]), jnp.min(c_ref[:sub, :] + buf_a[2][:sub, :]))
            values(top - 2, buf_a, first)
            return c_min0

        def older_tiles(c_min0):
            @pl.when(c_min0 < ATT_DEAD_CARRY)
            def _():
                scores(top - 2, buf_a, second, False)
                values(top - 2, buf_a, second)
                scores(top - 3, buf_b, both, False)
                n_pairs, c_min = lax.while_loop(live, tile_pair, (jnp.int32(0), c_min0))

                @pl.when(c_min < ATT_DEAD_CARRY)
                def _():
                    values(top - 3 - 2 * n_pairs, buf_b, both)

        def write_out():
            for p in range(n_p):
                o_ref[0, :, lanes[p]] = acc_ref[p].astype(o_ref.dtype)

        return first_tile_head, head, older_tiles, write_out

    groups = [pair_group(grp) for grp in range(q_ref.shape[2] // (2 * hd * n_p))]
    acc_all[...] = jnp.zeros_like(acc_all)
    c_all[...] = jnp.zeros_like(c_all)

    @pl.when(qi == 0)
    def _():
        for first_tile_head, _, _, _ in groups:
            first_tile_head()

    @pl.when(qi > 0)
    def _():
        carries = [head() for _, head, _, _ in groups]
        for (_, _, older_tiles, _), c_min0 in zip(groups, carries):
            older_tiles(c_min0)

    for _, _, _, write_out in groups:
        write_out()


def _sb_attention(q, k, v):
    b, l, d = q.shape
    t, n_p = ATT_T, ATT_PAIRS_TOGETHER
    tq = 2 * t
    hp = 2 * SB_HEAD_DIM * ATT_PAIRS_PER_STEP
    assert l % tq == 0 and d % hp == 0 and 2 * SB_HEAD_DIM == LANES and ATT_PAIRS_PER_STEP % n_p == 0
    rows = 4 * n_p * t
    n_g = ATT_PAIRS_PER_STEP // n_p
    stage = [pltpu.VMEM((n_g, rows, t), F32), pltpu.VMEM((n_g, rows, t), BF16),
             pltpu.VMEM((n_g, rows, LANES), F32)]
    return pl.pallas_call(
        functools.partial(_sb_attn_kernel, t=t, n_together=n_p),
        grid=(b, d // hp, l // tq),
        in_specs=[
            pl.BlockSpec((1, tq, hp), lambda bi, hi, qi: (bi, qi, hi)),
            pl.BlockSpec((1, l, hp), lambda bi, hi, qi: (bi, 0, hi)),
            pl.BlockSpec((1, l, hp), lambda bi, hi, qi: (bi, 0, hi)),
        ],
        out_specs=pl.BlockSpec((1, tq, hp), lambda bi, hi, qi: (bi, qi, hi)),
        out_shape=jax.ShapeDtypeStruct((b, l, d), BF16),
        scratch_shapes=stage + stage + [pltpu.VMEM((n_g, n_p, tq, LANES), F32),
                                        pltpu.VMEM((n_g, rows, LANES), F32)],
        compiler_params=_params(3),
        name="sb_attn",
    )(q, k, v)


def _sb_mixer(x, b, l, norm_g, w_qkv, layer):
    t, d = x.shape
    scale = LOG2_E * SB_HEAD_DIM ** -0.5
    q, k, v = _norm_proj(x, norm_g, w_qkv, layer,
                         [(0, d, scale), (d, d, 1.0), (2 * d, d, 1.0)], [BF16] * 3)
    o = _sb_attention(q.reshape(b, l, d), k.reshape(b, l, d), v.reshape(b, l, d))
    return o.reshape(t, d)


def _split3(x):
    hi = x.astype(BF16).astype(F32)
    rest = x - hi
    mid = rest.astype(BF16).astype(F32)
    return hi, mid, (rest - mid).astype(BF16).astype(F32)


def _head_selector(width):
    k = lax.broadcasted_iota(jnp.int32, (3 * SUBLANES, SSD_HPG * width), 0) % SUBLANES
    lane = lax.broadcasted_iota(jnp.int32, (3 * SUBLANES, SSD_HPG * width), 1)
    return jnp.where((lane >= k * width) & (lane < (k + 1) * width), 1.0, 0.0).astype(BF16)


def _ssd_kernel(xr_ref, br_ref, cr_ref, z_ref, dt_ref, dtn_ref, wx_ref, wb_ref, wc_ref, bx_ref,
                bb_ref, bc_ref, dtb_ref, alog_ref, dsk_ref, ng_ref, o_ref, h_ref, hist_ref,
                steps_row_ref, steps_parts_ref, *, n_groups, n_chunks):
    cl, hp, n = SSD_CHUNK, SSD_HPG * SSD_HEAD_DIM, SSD_STATE
    rows = n_groups * SUBLANES
    row = lax.broadcasted_iota(jnp.int32, (cl, cl), 0)
    col = lax.broadcasted_iota(jnp.int32, (cl, cl), 1)
    causal = col <= row
    lane_hp = lax.broadcasted_iota(jnp.int32, (cl, hp), 1)
    to_chunk_lanes = _head_selector(cl)
    to_head_lanes = _head_selector(SSD_HEAD_DIM)

    def stage_step_sizes(dt_blk):
        dt_in = (dt_blk + dtb_ref[...]).reshape(rows, cl)
        dt_all = jnp.maximum(dt_in, 0.0) + jnp.log(1.0 + jnp.exp(-jnp.abs(dt_in)))
        a_all = dt_all * (-jnp.exp(alog_ref[...].reshape(rows, 1)))
        lane_cl = lax.broadcasted_iota(jnp.int32, (rows, cl), 1)
        sh = 1
        while sh < cl:
            a_all = a_all + jnp.where(lane_cl >= sh, pltpu.roll(a_all, sh, axis=1), 0.0)
            sh *= 2
        te_all = jnp.exp(a_all[:, cl - 1:cl] - a_all) * dt_all
        a2 = a_all * LOG2_E
        steps_row_ref[0] = a2 - jnp.log2(dt_all)
        for i, part in enumerate(_split3(a2) + _split3(jnp.exp(a_all)) + _split3(te_all)):
            steps_parts_ref[i] = part

    def per_position(first_part, g, selector):
        parts = [steps_parts_ref[first_part + i, g * SUBLANES:(g + 1) * SUBLANES, :] for i in range(3)]
        return _dot_tn(jnp.concatenate(parts, axis=0).astype(BF16), selector)

    def dt_chunk(ref, ci):
        return ref[0, :, :, pl.ds(pl.multiple_of(ci * cl, cl), cl)]

    def chunk(ci, _):
        tok = pl.ds(pl.multiple_of(ci * cl, cl), cl)

        @pl.when((pl.program_id(2) == 0) & (ci == 0))
        def _():
            h_ref[...] = jnp.zeros_like(h_ref)
            hist_ref[...] = jnp.zeros_like(hist_ref)
            stage_step_sizes(dt_chunk(dt_ref, 0))

        src2_all = steps_row_ref[0]
        a_cols = [per_position(0, g, to_chunk_lanes) for g in range(n_groups)]
        decay_in = [per_position(3, g, to_head_lanes) for g in range(n_groups)]
        decay_out = [per_position(6, g, to_head_lanes) for g in range(n_groups)]
        dt_next = jnp.where(ci < n_chunks - 1, dt_chunk(dt_ref, jnp.minimum(ci + 1, n_chunks - 1)),
                            dt_chunk(dtn_ref, 0))
        stage_step_sizes(dt_next)

        raw = jnp.concatenate([xr_ref[tok, :], br_ref[tok, :], cr_ref[tok, :]], axis=1)
        ext = jnp.concatenate([hist_ref[...], raw], axis=0)
        hist_ref[...] = raw[cl - HALO:, :]
        cw = jnp.concatenate([wx_ref[...], wb_ref[...], wc_ref[...]], axis=1)
        acc = cw[0:1, :] * ext
        for kk in range(1, SSD_CONV):
            acc = cw[kk:kk + 1, :] * ext + pltpu.roll(acc, 1, axis=0)
        bias = jnp.concatenate([bx_ref[...], bb_ref[...], bc_ref[...]], axis=1)
        act = _silu(acc[HALO:, :] + bias)
        xs_all = act[:, :n_groups * hp]
        bm_all = act[:, n_groups * hp:n_groups * (hp + n)].astype(BF16)
        cm_all = act[:, n_groups * (hp + n):].astype(BF16)

        for g in range(n_groups):
            xs = xs_all[:, g * hp:(g + 1) * hp]
            bm = bm_all[:, g * n:(g + 1) * n]
            cm = cm_all[:, g * n:(g + 1) * n]
            src2 = src2_all[g * SUBLANES:(g + 1) * SUBLANES]

            cb = _dot_nt(cm, bm)
            ws, xblocks = [], []
            for r in range(SSD_HPG):
                seg = a_cols[g][:, r * cl:(r + 1) * cl] - src2[r:r + 1, :]
                ws.append((cb * jnp.exp2(jnp.where(causal, seg, -jnp.inf))).astype(BF16))
                in_head = (lane_hp >= r * SSD_HEAD_DIM) & (lane_hp < (r + 1) * SSD_HEAD_DIM)
                xblocks.append(jnp.where(in_head, xs, 0.0).astype(BF16))
            y = _dot(jnp.concatenate(ws, axis=1), jnp.concatenate(xblocks, axis=0))

            h_prev = h_ref[g]
            y = y + _dot(cm, h_prev.astype(BF16)) * decay_in[g]
            xw = (xs * decay_out[g]).astype(BF16)
            h_ref[g] = h_prev * decay_in[g][cl - 1:cl, :] + _dot_tn(bm, xw)

            y = (y + dsk_ref[g] * xs) * _silu(z_ref[tok, g * hp:(g + 1) * hp])
            o_ref[tok, g * hp:(g + 1) * hp] = _rms(y, ng_ref[g]).astype(o_ref.dtype)
        return 0

    lax.fori_loop(0, n_chunks, chunk, 0, unroll=True)


def _ssd_core(xbc, z, dt_t, conv_w, conv_b, dt_bias, a_log, d_skip, norm_g, b, l):
    t = xbc.shape[0]
    g, hpg, cl, n, ng = SSD_GROUPS, SSD_HPG, SSD_CHUNK, SSD_STATE, SSD_GROUPS_PER_STEP
    hp = hpg * SSD_HEAD_DIM
    d_inner = g * hp
    kc = SSD_CHUNKS_PER_STEP
    tl = kc * cl
    nc = l // tl
    assert l % tl == 0 and hp % LANES == 0 and n == LANES and g % ng == 0
    bb, cb = d_inner // (ng * n), (d_inner + g * n) // (ng * n)
    pad = lambda p: jnp.pad(p.reshape(g, hpg, 1), ((0, 0), (0, SUBLANES - hpg), (0, 0)))
    per_lane = lambda p: jnp.repeat(p.reshape(g, hpg), SSD_HEAD_DIM, axis=1).reshape(g, 1, hp)
    conv_b = conv_b.reshape(1, -1)
    tok = lambda bi, gi, ci: bi * nc + ci
    return pl.pallas_call(
        functools.partial(_ssd_kernel, n_groups=ng, n_chunks=kc),
        grid=(b, g // ng, nc),
        in_specs=[
            pl.BlockSpec((tl, ng * hp), lambda bi, gi, ci: (tok(bi, gi, ci), gi)),
            pl.BlockSpec((tl, ng * n), lambda bi, gi, ci: (tok(bi, gi, ci), bb + gi)),
            pl.BlockSpec((tl, ng * n), lambda bi, gi, ci: (tok(bi, gi, ci), cb + gi)),
            pl.BlockSpec((tl, ng * hp), lambda bi, gi, ci: (tok(bi, gi, ci), gi)),
            pl.BlockSpec((1, ng, SUBLANES, tl), lambda bi, gi, ci: (bi, gi, 0, ci)),
            pl.BlockSpec((1, ng, SUBLANES, tl), lambda bi, gi, ci: (bi, gi, 0, jnp.minimum(ci + 1, nc - 1))),
            pl.BlockSpec((SSD_CONV, ng * hp), lambda bi, gi, ci: (0, gi)),
            pl.BlockSpec((SSD_CONV, ng * n), lambda bi, gi, ci: (0, bb + gi)),
            pl.BlockSpec((SSD_CONV, ng * n), lambda bi, gi, ci: (0, cb + gi)),
            pl.BlockSpec((1, ng * hp), lambda bi, gi, ci: (0, gi)),
            pl.BlockSpec((1, ng * n), lambda bi, gi, ci: (0, bb + gi)),
            pl.BlockSpec((1, ng * n), lambda bi, gi, ci: (0, cb + gi)),
            pl.BlockSpec((ng, SUBLANES, 1), lambda bi, gi, ci: (gi, 0, 0)),
            pl.BlockSpec((ng, SUBLANES, 1), lambda bi, gi, ci: (gi, 0, 0)),
            pl.BlockSpec((ng, 1, hp), lambda bi, gi, ci: (gi, 0, 0)),
            pl.BlockSpec((ng, 1, hp), lambda bi, gi, ci: (gi, 0, 0)),
        ],
        out_specs=pl.BlockSpec((tl, ng * hp), lambda bi, gi, ci: (tok(bi, gi, ci), gi)),
        out_shape=jax.ShapeDtypeStruct((t, d_inner), BF16),
        scratch_shapes=[pltpu.VMEM((ng, n, hp), F32), pltpu.VMEM((HALO, ng * (hp + 2 * n)), F32),
                        pltpu.VMEM((1, ng * SUBLANES, cl), F32), pltpu.VMEM((9, ng * SUBLANES, cl), F32)],
        compiler_params=_params(3),
        name="ssd_core",
    )(xbc, xbc, xbc, z, dt_t, dt_t, conv_w, conv_w, conv_w, conv_b, conv_b, conv_b,
      pad(dt_bias), pad(a_log), per_lane(d_skip), norm_g.reshape(g, 1, hp))


def _ssd_mixer(x, b, l, norm_g, w_in, conv_w, conv_b, dt_bias, a_log, d_skip, ssd_norm, layer):
    g, hpg = SSD_GROUPS, SSD_HPG
    d_inner = ssd_norm.shape[0]
    conv_dim = conv_w.shape[1]
    n_heads = g * hpg
    z, xbc, dt = _norm_proj(
        x, norm_g, w_in, layer,
        [(0, d_inner, 1.0), (d_inner, conv_dim, 1.0), (d_inner + conv_dim, n_heads, 1.0)], [F32] * 3)
    dt_t = dt.reshape(b, l, g, hpg).transpose(0, 2, 3, 1)
    dt_t = jnp.pad(dt_t, ((0, 0), (0, 0), (0, SUBLANES - hpg), (0, 0)))
    return _ssd_core(xbc, z, dt_t, conv_w, conv_b, dt_bias, a_log, d_skip, ssd_norm, b, l)


def _short_conv_kernel(x_ref, g_ref, wi_ref, cw_ref, wo_ref, o_ref, ext_ref):
    tm, d = x_ref.shape

    @pl.when(pl.program_id(1) == 0)
    def _():
        ext_ref[0:HALO, :] = jnp.zeros((HALO, ext_ref.shape[1]), F32)

    x = x_ref[...]
    xn = _rms(x, g_ref[...]).astype(BF16)
    w_part = lambda i: wi_ref[:, i * d:(i + 1) * d].astype(BF16)
    ext_ref[HALO:HALO + tm, :] = _dot(xn, w_part(1)) * _dot(xn, w_part(2))
    cw = cw_ref[...]
    u = cw[0:1, :] * ext_ref[pl.ds(HALO - (SC_WIDTH - 1), tm), :]
    for kk in range(1, SC_WIDTH):
        u = u + cw[kk:kk + 1, :] * ext_ref[pl.ds(HALO - (SC_WIDTH - 1) + kk, tm), :]
    ext_ref[0:HALO, :] = ext_ref[tm:tm + HALO, :]
    gated = (_dot(xn, w_part(0)) * u).astype(BF16)
    o_ref[...] = x + _dot(gated, wo_ref[...].astype(BF16))


def _short_conv_layer(x, b, l, norm_g, w_in, conv_w, w_out, layer):
    t, d = x.shape
    tm = SC_TM
    nl = l // tm
    assert l % tm == 0
    return pl.pallas_call(
        _short_conv_kernel,
        grid=(b, nl),
        in_specs=[
            pl.BlockSpec((tm, d), lambda bi, li: (bi * nl + li, 0)),
            _resident((1, d)),
            _resident((None, d, 3 * d), (layer, 0, 0)),
            _resident((None, SC_WIDTH, d), (layer, 0, 0)),
            _resident((None, d, d), (layer, 0, 0)),
        ],
        out_specs=pl.BlockSpec((tm, d), lambda bi, li: (bi * nl + li, 0)),
        out_shape=jax.ShapeDtypeStruct((t, d), F32),
        scratch_shapes=[pltpu.VMEM((tm + HALO, d), F32)],
        compiler_params=_params(2),
        name="short_conv",
    )(x, norm_g.reshape(1, d), w_in, conv_w, w_out)


def kernel(x, ffn1_norm, ffn1_w_gu, ffn1_w_down, mix_norm, ffn2_norm, ffn2_w_gu, ffn2_w_down,
           sb_w_qkv, sb_w_o, ssd_w_in, ssd_conv_w, ssd_conv_b, ssd_dt_bias, ssd_a_log, ssd_d,
           ssd_norm, ssd_w_out, sc_w_in, sc_conv_w, sc_w_out, final_norm):
    b, l, d = x.shape
    depth = ffn1_norm.shape[0]
    h = x.reshape(b * l, d)
    for i in range(depth):
        h = _ffn(h, ffn1_norm[i], ffn1_w_gu, ffn1_w_down, i)
        kind, j = i % N_MIXERS, i // N_MIXERS
        mixer_out = None
        if kind == 0:
            mixer_out = (_sb_mixer(h, b, l, mix_norm[i], sb_w_qkv, j), sb_w_o, j)
        elif kind == 1:
            y = _ssd_mixer(h, b, l, mix_norm[i], ssd_w_in, ssd_conv_w[j], ssd_conv_b[j],
                           ssd_dt_bias[j], ssd_a_log[j], ssd_d[j], ssd_norm[j], j)
            mixer_out = (y, ssd_w_out, j)
        else:
            h = _short_conv_layer(h, b, l, mix_norm[i], sc_w_in, sc_conv_w, sc_w_out, j)
        h = _ffn(h, ffn2_norm[i], ffn2_w_gu, ffn2_w_down, i,
                 final_g=final_norm if i == depth - 1 else None, mixer_out=mixer_out)
    return h.reshape(b, l, d)
```

```python
import functools

import jax
import jax.numpy as jnp
from jax import lax
from jax.experimental import pallas as pl
from jax.experimental.pallas import tpu as pltpu

F32 = jnp.float32
BF16 = jnp.bfloat16

RMS_EPS = 1e-6
LOG2_E = 1.4426950408889634
N_MIXERS = 3
SB_HEAD_DIM = 64
SSD_HEAD_DIM = 64
SSD_GROUPS = 8
SSD_HPG = 4
SSD_STATE = 128
SSD_CONV = 4
SSD_CHUNK = 128
SC_WIDTH = 3

LANES = 128
SUBLANES = 8
VMEM_LIMIT_BYTES = 60 * 1024 * 1024

FFN_TM = 1024
FFN_FUSED_TM = 512
FFN_TF = 256
PROJ_TM = 512
PROJ_VMEM_BUDGET = 56 * 1024 * 1024
PROJ_TN = 512
ATT_T = 256
ATT_PAIRS_PER_STEP = 4
ATT_PAIRS_TOGETHER = 2
ATT_DEAD_CARRY = 160.0
SSD_GROUPS_PER_STEP = 4
SSD_CHUNKS_PER_STEP = 4
SC_TM = 1024
SC_TC = 256
HALO = SUBLANES


def _params(n_axes):
    return pltpu.CompilerParams(
        dimension_semantics=("arbitrary",) * n_axes,
        vmem_limit_bytes=VMEM_LIMIT_BYTES)


def _resident(shape, index=None):
    index = (0,) * len(shape) if index is None else index
    return pl.BlockSpec(shape, lambda *_: index, pipeline_mode=pl.Buffered(1))


def _rms(x, g):
    ms = jnp.mean(x * x, axis=-1, keepdims=True)
    return x * lax.rsqrt(ms + RMS_EPS) * g


def _silu(x):
    h = 0.5 * x
    return h + h * jnp.tanh(h)


def _dot(a, b):
    return jnp.dot(a, b, preferred_element_type=F32)


def _dot_nt(a, b):
    return lax.dot_general(a, b, (((1,), (1,)), ((), ())), preferred_element_type=F32)


def _dot_tn(a, b):
    return lax.dot_general(a, b, (((0,), (0,)), ((), ())), preferred_element_type=F32)


def _ffn_kernel(x_ref, g_ref, wgu_ref, wd_ref, *refs, n_chunks, tf, final_norm, mixer_out):
    refs = list(refs)
    y_ref, wo_ref = (refs.pop(0), refs.pop(0)) if mixer_out else (None, None)
    fg_ref = refs[0] if final_norm else None
    o_ref, xn_ref, acc_ref = refs[-3:]
    x = x_ref[...]
    if mixer_out:
        x = x + _dot(y_ref[...], wo_ref[...].astype(BF16))
    d_ff = n_chunks * tf
    xn_ref[...] = _rms(x, g_ref[...]).astype(BF16)
    for c in range(n_chunks):
        gate = _dot(xn_ref[...], wgu_ref[:, c * tf:(c + 1) * tf].astype(BF16))
        up = _dot(xn_ref[...], wgu_ref[:, d_ff + c * tf:d_ff + (c + 1) * tf].astype(BF16))
        h = (_silu(gate) * up).astype(BF16)
        part = _dot(h, wd_ref[c * tf:(c + 1) * tf, :].astype(BF16))
        if c == 0:
            acc_ref[...] = part
        else:
            acc_ref[...] += part
    y = x + 0.5 * acc_ref[...]
    if final_norm:
        y = _rms(y, fg_ref[...])
    o_ref[...] = y


def _ffn(x, norm_g, w_gu, w_down, layer, final_g=None, mixer_out=None):
    t, d = x.shape
    d_ff = w_down.shape[1]
    tm, tf = (FFN_TM if mixer_out is None else FFN_FUSED_TM), FFN_TF
    n_chunks = d_ff // tf
    assert t % tm == 0 and d_ff % tf == 0
    final_norm = final_g is not None
    extra = [final_g.reshape(1, d)] if final_norm else []
    pre, pre_specs = [], []
    if mixer_out is not None:
        y, w_o, j = mixer_out
        k = y.shape[1]
        pre = [y, w_o]
        pre_specs = [pl.BlockSpec((tm, k), lambda i: (i, 0)), _resident((None, k, d), (j, 0, 0))]
    return pl.pallas_call(
        functools.partial(_ffn_kernel, n_chunks=n_chunks, tf=tf, final_norm=final_norm,
                          mixer_out=mixer_out is not None),
        grid=(t // tm,),
        in_specs=[
            pl.BlockSpec((tm, d), lambda i: (i, 0)),
            _resident((1, d)),
            _resident((None, d, 2 * d_ff), (layer, 0, 0)),
            _resident((None, d_ff, d), (layer, 0, 0)),
        ] + pre_specs + [_resident((1, d))] * len(extra),
        out_specs=pl.BlockSpec((tm, d), lambda i: (i, 0)),
        out_shape=jax.ShapeDtypeStruct((t, d), F32),
        scratch_shapes=[pltpu.VMEM((tm, d), BF16), pltpu.VMEM((tm, d), F32)],
        compiler_params=_params(1),
        name="ffn",
    )(x, norm_g.reshape(1, d), w_gu, w_down, *pre, *extra)


def _norm_proj_kernel(x_ref, g_ref, w_ref, *o_refs, segments, tn):
    xn = _rms(x_ref[...], g_ref[...]).astype(BF16)
    for o_ref, (start, width, scale) in zip(o_refs, segments):
        for n0 in range(0, width, tn):
            n1 = min(n0 + tn, width)
            w = w_ref[:, start + n0:start + n1]
            if scale != 1.0:
                w = w * scale
            o_ref[:, n0:n1] = _dot(xn, w.astype(BF16)).astype(o_ref.dtype)


def _norm_proj(x, norm_g, w, layer, segments, out_dtypes):
    t, d = x.shape
    row_bytes = 4 * d + sum(s[1] * jnp.dtype(dt).itemsize for s, dt in zip(segments, out_dtypes))
    tm = next(c for c in (2 * PROJ_TM, PROJ_TM)
              if t % c == 0 and 4 * w[0].size + 2 * c * row_bytes <= PROJ_VMEM_BUDGET)
    assert all(s[0] % LANES == 0 for s in segments)
    return pl.pallas_call(
        functools.partial(_norm_proj_kernel, segments=tuple(segments), tn=PROJ_TN),
        grid=(t // tm,),
        in_specs=[pl.BlockSpec((tm, d), lambda i: (i, 0)), _resident((1, d)),
                  _resident((None,) + w.shape[1:], (layer, 0, 0))],
        out_specs=[pl.BlockSpec((tm, s[1]), lambda i: (i, 0)) for s in segments],
        out_shape=[jax.ShapeDtypeStruct((t, s[1]), dt) for s, dt in zip(segments, out_dtypes)],
        compiler_params=_params(1),
        name="norm_proj",
    )(x, norm_g.reshape(1, d), w)


def _sb_attn_kernel(q_ref, k_ref, v_ref, o_ref, lb_a, sp_a, rs_a, lb_b, sp_b, rs_b, acc_all, c_all,
                    *, t, n_together):
    qi = pl.program_id(2)
    hd = SB_HEAD_DIM
    n_p = n_together
    sub = 2 * n_p * t
    both, first, second = (0, 2), (0, 1), (1, 2)
    in_a = lax.broadcasted_iota(jnp.int32, (1, 2 * hd), 1) < hd
    row = lax.broadcasted_iota(jnp.int32, (t, t), 0)
    col = lax.broadcasted_iota(jnp.int32, (t, t), 1)
    strict_lower = jnp.concatenate([col < row] * (2 * n_p), axis=0)
    tri = jnp.where(row > col, 1.0, 0.0).astype(BF16)
    sign_bit = jnp.uint32(0x80000000)
    top = 2 * qi + 1

    def mask_diagonal(x, fill):
        head = jnp.where(strict_lower, x[:sub], fill)
        return head if x.shape[0] == sub else jnp.concatenate([head, x[sub:]], axis=0)

    def pair_group(grp):
        lanes = [pl.ds((grp * n_p + p) * 2 * hd, 2 * hd) for p in range(n_p)]
        buf_a = (lb_a.at[grp], sp_a.at[grp], rs_a.at[grp])
        buf_b = (lb_b.at[grp], sp_b.at[grp], rs_b.at[grp])
        acc_ref, c_ref = acc_all.at[grp], c_all.at[grp]
        q_parts = []
        for s in range(2):
            for p in range(n_p):
                q2 = q_ref[0, s * t:(s + 1) * t, lanes[p]]
                q_parts += [jnp.where(in_a, q2, 0), jnp.where(in_a, 0, q2)]
        qs = jnp.concatenate(q_parts, axis=0)

        def scores(j, buf, subs, diag):
            lb_ref, sp_ref, rs_ref = buf
            r0, r1 = subs[0] * sub, subs[1] * sub
            rows_k = pl.ds(pl.multiple_of(j * t, t), t)
            k2 = [k_ref[0, rows_k, lanes[p]] for p in range(n_p)]
            z = jnp.concatenate(
                [_dot_nt(qs[(s * n_p + p) * 2 * t:(s * n_p + p + 1) * 2 * t], k2[p])
                 for s in range(*subs) for p in range(n_p)], axis=0)
            neg_abs = lax.bitcast_convert_type(lax.bitcast_convert_type(z, jnp.uint32) | sign_bit, F32)
            lb = jnp.minimum(z, 0.0) - jnp.log2(1.0 + jnp.exp2(neg_abs))
            sp = z - lb
            if diag:
                sp, lb = mask_diagonal(sp, 0.0), mask_diagonal(lb, -jnp.inf)
            lb_ref[r0:r1, :] = lb
            sp_ref[r0:r1, :] = sp.astype(BF16)
            rs_ref[r0:r1, :] = jnp.broadcast_to(jnp.sum(sp, axis=-1, keepdims=True), (r1 - r0, LANES))

        def values(j, buf, subs):
            lb_ref, sp_ref, rs_ref = buf
            r0, r1 = subs[0] * sub, subs[1] * sub
            rows_k = pl.ds(pl.multiple_of(j * t, t), t)
            c = c_ref[r0:r1, :]
            tail = _dot(sp_ref[r0:r1, :], tri)
            c_wide = jnp.concatenate([c] * (t // LANES), axis=1)
            att = jnp.exp2(lb_ref[r0:r1, :] - tail - c_wide).astype(BF16)
            c_ref[r0:r1, :] = c + rs_ref[r0:r1, :]
            for p in range(n_p):
                blocks = []
                for s in range(subs[1] - subs[0]):
                    base = (s * n_p + p) * 2 * t
                    blocks.append(jnp.concatenate([att[base:base + t], att[base + t:base + 2 * t]], axis=1))
                v2 = v_ref[0, rows_k, lanes[p]]
                vv = jnp.concatenate([jnp.where(in_a, v2, 0), jnp.where(in_a, 0, v2)], axis=0)
                acc_ref[p, subs[0] * t:subs[1] * t, :] += _dot(jnp.concatenate(blocks, axis=0), vv)

        def tile_pair(state):
            i = state[0]
            j = top - 3 - 2 * i
            values(j, buf_b, both)
            scores(j - 1, buf_a, both, False)
            values(j - 1, buf_a, both)
            scores(j - 2, buf_b, both, False)
            return i + 1, jnp.min(c_ref[...])

        def live(state):
            return (state[0] < qi - 1) & (state[1] < ATT_DEAD_CARRY)

        def first_tile_head():
            scores(top, buf_a, second, True)
            values(top, buf_a, second)
            scores(top - 1, buf_b, both, True)
            values(top - 1, buf_b, both)

        def head():
            scores(top, buf_a, second, True)
            scores(top - 1, buf_b, both, True)
            values(top, buf_a, second)
            scores(top - 2, buf_a, first, False)
            values(top - 1, buf_b, both)
            c_min0 = jnp.minimum(jnp.min(c_ref[sub:, :]), jnp.min(c_ref[:sub, :] + buf_a[2][:sub, :]))
            values(top - 2, buf_a, first)
            return c_min0

        def older_tiles(c_min0):
            @pl.when(c_min0 < ATT_DEAD_CARRY)
            def _():
                scores(top - 2, buf_a, second, False)
                values(top - 2, buf_a, second)
                scores(top - 3, buf_b, both, False)
                n_pairs, c_min = lax.while_loop(live, tile_pair, (jnp.int32(0), c_min0))

                @pl.when(c_min < ATT_DEAD_CARRY)
                def _():
                    values(top - 3 - 2 * n_pairs, buf_b, both)

        def write_out():
            for p in range(n_p):
                o_ref[0, :, lanes[p]] = acc_ref[p].astype(o_ref.dtype)

        return first_tile_head, head, older_tiles, write_out

    groups = [pair_group(grp) for grp in range(q_ref.shape[2] // (2 * hd * n_p))]
    acc_all[...] = jnp.zeros_like(acc_all)
    c_all[...] = jnp.zeros_like(c_all)

    @pl.when(qi == 0)
    def _():
        for first_tile_head, _, _, _ in groups:
            first_tile_head()

    @pl.when(qi > 0)
    def _():
        carries = [head() for _, head, _, _ in groups]
        for (_, _, older_tiles, _), c_min0 in zip(groups, carries):
            older_tiles(c_min0)

    for _, _, _, write_out in groups:
        write_out()


def _sb_attention(q, k, v):
    b, l, d = q.shape
    t, n_p = ATT_T, ATT_PAIRS_TOGETHER
    tq = 2 * t
    hp = 2 * SB_HEAD_DIM * ATT_PAIRS_PER_STEP
    assert l % tq == 0 and d % hp == 0 and 2 * SB_HEAD_DIM == LANES and ATT_PAIRS_PER_STEP % n_p == 0
    rows = 4 * n_p * t
    n_g = ATT_PAIRS_PER_STEP // n_p
    stage = [pltpu.VMEM((n_g, rows, t), F32), pltpu.VMEM((n_g, rows, t), BF16),
             pltpu.VMEM((n_g, rows, LANES), F32)]
    return pl.pallas_call(
        functools.partial(_sb_attn_kernel, t=t, n_together=n_p),
        grid=(b, d // hp, l // tq),
        in_specs=[
            pl.BlockSpec((1, tq, hp), lambda bi, hi, qi: (bi, qi, hi)),
            pl.BlockSpec((1, l, hp), lambda bi, hi, qi: (bi, 0, hi)),
            pl.BlockSpec((1, l, hp), lambda bi, hi, qi: (bi, 0, hi)),
        ],
        out_specs=pl.BlockSpec((1, tq, hp), lambda bi, hi, qi: (bi, qi, hi)),
        out_shape=jax.ShapeDtypeStruct((b, l, d), BF16),
        scratch_shapes=stage + stage + [pltpu.VMEM((n_g, n_p, tq, LANES), F32),
                                        pltpu.VMEM((n_g, rows, LANES), F32)],
        compiler_params=_params(3),
        name="sb_attn",
    )(q, k, v)


def _sb_mixer(x, b, l, norm_g, w_qkv, layer):
    t, d = x.shape
    scale = LOG2_E * SB_HEAD_DIM ** -0.5
    q, k, v = _norm_proj(x, norm_g, w_qkv, layer,
                         [(0, d, scale), (d, d, 1.0), (2 * d, d, 1.0)], [BF16] * 3)
    o = _sb_attention(q.reshape(b, l, d), k.reshape(b, l, d), v.reshape(b, l, d))
    return o.reshape(t, d)


def _split3(x):
    hi = x.astype(BF16).astype(F32)
    rest = x - hi
    mid = rest.astype(BF16).astype(F32)
    return hi, mid, (rest - mid).astype(BF16).astype(F32)


def _head_selector(width):
    k = lax.broadcasted_iota(jnp.int32, (3 * SUBLANES, SSD_HPG * width), 0) % SUBLANES
    lane = lax.broadcasted_iota(jnp.int32, (3 * SUBLANES, SSD_HPG * width), 1)
    return jnp.where((lane >= k * width) & (lane < (k + 1) * width), 1.0, 0.0).astype(BF16)


def _ssd_kernel(xr_ref, br_ref, cr_ref, z_ref, dt_ref, dtn_ref, wx_ref, wb_ref, wc_ref, bx_ref,
                bb_ref, bc_ref, dtb_ref, alog_ref, dsk_ref, ng_ref, o_ref, h_ref, hist_ref,
                steps_row_ref, steps_parts_ref, *, n_groups, n_chunks):
    cl, hp, n = SSD_CHUNK, SSD_HPG * SSD_HEAD_DIM, SSD_STATE
    rows = n_groups * SUBLANES
    row = lax.broadcasted_iota(jnp.int32, (cl, cl), 0)
    col = lax.broadcasted_iota(jnp.int32, (cl, cl), 1)
    causal = col <= row
    lane_hp = lax.broadcasted_iota(jnp.int32, (cl, hp), 1)
    to_chunk_lanes = _head_selector(cl)
    to_head_lanes = _head_selector(SSD_HEAD_DIM)

    def stage_step_sizes(dt_blk):
        dt_in = (dt_blk + dtb_ref[...]).reshape(rows, cl)
        dt_all = jnp.maximum(dt_in, 0.0) + jnp.log(1.0 + jnp.exp(-jnp.abs(dt_in)))
        a_all = dt_all * (-jnp.exp(alog_ref[...].reshape(rows, 1)))
        lane_cl = lax.broadcasted_iota(jnp.int32, (rows, cl), 1)
        sh = 1
        while sh < cl:
            a_all = a_all + jnp.where(lane_cl >= sh, pltpu.roll(a_all, sh, axis=1), 0.0)
            sh *= 2
        te_all = jnp.exp(a_all[:, cl - 1:cl] - a_all) * dt_all
        a2 = a_all * LOG2_E
        steps_row_ref[0] = a2 - jnp.log2(dt_all)
        for i, part in enumerate(_split3(a2) + _split3(jnp.exp(a_all)) + _split3(te_all)):
            steps_parts_ref[i] = part

    def per_position(first_part, g, selector):
        parts = [steps_parts_ref[first_part + i, g * SUBLANES:(g + 1) * SUBLANES, :] for i in range(3)]
        return _dot_tn(jnp.concatenate(parts, axis=0).astype(BF16), selector)

    def dt_chunk(ref, ci):
        return ref[0, :, :, pl.ds(pl.multiple_of(ci * cl, cl), cl)]

    def chunk(ci, _):
        tok = pl.ds(pl.multiple_of(ci * cl, cl), cl)

        @pl.when((pl.program_id(2) == 0) & (ci == 0))
        def _():
            h_ref[...] = jnp.zeros_like(h_ref)
            hist_ref[...] = jnp.zeros_like(hist_ref)
            stage_step_sizes(dt_chunk(dt_ref, 0))

        src2_all = steps_row_ref[0]
        a_cols = [per_position(0, g, to_chunk_lanes) for g in range(n_groups)]
        decay_in = [per_position(3, g, to_head_lanes) for g in range(n_groups)]
        decay_out = [per_position(6, g, to_head_lanes) for g in range(n_groups)]
        dt_next = jnp.where(ci < n_chunks - 1, dt_chunk(dt_ref, jnp.minimum(ci + 1, n_chunks - 1)),
                            dt_chunk(dtn_ref, 0))
        stage_step_sizes(dt_next)

        raw = jnp.concatenate([xr_ref[tok, :], br_ref[tok, :], cr_ref[tok, :]], axis=1)
        ext = jnp.concatenate([hist_ref[...], raw], axis=0)
        hist_ref[...] = raw[cl - HALO:, :]
        cw = jnp.concatenate([wx_ref[...], wb_ref[...], wc_ref[...]], axis=1)
        acc = cw[0:1, :] * ext
        for kk in range(1, SSD_CONV):
            acc = cw[kk:kk + 1, :] * ext + pltpu.roll(acc, 1, axis=0)
        bias = jnp.concatenate([bx_ref[...], bb_ref[...], bc_ref[...]], axis=1)
        act = _silu(acc[HALO:, :] + bias)
        xs_all = act[:, :n_groups * hp]
        bm_all = act[:, n_groups * hp:n_groups * (hp + n)].astype(BF16)
        cm_all = act[:, n_groups * (hp + n):].astype(BF16)

        for g in range(n_groups):
            xs = xs_all[:, g * hp:(g + 1) * hp]
            bm = bm_all[:, g * n:(g + 1) * n]
            cm = cm_all[:, g * n:(g + 1) * n]
            src2 = src2_all[g * SUBLANES:(g + 1) * SUBLANES]

            cb = _dot_nt(cm, bm)
            ws, xblocks = [], []
            for r in range(SSD_HPG):
                seg = a_cols[g][:, r * cl:(r + 1) * cl] - src2[r:r + 1, :]
                ws.append((cb * jnp.exp2(jnp.where(causal, seg, -jnp.inf))).astype(BF16))
                in_head = (lane_hp >= r * SSD_HEAD_DIM) & (lane_hp < (r + 1) * SSD_HEAD_DIM)
                xblocks.append(jnp.where(in_head, xs, 0.0).astype(BF16))
            y = _dot(jnp.concatenate(ws, axis=1), jnp.concatenate(xblocks, axis=0))

            h_prev = h_ref[g]
            y = y + _dot(cm, h_prev.astype(BF16)) * decay_in[g]
            xw = (xs * decay_out[g]).astype(BF16)
            h_ref[g] = h_prev * decay_in[g][cl - 1:cl, :] + _dot_tn(bm, xw)

            y = (y + dsk_ref[g] * xs) * _silu(z_ref[tok, g * hp:(g + 1) * hp])
            o_ref[tok, g * hp:(g + 1) * hp] = _rms(y, ng_ref[g]).astype(o_ref.dtype)
        return 0

    lax.fori_loop(0, n_chunks, chunk, 0, unroll=True)


def _ssd_core(xbc, z, dt_t, conv_w, conv_b, dt_bias, a_log, d_skip, norm_g, b, l):
    t = xbc.shape[0]
    g, hpg, cl, n, ng = SSD_GROUPS, SSD_HPG, SSD_CHUNK, SSD_STATE, SSD_GROUPS_PER_STEP
    hp = hpg * SSD_HEAD_DIM
    d_inner = g * hp
    kc = SSD_CHUNKS_PER_STEP
    tl = kc * cl
    nc = l // tl
    assert l % tl == 0 and hp % LANES == 0 and n == LANES and g % ng == 0
    bb, cb = d_inner // (ng * n), (d_inner + g * n) // (ng * n)
    pad = lambda p: jnp.pad(p.reshape(g, hpg, 1), ((0, 0), (0, SUBLANES - hpg), (0, 0)))
    per_lane = lambda p: jnp.repeat(p.reshape(g, hpg), SSD_HEAD_DIM, axis=1).reshape(g, 1, hp)
    conv_b = conv_b.reshape(1, -1)
    tok = lambda bi, gi, ci: bi * nc + ci
    return pl.pallas_call(
        functools.partial(_ssd_kernel, n_groups=ng, n_chunks=kc),
        grid=(b, g // ng, nc),
        in_specs=[
            pl.BlockSpec((tl, ng * hp), lambda bi, gi, ci: (tok(bi, gi, ci), gi)),
            pl.BlockSpec((tl, ng * n), lambda bi, gi, ci: (tok(bi, gi, ci), bb + gi)),
            pl.BlockSpec((tl, ng * n), lambda bi, gi, ci: (tok(bi, gi, ci), cb + gi)),
            pl.BlockSpec((tl, ng * hp), lambda bi, gi, ci: (tok(bi, gi, ci), gi)),
            pl.BlockSpec((1, ng, SUBLANES, tl), lambda bi, gi, ci: (bi, gi, 0, ci)),
            pl.BlockSpec((1, ng, SUBLANES, tl), lambda bi, gi, ci: (bi, gi, 0, jnp.minimum(ci + 1, nc - 1))),
            pl.BlockSpec((SSD_CONV, ng * hp), lambda bi, gi, ci: (0, gi)),
            pl.BlockSpec((SSD_CONV, ng * n), lambda bi, gi, ci: (0, bb + gi)),
            pl.BlockSpec((SSD_CONV, ng * n), lambda bi, gi, ci: (0, cb + gi)),
            pl.BlockSpec((1, ng * hp), lambda bi, gi, ci: (0, gi)),
            pl.BlockSpec((1, ng * n), lambda bi, gi, ci: (0, bb + gi)),
            pl.BlockSpec((1, ng * n), lambda bi, gi, ci: (0, cb + gi)),
            pl.BlockSpec((ng, SUBLANES, 1), lambda bi, gi, ci: (gi, 0, 0)),
            pl.BlockSpec((ng, SUBLANES, 1), lambda bi, gi, ci: (gi, 0, 0)),
            pl.BlockSpec((ng, 1, hp), lambda bi, gi, ci: (gi, 0, 0)),
            pl.BlockSpec((ng, 1, hp), lambda bi, gi, ci: (gi, 0, 0)),
        ],
        out_specs=pl.BlockSpec((tl, ng * hp), lambda bi, gi, ci: (tok(bi, gi, ci), gi)),
        out_shape=jax.ShapeDtypeStruct((t, d_inner), BF16),
        scratch_shapes=[pltpu.VMEM((ng, n, hp), F32), pltpu.VMEM((HALO, ng * (hp + 2 * n)), F32),
                        pltpu.VMEM((1, ng * SUBLANES, cl), F32), pltpu.VMEM((9, ng * SUBLANES, cl), F32)],
        compiler_params=_params(3),
        name="ssd_core",
    )(xbc, xbc, xbc, z, dt_t, dt_t, conv_w, conv_w, conv_w, conv_b, conv_b, conv_b,
      pad(dt_bias), pad(a_log), per_lane(d_skip), norm_g.reshape(g, 1, hp))


def _ssd_mixer(x, b, l, norm_g, w_in, conv_w, conv_b, dt_bias, a_log, d_skip, ssd_norm, layer):
    g, hpg = SSD_GROUPS, SSD_HPG
    d_inner = ssd_norm.shape[0]
    conv_dim = conv_w.shape[1]
    n_heads = g * hpg
    z, xbc, dt = _norm_proj(
        x, norm_g, w_in, layer,
        [(0, d_inner, 1.0), (d_inner, conv_dim, 1.0), (d_inner + conv_dim, n_heads, 1.0)], [F32] * 3)
    dt_t = dt.reshape(b, l, g, hpg).transpose(0, 2, 3, 1)
    dt_t = jnp.pad(dt_t, ((0, 0), (0, 0), (0, SUBLANES - hpg), (0, 0)))
    return _ssd_core(xbc, z, dt_t, conv_w, conv_b, dt_bias, a_log, d_skip, ssd_norm, b, l)


def _short_conv_kernel(x_ref, g_ref, wi_ref, cw_ref, wo_ref, o_ref, ext_ref, xn_ref, acc_ref):
    tm, d = x_ref.shape

    @pl.when(pl.program_id(1) == 0)
    def _():
        ext_ref[0:HALO, :] = jnp.zeros((HALO, ext_ref.shape[1]), F32)

    x = x_ref[...]
    xn_ref[...] = _rms(x, g_ref[...]).astype(BF16)
    for c in range(d // SC_TC):
        cols = slice(c * SC_TC, (c + 1) * SC_TC)
        w_part = lambda i, c=c: wi_ref[:, i * d + c * SC_TC:i * d + (c + 1) * SC_TC].astype(BF16)
        ext_ref[HALO:HALO + tm, cols] = _dot(xn_ref[...], w_part(1)) * _dot(xn_ref[...], w_part(2))
        cw = cw_ref[:, cols]
        u = cw[0:1, :] * ext_ref[pl.ds(HALO - (SC_WIDTH - 1), tm), cols]
        for kk in range(1, SC_WIDTH):
            u = u + cw[kk:kk + 1, :] * ext_ref[pl.ds(HALO - (SC_WIDTH - 1) + kk, tm), cols]
        gated = (_dot(xn_ref[...], w_part(0)) * u).astype(BF16)
        part = _dot(gated, wo_ref[cols, :].astype(BF16))
        if c == 0:
            acc_ref[...] = part
        else:
            acc_ref[...] += part
    ext_ref[0:HALO, :] = ext_ref[tm:tm + HALO, :]
    o_ref[...] = x + acc_ref[...]


def _short_conv_layer(x, b, l, norm_g, w_in, conv_w, w_out, layer):
    t, d = x.shape
    tm = SC_TM
    nl = l // tm
    assert l % tm == 0
    return pl.pallas_call(
        _short_conv_kernel,
        grid=(b, nl),
        in_specs=[
            pl.BlockSpec((tm, d), lambda bi, li: (bi * nl + li, 0)),
            _resident((1, d)),
            _resident((None, d, 3 * d), (layer, 0, 0)),
            _resident((None, SC_WIDTH, d), (layer, 0, 0)),
            _resident((None, d, d), (layer, 0, 0)),
        ],
        out_specs=pl.BlockSpec((tm, d), lambda bi, li: (bi * nl + li, 0)),
        out_shape=jax.ShapeDtypeStruct((t, d), F32),
        scratch_shapes=[pltpu.VMEM((tm + HALO, d), F32), pltpu.VMEM((tm, d), BF16),
                        pltpu.VMEM((tm, d), F32)],
        compiler_params=_params(2),
        name="short_conv",
    )(x, norm_g.reshape(1, d), w_in, conv_w, w_out)


def kernel(x, ffn1_norm, ffn1_w_gu, ffn1_w_down, mix_norm, ffn2_norm, ffn2_w_gu, ffn2_w_down,
           sb_w_qkv, sb_w_o, ssd_w_in, ssd_conv_w, ssd_conv_b, ssd_dt_bias, ssd_a_log, ssd_d,
           ssd_norm, ssd_w_out, sc_w_in, sc_conv_w, sc_w_out, final_norm):
    b, l, d = x.shape
    depth = ffn1_norm.shape[0]
    h = x.reshape(b * l, d)
    for i in range(depth):
        h = _ffn(h, ffn1_norm[i], ffn1_w_gu, ffn1_w_down, i)
        kind, j = i % N_MIXERS, i // N_MIXERS
        mixer_out = None
        if kind == 0:
            mixer_out = (_sb_mixer(h, b, l, mix_norm[i], sb_w_qkv, j), sb_w_o, j)
        elif kind == 1:
            y = _ssd_mixer(h, b, l, mix_norm[i], ssd_w_in, ssd_conv_w[j], ssd_conv_b[j],
                           ssd_dt_bias[j], ssd_a_log[j], ssd_d[j], ssd_norm[j], j)
            mixer_out = (y, ssd_w_out, j)
        else:
            h = _short_conv_layer(h, b, l, mix_norm[i], sc_w_in, sc_conv_w, sc_w_out, j)
        h = _ffn(h, ffn2_norm[i], ffn2_w_gu, ffn2_w_down, i,
                 final_g=final_norm if i == depth - 1 else None, mixer_out=mixer_out)
    return h.reshape(b, l, d)
```

```python
import functools

import jax
import jax.numpy as jnp
from jax import lax
from jax.experimental import pallas as pl
from jax.experimental.pallas import tpu as pltpu

F32 = jnp.float32
BF16 = jnp.bfloat16

RMS_EPS = 1e-6
LOG2_E = 1.4426950408889634
N_MIXERS = 3
SB_HEAD_DIM = 64
SSD_HEAD_DIM = 64
SSD_GROUPS = 8
SSD_HPG = 4
SSD_STATE = 128
SSD_CONV = 4
SSD_CHUNK = 128
SC_WIDTH = 3

LANES = 128
SUBLANES = 8
VMEM_LIMIT_BYTES = 60 * 1024 * 1024

FFN_TM = 1024
FFN_FUSED_TM = 512
FFN_TF = 256
PROJ_TM = 512
PROJ_VMEM_BUDGET = 56 * 1024 * 1024
PROJ_TN = 512
ATT_T = 256
ATT_PAIRS_PER_STEP = 4
ATT_PAIRS_TOGETHER = 2
ATT_DEAD_CARRY = 160.0
SSD_GROUPS_PER_STEP = 4
SSD_CHUNKS_PER_STEP = 4
SC_TM = 1024
HALO = SUBLANES


def _params(n_axes):
    return pltpu.CompilerParams(
        dimension_semantics=("arbitrary",) * n_axes,
        vmem_limit_bytes=VMEM_LIMIT_BYTES)


def _resident(shape, index=None):
    index = (0,) * len(shape) if index is None else index
    return pl.BlockSpec(shape, lambda *_: index, pipeline_mode=pl.Buffered(1))


def _rms(x, g):
    ms = jnp.mean(x * x, axis=-1, keepdims=True)
    return x * lax.rsqrt(ms + RMS_EPS) * g


def _silu(x):
    h = 0.5 * x
    return h + h * jnp.tanh(h)


def _dot(a, b):
    return jnp.dot(a, b, preferred_element_type=F32)


def _dot_nt(a, b):
    return lax.dot_general(a, b, (((1,), (1,)), ((), ())), preferred_element_type=F32)


def _dot_tn(a, b):
    return lax.dot_general(a, b, (((0,), (0,)), ((), ())), preferred_element_type=F32)


def _ffn_kernel(x_ref, g_ref, wgu_ref, wd_ref, *refs, n_chunks, tf, final_norm, mixer_out):
    refs = list(refs)
    y_ref, wo_ref = (refs.pop(0), refs.pop(0)) if mixer_out else (None, None)
    fg_ref = refs[0] if final_norm else None
    o_ref, xn_ref, acc_ref = refs[-3:]
    x = x_ref[...]
    if mixer_out:
        x = x + _dot(y_ref[...], wo_ref[...].astype(BF16))
    d_ff = n_chunks * tf
    xn_ref[...] = _rms(x, g_ref[...]).astype(BF16)
    for c in range(n_chunks):
        gate = _dot(xn_ref[...], wgu_ref[:, c * tf:(c + 1) * tf].astype(BF16))
        up = _dot(xn_ref[...], wgu_ref[:, d_ff + c * tf:d_ff + (c + 1) * tf].astype(BF16))
        h = (_silu(gate) * up).astype(BF16)
        part = _dot(h, wd_ref[c * tf:(c + 1) * tf, :].astype(BF16))
        if c == 0:
            acc_ref[...] = part
        else:
            acc_ref[...] += part
    y = x + 0.5 * acc_ref[...]
    if final_norm:
        y = _rms(y, fg_ref[...])
    o_ref[...] = y


def _ffn(x, norm_g, w_gu, w_down, layer, final_g=None, mixer_out=None):
    t, d = x.shape
    d_ff = w_down.shape[1]
    tm, tf = (FFN_TM if mixer_out is None else FFN_FUSED_TM), FFN_TF
    n_chunks = d_ff // tf
    assert t % tm == 0 and d_ff % tf == 0
    final_norm = final_g is not None
    extra = [final_g.reshape(1, d)] if final_norm else []
    pre, pre_specs = [], []
    if mixer_out is not None:
        y, w_o, j = mixer_out
        k = y.shape[1]
        pre = [y, w_o]
        pre_specs = [pl.BlockSpec((tm, k), lambda i: (i, 0)), _resident((None, k, d), (j, 0, 0))]
    return pl.pallas_call(
        functools.partial(_ffn_kernel, n_chunks=n_chunks, tf=tf, final_norm=final_norm,
                          mixer_out=mixer_out is not None),
        grid=(t // tm,),
        in_specs=[
            pl.BlockSpec((tm, d), lambda i: (i, 0)),
            _resident((1, d)),
            _resident((None, d, 2 * d_ff), (layer, 0, 0)),
            _resident((None, d_ff, d), (layer, 0, 0)),
        ] + pre_specs + [_resident((1, d))] * len(extra),
        out_specs=pl.BlockSpec((tm, d), lambda i: (i, 0)),
        out_shape=jax.ShapeDtypeStruct((t, d), F32),
        scratch_shapes=[pltpu.VMEM((tm, d), BF16), pltpu.VMEM((tm, d), F32)],
        compiler_params=_params(1),
        name="ffn",
    )(x, norm_g.reshape(1, d), w_gu, w_down, *pre, *extra)


def _norm_proj_kernel(x_ref, g_ref, w_ref, *o_refs, segments, tn):
    xn = _rms(x_ref[...], g_ref[...]).astype(BF16)
    for o_ref, (start, width, scale) in zip(o_refs, segments):
        for n0 in range(0, width, tn):
            n1 = min(n0 + tn, width)
            w = w_ref[:, start + n0:start + n1]
            if scale != 1.0:
                w = w * scale
            o_ref[:, n0:n1] = _dot(xn, w.astype(BF16)).astype(o_ref.dtype)


def _norm_proj(x, norm_g, w, layer, segments, out_dtypes):
    t, d = x.shape
    row_bytes = 4 * d + sum(s[1] * jnp.dtype(dt).itemsize for s, dt in zip(segments, out_dtypes))
    tm = next(c for c in (2 * PROJ_TM, PROJ_TM)
              if t % c == 0 and 4 * w[0].size + 2 * c * row_bytes <= PROJ_VMEM_BUDGET)
    assert all(s[0] % LANES == 0 for s in segments)
    return pl.pallas_call(
        functools.partial(_norm_proj_kernel, segments=tuple(segments), tn=PROJ_TN),
        grid=(t // tm,),
        in_specs=[pl.BlockSpec((tm, d), lambda i: (i, 0)), _resident((1, d)),
                  _resident((None,) + w.shape[1:], (layer, 0, 0))],
        out_specs=[pl.BlockSpec((tm, s[1]), lambda i: (i, 0)) for s in segments],
        out_shape=[jax.ShapeDtypeStruct((t, s[1]), dt) for s, dt in zip(segments, out_dtypes)],
        compiler_params=_params(1),
        name="norm_proj",
    )(x, norm_g.reshape(1, d), w)


def _sb_attn_kernel(q_ref, k_ref, v_ref, o_ref, lb_a, sp_a, rs_a, lb_b, sp_b, rs_b, acc_all, c_all,
                    *, t, n_together):
    qi = pl.program_id(2)
    hd = SB_HEAD_DIM
    n_p = n_together
    sub = 2 * n_p * t
    both, first, second = (0, 2), (0, 1), (1, 2)
    in_a = lax.broadcasted_iota(jnp.int32, (1, 2 * hd), 1) < hd
    row = lax.broadcasted_iota(jnp.int32, (t, t), 0)
    col = lax.broadcasted_iota(jnp.int32, (t, t), 1)
    strict_lower = jnp.concatenate([col < row] * (2 * n_p), axis=0)
    tri = jnp.where(row > col, 1.0, 0.0).astype(BF16)
    sign_bit = jnp.uint32(0x80000000)
    top = 2 * qi + 1

    def mask_diagonal(x, fill):
        head = jnp.where(strict_lower, x[:sub], fill)
        return head if x.shape[0] == sub else jnp.concatenate([head, x[sub:]], axis=0)

    def pair_group(grp):
        lanes = [pl.ds((grp * n_p + p) * 2 * hd, 2 * hd) for p in range(n_p)]
        buf_a = (lb_a.at[grp], sp_a.at[grp], rs_a.at[grp])
        buf_b = (lb_b.at[grp], sp_b.at[grp], rs_b.at[grp])
        acc_ref, c_ref = acc_all.at[grp], c_all.at[grp]
        q_parts = []
        for s in range(2):
            for p in range(n_p):
                q2 = q_ref[0, s * t:(s + 1) * t, lanes[p]]
                q_parts += [jnp.where(in_a, q2, 0), jnp.where(in_a, 0, q2)]
        qs = jnp.concatenate(q_parts, axis=0)

        def scores(j, buf, subs, diag):
            lb_ref, sp_ref, rs_ref = buf
            r0, r1 = subs[0] * sub, subs[1] * sub
            rows_k = pl.ds(pl.multiple_of(j * t, t), t)
            k2 = [k_ref[0, rows_k, lanes[p]] for p in range(n_p)]
            z = jnp.concatenate(
                [_dot_nt(qs[(s * n_p + p) * 2 * t:(s * n_p + p + 1) * 2 * t], k2[p])
                 for s in range(*subs) for p in range(n_p)], axis=0)
            neg_abs = lax.bitcast_convert_type(lax.bitcast_convert_type(z, jnp.uint32) | sign_bit, F32)
            lb = jnp.minimum(z, 0.0) - jnp.log2(1.0 + jnp.exp2(neg_abs))
            sp = z - lb
            if diag:
                sp, lb = mask_diagonal(sp, 0.0), mask_diagonal(lb, -jnp.inf)
            lb_ref[r0:r1, :] = lb
            sp_ref[r0:r1, :] = sp.astype(BF16)
            rs_ref[r0:r1, :] = jnp.broadcast_to(jnp.sum(sp, axis=-1, keepdims=True), (r1 - r0, LANES))

        def values(j, buf, subs):
            lb_ref, sp_ref, rs_ref = buf
            r0, r1 = subs[0] * sub, subs[1] * sub
            rows_k = pl.ds(pl.multiple_of(j * t, t), t)
            c = c_ref[r0:r1, :]
            tail = _dot(sp_ref[r0:r1, :], tri)
            c_wide = jnp.concatenate([c] * (t // LANES), axis=1)
            att = jnp.exp2(lb_ref[r0:r1, :] - tail - c_wide).astype(BF16)
            c_ref[r0:r1, :] = c + rs_ref[r0:r1, :]
            for p in range(n_p):
                blocks = []
                for s in range(subs[1] - subs[0]):
                    base = (s * n_p + p) * 2 * t
                    blocks.append(jnp.concatenate([att[base:base + t], att[base + t:base + 2 * t]], axis=1))
                v2 = v_ref[0, rows_k, lanes[p]]
                vv = jnp.concatenate([jnp.where(in_a, v2, 0), jnp.where(in_a, 0, v2)], axis=0)
                acc_ref[p, subs[0] * t:subs[1] * t, :] += _dot(jnp.concatenate(blocks, axis=0), vv)

        def tile_pair(state):
            i = state[0]
            j = top - 3 - 2 * i
            values(j, buf_b, both)
            scores(j - 1, buf_a, both, False)
            values(j - 1, buf_a, both)
            scores(j - 2, buf_b, both, False)
            return i + 1, jnp.min(c_ref[...])

        def live(state):
            return (state[0] < qi - 1) & (state[1] < ATT_DEAD_CARRY)

        def first_tile_head():
            scores(top, buf_a, second, True)
            values(top, buf_a, second)
            scores(top - 1, buf_b, both, True)
            values(top - 1, buf_b, both)

        def head():
            scores(top, buf_a, second, True)
            scores(top - 1, buf_b, both, True)
            values(top, buf_a, second)
            scores(top - 2, buf_a, first, False)
            values(top - 1, buf_b, both)
            c_min0 = jnp.minimum(jnp.min(c_ref[sub:, :]), jnp.min(c_ref[:sub, :] + buf_a[2][:sub, :]))
            values(top - 2, buf_a, first)
            return c_min0

        def older_tiles(c_min0):
            @pl.when(c_min0 < ATT_DEAD_CARRY)
            def _():
                scores(top - 2, buf_a, second, False)
                values(top - 2, buf_a, second)
                scores(top - 3, buf_b, both, False)
                n_pairs, c_min = lax.while_loop(live, tile_pair, (jnp.int32(0), c_min0))

                @pl.when(c_min < ATT_DEAD_CARRY)
                def _():
                    values(top - 3 - 2 * n_pairs, buf_b, both)

        def write_out():
            for p in range(n_p):
                o_ref[0, :, lanes[p]] = acc_ref[p].astype(o_ref.dtype)

        return first_tile_head, head, older_tiles, write_out

    groups = [pair_group(grp) for grp in range(q_ref.shape[2] // (2 * hd * n_p))]
    acc_all[...] = jnp.zeros_like(acc_all)
    c_all[...] = jnp.zeros_like(c_all)

    @pl.when(qi == 0)
    def _():
        for first_tile_head, _, _, _ in groups:
            first_tile_head()

    @pl.when(qi > 0)
    def _():
        carries = [head() for _, head, _, _ in groups]
        for (_, _, older_tiles, _), c_min0 in zip(groups, carries):
            older_tiles(c_min0)

    for _, _, _, write_out in groups:
        write_out()


def _sb_attention(q, k, v):
    b, l, d = q.shape
    t, n_p = ATT_T, ATT_PAIRS_TOGETHER
    tq = 2 * t
    hp = 2 * SB_HEAD_DIM * ATT_PAIRS_PER_STEP
    assert l % tq == 0 and d % hp == 0 and 2 * SB_HEAD_DIM == LANES and ATT_PAIRS_PER_STEP % n_p == 0
    rows = 4 * n_p * t
    n_g = ATT_PAIRS_PER_STEP // n_p
    stage = [pltpu.VMEM((n_g, rows, t), F32), pltpu.VMEM((n_g, rows, t), BF16),
             pltpu.VMEM((n_g, rows, LANES), F32)]
    return pl.pallas_call(
        functools.partial(_sb_attn_kernel, t=t, n_together=n_p),
        grid=(b, d // hp, l // tq),
        in_specs=[
            pl.BlockSpec((1, tq, hp), lambda bi, hi, qi: (bi, qi, hi)),
            pl.BlockSpec((1, l, hp), lambda bi, hi, qi: (bi, 0, hi)),
            pl.BlockSpec((1, l, hp), lambda bi, hi, qi: (bi, 0, hi)),
        ],
        out_specs=pl.BlockSpec((1, tq, hp), lambda bi, hi, qi: (bi, qi, hi)),
        out_shape=jax.ShapeDtypeStruct((b, l, d), BF16),
        scratch_shapes=stage + stage + [pltpu.VMEM((n_g, n_p, tq, LANES), F32),
                                        pltpu.VMEM((n_g, rows, LANES), F32)],
        compiler_params=_params(3),
        name="sb_attn",
    )(q, k, v)


def _sb_mixer(x, b, l, norm_g, w_qkv, layer):
    t, d = x.shape
    scale = LOG2_E * SB_HEAD_DIM ** -0.5
    q, k, v = _norm_proj(x, norm_g, w_qkv, layer,
                         [(0, d, scale), (d, d, 1.0), (2 * d, d, 1.0)], [BF16] * 3)
    o = _sb_attention(q.reshape(b, l, d), k.reshape(b, l, d), v.reshape(b, l, d))
    return o.reshape(t, d)


def _split3(x):
    hi = x.astype(BF16).astype(F32)
    rest = x - hi
    mid = rest.astype(BF16).astype(F32)
    return hi, mid, (rest - mid).astype(BF16).astype(F32)


def _head_selector(width):
    k = lax.broadcasted_iota(jnp.int32, (3 * SUBLANES, SSD_HPG * width), 0) % SUBLANES
    lane = lax.broadcasted_iota(jnp.int32, (3 * SUBLANES, SSD_HPG * width), 1)
    return jnp.where((lane >= k * width) & (lane < (k + 1) * width), 1.0, 0.0).astype(BF16)


def _ssd_kernel(xr_ref, br_ref, cr_ref, z_ref, dt_ref, dtn_ref, wx_ref, wb_ref, wc_ref, bx_ref,
                bb_ref, bc_ref, dtb_ref, alog_ref, dsk_ref, ng_ref, o_ref, h_ref, hist_ref,
                steps_row_ref, steps_parts_ref, *, n_groups, n_chunks):
    cl, hp, n = SSD_CHUNK, SSD_HPG * SSD_HEAD_DIM, SSD_STATE
    rows = n_groups * SUBLANES
    row = lax.broadcasted_iota(jnp.int32, (cl, cl), 0)
    col = lax.broadcasted_iota(jnp.int32, (cl, cl), 1)
    causal = col <= row
    lane_hp = lax.broadcasted_iota(jnp.int32, (cl, hp), 1)
    to_chunk_lanes = _head_selector(cl)
    to_head_lanes = _head_selector(SSD_HEAD_DIM)

    def stage_step_sizes(dt_blk):
        dt_in = (dt_blk + dtb_ref[...]).reshape(rows, cl)
        dt_all = jnp.maximum(dt_in, 0.0) + jnp.log(1.0 + jnp.exp(-jnp.abs(dt_in)))
        a_all = dt_all * (-jnp.exp(alog_ref[...].reshape(rows, 1)))
        lane_cl = lax.broadcasted_iota(jnp.int32, (rows, cl), 1)
        sh = 1
        while sh < cl:
            a_all = a_all + jnp.where(lane_cl >= sh, pltpu.roll(a_all, sh, axis=1), 0.0)
            sh *= 2
        te_all = jnp.exp(a_all[:, cl - 1:cl] - a_all) * dt_all
        a2 = a_all * LOG2_E
        steps_row_ref[0] = a2 - jnp.log2(dt_all)
        for i, part in enumerate(_split3(a2) + _split3(jnp.exp(a_all)) + _split3(te_all)):
            steps_parts_ref[i] = part

    def per_position(first_part, g, selector):
        parts = [steps_parts_ref[first_part + i, g * SUBLANES:(g + 1) * SUBLANES, :] for i in range(3)]
        return _dot_tn(jnp.concatenate(parts, axis=0).astype(BF16), selector)

    def dt_chunk(ref, ci):
        return ref[0, :, :, pl.ds(pl.multiple_of(ci * cl, cl), cl)]

    def chunk(ci, _):
        tok = pl.ds(pl.multiple_of(ci * cl, cl), cl)

        @pl.when((pl.program_id(2) == 0) & (ci == 0))
        def _():
            h_ref[...] = jnp.zeros_like(h_ref)
            hist_ref[...] = jnp.zeros_like(hist_ref)
            stage_step_sizes(dt_chunk(dt_ref, 0))

        src2_all = steps_row_ref[0]
        a_cols = [per_position(0, g, to_chunk_lanes) for g in range(n_groups)]
        decay_in = [per_position(3, g, to_head_lanes) for g in range(n_groups)]
        decay_out = [per_position(6, g, to_head_lanes) for g in range(n_groups)]
        dt_next = jnp.where(ci < n_chunks - 1, dt_chunk(dt_ref, jnp.minimum(ci + 1, n_chunks - 1)),
                            dt_chunk(dtn_ref, 0))
        stage_step_sizes(dt_next)

        raw = jnp.concatenate([xr_ref[tok, :], br_ref[tok, :], cr_ref[tok, :]], axis=1)
        ext = jnp.concatenate([hist_ref[...], raw], axis=0)
        hist_ref[...] = raw[cl - HALO:, :]
        cw = jnp.concatenate([wx_ref[...], wb_ref[...], wc_ref[...]], axis=1)
        acc = cw[0:1, :] * ext
        for kk in range(1, SSD_CONV):
            acc = cw[kk:kk + 1, :] * ext + pltpu.roll(acc, 1, axis=0)
        bias = jnp.concatenate([bx_ref[...], bb_ref[...], bc_ref[...]], axis=1)
        act = _silu(acc[HALO:, :] + bias)
        xs_all = act[:, :n_groups * hp]
        bm_all = act[:, n_groups * hp:n_groups * (hp + n)].astype(BF16)
        cm_all = act[:, n_groups * (hp + n):].astype(BF16)

        for g in range(n_groups):
            xs = xs_all[:, g * hp:(g + 1) * hp]
            bm = bm_all[:, g * n:(g + 1) * n]
            cm = cm_all[:, g * n:(g + 1) * n]
            src2 = src2_all[g * SUBLANES:(g + 1) * SUBLANES]

            cb = _dot_nt(cm, bm)
            ws, xblocks = [], []
            for r in range(SSD_HPG):
                seg = a_cols[g][:, r * cl:(r + 1) * cl] - src2[r:r + 1, :]
                ws.append((cb * jnp.exp2(jnp.where(causal, seg, -jnp.inf))).astype(BF16))
                in_head = (lane_hp >= r * SSD_HEAD_DIM) & (lane_hp < (r + 1) * SSD_HEAD_DIM)
                xblocks.append(jnp.where(in_head, xs, 0.0).astype(BF16))
            y = _dot(jnp.concatenate(ws, axis=1), jnp.concatenate(xblocks, axis=0))

            h_prev = h_ref[g]
            y = y + _dot(cm, h_prev.astype(BF16)) * decay_in[g]
            xw = (xs * decay_out[g]).astype(BF16)
            h_ref[g] = h_prev * decay_in[g][cl - 1:cl, :] + _dot_tn(bm, xw)

            y = (y + dsk_ref[g] * xs) * _silu(z_ref[tok, g * hp:(g + 1) * hp])
            o_ref[tok, g * hp:(g + 1) * hp] = _rms(y, ng_ref[g]).astype(o_ref.dtype)
        return 0

    lax.fori_loop(0, n_chunks, chunk, 0, unroll=True)


def _ssd_core(xbc, z, dt_t, conv_w, conv_b, dt_bias, a_log, d_skip, norm_g, b, l):
    t = xbc.shape[0]
    g, hpg, cl, n, ng = SSD_GROUPS, SSD_HPG, SSD_CHUNK, SSD_STATE, SSD_GROUPS_PER_STEP
    hp = hpg * SSD_HEAD_DIM
    d_inner = g * hp
    kc = SSD_CHUNKS_PER_STEP
    tl = kc * cl
    nc = l // tl
    assert l % tl == 0 and hp % LANES == 0 and n == LANES and g % ng == 0
    bb, cb = d_inner // (ng * n), (d_inner + g * n) // (ng * n)
    pad = lambda p: jnp.pad(p.reshape(g, hpg, 1), ((0, 0), (0, SUBLANES - hpg), (0, 0)))
    per_lane = lambda p: jnp.repeat(p.reshape(g, hpg), SSD_HEAD_DIM, axis=1).reshape(g, 1, hp)
    conv_b = conv_b.reshape(1, -1)
    tok = lambda bi, gi, ci: bi * nc + ci
    return pl.pallas_call(
        functools.partial(_ssd_kernel, n_groups=ng, n_chunks=kc),
        grid=(b, g // ng, nc),
        in_specs=[
            pl.BlockSpec((tl, ng * hp), lambda bi, gi, ci: (tok(bi, gi, ci), gi)),
            pl.BlockSpec((tl, ng * n), lambda bi, gi, ci: (tok(bi, gi, ci), bb + gi)),
            pl.BlockSpec((tl, ng * n), lambda bi, gi, ci: (tok(bi, gi, ci), cb + gi)),
            pl.BlockSpec((tl, ng * hp), lambda bi, gi, ci: (tok(bi, gi, ci), gi)),
            pl.BlockSpec((1, ng, SUBLANES, tl), lambda bi, gi, ci: (bi, gi, 0, ci)),
            pl.BlockSpec((1, ng, SUBLANES, tl), lambda bi, gi, ci: (bi, gi, 0, jnp.minimum(ci + 1, nc - 1))),
            pl.BlockSpec((SSD_CONV, ng * hp), lambda bi, gi, ci: (0, gi)),
            pl.BlockSpec((SSD_CONV, ng * n), lambda bi, gi, ci: (0, bb + gi)),
            pl.BlockSpec((SSD_CONV, ng * n), lambda bi, gi, ci: (0, cb + gi)),
            pl.BlockSpec((1, ng * hp), lambda bi, gi, ci: (0, gi)),
            pl.BlockSpec((1, ng * n), lambda bi, gi, ci: (0, bb + gi)),
            pl.BlockSpec((1, ng * n), lambda bi, gi, ci: (0, cb + gi)),
            pl.BlockSpec((ng, SUBLANES, 1), lambda bi, gi, ci: (gi, 0, 0)),
            pl.BlockSpec((ng, SUBLANES, 1), lambda bi, gi, ci: (gi, 0, 0)),
            pl.BlockSpec((ng, 1, hp), lambda bi, gi, ci: (gi, 0, 0)),
            pl.BlockSpec((ng, 1, hp), lambda bi, gi, ci: (gi, 0, 0)),
        ],
        out_specs=pl.BlockSpec((tl, ng * hp), lambda bi, gi, ci: (tok(bi, gi, ci), gi)),
        out_shape=jax.ShapeDtypeStruct((t, d_inner), BF16),
        scratch_shapes=[pltpu.VMEM((ng, n, hp), F32), pltpu.VMEM((HALO, ng * (hp + 2 * n)), F32),
                        pltpu.VMEM((1, ng * SUBLANES, cl), F32), pltpu.VMEM((9, ng * SUBLANES, cl), F32)],
        compiler_params=_params(3),
        name="ssd_core",
    )(xbc, xbc, xbc, z, dt_t, dt_t, conv_w, conv_w, conv_w, conv_b, conv_b, conv_b,
      pad(dt_bias), pad(a_log), per_lane(d_skip), norm_g.reshape(g, 1, hp))


def _ssd_mixer(x, b, l, norm_g, w_in, conv_w, conv_b, dt_bias, a_log, d_skip, ssd_norm, layer):
    g, hpg = SSD_GROUPS, SSD_HPG
    d_inner = ssd_norm.shape[0]
    conv_dim = conv_w.shape[1]
    n_heads = g * hpg
    z, xbc, dt = _norm_proj(
        x, norm_g, w_in, layer,
        [(0, d_inner, 1.0), (d_inner, conv_dim, 1.0), (d_inner + conv_dim, n_heads, 1.0)], [F32] * 3)
    dt_t = dt.reshape(b, l, g, hpg).transpose(0, 2, 3, 1)
    dt_t = jnp.pad(dt_t, ((0, 0), (0, 0), (0, SUBLANES - hpg), (0, 0)))
    return _ssd_core(xbc, z, dt_t, conv_w, conv_b, dt_bias, a_log, d_skip, ssd_norm, b, l)


def _short_conv_kernel(x_ref, g_ref, wi_ref, cw_ref, o_ref, ext_ref):
    tm, d = x_ref.shape

    @pl.when(pl.program_id(1) == 0)
    def _():
        ext_ref[0:HALO, :] = jnp.zeros((HALO, ext_ref.shape[1]), F32)

    xn = _rms(x_ref[...], g_ref[...]).astype(BF16)
    w_part = lambda i: wi_ref[:, i * d:(i + 1) * d].astype(BF16)
    ext_ref[HALO:HALO + tm, :] = _dot(xn, w_part(1)) * _dot(xn, w_part(2))
    cw = cw_ref[...]
    u = cw[0:1, :] * ext_ref[pl.ds(HALO - (SC_WIDTH - 1), tm), :]
    for kk in range(1, SC_WIDTH):
        u = u + cw[kk:kk + 1, :] * ext_ref[pl.ds(HALO - (SC_WIDTH - 1) + kk, tm), :]
    ext_ref[0:HALO, :] = ext_ref[tm:tm + HALO, :]
    o_ref[...] = (_dot(xn, w_part(0)) * u).astype(BF16)


def _short_conv_layer(x, b, l, norm_g, w_in, conv_w, layer):
    t, d = x.shape
    tm = SC_TM
    nl = l // tm
    assert l % tm == 0
    return pl.pallas_call(
        _short_conv_kernel,
        grid=(b, nl),
        in_specs=[
            pl.BlockSpec((tm, d), lambda bi, li: (bi * nl + li, 0)),
            _resident((1, d)),
            _resident((None, d, 3 * d), (layer, 0, 0)),
            _resident((None, SC_WIDTH, d), (layer, 0, 0)),
        ],
        out_specs=pl.BlockSpec((tm, d), lambda bi, li: (bi * nl + li, 0)),
        out_shape=jax.ShapeDtypeStruct((t, d), BF16),
        scratch_shapes=[pltpu.VMEM((tm + HALO, d), F32)],
        compiler_params=_params(2),
        name="short_conv",
    )(x, norm_g.reshape(1, d), w_in, conv_w)


def kernel(x, ffn1_norm, ffn1_w_gu, ffn1_w_down, mix_norm, ffn2_norm, ffn2_w_gu, ffn2_w_down,
           sb_w_qkv, sb_w_o, ssd_w_in, ssd_conv_w, ssd_conv_b, ssd_dt_bias, ssd_a_log, ssd_d,
           ssd_norm, ssd_w_out, sc_w_in, sc_conv_w, sc_w_out, final_norm):
    b, l, d = x.shape
    depth = ffn1_norm.shape[0]
    h = x.reshape(b * l, d)
    for i in range(depth):
        h = _ffn(h, ffn1_norm[i], ffn1_w_gu, ffn1_w_down, i)
        kind, j = i % N_MIXERS, i // N_MIXERS
        mixer_out = None
        if kind == 0:
            mixer_out = (_sb_mixer(h, b, l, mix_norm[i], sb_w_qkv, j), sb_w_o, j)
        elif kind == 1:
            y = _ssd_mixer(h, b, l, mix_norm[i], ssd_w_in, ssd_conv_w[j], ssd_conv_b[j],
                           ssd_dt_bias[j], ssd_a_log[j], ssd_d[j], ssd_norm[j], j)
            mixer_out = (y, ssd_w_out, j)
        else:
            mixer_out = (_short_conv_layer(h, b, l, mix_norm[i], sc_w_in, sc_conv_w, j), sc_w_out, j)
        h = _ffn(h, ffn2_norm[i], ffn2_w_gu, ffn2_w_down, i,
                 final_g=final_norm if i == depth - 1 else None, mixer_out=mixer_out)
    return h.reshape(b, l, d)
```
